```python
import math
import jax, jax.numpy as jnp
from jax import lax
import numpy as np

D_MODEL = 2048
BATCH = 2
SEQ = 4096
DEPTH = 2
DEC_BATCH = 8
DEC_SEQ = 1
PAST_LEN = 16384
PAGE_SIZE = 128

D_MIX = D_MODEL
N_MIXERS = 4
W_BR = D_MIX // N_MIXERS
HEAD_DIM = 64
H_A = W_BR // HEAD_DIM
H_B = W_BR // HEAD_DIM
H_C = W_BR // HEAD_DIM
CONV_W = 4
GDN_CHUNK = 64
LORA_B = 32
D_B_IN = 3 * W_BR + 2 * LORA_B
POOL_WINDOWS = (2, 4, 8, 16)
POOL_GROUPS = len(POOL_WINDOWS)
POOL_GW = W_BR // POOL_GROUPS
POOL_BUF = max(POOL_WINDOWS) - 1
Q_BLOCK = 128
D_IN = (3 * W_BR + 2 * H_A + W_BR) + (D_B_IN + W_BR) + (3 * W_BR + H_C + W_BR) + 2 * W_BR
ALPHA_DN = (2.0 * DEPTH) ** 0.25
BETA_DN = (8.0 * DEPTH) ** -0.25
LN_EPS = 1e-5
GN_EPS = 64e-5
RMS_EPS = 1e-6
L2_EPS = 1e-6
RWKV_DECAY_SCALE = math.exp(-0.5)
NEG_INF = -1e30
F32 = jnp.float32

kernel_name = 'hymba_gdn_rwkv7_fox_pool_step'


def _silu(x):
    return x * jax.nn.sigmoid(x)


def _l2norm(x):
    return x * lax.rsqrt(jnp.sum(x * x, axis=-1, keepdims=True) + L2_EPS)


def _layernorm(x, g, b):
    xf = x.astype(F32)
    mu = jnp.mean(xf, axis=-1, keepdims=True)
    var = jnp.mean(jnp.square(xf - mu), axis=-1, keepdims=True)
    return ((xf - mu) * lax.rsqrt(var + LN_EPS) * g.astype(F32) + b.astype(F32)).astype(x.dtype)


def _split_proj(proj):
    sizes = (3 * W_BR, H_A, H_A, W_BR, D_B_IN, W_BR, 3 * W_BR, H_C, W_BR, W_BR, W_BR)
    idx, acc = [], 0
    for s in sizes[:-1]:
        acc += s
        idx.append(acc)
    return jnp.split(proj, idx, axis=-1)


def _causal_conv(u, buf, w):
    T = u.shape[1]
    ext = jnp.concatenate([buf.astype(F32), u.astype(F32)], axis=1)
    w = w.astype(F32)
    out = ext[:, 0:T] * w[0]
    for i in range(1, CONV_W):
        out = out + ext[:, i:i + T] * w[i]
    return out, ext[:, T:]


def _to_chunks4(a, nc):
    N, T, H, D = a.shape
    a = jnp.pad(a.astype(F32), ((0, 0), (0, nc * GDN_CHUNK - T), (0, 0), (0, 0)))
    return a.reshape(N, nc, GDN_CHUNK, H, D).transpose(1, 0, 3, 2, 4)


def _to_chunks3(a, nc):
    N, T, H = a.shape
    a = jnp.pad(a.astype(F32), ((0, 0), (0, nc * GDN_CHUNK - T), (0, 0)))
    return a.reshape(N, nc, GDN_CHUNK, H).transpose(1, 0, 3, 2)


def _gated_delta_chunked(q, k, v, g, beta, S0):
    N, T, H, Dk = q.shape
    Dv = v.shape[-1]
    C = GDN_CHUNK
    nc = -(-T // C)
    qc, kc, vc = _to_chunks4(q, nc), _to_chunks4(k, nc), _to_chunks4(v, nc)
    gc, bc = _to_chunks3(g, nc), _to_chunks3(beta, nc)
    gcum = jnp.cumsum(gc, axis=-1)
    tril = jnp.tril(jnp.ones((C, C), dtype=bool))
    eye = jnp.eye(C, dtype=F32)
    decay = jnp.exp(jnp.where(tril, gcum[..., :, None] - gcum[..., None, :], -jnp.inf))
    kb = kc * bc[..., None]
    vb = vc * bc[..., None]
    lower = jnp.einsum('cnhik,cnhjk->cnhij', kb, kc) * decay * (1.0 - eye)
    tri = lower + eye
    u = lax.linalg.triangular_solve(tri, vb, left_side=True, lower=True)
    w = lax.linalg.triangular_solve(tri, kb * jnp.exp(gcum)[..., None], left_side=True, lower=True)

    def step(S, inp):
        q_i, k_i, u_i, w_i, g_i, dec_i = inp
        v_new = u_i - jnp.einsum('nhck,nhkv->nhcv', w_i, S)
        attn = jnp.einsum('nhik,nhjk->nhij', q_i, k_i) * dec_i
        o_i = (jnp.einsum('nhck,nhkv->nhcv', q_i * jnp.exp(g_i)[..., None], S)
               + jnp.einsum('nhij,nhjv->nhiv', attn, v_new))
        g_last = g_i[..., -1]
        k_dec = k_i * jnp.exp(g_last[..., None] - g_i)[..., None]
        S = S * jnp.exp(g_last)[..., None, None] + jnp.einsum('nhck,nhcv->nhkv', k_dec, v_new)
        return S, o_i

    S, o = lax.scan(step, S0.astype(F32), (qc, kc, u, w, gcum, decay))
    o = o.transpose(1, 0, 3, 2, 4).reshape(N, nc * C, H, Dv)[:, :T]
    return o, S


def _gdn_branch(qkv, a_col, b_col, conv_buf, S0, conv_w, A_log, dt_bias, norm_g):
    N, T, _ = qkv.shape
    c, new_buf = _causal_conv(qkv, conv_buf, conv_w)
    c = _silu(c)
    q, k, v = jnp.split(c, 3, axis=-1)
    q = _l2norm(q.reshape(N, T, H_A, HEAD_DIM)) * (HEAD_DIM ** -0.5)
    k = _l2norm(k.reshape(N, T, H_A, HEAD_DIM))
    v = v.reshape(N, T, H_A, HEAD_DIM)
    g = -jnp.exp(A_log.astype(F32)) * jax.nn.softplus(a_col.astype(F32) + dt_bias.astype(F32))
    beta = jax.nn.sigmoid(b_col.astype(F32))
    o, S = _gated_delta_chunked(q, k, v, g, beta, S0)
    o = o * lax.rsqrt(jnp.mean(o * o, axis=-1, keepdims=True) + RMS_EPS) * norm_g.astype(F32)
    return o.reshape(N, T, W_BR), new_buf.astype(conv_buf.dtype), S.astype(S0.dtype)


def _rwkv7_scan(r, w, k, v, kk, a, S0):
    def step(S, inp):
        r_t, w_t, k_t, v_t, kk_t, a_t = inp
        s_kk = jnp.einsum('nhvk,nhk->nhv', S, -kk_t)
        S = (S * w_t[:, :, None, :] + s_kk[..., :, None] * (kk_t * a_t)[:, :, None, :]
             + v_t[..., :, None] * k_t[:, :, None, :])
        return S, jnp.einsum('nhvk,nhk->nhv', S, r_t)

    xs = tuple(t.transpose(1, 0, 2, 3) for t in (r, w, k, v, kk, a))
    S, y = lax.scan(step, S0.astype(F32), xs)
    return y.transpose(1, 0, 2, 3), S


def _rwkv7_branch(p, shift_buf, S0, mu, w0, w_up, a0, a_up, xi, alpha, rho, gn_g, gn_b):
    N, T, _ = p.shape
    pf = p.astype(F32)
    prev = jnp.concatenate([shift_buf.astype(F32)[:, None, :], pf[:, :-1]], axis=1)
    ps = pf + (prev - pf) * mu.astype(F32)
    r, k, v, wl, al = jnp.split(ps, [W_BR, 2 * W_BR, 3 * W_BR, 3 * W_BR + LORA_B], axis=-1)
    d = w0.astype(F32) + jnp.tanh(wl) @ w_up.astype(F32)
    decay = jnp.exp(-RWKV_DECAY_SCALE * jax.nn.sigmoid(d))
    a = jax.nn.sigmoid(a0.astype(F32) + al @ a_up.astype(F32))

    def heads(t):
        return t.reshape(N, T, H_B, HEAD_DIM)

    kk = _l2norm(heads(k * xi.astype(F32)))
    k = k * (1.0 + (a - 1.0) * alpha.astype(F32))
    y, S = _rwkv7_scan(heads(r), heads(decay), heads(k), heads(v), kk, heads(a), S0)
    mu_y = jnp.mean(y, axis=-1, keepdims=True)
    var = jnp.mean(jnp.square(y - mu_y), axis=-1, keepdims=True)
    yn = ((y - mu_y) * lax.rsqrt(var + GN_EPS)).reshape(N, T, W_BR) * gn_g.astype(F32) + gn_b.astype(F32)
    bonus = jnp.sum(heads(r * k * rho.astype(F32)), axis=-1, keepdims=True) * heads(v)
    return yn + bonus.reshape(N, T, W_BR), p[:, -1].astype(shift_buf.dtype), S.astype(S0.dtype)


def _fox_block(qb, cqb, qpos, k, v, ck, kpos):
    s = jnp.einsum('nqhd,nshd->nhqs', qb, k) * (HEAD_DIM ** -0.5)
    s = s + jnp.swapaxes(cqb, 1, 2)[..., :, None] - jnp.swapaxes(ck, 1, 2)[..., None, :]
    s = jnp.where((kpos[None, :] <= qpos[:, None])[None, None], s, NEG_INF)
    return jnp.einsum('nhqs,nshd->nqhd', jax.nn.softmax(s, axis=-1), v)


def _fox_prompt(q, k, v, c):
    N, T, H, D = q.shape
    nb = T // Q_BLOCK
    pos = jnp.arange(T)
    qb = q.reshape(N, nb, Q_BLOCK, H, D).transpose(1, 0, 2, 3, 4)
    cb = c.reshape(N, nb, Q_BLOCK, H).transpose(1, 0, 2, 3)
    pb = pos.reshape(nb, Q_BLOCK)
    out = lax.map(lambda t: _fox_block(t[0], t[1], t[2], k, v, c, pos), (qb, cb, pb))
    return out.transpose(1, 0, 2, 3, 4).reshape(N, T, H, D)


def _fox_branch(qkv, f_col, b_f, start_pos, past):
    N, T, _ = qkv.shape
    q, k, v = [t.reshape(N, T, H_C, HEAD_DIM).astype(F32) for t in jnp.split(qkv, 3, axis=-1)]
    logf = jax.nn.log_sigmoid(f_col.astype(F32) + b_f.astype(F32))
    if past is None:
        c = jnp.cumsum(logf, axis=1)
        o = _fox_prompt(q, k, v, c)
    else:
        k_past, v_past, lf_past = past
        k_all = jnp.concatenate([k_past.astype(F32), k], axis=1)
        v_all = jnp.concatenate([v_past.astype(F32), v], axis=1)
        c_all = jnp.cumsum(jnp.concatenate([lf_past.astype(F32), logf], axis=1), axis=1)
        kpos = jnp.arange(k_all.shape[1])
        qpos = start_pos + jnp.arange(T)
        o = _fox_block(q, c_all[:, -T:], qpos, k_all, v_all, c_all, kpos)
    return o.reshape(N, T, W_BR), k, v, logf


def _pool_branch(u, buf, start_pos, pool_w, pool_scale):
    N, T, _ = u.shape
    ext = jnp.concatenate([buf.astype(F32), u.astype(F32)], axis=1)
    cs = jnp.concatenate([jnp.zeros_like(ext[:, :1]), jnp.cumsum(ext, axis=1)], axis=1)
    end = cs[:, POOL_BUF + 1:]
    pos = start_pos + jnp.arange(T)
    groups = []
    for gi, wdw in enumerate(POOL_WINDOWS):
        lo, hi = gi * POOL_GW, (gi + 1) * POOL_GW
        s = end[..., lo:hi] - cs[:, POOL_BUF + 1 - wdw:POOL_BUF + 1 - wdw + T, lo:hi]
        cnt = jnp.minimum(wdw, pos + 1).astype(F32)[None, :, None]
        groups.append(s / cnt)
    pooled = jnp.concatenate(groups, axis=-1) - u.astype(F32)
    mixed = jnp.einsum('ntgc,gcd->ntgd', pooled.reshape(N, T, POOL_GROUPS, POOL_GW), pool_w.astype(F32))
    return mixed.reshape(N, T, W_BR) * pool_scale.astype(F32), ext[:, -POOL_BUF:].astype(buf.dtype)


def _layer(x, start_pos, st, past, P):
    S_A, conv_buf, S_B, shift_buf, pool_buf = st
    proj = jnp.einsum('ntd,de->nte', x, P['w_in'])
    qkv_A, a_A, b_A, z_A, p_B, z_B, qkv_C, f_C, z_C, u_D, z_D = _split_proj(proj)
    o_A, conv_n, S_A_n = _gdn_branch(qkv_A, a_A, b_A, conv_buf, S_A, P['conv_A'], P['A_log'],
                                     P['dt_bias'], P['norm_A'])
    o_B, shift_n, S_B_n = _rwkv7_branch(p_B, shift_buf, S_B, P['mu_B'], P['w0_B'], P['w_up_B'], P['a0_B'],
                                        P['a_up_B'], P['xi_B'], P['alpha_B'], P['rho_B'], P['gn_g_B'], P['gn_b_B'])
    o_C, k_n, v_n, lf_n = _fox_branch(qkv_C, f_C, P['b_f_C'], start_pos, past)
    o_D, pool_n = _pool_branch(u_D, pool_buf, start_pos, P['pool_w_D'], P['pool_scale_D'])
    gated = jnp.concatenate([o_A * _silu(z_A.astype(F32)), o_B * _silu(z_B.astype(F32)),
                             o_C * _silu(z_C.astype(F32)), o_D * _silu(z_D.astype(F32))], axis=-1).astype(x.dtype)
    out = jnp.einsum('nte,ed->ntd', gated, P['w_out'])
    y = _layernorm(ALPHA_DN * x + out, P['ln_g'], P['ln_b'])
    dt = x.dtype
    new = (S_A_n, conv_n, S_B_n, shift_n, k_n.astype(dt), v_n.astype(dt), lf_n.astype(dt), pool_n)
    return y, new


def setup_inputs(seed: int = 0) -> dict:
    key = jax.random.key(seed)
    ks = jax.random.split(key, 40)
    n_pages = PAST_LEN // PAGE_SIZE
    n_phys = (DEC_BATCH * n_pages * 5) // 4

    def nrm(k, shape, s):
        return jax.random.normal(k, shape, F32) * s

    x_prompt = nrm(ks[0], (BATCH, SEQ, D_MODEL), 1.0)
    x_sample = nrm(ks[1], (DEC_BATCH, DEC_SEQ, D_MODEL), 1.0)
    state_A_S = nrm(ks[2], (DEPTH, DEC_BATCH, H_A, HEAD_DIM, HEAD_DIM), 0.5)
    state_A_conv = nrm(ks[3], (DEPTH, DEC_BATCH, CONV_W - 1, 3 * W_BR), 1.0)
    state_B_S = nrm(ks[4], (DEPTH, DEC_BATCH, H_B, HEAD_DIM, HEAD_DIM), 0.5)
    state_B_shift = nrm(ks[5], (DEPTH, DEC_BATCH, D_B_IN), 1.0)
    cache_C_k = nrm(ks[6], (DEPTH, n_phys, PAGE_SIZE, H_C, HEAD_DIM), 1.0)
    cache_C_v = nrm(ks[7], (DEPTH, n_phys, PAGE_SIZE, H_C, HEAD_DIM), 1.0)
    cache_C_logf = jax.nn.log_sigmoid(nrm(ks[8], (DEPTH, n_phys, PAGE_SIZE, H_C), 1.0) + 2.0)
    state_D_buf = nrm(ks[9], (DEPTH, DEC_BATCH, POOL_BUF, W_BR), 1.0)
    page_table = jax.random.permutation(ks[10], n_phys)[:DEC_BATCH * n_pages].reshape(
        DEC_BATCH, n_pages).astype(jnp.int32)

    w_in = nrm(ks[11], (DEPTH, D_MODEL, D_IN), D_MODEL ** -0.5)
    conv_A = nrm(ks[12], (DEPTH, CONV_W, 3 * W_BR), CONV_W ** -0.5)
    A_log = jnp.log(jax.random.uniform(ks[13], (DEPTH, H_A), F32, 1.0, 16.0))
    dt0 = jnp.exp(jax.random.uniform(ks[14], (DEPTH, H_A), F32, math.log(1e-3), math.log(1e-1)))
    dt_bias = dt0 + jnp.log(-jnp.expm1(-dt0))
    norm_A = 1.0 + nrm(ks[15], (DEPTH, HEAD_DIM), 0.05)
    mu_B = jax.random.uniform(ks[16], (DEPTH, D_B_IN), F32)
    w0_B = -0.5 + nrm(ks[17], (DEPTH, W_BR), 0.5)
    w_up_B = nrm(ks[18], (DEPTH, LORA_B, W_BR), LORA_B ** -0.5)
    a0_B = nrm(ks[19], (DEPTH, W_BR), 0.1)
    a_up_B = nrm(ks[20], (DEPTH, LORA_B, W_BR), LORA_B ** -0.5)
    xi_B = 0.85 + nrm(ks[21], (DEPTH, W_BR), 0.05)
    alpha_B = 1.0 + nrm(ks[22], (DEPTH, W_BR), 0.05)
    rho_B = nrm(ks[23], (DEPTH, W_BR), 0.1)
    gn_g_B = 1.0 + nrm(ks[24], (DEPTH, W_BR), 0.05)
    gn_b_B = nrm(ks[25], (DEPTH, W_BR), 0.02)
    b_f_C = 2.0 + nrm(ks[26], (DEPTH, H_C), 0.1)
    pool_w_D = nrm(ks[27], (DEPTH, POOL_GROUPS, POOL_GW, POOL_GW), POOL_GW ** -0.5)
    pool_scale_D = 0.5 + nrm(ks[28], (DEPTH, W_BR), 0.1)
    w_out = nrm(ks[29], (DEPTH, D_MIX, D_MODEL), D_MIX ** -0.5 * BETA_DN)
    ln_g = 1.0 + nrm(ks[30], (DEPTH, D_MODEL), 0.05)
    ln_b = nrm(ks[31], (DEPTH, D_MODEL), 0.02)
    return {'x_prompt': x_prompt, 'x_sample': x_sample, 'state_A_S': state_A_S, 'state_A_conv': state_A_conv,
            'state_B_S': state_B_S, 'state_B_shift': state_B_shift, 'cache_C_k': cache_C_k,
            'cache_C_v': cache_C_v, 'cache_C_logf': cache_C_logf, 'state_D_buf': state_D_buf,
            'page_table': page_table, 'w_in': w_in, 'conv_A': conv_A, 'A_log': A_log, 'dt_bias': dt_bias,
            'norm_A': norm_A, 'mu_B': mu_B, 'w0_B': w0_B, 'w_up_B': w_up_B, 'a0_B': a0_B, 'a_up_B': a_up_B,
            'xi_B': xi_B, 'alpha_B': alpha_B, 'rho_B': rho_B, 'gn_g_B': gn_g_B, 'gn_b_B': gn_b_B,
            'b_f_C': b_f_C, 'pool_w_D': pool_w_D, 'pool_scale_D': pool_scale_D, 'w_out': w_out,
            'ln_g': ln_g, 'ln_b': ln_b}


def reference(x_prompt, x_sample, state_A_S, state_A_conv, state_B_S, state_B_shift, cache_C_k, cache_C_v,
              cache_C_logf, state_D_buf, page_table, w_in, conv_A, A_log, dt_bias, norm_A, mu_B, w0_B, w_up_B,
              a0_B, a_up_B, xi_B, alpha_B, rho_B, gn_g_B, gn_b_B, b_f_C, pool_w_D, pool_scale_D, w_out,
              ln_g, ln_b):
    dt = x_prompt.dtype
    nb = x_prompt.shape[0]
    nd = page_table.shape[0]
    past_len = page_table.shape[1] * PAGE_SIZE
    y_prompt, y_sample = x_prompt, x_sample
    prompt_new, sample_new = [], []
    for l in range(DEPTH):
        P = {'w_in': w_in[l], 'conv_A': conv_A[l], 'A_log': A_log[l], 'dt_bias': dt_bias[l], 'norm_A': norm_A[l],
             'mu_B': mu_B[l], 'w0_B': w0_B[l], 'w_up_B': w_up_B[l], 'a0_B': a0_B[l], 'a_up_B': a_up_B[l],
             'xi_B': xi_B[l], 'alpha_B': alpha_B[l], 'rho_B': rho_B[l], 'gn_g_B': gn_g_B[l], 'gn_b_B': gn_b_B[l],
             'b_f_C': b_f_C[l], 'pool_w_D': pool_w_D[l], 'pool_scale_D': pool_scale_D[l], 'w_out': w_out[l],
             'ln_g': ln_g[l], 'ln_b': ln_b[l]}
        st_p = (jnp.zeros((nb, H_A, HEAD_DIM, HEAD_DIM), dt), jnp.zeros((nb, CONV_W - 1, 3 * W_BR), dt),
                jnp.zeros((nb, H_B, HEAD_DIM, HEAD_DIM), dt), jnp.zeros((nb, D_B_IN), dt),
                jnp.zeros((nb, POOL_BUF, W_BR), dt))
        y_prompt, new_p = _layer(y_prompt, 0, st_p, None, P)
        past = (cache_C_k[l][page_table].reshape(nd, past_len, H_C, HEAD_DIM),
                cache_C_v[l][page_table].reshape(nd, past_len, H_C, HEAD_DIM),
                cache_C_logf[l][page_table].reshape(nd, past_len, H_C))
        st_s = (state_A_S[l], state_A_conv[l], state_B_S[l], state_B_shift[l], state_D_buf[l])
        y_sample, new_s = _layer(y_sample, past_len, st_s, past, P)
        prompt_new.append(new_p)
        sample_new.append(new_s)
    p_A_S, p_A_conv, p_B_S, p_B_shift, p_C_k, p_C_v, p_C_logf, p_D_buf = [
        jnp.stack([n[i] for n in prompt_new]) for i in range(8)]
    s_A_S, s_A_conv, s_B_S, s_B_shift, s_C_k, s_C_v, s_C_logf, s_D_buf = [
        jnp.stack([n[i] for n in sample_new]) for i in range(8)]
    return (y_prompt, y_sample, p_A_S, p_A_conv, p_B_S, p_B_shift, p_C_k, p_C_v, p_C_logf, p_D_buf,
            s_A_S, s_A_conv, s_B_S, s_B_shift, s_C_k, s_C_v, s_C_logf, s_D_buf)
```

```python
import functools
import math

import jax
import jax.numpy as jnp
from jax import lax
from jax.experimental import pallas as pl
from jax.experimental.pallas import tpu as pltpu

F32 = jnp.float32
BF16 = jnp.bfloat16
HI = lax.Precision.HIGHEST

D_MODEL = 2048
W_BR = 512
HEAD_DIM = 64
N_HEADS = W_BR // HEAD_DIM
CONV_W = 4
CHUNK = 64
LORA_B = 32
POOL_WINDOWS = (2, 4, 8, 16)
POOL_GW = W_BR // len(POOL_WINDOWS)
POOL_BUF = max(POOL_WINDOWS) - 1
PAGE_SIZE = 128
DEPTH = 2
ALPHA_DN = (2.0 * DEPTH) ** 0.25
LN_EPS = 1e-5
GN_EPS = 64e-5
RMS_EPS = 1e-6
L2_EPS = 1e-6
RWKV_DECAY_SCALE = math.exp(-0.5)
NEG_INF = -1e30

C_QKV_A, C_RKV_B, C_QKV_C = 0, 1536, 3072
C_Z_A, C_Z_B, C_Z_C, C_U_D, C_Z_D = 4608, 5120, 5632, 6144, 6656
N_MAIN = 7168
S_A, S_B, S_WL, S_AL, S_F = 0, 8, 16, 48, 80
N_SMALL = 128
O_QKV_A, O_A_A, O_Z_A, O_P_B, O_WL, O_Z_B, O_QKV_C, O_F_C, O_Z_C, O_U_D, O_Z_D, D_IN = (
    0, 1536, 1552, 2064, 3600, 3664, 4176, 5712, 5720, 6232, 6744, 7256)

VMEM_LIMIT = 48 * 1024 * 1024


def _cparams(sem):
    return pltpu.CompilerParams(dimension_semantics=sem, vmem_limit_bytes=VMEM_LIMIT)


def _mm(a, b, prec=None):
    return jnp.dot(a, b, preferred_element_type=F32, precision=prec)


def _mm_nt(a, b, prec=None):
    return lax.dot_general(a, b, (((1,), (1,)), ((), ())), preferred_element_type=F32, precision=prec)


def _mm_tn(a, b, prec=None):
    return lax.dot_general(a, b, (((0,), (0,)), ((), ())), preferred_element_type=F32, precision=prec)


def _split3(x):
    hi = x.astype(BF16)
    r1 = x - hi.astype(F32)
    mid = r1.astype(BF16)
    lo = (r1 - mid.astype(F32)).astype(BF16)
    return hi, mid, lo


def _dot3_l(x, b_bf16):
    hi, mid, lo = _split3(x)
    return _mm(hi, b_bf16) + _mm(mid, b_bf16) + _mm(lo, b_bf16)


def _dot3_r(a_bf16, x):
    hi, mid, lo = _split3(x)
    return _mm(a_bf16, hi) + _mm(a_bf16, mid) + _mm(a_bf16, lo)


def _sigmoid(x):
    return 1.0 / (1.0 + jnp.exp(-x))


def _silu(x):
    return x * _sigmoid(x)


def _softplus(x):
    return jnp.maximum(x, 0.0) + jnp.log1p(jnp.exp(-jnp.abs(x)))


def _log_sigmoid(x):
    return -_softplus(-x)


def _iota2(shape, dim):
    return lax.broadcasted_iota(jnp.int32, shape, dim)


def _head_block_diag():
    r = _iota2((W_BR, W_BR), 0) // HEAD_DIM
    c = _iota2((W_BR, W_BR), 1) // HEAD_DIM
    return jnp.where(r == c, 1.0, 0.0).astype(BF16)


def _chunk_tril(n):
    r = _iota2((n, n), 0)
    c = _iota2((n, n), 1)
    return jnp.where((r >= c) & (r // CHUNK == c // CHUNK), 1.0, 0.0).astype(BF16)


def _tri_inv(a, eye, blk):
    ad = a * blk
    ao = a - ad
    a2 = _mm(ad, ad, HI)
    a4 = _mm(a2, a2, HI)
    a8 = _mm(a4, a4, HI)
    p = eye - ad
    p = p + _mm(p, a2, HI)
    p = p + _mm(p, a4, HI)
    p = p + _mm(p, a8, HI)
    n = _mm(p, ao, HI)
    n2 = _mm(n, n, HI)
    m = eye - n
    m = m + _mm(m, n2, HI)
    return _mm(m, p, HI)


def _chunk_masks():
    r = _iota2((CHUNK, CHUNK), 0)
    c = _iota2((CHUNK, CHUNK), 1)
    eye = jnp.where(r == c, 1.0, 0.0).astype(F32)
    strict = jnp.where(r > c, 1.0, 0.0).astype(F32)
    incl = jnp.where(r >= c, 1.0, 0.0).astype(F32)
    blk = jnp.where(r // 16 == c // 16, 1.0, 0.0).astype(F32)
    return eye, strict, incl, blk


def _proj_kernel(x_ref, w_ref, ws_ref, main_ref, small_ref, xb_ref):
    @pl.when(pl.program_id(1) == 0)
    def _():
        xb = x_ref[...].astype(BF16)
        xb_ref[...] = xb
        small_ref[...] = _mm(xb, ws_ref[...])

    main_ref[...] = _mm(xb_ref[...], w_ref[...])


def _proj(x2d, w_main, w_small, tm, tn=1024):
    m = x2d.shape[0]
    return pl.pallas_call(
        _proj_kernel,
        grid=(m // tm, N_MAIN // tn),
        in_specs=[pl.BlockSpec((tm, D_MODEL), lambda i, j: (i, 0)),
                  pl.BlockSpec((D_MODEL, tn), lambda i, j: (0, j)),
                  pl.BlockSpec((D_MODEL, N_SMALL), lambda i, j: (0, 0))],
        out_specs=[pl.BlockSpec((tm, tn), lambda i, j: (i, j)),
                   pl.BlockSpec((tm, N_SMALL), lambda i, j: (i, 0))],
        out_shape=[jax.ShapeDtypeStruct((m, N_MAIN), F32), jax.ShapeDtypeStruct((m, N_SMALL), F32)],
        scratch_shapes=[pltpu.VMEM((tm, D_MODEL), BF16)],
        compiler_params=_cparams(("arbitrary", "arbitrary")),
        name="proj",
    )(x2d, w_main, w_small)


def _gates_kernel(small_ref, bias_ref, logf_ref, c_ref, ct_ref, carry_ref, *, tb):
    @pl.when(pl.program_id(1) == 0)
    def _():
        carry_ref[...] = jnp.zeros_like(carry_ref)

    logf = _log_sigmoid(small_ref[0] + bias_ref[...])
    r = _iota2((tb, tb), 0)
    c = _iota2((tb, tb), 1)
    tril = jnp.where(r >= c, 1.0, 0.0).astype(BF16)
    cum = _dot3_r(tril, logf) + carry_ref[0:1, :]
    carry_ref[...] = jnp.broadcast_to(cum[tb - 1:tb, :], carry_ref.shape)
    logf_ref[0] = logf
    c_ref[0] = cum
    ct_ref[0] = cum.T[S_F:S_F + N_HEADS, :]


def _gates(small3, bias_row, tb):
    n, t, _ = small3.shape
    return pl.pallas_call(
        functools.partial(_gates_kernel, tb=tb),
        grid=(n, t // tb),
        in_specs=[pl.BlockSpec((1, tb, N_SMALL), lambda b, j: (b, j, 0)),
                  pl.BlockSpec((1, N_SMALL), lambda b, j: (0, 0))],
        out_specs=[pl.BlockSpec((1, tb, N_SMALL), lambda b, j: (b, j, 0)),
                   pl.BlockSpec((1, tb, N_SMALL), lambda b, j: (b, j, 0)),
                   pl.BlockSpec((1, N_HEADS, tb), lambda b, j: (b, 0, j))],
        out_shape=[jax.ShapeDtypeStruct((n, t, N_SMALL), F32), jax.ShapeDtypeStruct((n, t, N_SMALL), F32),
                   jax.ShapeDtypeStruct((n, N_HEADS, t), F32)],
        scratch_shapes=[pltpu.VMEM((8, N_SMALL), F32)],
        compiler_params=_cparams(("arbitrary", "arbitrary")),
        name="fox_gates",
    )(small3, bias_row)


def _fox_kernel(q_ref, k_ref, v_ref, c_ref, ct_ref, o_ref, m_s, l_s, acc_s, *, tq):
    i = pl.program_id(1)
    j = pl.program_id(2)

    @pl.when(j == 0)
    def _():
        m_s[...] = jnp.full(m_s.shape, NEG_INF, F32)
        l_s[...] = jnp.zeros_like(l_s)
        acc_s[...] = jnp.zeros_like(acc_s)

    def step(masked):
        q = q_ref[0] * (HEAD_DIM ** -0.5)
        k = k_ref[0]
        v = v_ref[0]
        cq = c_ref[0]
        ct = ct_ref[0]
        if masked:
            keep = _iota2((tq, tq), 0) >= _iota2((tq, tq), 1)
        for h in range(N_HEADS):
            hs = slice(h * HEAD_DIM, (h + 1) * HEAD_DIM)
            s = _mm_nt(q[:, hs].astype(BF16), k[:, hs].astype(BF16))
            s = s + cq[:, S_F + h:S_F + h + 1] - ct[h:h + 1, :]
            if masked:
                s = jnp.where(keep, s, NEG_INF)
            m_old = m_s[h][:, 0:1]
            l_old = l_s[h][:, 0:1]
            m_new = jnp.maximum(m_old, jnp.max(s, axis=1, keepdims=True))
            alpha = jnp.exp(m_old - m_new)
            p = jnp.exp(s - m_new)
            l_new = alpha * l_old + jnp.sum(p, axis=1, keepdims=True)
            acc_s[:, hs] = alpha * acc_s[:, hs] + _mm(p.astype(BF16), v[:, hs].astype(BF16))
            m_s[h] = jnp.broadcast_to(m_new, (tq, 128))
            l_s[h] = jnp.broadcast_to(l_new, (tq, 128))

    @pl.when(j < i)
    def _():
        step(False)

    @pl.when(j == i)
    def _():
        step(True)
        for h in range(N_HEADS):
            hs = slice(h * HEAD_DIM, (h + 1) * HEAD_DIM)
            o_ref[0, :, hs] = acc_s[:, hs] / l_s[h][:, 0:1]


def _fox_prompt(main3, c3, ct3, tq):
    n, t, _ = main3.shape
    nb = t // tq
    qb, kb, vb = C_QKV_C // W_BR, C_QKV_C // W_BR + 1, C_QKV_C // W_BR + 2
    return pl.pallas_call(
        functools.partial(_fox_kernel, tq=tq),
        grid=(n, nb, nb),
        in_specs=[pl.BlockSpec((1, tq, W_BR), lambda b, i, j: (b, i, qb)),
                  pl.BlockSpec((1, tq, W_BR), lambda b, i, j: (b, jnp.minimum(i, j), kb)),
                  pl.BlockSpec((1, tq, W_BR), lambda b, i, j: (b, jnp.minimum(i, j), vb)),
                  pl.BlockSpec((1, tq, N_SMALL), lambda b, i, j: (b, i, 0)),
                  pl.BlockSpec((1, N_HEADS, tq), lambda b, i, j: (b, 0, jnp.minimum(i, j)))],
        out_specs=pl.BlockSpec((1, tq, W_BR), lambda b, i, j: (b, i, 0)),
        out_shape=jax.ShapeDtypeStruct((n, t, W_BR), F32),
        scratch_shapes=[pltpu.VMEM((N_HEADS, tq, 128), F32), pltpu.VMEM((N_HEADS, tq, 128), F32),
                        pltpu.VMEM((tq, W_BR), F32)],
        compiler_params=_cparams(("arbitrary", "arbitrary", "arbitrary")),
        name="fox_prompt",
    )(main3, main3, main3, c3, ct3)


def _gdn_kernel(qkv_ref, small_ref, convw_ref, gpar_ref, eg_ref, eb_ref, normg_ref,
                o_ref, s_out_ref, ext_s, q_s, k_s, v_s, b_s, g_s, o_s, st_s, *, tb):
    j = pl.program_id(1)

    @pl.when(j == 0)
    def _():
        ext_s[0:8, :] = jnp.zeros((8, 3 * W_BR), F32)
        st_s[...] = jnp.zeros_like(st_s)

    u = qkv_ref[0]
    ext_s[8:8 + tb, :] = u
    cw = convw_ref[...]
    c = (ext_s[5:5 + tb, :] * cw[0:1, :] + ext_s[6:6 + tb, :] * cw[1:2, :]
         + ext_s[7:7 + tb, :] * cw[2:3, :] + u * cw[3:4, :])
    ext_s[0:8, :] = u[tb - 8:tb, :]
    c = _silu(c)
    bd = _head_block_diag()
    q = c[:, 0:W_BR]
    k = c[:, W_BR:2 * W_BR]
    q_s[...] = q * lax.rsqrt(_dot3_l(q * q, bd) + L2_EPS) * (HEAD_DIM ** -0.5)
    k_s[...] = k * lax.rsqrt(_dot3_l(k * k, bd) + L2_EPS)
    v_s[...] = c[:, 2 * W_BR:3 * W_BR]

    sm = small_ref[0]
    gpar = gpar_ref[...]
    g = gpar[0:1, :] * _softplus(sm + gpar[1:2, :])
    beta = _sigmoid(sm)
    gcum = _dot3_r(_chunk_tril(tb), g)
    g_s[...] = _dot3_l(gcum, eg_ref[...])
    b_s[...] = _dot3_l(beta, eb_ref[...])

    eye, strict, incl, blk = _chunk_masks()
    ones_b = jnp.ones((CHUNK, CHUNK), BF16)

    def chunk(ci, carry):
        r0 = pl.multiple_of(ci * CHUNK, CHUNK)
        rows = pl.ds(r0, CHUNK)
        for h in range(N_HEADS):
            hs = slice(h * HEAD_DIM, (h + 1) * HEAD_DIM)
            qh = q_s[rows, hs]
            kh = k_s[rows, hs]
            vh = v_s[rows, hs]
            bh = b_s[rows, hs]
            gc = g_s[rows, hs]
            grow = _dot3_r(ones_b, gc * eye)
            dlog = gc - grow
            decay = jnp.exp(jnp.where(incl > 0, dlog, NEG_INF))
            kb = kh * bh
            vb = vh * bh
            a = _mm_nt(kb.astype(BF16), kh.astype(BF16)) * decay * strict
            tinv = _tri_inv(a, eye, blk)
            uu = _mm(tinv, vb, HI)
            ww = _mm(tinv, kb * jnp.exp(gc), HI)
            st = st_s[h]
            stb = st.astype(BF16)
            v_new = uu - _mm(ww.astype(BF16), stb)
            attn = _mm_nt(qh.astype(BF16), kh.astype(BF16)) * decay
            o_h = _mm((qh * jnp.exp(gc)).astype(BF16), stb) + _mm(attn.astype(BF16), v_new.astype(BF16))
            glast = gc[CHUNK - 1:CHUNK, :]
            k_dec = kh * jnp.exp(glast - gc)
            st_s[h] = st * jnp.exp(glast) + _mm_tn(k_dec.astype(BF16), v_new.astype(BF16))
            o_s[rows, hs] = o_h
        return carry

    lax.fori_loop(0, tb // CHUNK, chunk, 0)

    o = o_s[...]
    ms = _dot3_l(o * o, bd) * (1.0 / HEAD_DIM)
    o_ref[0] = o * lax.rsqrt(ms + RMS_EPS) * normg_ref[...]

    @pl.when(j == pl.num_programs(1) - 1)
    def _():
        s_out_ref[0] = st_s[...]


def _gdn_prompt(main3, small3, conv_w, gpar, eg, eb, normg, tb):
    n, t, _ = main3.shape
    full = lambda shape: pl.BlockSpec(shape, lambda b, j: (0,) * len(shape))
    return pl.pallas_call(
        functools.partial(_gdn_kernel, tb=tb),
        grid=(n, t // tb),
        in_specs=[pl.BlockSpec((1, tb, 3 * W_BR), lambda b, j: (b, j, C_QKV_A // (3 * W_BR))),
                  pl.BlockSpec((1, tb, N_SMALL), lambda b, j: (b, j, 0)),
                  full((CONV_W, 3 * W_BR)), full((8, N_SMALL)), full((N_SMALL, W_BR)), full((N_SMALL, W_BR)),
                  full((1, W_BR))],
        out_specs=[pl.BlockSpec((1, tb, W_BR), lambda b, j: (b, j, 0)),
                   pl.BlockSpec((1, N_HEADS, HEAD_DIM, HEAD_DIM), lambda b, j: (b, 0, 0, 0))],
        out_shape=[jax.ShapeDtypeStruct((n, t, W_BR), F32),
                   jax.ShapeDtypeStruct((n, N_HEADS, HEAD_DIM, HEAD_DIM), F32)],
        scratch_shapes=[pltpu.VMEM((tb + 8, 3 * W_BR), F32)] + [pltpu.VMEM((tb, W_BR), F32)] * 6
                       + [pltpu.VMEM((N_HEADS, HEAD_DIM, HEAD_DIM), F32)],
        compiler_params=_cparams(("arbitrary", "arbitrary")),
        name="gdn_prompt",
    )(main3, small3, conv_w, gpar, eg, eb, normg)


def _rwkv_kernel(p_ref, small_ref, mu_ref, mus_ref, wup_ref, aup_ref, par_ref,
                 o_ref, s_out_ref, ext_s, exts_s, ah_s, bh_s, kh_s, rh_s, be_s, ke_s, v_s, wl_s, y_s, st_s, *, tb):
    j = pl.program_id(1)

    @pl.when(j == 0)
    def _():
        ext_s[0:8, :] = jnp.zeros((8, 3 * W_BR), F32)
        exts_s[0:8, :] = jnp.zeros((8, N_SMALL), F32)
        st_s[...] = jnp.zeros_like(st_s)

    p = p_ref[0]
    sm = small_ref[0]
    ext_s[8:8 + tb, :] = p
    exts_s[8:8 + tb, :] = sm
    prev = ext_s[7:7 + tb, :]
    prevs = exts_s[7:7 + tb, :]
    ext_s[0:8, :] = p[tb - 8:tb, :]
    exts_s[0:8, :] = sm[tb - 8:tb, :]
    ps = p + (prev - p) * mu_ref[...]
    pss = sm + (prevs - sm) * mus_ref[...]
    r = ps[:, 0:W_BR]
    k = ps[:, W_BR:2 * W_BR]
    v = ps[:, 2 * W_BR:3 * W_BR]
    par = par_ref[...]
    w0, a0, xi, alpha, rho, gn_g, gn_b = (par[i:i + 1, :] for i in range(7))
    d = w0 + _mm(jnp.tanh(pss), wup_ref[...], HI)
    logw = -RWKV_DECAY_SCALE * _sigmoid(d)
    a = _sigmoid(a0 + _mm(pss, aup_ref[...], HI))
    bd = _head_block_diag()
    kx = k * xi
    kk = kx * lax.rsqrt(_dot3_l(kx * kx, bd) + L2_EPS)
    k2 = k * (1.0 + (a - 1.0) * alpha)
    lc = _dot3_r(_chunk_tril(tb), logw)
    nb = -(a * kk)
    ah_s[...] = kk * jnp.exp(lc - logw)
    bh_s[...] = nb * jnp.exp(-lc)
    kh_s[...] = k2 * jnp.exp(-lc)
    rh_s[...] = r * jnp.exp(lc)
    v_s[...] = v
    rr = _iota2((tb, tb), 0)
    cc = _iota2((tb, tb), 1)
    last = jnp.where(cc == (rr // CHUNK) * CHUNK + (CHUNK - 1), 1.0, 0.0).astype(BF16)
    ll = _dot3_r(last, lc)
    be_s[...] = nb * jnp.exp(ll - lc)
    ke_s[...] = k2 * jnp.exp(ll - lc)
    wl_s[...] = jnp.exp(ll)

    eye, strict, incl, blk = _chunk_masks()

    def chunk(ci, carry):
        r0 = pl.multiple_of(ci * CHUNK, CHUNK)
        rows = pl.ds(r0, CHUNK)
        for h in range(N_HEADS):
            hs = slice(h * HEAD_DIM, (h + 1) * HEAD_DIM)
            ah = ah_s[rows, hs]
            bh = bh_s[rows, hs]
            kh = kh_s[rows, hs]
            rh = rh_s[rows, hs]
            vh = v_s[rows, hs]
            st = st_s[h]
            a_ab = _mm_nt(ah, bh, HI) * strict
            a_ak = _mm_nt(ah, kh, HI) * strict
            r_b = _mm_nt(rh, bh, HI) * incl
            r_k = _mm_nt(rh, kh, HI) * incl
            rhs = _mm_nt(ah, st, HI) + _mm(a_ak, vh, HI)
            tinv = _tri_inv(-a_ab, eye, blk)
            uu = _mm(tinv, rhs, HI)
            y_h = _mm_nt(rh, st, HI) + _mm(r_b, uu, HI) + _mm(r_k, vh, HI)
            wl = wl_s[pl.ds(r0, 1), hs]
            st_s[h] = st * wl + _mm_tn(uu, be_s[rows, hs], HI) + _mm_tn(vh, ke_s[rows, hs], HI)
            y_s[rows, hs] = y_h
        return carry

    lax.fori_loop(0, tb // CHUNK, chunk, 0)

    y = y_s[...]
    mean = _dot3_l(y, bd) * (1.0 / HEAD_DIM)
    yc = y - mean
    var = _dot3_l(yc * yc, bd) * (1.0 / HEAD_DIM)
    yn = yc * lax.rsqrt(var + GN_EPS) * gn_g + gn_b
    bonus = _dot3_l(r * k2 * rho, bd) * v
    o_ref[0] = yn + bonus

    @pl.when(j == pl.num_programs(1) - 1)
    def _():
        s_out_ref[0] = st_s[...]


def _rwkv_prompt(main3, small3, mu, mus, wup, aup, par, tb):
    n, t, _ = main3.shape
    full = lambda shape: pl.BlockSpec(shape, lambda b, j: (0,) * len(shape))
    return pl.pallas_call(
        functools.partial(_rwkv_kernel, tb=tb),
        grid=(n, t // tb),
        in_specs=[pl.BlockSpec((1, tb, 3 * W_BR), lambda b, j: (b, j, C_RKV_B // (3 * W_BR))),
                  pl.BlockSpec((1, tb, N_SMALL), lambda b, j: (b, j, 0)),
                  full((1, 3 * W_BR)), full((1, N_SMALL)), full((N_SMALL, W_BR)), full((N_SMALL, W_BR)),
                  full((8, W_BR))],
        out_specs=[pl.BlockSpec((1, tb, W_BR), lambda b, j: (b, j, 0)),
                   pl.BlockSpec((1, N_HEADS, HEAD_DIM, HEAD_DIM), lambda b, j: (b, 0, 0, 0))],
        out_shape=[jax.ShapeDtypeStruct((n, t, W_BR), F32),
                   jax.ShapeDtypeStruct((n, N_HEADS, HEAD_DIM, HEAD_DIM), F32)],
        scratch_shapes=[pltpu.VMEM((tb + 8, 3 * W_BR), F32), pltpu.VMEM((tb + 8, N_SMALL), F32)]
                       + [pltpu.VMEM((tb, W_BR), F32)] * 9
                       + [pltpu.VMEM((N_HEADS, HEAD_DIM, HEAD_DIM), F32)],
        compiler_params=_cparams(("arbitrary", "arbitrary")),
        name="rwkv_prompt",
    )(main3, small3, mu, mus, wup, aup, par)


def _pool_kernel(u_ref, w_ref, scale_ref, o_ref, ext_s, *, tb):
    j = pl.program_id(1)

    @pl.when(j == 0)
    def _():
        ext_s[0:16, :] = jnp.zeros((16, W_BR), F32)

    u = u_ref[0]
    ext_s[16:16 + tb, :] = u
    pos = j * tb + _iota2((tb, POOL_GW), 0)
    outs = []
    for gi, wdw in enumerate(POOL_WINDOWS):
        ls = slice(gi * POOL_GW, (gi + 1) * POOL_GW)
        s = u[:, ls]
        for sh in range(1, wdw):
            s = s + ext_s[16 - sh:16 - sh + tb, ls]
        cnt = jnp.minimum(wdw, pos + 1).astype(F32)
        pooled = s / cnt - u[:, ls]
        outs.append(_mm(pooled.astype(BF16), w_ref[gi].astype(BF16)))
    ext_s[0:16, :] = u[tb - 16:tb, :]
    o_ref[0] = jnp.concatenate(outs, axis=1) * scale_ref[...]


def _pool_prompt(main3, pool_w, scale, tb):
    n, t, _ = main3.shape
    return pl.pallas_call(
        functools.partial(_pool_kernel, tb=tb),
        grid=(n, t // tb),
        in_specs=[pl.BlockSpec((1, tb, W_BR), lambda b, j: (b, j, C_U_D // W_BR)),
                  pl.BlockSpec((len(POOL_WINDOWS), POOL_GW, POOL_GW), lambda b, j: (0, 0, 0)),
                  pl.BlockSpec((1, W_BR), lambda b, j: (0, 0))],
        out_specs=pl.BlockSpec((1, tb, W_BR), lambda b, j: (b, j, 0)),
        out_shape=jax.ShapeDtypeStruct((n, t, W_BR), F32),
        scratch_shapes=[pltpu.VMEM((tb + 16, W_BR), F32)],
        compiler_params=_cparams(("arbitrary", "arbitrary")),
        name="pool_prompt",
    )(main3, pool_w, scale)


def _out_kernel(oa_ref, ob_ref, oc_ref, od_ref, za_ref, zb_ref, zc_ref, zd_ref, x_ref, w_ref, g_ref, b_ref, y_ref):
    acc = ALPHA_DN * x_ref[...]
    for i, (o_r, z_r) in enumerate(((oa_ref, za_ref), (ob_ref, zb_ref), (oc_ref, zc_ref), (od_ref, zd_ref))):
        gated = (o_r[...] * _silu(z_r[...])).astype(BF16)
        acc = acc + _mm(gated, w_ref[i * W_BR:(i + 1) * W_BR, :])
    mu = jnp.mean(acc, axis=-1, keepdims=True)
    xc = acc - mu
    var = jnp.mean(xc * xc, axis=-1, keepdims=True)
    y_ref[...] = xc * lax.rsqrt(var + LN_EPS) * g_ref[...] + b_ref[...]


def _out_proj(o_a, o_b, o_c, o_d, main2, x2d, w_out, ln_g, ln_b, tm):
    m = x2d.shape[0]
    ospec = pl.BlockSpec((tm, W_BR), lambda i: (i, 0))
    zspec = lambda col: pl.BlockSpec((tm, W_BR), lambda i: (i, col // W_BR))
    return pl.pallas_call(
        _out_kernel,
        grid=(m // tm,),
        in_specs=[ospec, ospec, ospec, ospec, zspec(C_Z_A), zspec(C_Z_B), zspec(C_Z_C), zspec(C_Z_D),
                  pl.BlockSpec((tm, D_MODEL), lambda i: (i, 0)),
                  pl.BlockSpec((D_MODEL, D_MODEL), lambda i: (0, 0)),
                  pl.BlockSpec((1, D_MODEL), lambda i: (0, 0)), pl.BlockSpec((1, D_MODEL), lambda i: (0, 0))],
        out_specs=pl.BlockSpec((tm, D_MODEL), lambda i: (i, 0)),
        out_shape=jax.ShapeDtypeStruct((m, D_MODEL), F32),
        compiler_params=_cparams(("arbitrary",)),
        name="out_proj",
    )(o_a, o_b, o_c, o_d, main2, main2, main2, main2, x2d, w_out, ln_g, ln_b)


def _rows8(row, nrows=1):
    return jnp.where(_iota2((8, row.shape[1]), 0) < nrows, jnp.broadcast_to(row, (8, row.shape[1])), 0.0)


def _dec_kernel(main_ref, small_ref, sa_ref, conv_ref, sb_ref, shift_ref, shifts_ref, dbuf_ref,
                convw_ref, gpar_ref, normg_ref, mu_ref, mus_ref, wup_ref, aup_ref, par_ref, poolw_ref, pscale_ref,
                oa_ref, ob_ref, od_ref, sa_out, conv_out, sb_out, dbuf_out, ext_s, *, pos):
    row = main_ref[0]
    sm = small_ref[0]
    bd = _head_block_diag()

    u = row[:, C_QKV_A:C_QKV_A + 3 * W_BR]
    buf = conv_ref[0]
    cw = convw_ref[...]
    c = buf[0:1, :] * cw[0:1, :] + buf[1:2, :] * cw[1:2, :] + buf[2:3, :] * cw[2:3, :] + u * cw[3:4, :]
    conv_out[0, 0:2, :] = buf[1:3, :]
    conv_out[0, 2:3, :] = u
    c = _silu(c)
    q = c[:, 0:W_BR]
    k = c[:, W_BR:2 * W_BR]
    v = c[:, 2 * W_BR:3 * W_BR]
    q = q * lax.rsqrt(_dot3_l(_rows8(q * q), bd)[0:1, :] + L2_EPS) * (HEAD_DIM ** -0.5)
    k = k * lax.rsqrt(_dot3_l(_rows8(k * k), bd)[0:1, :] + L2_EPS)
    gpar = gpar_ref[...]
    g = gpar[0:1, :] * _softplus(sm + gpar[1:2, :])
    beta = _sigmoid(sm)
    o_parts = []
    for h in range(N_HEADS):
        hs = slice(h * HEAD_DIM, (h + 1) * HEAD_DIM)
        eg = jnp.exp(g[:, S_A + h:S_A + h + 1])
        bt = beta[:, S_B + h:S_B + h + 1]
        st = sa_ref[0, h]
        kh = k[:, hs]
        k8 = _rows8(kh)
        ks = _mm(k8, st, HI)[0:1, :]
        v_new = bt * (v[:, hs] - eg * ks)
        st_new = st * eg + _mm_tn(k8, _rows8(v_new), HI)
        sa_out[0, h] = st_new
        o_parts.append(_mm(_rows8(q[:, hs]), st_new, HI)[0:1, :])
    o = jnp.concatenate(o_parts, axis=1)
    ms = _dot3_l(_rows8(o * o), bd)[0:1, :] * (1.0 / HEAD_DIM)
    oa_ref[0] = o * lax.rsqrt(ms + RMS_EPS) * normg_ref[...]

    p = row[:, C_RKV_B:C_RKV_B + 3 * W_BR]
    ps = p + (shift_ref[0] - p) * mu_ref[...]
    pss = sm + (shifts_ref[0] - sm) * mus_ref[...]
    r = ps[:, 0:W_BR]
    k = ps[:, W_BR:2 * W_BR]
    v = ps[:, 2 * W_BR:3 * W_BR]
    par = par_ref[...]
    w0, a0, xi, alpha, rho, gn_g, gn_b = (par[i:i + 1, :] for i in range(7))
    d = w0 + _mm(_rows8(jnp.tanh(pss)), wup_ref[...], HI)[0:1, :]
    decay = jnp.exp(-RWKV_DECAY_SCALE * _sigmoid(d))
    a = _sigmoid(a0 + _mm(_rows8(pss), aup_ref[...], HI)[0:1, :])
    kx = k * xi
    kk = kx * lax.rsqrt(_dot3_l(_rows8(kx * kx), bd)[0:1, :] + L2_EPS)
    k2 = k * (1.0 + (a - 1.0) * alpha)
    y_parts = []
    for h in range(N_HEADS):
        hs = slice(h * HEAD_DIM, (h + 1) * HEAD_DIM)
        st = sb_ref[0, h]
        kkh = kk[:, hs]
        s_kk = _mm_nt(_rows8(-kkh), st, HI)[0:1, :]
        lhs = jnp.where(_iota2((8, HEAD_DIM), 0) == 0, jnp.broadcast_to(s_kk, (8, HEAD_DIM)),
                        jnp.where(_iota2((8, HEAD_DIM), 0) == 1, jnp.broadcast_to(v[:, hs], (8, HEAD_DIM)), 0.0))
        rhs = jnp.where(_iota2((8, HEAD_DIM), 0) == 0, jnp.broadcast_to(kkh * a[:, hs], (8, HEAD_DIM)),
                        jnp.where(_iota2((8, HEAD_DIM), 0) == 1, jnp.broadcast_to(k2[:, hs], (8, HEAD_DIM)), 0.0))
        st_new = st * decay[:, hs] + _mm_tn(lhs, rhs, HI)
        sb_out[0, h] = st_new
        y_parts.append(_mm_nt(_rows8(r[:, hs]), st_new, HI)[0:1, :])
    y = jnp.concatenate(y_parts, axis=1)
    mean = _dot3_l(_rows8(y), bd)[0:1, :] * (1.0 / HEAD_DIM)
    yc = y - mean
    var = _dot3_l(_rows8(yc * yc), bd)[0:1, :] * (1.0 / HEAD_DIM)
    yn = yc * lax.rsqrt(var + GN_EPS) * gn_g + gn_b
    bonus = _dot3_l(_rows8(r * k2 * rho), bd)[0:1, :] * v
    ob_ref[0] = yn + bonus

    ud = row[:, C_U_D:C_U_D + W_BR]
    ext_s[0:POOL_BUF, :] = dbuf_ref[0]
    ext_s[POOL_BUF:POOL_BUF + 1, :] = ud
    dbuf_out[0] = ext_s[1:POOL_BUF + 1, :]
    outs = []
    for gi, wdw in enumerate(POOL_WINDOWS):
        ls = slice(gi * POOL_GW, (gi + 1) * POOL_GW)
        s = jnp.sum(ext_s[POOL_BUF + 1 - wdw:POOL_BUF + 1, ls], axis=0, keepdims=True)
        pooled = s / float(min(wdw, pos + 1)) - ud[:, ls]
        outs.append(_mm(_rows8(pooled).astype(BF16), poolw_ref[gi].astype(BF16))[0:1, :])
    od_ref[0] = jnp.concatenate(outs, axis=1) * pscale_ref[...]


def _decode_mixers(main_s, small_s, st_a, conv_a, st_b, shift_rkv, shift_small, dbuf,
                   conv_w, gpar, normg, mu, mus, wup, aup, par, pool_w, pscale, pos):
    nd = main_s.shape[0]
    per_seq = lambda shape: pl.BlockSpec((1,) + shape, lambda b: (b,) + (0,) * len(shape))
    full = lambda shape: pl.BlockSpec(shape, lambda b: (0,) * len(shape))
    hh = (N_HEADS, HEAD_DIM, HEAD_DIM)
    return pl.pallas_call(
        functools.partial(_dec_kernel, pos=pos),
        grid=(nd,),
        in_specs=[per_seq((1, N_MAIN)), per_seq((1, N_SMALL)), per_seq(hh), per_seq((CONV_W - 1, 3 * W_BR)),
                  per_seq(hh), per_seq((1, 3 * W_BR)), per_seq((1, N_SMALL)), per_seq((POOL_BUF, W_BR)),
                  full((CONV_W, 3 * W_BR)), full((8, N_SMALL)), full((1, W_BR)),
                  full((1, 3 * W_BR)), full((1, N_SMALL)), full((N_SMALL, W_BR)), full((N_SMALL, W_BR)),
                  full((8, W_BR)), full((len(POOL_WINDOWS), POOL_GW, POOL_GW)), full((1, W_BR))],
        out_specs=[per_seq((1, W_BR)), per_seq((1, W_BR)), per_seq((1, W_BR)), per_seq(hh),
                   per_seq((CONV_W - 1, 3 * W_BR)), per_seq(hh), per_seq((POOL_BUF, W_BR))],
        out_shape=[jax.ShapeDtypeStruct((nd, 1, W_BR), F32)] * 3
                  + [jax.ShapeDtypeStruct((nd,) + hh, F32), jax.ShapeDtypeStruct((nd, CONV_W - 1, 3 * W_BR), F32),
                     jax.ShapeDtypeStruct((nd,) + hh, F32), jax.ShapeDtypeStruct((nd, POOL_BUF, W_BR), F32)],
        scratch_shapes=[pltpu.VMEM((16, W_BR), F32)],
        compiler_params=_cparams(("arbitrary",)),
        name="decode_mixers",
    )(main_s.reshape(nd, 1, N_MAIN), small_s.reshape(nd, 1, N_SMALL), st_a, conv_a, st_b,
      shift_rkv.reshape(nd, 1, 3 * W_BR), shift_small.reshape(nd, 1, N_SMALL), dbuf,
      conv_w, gpar, normg, mu, mus, wup, aup, par, pool_w, pscale)


def _dfox_kernel(pt_ref, qkv_ref, small_ref, bias_ref, kp_ref, vp_ref, lf_ref, e_ref, et_ref,
                 o_ref, lf_out, m_s, l_s, acc_s, r_s):
    j = pl.program_id(1)
    row = qkv_ref[0]
    q = row[:, 0:W_BR] * (HEAD_DIM ** -0.5)
    e = e_ref[...]
    et = et_ref[...]
    lane_r = _iota2((N_SMALL, N_SMALL), 0)
    lane_c = _iota2((N_SMALL, N_SMALL), 1)

    @pl.when(j == 0)
    def _():
        lf_new = _log_sigmoid(small_ref[0] + bias_ref[...])
        lf_out[0] = lf_new
        m_s[...] = _mm(_rows8(q * row[:, W_BR:2 * W_BR]).astype(BF16), e)[0:1, :]
        l_s[...] = jnp.ones_like(l_s)
        acc_s[...] = row[:, 2 * W_BR:3 * W_BR]
        sel = jnp.where(lane_r == lane_c + S_F, 1.0, 0.0).astype(BF16)
        r_s[...] = _dot3_l(_rows8(lf_new), sel)[0:1, :]

    kp = kp_ref[0, 0]
    vp = vp_ref[0, 0]
    lfx = lf_ref[0, 0]
    hsel = jnp.where(lane_r % N_HEADS == lane_c, 1.0, 0.0).astype(BF16)
    tok_c = lane_c // N_HEADS
    later = jnp.zeros((PAGE_SIZE, N_SMALL), F32)
    for r in range(PAGE_SIZE * N_HEADS // N_SMALL):
        later = later + jnp.where(r * (N_SMALL // N_HEADS) + tok_c > lane_r, lfx[r:r + 1, :], 0.0)
    suffix = _dot3_l(later, hsel)
    total = _dot3_l(_rows8(jnp.sum(lfx, axis=0, keepdims=True)), hsel)[0:1, :]
    r_old = r_s[...]
    s = _mm((kp * q).astype(BF16), e) + suffix + r_old
    m_old = m_s[...]
    m_new = jnp.maximum(m_old, jnp.max(s, axis=0, keepdims=True))
    alpha = jnp.exp(m_old - m_new)
    pexp = jnp.exp(s - m_new)
    l_s[...] = alpha * l_s[...] + jnp.sum(pexp, axis=0, keepdims=True)
    pfull = _mm(pexp.astype(BF16), et)
    afull = _mm(_rows8(alpha), et.astype(F32), HI)[0:1, :]
    acc_s[...] = acc_s[...] * afull + jnp.sum(pfull * vp, axis=0, keepdims=True)
    m_s[...] = m_new
    r_s[...] = r_old + total

    @pl.when(j == pl.num_programs(1) - 1)
    def _():
        lfull = _mm(_rows8(l_s[...]), et.astype(F32), HI)[0:1, :]
        o_ref[0] = acc_s[...] / lfull


def _fox_decode(page_table, qkv_s, small_s, bias_row, cache_k, cache_v, cache_lf, e_mat, et_mat):
    nd, n_pages = page_table.shape
    n_phys = cache_k.shape[0]
    kv = cache_k.reshape(n_phys, 1, PAGE_SIZE, W_BR)
    vv = cache_v.reshape(n_phys, 1, PAGE_SIZE, W_BR)
    lf_rows = PAGE_SIZE * N_HEADS // N_SMALL
    lf = cache_lf.reshape(n_phys, 1, lf_rows, N_SMALL)
    page = lambda b, j, pt: (pt[b, n_pages - 1 - j], 0, 0, 0)
    grid_spec = pltpu.PrefetchScalarGridSpec(
        num_scalar_prefetch=1,
        grid=(nd, n_pages),
        in_specs=[pl.BlockSpec((1, 1, 3 * W_BR), lambda b, j, pt: (b, 0, 0)),
                  pl.BlockSpec((1, 1, N_SMALL), lambda b, j, pt: (b, 0, 0)),
                  pl.BlockSpec((1, N_SMALL), lambda b, j, pt: (0, 0)),
                  pl.BlockSpec((1, 1, PAGE_SIZE, W_BR), page),
                  pl.BlockSpec((1, 1, PAGE_SIZE, W_BR), page),
                  pl.BlockSpec((1, 1, lf_rows, N_SMALL), page),
                  pl.BlockSpec((W_BR, N_SMALL), lambda b, j, pt: (0, 0)),
                  pl.BlockSpec((N_SMALL, W_BR), lambda b, j, pt: (0, 0))],
        out_specs=[pl.BlockSpec((1, 1, W_BR), lambda b, j, pt: (b, 0, 0)),
                   pl.BlockSpec((1, 1, N_SMALL), lambda b, j, pt: (b, 0, 0))],
        scratch_shapes=[pltpu.VMEM((1, N_SMALL), F32), pltpu.VMEM((1, N_SMALL), F32), pltpu.VMEM((1, W_BR), F32),
                        pltpu.VMEM((1, N_SMALL), F32)],
    )
    return pl.pallas_call(
        _dfox_kernel,
        grid_spec=grid_spec,
        out_shape=[jax.ShapeDtypeStruct((nd, 1, W_BR), F32), jax.ShapeDtypeStruct((nd, 1, N_SMALL), F32)],
        compiler_params=_cparams(("arbitrary", "arbitrary")),
        name="fox_decode",
    )(page_table, qkv_s.reshape(nd, 1, 3 * W_BR), small_s.reshape(nd, 1, N_SMALL), bias_row, kv, vv, lf, e_mat, et_mat)


def _small_row(pairs):
    row = jnp.zeros((N_SMALL,), F32)
    for off, vec in pairs:
        row = row.at[off:off + vec.shape[0]].set(vec.astype(F32))
    return row[None, :]


def _layer_params(l, w_in, conv_A, A_log, dt_bias, norm_A, mu_B, w0_B, w_up_B, a0_B, a_up_B, xi_B, alpha_B, rho_B,
                  gn_g_B, gn_b_B, b_f_C, pool_w_D, pool_scale_D, w_out, ln_g, ln_b):
    w = w_in[l]
    cols = lambda a, n: w[:, a:a + n]
    w_main = jnp.concatenate([cols(O_QKV_A, 1536), cols(O_P_B, 1536), cols(O_QKV_C, 1536), cols(O_Z_A, 512),
                              cols(O_Z_B, 512), cols(O_Z_C, 512), cols(O_U_D, 512), cols(O_Z_D, 512)],
                             axis=1).astype(BF16)
    w_small = jnp.concatenate([cols(O_A_A, 16), cols(O_WL, 64), cols(O_F_C, 8),
                               jnp.zeros((D_MODEL, N_SMALL - 88), F32)], axis=1).astype(BF16)
    gpar = jnp.concatenate([_small_row([(S_A, -jnp.exp(A_log[l].astype(F32)))]), _small_row([(S_A, dt_bias[l])]),
                            jnp.zeros((6, N_SMALL), F32)], axis=0)
    head_of_lane = jnp.arange(W_BR) // HEAD_DIM
    lane = jnp.arange(N_SMALL)[:, None]
    eg = (lane == head_of_lane[None, :] + S_A).astype(BF16)
    eb = (lane == head_of_lane[None, :] + S_B).astype(BF16)
    e_mat = (head_of_lane[:, None] == jnp.arange(N_SMALL)[None, :]).astype(BF16)
    et_mat = (jnp.arange(N_SMALL)[:, None] == head_of_lane[None, :]).astype(BF16)
    mu = mu_B[l].astype(F32)
    wup = jnp.zeros((N_SMALL, W_BR), F32).at[S_WL:S_WL + LORA_B].set(w_up_B[l].astype(F32))
    aup = jnp.zeros((N_SMALL, W_BR), F32).at[S_AL:S_AL + LORA_B].set(a_up_B[l].astype(F32))
    par = jnp.stack([w0_B[l], a0_B[l], xi_B[l], alpha_B[l], rho_B[l], gn_g_B[l], gn_b_B[l],
                     jnp.zeros((W_BR,), F32)]).astype(F32)
    return dict(
        w_main=w_main, w_small=w_small, conv_w=conv_A[l].astype(F32), gpar=gpar, eg=eg, eb=eb, e_mat=e_mat,
        et_mat=et_mat, normg=jnp.tile(norm_A[l].astype(F32), N_HEADS)[None, :],
        mu=mu[None, 0:3 * W_BR], mus=_small_row([(S_WL, mu[3 * W_BR:])]), wup=wup, aup=aup, par=par,
        fbias=_small_row([(S_F, b_f_C[l])]), pool_w=pool_w_D[l].astype(F32), pscale=pool_scale_D[l].astype(F32)[None, :],
        w_out=w_out[l].astype(BF16), ln_g=ln_g[l].astype(F32)[None, :], ln_b=ln_b[l].astype(F32)[None, :])


def _prompt_layer(x3, P, tb_rec, tq, tb_gate, tb_pool, tm_proj, tm_out):
    n, t, _ = x3.shape
    x2 = x3.reshape(n * t, D_MODEL)
    main2, small2 = _proj(x2, P['w_main'], P['w_small'], tm_proj)
    main3 = main2.reshape(n, t, N_MAIN)
    small3 = small2.reshape(n, t, N_SMALL)
    o_a, s_a = _gdn_prompt(main3, small3, P['conv_w'], P['gpar'], P['eg'], P['eb'], P['normg'], tb_rec)
    o_b, s_b = _rwkv_prompt(main3, small3, P['mu'], P['mus'], P['wup'], P['aup'], P['par'], tb_rec)
    logf3, c3, ct3 = _gates(small3, P['fbias'], tb_gate)
    o_c = _fox_prompt(main3, c3, ct3, tq)
    o_d = _pool_prompt(main3, P['pool_w'], P['pscale'], tb_pool)
    flat = lambda a: a.reshape(n * t, W_BR)
    y2 = _out_proj(flat(o_a), flat(o_b), flat(o_c), flat(o_d), main2, x2, P['w_out'], P['ln_g'], P['ln_b'], tm_out)
    new = (s_a,
           main3[:, t - (CONV_W - 1):, C_QKV_A:C_QKV_A + 3 * W_BR],
           s_b,
           jnp.concatenate([main3[:, t - 1, C_RKV_B:C_RKV_B + 3 * W_BR], small3[:, t - 1, S_WL:S_WL + 2 * LORA_B]], axis=-1),
           main3[:, :, C_QKV_C + W_BR:C_QKV_C + 2 * W_BR].reshape(n, t, N_HEADS, HEAD_DIM),
           main3[:, :, C_QKV_C + 2 * W_BR:C_QKV_C + 3 * W_BR].reshape(n, t, N_HEADS, HEAD_DIM),
           logf3[:, :, S_F:S_F + N_HEADS],
           main3[:, t - POOL_BUF:, C_U_D:C_U_D + W_BR])
    return y2.reshape(n, t, D_MODEL), new


def _sample_layer(x3, st, cache, page_table, P):
    nd = x3.shape[0]
    st_a, conv_a, st_b, shift_b, dbuf = st
    cache_k, cache_v, cache_lf = cache
    x2 = x3.reshape(nd, D_MODEL)
    main_s, small_s = _proj(x2, P['w_main'], P['w_small'], nd)
    shift_small = jnp.zeros((nd, N_SMALL), F32).at[:, S_WL:S_WL + 2 * LORA_B].set(shift_b[:, 3 * W_BR:].astype(F32))
    o_a, o_b, o_d, s_a, conv_n, s_b, dbuf_n = _decode_mixers(
        main_s, small_s, st_a, conv_a, st_b, shift_b[:, 0:3 * W_BR], shift_small, dbuf,
        P['conv_w'], P['gpar'], P['normg'], P['mu'], P['mus'], P['wup'], P['aup'], P['par'], P['pool_w'], P['pscale'],
        page_table.shape[1] * PAGE_SIZE)
    qkv_s = main_s[:, C_QKV_C:C_QKV_C + 3 * W_BR]
    o_c, lf_new = _fox_decode(page_table, qkv_s, small_s, P['fbias'], cache_k, cache_v, cache_lf,
                              P['e_mat'], P['et_mat'])
    flat = lambda a: a.reshape(nd, W_BR)
    y2 = _out_proj(flat(o_a), flat(o_b), flat(o_c), flat(o_d), main_s, x2, P['w_out'], P['ln_g'], P['ln_b'], nd)
    new = (s_a, conv_n, s_b,
           jnp.concatenate([main_s[:, C_RKV_B:C_RKV_B + 3 * W_BR], small_s[:, S_WL:S_WL + 2 * LORA_B]], axis=-1),
           qkv_s[:, W_BR:2 * W_BR].reshape(nd, 1, N_HEADS, HEAD_DIM),
           qkv_s[:, 2 * W_BR:3 * W_BR].reshape(nd, 1, N_HEADS, HEAD_DIM),
           lf_new[:, :, S_F:S_F + N_HEADS],
           dbuf_n)
    return y2.reshape(nd, 1, D_MODEL), new


def kernel(x_prompt, x_sample, state_A_S, state_A_conv, state_B_S, state_B_shift, cache_C_k, cache_C_v, cache_C_logf, state_D_buf, page_table, w_in, conv_A, A_log, dt_bias, norm_A, mu_B, w0_B, w_up_B, a0_B, a_up_B, xi_B, alpha_B, rho_B, gn_g_B, gn_b_B, b_f_C, pool_w_D, pool_scale_D, w_out, ln_g, ln_b):
    depth = w_in.shape[0]
    t = x_prompt.shape[1]
    tb_rec = min(256, t)
    tq = min(256, t)
    tb_gate = min(512, t)
    tb_pool = min(512, t)
    tm_proj = min(512, x_prompt.shape[0] * t)
    tm_out = min(256, x_prompt.shape[0] * t)
    y_p, y_s = x_prompt, x_sample
    prompt_new, sample_new = [], []
    for l in range(depth):
        P = _layer_params(l, w_in, conv_A, A_log, dt_bias, norm_A, mu_B, w0_B, w_up_B, a0_B, a_up_B, xi_B, alpha_B,
                          rho_B, gn_g_B, gn_b_B, b_f_C, pool_w_D, pool_scale_D, w_out, ln_g, ln_b)
        y_p, new_p = _prompt_layer(y_p, P, tb_rec, tq, tb_gate, tb_pool, tm_proj, tm_out)
        st = (state_A_S[l], state_A_conv[l], state_B_S[l], state_B_shift[l], state_D_buf[l])
        y_s, new_s = _sample_layer(y_s, st, (cache_C_k[l], cache_C_v[l], cache_C_logf[l]), page_table, P)
        prompt_new.append(new_p)
        sample_new.append(new_s)
    p_out = [jnp.stack([n[i] for n in prompt_new]) for i in range(8)]
    s_out = [jnp.stack([n[i] for n in sample_new]) for i in range(8)]
    p_a_s, p_a_conv, p_b_s, p_b_shift, p_c_k, p_c_v, p_c_logf, p_d_buf = p_out
    s_a_s, s_a_conv, s_b_s, s_b_shift, s_c_k, s_c_v, s_c_logf, s_d_buf = s_out
    return (y_p, y_s, p_a_s, p_a_conv, p_b_s, p_b_shift, p_c_k, p_c_v, p_c_logf, p_d_buf,
            s_a_s, s_a_conv, s_b_s, s_b_shift, s_c_k, s_c_v, s_c_logf, s_d_buf)
```

```python
import functools
import math

import jax
import jax.numpy as jnp
from jax import lax
from jax.experimental import pallas as pl
from jax.experimental.pallas import tpu as pltpu

F32 = jnp.float32
BF16 = jnp.bfloat16
HI = lax.Precision.HIGHEST

D_MODEL = 2048
W_BR = 512
HEAD_DIM = 64
N_HEADS = W_BR // HEAD_DIM
CONV_W = 4
CHUNK = 64
LORA_B = 32
POOL_WINDOWS = (2, 4, 8, 16)
POOL_GW = W_BR // len(POOL_WINDOWS)
POOL_BUF = max(POOL_WINDOWS) - 1
PAGE_SIZE = 128
LANES = 128
DEPTH = 2
ALPHA_DN = (2.0 * DEPTH) ** 0.25
LN_EPS = 1e-5
GN_EPS = 64e-5
RMS_EPS = 1e-6
L2_EPS = 1e-6
RWKV_DECAY_SCALE = math.exp(-0.5)
NEG_INF = -1e30

C_QKV_A, C_RKV_B, C_QKV_C = 0, 1536, 3072
C_Z_A, C_Z_B, C_Z_C, C_U_D, C_Z_D = 4608, 5120, 5632, 6144, 6656
N_MAIN = 7168
S_A, S_B, S_WL, S_AL, S_F = 0, 8, 16, 48, 80
N_SMALL = 128
O_QKV_A, O_A_A, O_Z_A, O_P_B, O_WL, O_Z_B, O_QKV_C, O_F_C, O_Z_C, O_U_D, O_Z_D, D_IN = (
    0, 1536, 1552, 2064, 3600, 3664, 4176, 5712, 5720, 6232, 6744, 7256)

VMEM_LIMIT = 48 * 1024 * 1024


def _cparams(sem):
    return pltpu.CompilerParams(dimension_semantics=sem, vmem_limit_bytes=VMEM_LIMIT)


def _mm(a, b, prec=None):
    return jnp.dot(a, b, preferred_element_type=F32, precision=prec)


def _mm_nt(a, b, prec=None):
    return lax.dot_general(a, b, (((1,), (1,)), ((), ())), preferred_element_type=F32, precision=prec)


def _mm_tn(a, b, prec=None):
    return lax.dot_general(a, b, (((0,), (0,)), ((), ())), preferred_element_type=F32, precision=prec)


def _split3(x):
    hi = x.astype(BF16)
    r1 = x - hi.astype(F32)
    mid = r1.astype(BF16)
    lo = (r1 - mid.astype(F32)).astype(BF16)
    return hi, mid, lo


def _dot3_l(x, b_bf16):
    hi, mid, lo = _split3(x)
    return _mm(hi, b_bf16) + _mm(mid, b_bf16) + _mm(lo, b_bf16)


def _dot3_r(a_bf16, x):
    hi, mid, lo = _split3(x)
    return _mm(a_bf16, hi) + _mm(a_bf16, mid) + _mm(a_bf16, lo)


def _sigmoid(x):
    return 1.0 / (1.0 + jnp.exp(-x))


def _silu(x):
    return x * _sigmoid(x)


def _softplus(x):
    return jnp.maximum(x, 0.0) + jnp.log1p(jnp.exp(-jnp.abs(x)))


def _log_sigmoid(x):
    return -_softplus(-x)


def _iota2(shape, dim):
    return lax.broadcasted_iota(jnp.int32, shape, dim)


def _head_block_diag():
    r = _iota2((W_BR, W_BR), 0) // HEAD_DIM
    c = _iota2((W_BR, W_BR), 1) // HEAD_DIM
    return jnp.where(r == c, 1.0, 0.0).astype(BF16)


def _chunk_tril(n):
    r = _iota2((n, n), 0)
    c = _iota2((n, n), 1)
    return jnp.where((r >= c) & (r // CHUNK == c // CHUNK), 1.0, 0.0).astype(BF16)


def _split2(x):
    hi = x.astype(BF16)
    return hi, (x - hi.astype(F32)).astype(BF16)


_DN_NN = (((1,), (0,)), ((), ()))
_DN_NT = (((1,), (1,)), ((), ()))
_DN_TN = (((0,), (0,)), ((), ()))


def _mm3s(a, b, dn=_DN_NN):
    dot = lambda x, y: lax.dot_general(x, y, dn, preferred_element_type=F32)
    return dot(a[0], b[0]) + dot(a[0], b[1]) + dot(a[1], b[0])


def _mm3(a, b, dn=_DN_NN):
    return _mm3s(_split2(a), _split2(b), dn)


def _tri_solve(a_list, rhs_list, eye, blk):
    each = lambda f, *ls: [f(*xs) for xs in zip(*ls)]
    split = lambda ls: each(_split2, ls)
    ad = each(lambda a: a * blk, a_list)
    ad_s = split(ad)
    a2_s = split(each(_mm3s, ad_s, ad_s))
    a4_s = split(each(_mm3s, a2_s, a2_s))
    a8_s = split(each(_mm3s, a4_s, a4_s))
    p = each(lambda x: eye - x, ad)
    for pw_s in (a2_s, a4_s, a8_s):
        p = each(lambda x, y: x + y, p, each(_mm3s, split(p), pw_s))
    p_s = split(p)
    n = each(_mm3s, p_s, split(each(lambda a, d: a - d, a_list, ad)))
    n_s = split(n)
    n2_s = split(each(_mm3s, n_s, n_s))
    m = each(lambda x: eye - x, n)
    m = each(lambda x, y: x + y, m, each(_mm3s, split(m), n2_s))
    return each(_mm3s, split(m), split(each(_mm3s, p_s, split(rhs_list))))


def _chunk_masks():
    r = _iota2((CHUNK, CHUNK), 0)
    c = _iota2((CHUNK, CHUNK), 1)
    eye = jnp.where(r == c, 1.0, 0.0).astype(F32)
    strict = jnp.where(r > c, 1.0, 0.0).astype(F32)
    incl = jnp.where(r >= c, 1.0, 0.0).astype(F32)
    blk = jnp.where(r // 16 == c // 16, 1.0, 0.0).astype(F32)
    return eye, strict, incl, blk


def _proj_kernel(x_ref, w_ref, ws_ref, main_ref, small_ref, xb_ref):
    @pl.when(pl.program_id(1) == 0)
    def _():
        xb = x_ref[...].astype(BF16)
        xb_ref[...] = xb
        small_ref[...] = _mm(xb, ws_ref[...])

    main_ref[...] = _mm(xb_ref[...], w_ref[...])


def _proj(x2d, w_main, w_small, tm, tn=1024):
    m = x2d.shape[0]
    return pl.pallas_call(
        _proj_kernel,
        grid=(m // tm, N_MAIN // tn),
        in_specs=[pl.BlockSpec((tm, D_MODEL), lambda i, j: (i, 0)),
                  pl.BlockSpec((D_MODEL, tn), lambda i, j: (0, j)),
                  pl.BlockSpec((D_MODEL, N_SMALL), lambda i, j: (0, 0))],
        out_specs=[pl.BlockSpec((tm, tn), lambda i, j: (i, j)),
                   pl.BlockSpec((tm, N_SMALL), lambda i, j: (i, 0))],
        out_shape=[jax.ShapeDtypeStruct((m, N_MAIN), F32), jax.ShapeDtypeStruct((m, N_SMALL), F32)],
        scratch_shapes=[pltpu.VMEM((tm, D_MODEL), BF16)],
        compiler_params=_cparams(("arbitrary", "arbitrary")),
        name="proj",
    )(x2d, w_main, w_small)


def _gates_kernel(small_ref, bias_ref, logf_ref, c_ref, ct_ref, carry_ref, *, tb):
    @pl.when(pl.program_id(1) == 0)
    def _():
        carry_ref[...] = jnp.zeros_like(carry_ref)

    logf = _log_sigmoid(small_ref[0] + bias_ref[...])
    r = _iota2((tb, tb), 0)
    c = _iota2((tb, tb), 1)
    tril = jnp.where(r >= c, 1.0, 0.0).astype(BF16)
    cum = _dot3_r(tril, logf) + carry_ref[0:1, :]
    carry_ref[...] = jnp.broadcast_to(cum[tb - 1:tb, :], carry_ref.shape)
    logf_ref[0] = logf
    c_ref[0] = cum
    ct_ref[0] = cum.T[S_F:S_F + N_HEADS, :]


def _gates(small3, bias_row, tb):
    n, t, _ = small3.shape
    return pl.pallas_call(
        functools.partial(_gates_kernel, tb=tb),
        grid=(n, t // tb),
        in_specs=[pl.BlockSpec((1, tb, N_SMALL), lambda b, j: (b, j, 0)),
                  pl.BlockSpec((1, N_SMALL), lambda b, j: (0, 0))],
        out_specs=[pl.BlockSpec((1, tb, N_SMALL), lambda b, j: (b, j, 0)),
                   pl.BlockSpec((1, tb, N_SMALL), lambda b, j: (b, j, 0)),
                   pl.BlockSpec((1, N_HEADS, tb), lambda b, j: (b, 0, j))],
        out_shape=[jax.ShapeDtypeStruct((n, t, N_SMALL), F32), jax.ShapeDtypeStruct((n, t, N_SMALL), F32),
                   jax.ShapeDtypeStruct((n, N_HEADS, t), F32)],
        scratch_shapes=[pltpu.VMEM((8, N_SMALL), F32)],
        compiler_params=_cparams(("arbitrary", "arbitrary")),
        name="fox_gates",
    )(small3, bias_row)


def _fox_kernel(q_ref, k_ref, v_ref, c_ref, ct_ref, o_ref, m_s, l_s, acc_s, cq_s, *, tq):
    i = pl.program_id(1)
    j = pl.program_id(2)
    n_pairs = W_BR // LANES
    lo_half = _iota2((tq, LANES), 1) < HEAD_DIM
    pair_lanes = [slice(pr * LANES, (pr + 1) * LANES) for pr in range(n_pairs)]

    @pl.when(j == 0)
    def _():
        m_s[...] = jnp.full(m_s.shape, NEG_INF, F32)
        l_s[...] = jnp.zeros_like(l_s)
        acc_s[...] = jnp.zeros_like(acc_s)
        cq = c_ref[0]
        for h in range(N_HEADS):
            cq_s[h] = jnp.broadcast_to(cq[:, S_F + h:S_F + h + 1], (tq, LANES))

    def step(masked):
        q = q_ref[0] * (HEAD_DIM ** -0.5)
        k = k_ref[0].astype(BF16)
        v = v_ref[0]
        ct = ct_ref[0]
        wide = lambda x: jnp.concatenate([x] * (tq // LANES), axis=1)
        if masked:
            keep = _iota2((tq, tq), 0) >= _iota2((tq, tq), 1)
        s_all = []
        for ps in pair_lanes:
            qp, kp = q[:, ps], k[:, ps]
            s_all.append(_mm_nt(jnp.where(lo_half, qp, 0.0).astype(BF16), kp))
            s_all.append(_mm_nt(jnp.where(lo_half, 0.0, qp).astype(BF16), kp))
        p_all, alpha_all = [], []
        for h in range(N_HEADS):
            s = s_all[h] + wide(cq_s[h]) - ct[h:h + 1, :]
            if masked:
                s = jnp.where(keep, s, NEG_INF)
            m_old = m_s[h]
            m_new = jnp.maximum(m_old, jnp.max(s, axis=1, keepdims=True))
            alpha = jnp.exp(m_old - m_new)
            p = jnp.exp(s - wide(m_new))
            l_s[h] = alpha * l_s[h] + jnp.sum(p, axis=1, keepdims=True)
            m_s[h] = m_new
            p_all.append(p.astype(BF16))
            alpha_all.append(alpha)
        for pr, ps in enumerate(pair_lanes):
            vp = v[:, ps]
            v_bd = jnp.concatenate([jnp.where(lo_half, vp, 0.0), jnp.where(lo_half, 0.0, vp)], axis=0).astype(BF16)
            p_pair = jnp.concatenate([p_all[2 * pr], p_all[2 * pr + 1]], axis=1)
            alpha_p = jnp.where(lo_half, alpha_all[2 * pr], alpha_all[2 * pr + 1])
            acc_s[:, ps] = alpha_p * acc_s[:, ps] + _mm(p_pair, v_bd)

    @pl.when(j < i)
    def _():
        step(False)

    @pl.when(j == i)
    def _():
        step(True)
        for pr, ps in enumerate(pair_lanes):
            o_ref[0, :, ps] = acc_s[:, ps] / jnp.where(lo_half, l_s[2 * pr], l_s[2 * pr + 1])


def _fox_prompt(main3, c3, ct3, tq):
    n, t, _ = main3.shape
    nb = t // tq
    qb, kb, vb = C_QKV_C // W_BR, C_QKV_C // W_BR + 1, C_QKV_C // W_BR + 2
    return pl.pallas_call(
        functools.partial(_fox_kernel, tq=tq),
        grid=(n, nb, nb),
        in_specs=[pl.BlockSpec((1, tq, W_BR), lambda b, i, j: (b, i, qb)),
                  pl.BlockSpec((1, tq, W_BR), lambda b, i, j: (b, jnp.minimum(i, j), kb)),
                  pl.BlockSpec((1, tq, W_BR), lambda b, i, j: (b, jnp.minimum(i, j), vb)),
                  pl.BlockSpec((1, tq, N_SMALL), lambda b, i, j: (b, i, 0)),
                  pl.BlockSpec((1, N_HEADS, tq), lambda b, i, j: (b, 0, jnp.minimum(i, j)))],
        out_specs=pl.BlockSpec((1, tq, W_BR), lambda b, i, j: (b, i, 0)),
        out_shape=jax.ShapeDtypeStruct((n, t, W_BR), F32),
        scratch_shapes=[pltpu.VMEM((N_HEADS, tq, LANES), F32), pltpu.VMEM((N_HEADS, tq, LANES), F32),
                        pltpu.VMEM((tq, W_BR), F32), pltpu.VMEM((N_HEADS, tq, LANES), F32)],
        compiler_params=_cparams(("arbitrary", "arbitrary", "arbitrary")),
        name="fox_prompt",
    )(main3, main3, main3, c3, ct3)


def _gdn_kernel(qkv_ref, small_ref, convw_ref, gpar_ref, eg_ref, eb_ref, normg_ref,
                o_ref, s_out_ref, ext_s, q_s, k_s, v_s, b_s, g_s, o_s, st_s, *, tb):
    j = pl.program_id(1)

    @pl.when(j == 0)
    def _():
        ext_s[0:8, :] = jnp.zeros((8, 3 * W_BR), F32)
        st_s[...] = jnp.zeros_like(st_s)

    u = qkv_ref[0]
    ext_s[8:8 + tb, :] = u
    cw = convw_ref[...]
    c = (ext_s[5:5 + tb, :] * cw[0:1, :] + ext_s[6:6 + tb, :] * cw[1:2, :]
         + ext_s[7:7 + tb, :] * cw[2:3, :] + u * cw[3:4, :])
    ext_s[0:8, :] = u[tb - 8:tb, :]
    c = _silu(c)
    bd = _head_block_diag()
    q = c[:, 0:W_BR]
    k = c[:, W_BR:2 * W_BR]
    q_s[...] = q * lax.rsqrt(_dot3_l(q * q, bd) + L2_EPS) * (HEAD_DIM ** -0.5)
    k_s[...] = k * lax.rsqrt(_dot3_l(k * k, bd) + L2_EPS)
    v_s[...] = c[:, 2 * W_BR:3 * W_BR]

    sm = small_ref[0]
    gpar = gpar_ref[...]
    g = gpar[0:1, :] * _softplus(sm + gpar[1:2, :])
    beta = _sigmoid(sm)
    gcum = _dot3_r(_chunk_tril(tb), g)
    g_s[...] = _dot3_l(gcum, eg_ref[...])
    b_s[...] = _dot3_l(beta, eb_ref[...])

    eye, strict, incl, blk = _chunk_masks()
    ones_b = jnp.ones((CHUNK, CHUNK), BF16)

    def chunk(ci, carry):
        r0 = pl.multiple_of(ci * CHUNK, CHUNK)
        rows = pl.ds(r0, CHUNK)
        q_c, k_c, v_c, b_c, g_c = q_s[rows, :], k_s[rows, :], v_s[rows, :], b_s[rows, :], g_s[rows, :]
        sts = [st_s[h] for h in range(N_HEADS)]
        heads = range(N_HEADS)
        lanes = [slice(h * HEAD_DIM, (h + 1) * HEAD_DIM) for h in heads]
        q_h, k_h, v_h, b_h, g_h = ([x[:, hs] for hs in lanes] for x in (q_c, k_c, v_c, b_c, g_c))
        grow = [_dot3_r(ones_b, g_h[h] * eye) for h in heads]
        decay = [jnp.exp(jnp.where(incl > 0, g_h[h] - grow[h], NEG_INF)) for h in heads]
        kb = [k_h[h] * b_h[h] for h in heads]
        both = [_mm_nt(jnp.concatenate([kb[h], q_h[h]], axis=0).astype(BF16), k_h[h].astype(BF16))
                for h in heads]
        a = [both[h][0:CHUNK, :] * decay[h] * strict for h in heads]
        attn = [(both[h][CHUNK:2 * CHUNK, :] * decay[h]).astype(BF16) for h in heads]
        eg = [jnp.exp(g_h[h]) for h in heads]
        sol = _tri_solve(a, [jnp.concatenate([v_h[h] * b_h[h], kb[h] * eg[h]], axis=1) for h in heads], eye, blk)
        stb = [sts[h].astype(BF16) for h in heads]
        v_new = [sol[h][:, 0:HEAD_DIM] - _mm(sol[h][:, HEAD_DIM:2 * HEAD_DIM].astype(BF16), stb[h]) for h in heads]
        vnb = [v_new[h].astype(BF16) for h in heads]
        outs = [_mm((q_h[h] * eg[h]).astype(BF16), stb[h]) + _mm(attn[h], vnb[h]) for h in heads]
        glast = [g_h[h][CHUNK - 1:CHUNK, :] for h in heads]
        new_sts = [sts[h] * jnp.exp(glast[h])
                   + _mm_tn((k_h[h] * jnp.exp(glast[h] - g_h[h])).astype(BF16), vnb[h]) for h in heads]
        o_s[rows, :] = jnp.concatenate(outs, axis=1)
        for h in range(N_HEADS):
            st_s[h] = new_sts[h]
        return carry

    lax.fori_loop(0, tb // CHUNK, chunk, 0)

    o = o_s[...]
    ms = _dot3_l(o * o, bd) * (1.0 / HEAD_DIM)
    o_ref[0] = o * lax.rsqrt(ms + RMS_EPS) * normg_ref[...]

    @pl.when(j == pl.num_programs(1) - 1)
    def _():
        s_out_ref[0] = st_s[...]


def _gdn_prompt(main3, small3, conv_w, gpar, eg, eb, normg, tb):
    n, t, _ = main3.shape
    full = lambda shape: pl.BlockSpec(shape, lambda b, j: (0,) * len(shape))
    return pl.pallas_call(
        functools.partial(_gdn_kernel, tb=tb),
        grid=(n, t // tb),
        in_specs=[pl.BlockSpec((1, tb, 3 * W_BR), lambda b, j: (b, j, C_QKV_A // (3 * W_BR))),
                  pl.BlockSpec((1, tb, N_SMALL), lambda b, j: (b, j, 0)),
                  full((CONV_W, 3 * W_BR)), full((8, N_SMALL)), full((N_SMALL, W_BR)), full((N_SMALL, W_BR)),
                  full((1, W_BR))],
        out_specs=[pl.BlockSpec((1, tb, W_BR), lambda b, j: (b, j, 0)),
                   pl.BlockSpec((1, N_HEADS, HEAD_DIM, HEAD_DIM), lambda b, j: (b, 0, 0, 0))],
        out_shape=[jax.ShapeDtypeStruct((n, t, W_BR), F32),
                   jax.ShapeDtypeStruct((n, N_HEADS, HEAD_DIM, HEAD_DIM), F32)],
        scratch_shapes=[pltpu.VMEM((tb + 8, 3 * W_BR), F32)] + [pltpu.VMEM((tb, W_BR), F32)] * 6
                       + [pltpu.VMEM((N_HEADS, HEAD_DIM, HEAD_DIM), F32)],
        compiler_params=_cparams(("arbitrary", "arbitrary")),
        name="gdn_prompt",
    )(main3, small3, conv_w, gpar, eg, eb, normg)


def _rwkv_kernel(p_ref, small_ref, mu_ref, mus_ref, wup_ref, aup_ref, par_ref,
                 o_ref, s_out_ref, ext_s, exts_s, ah_s, bh_s, kh_s, rh_s, be_s, ke_s, v_s, wl_s, y_s, st_s, *, tb):
    j = pl.program_id(1)

    @pl.when(j == 0)
    def _():
        ext_s[0:8, :] = jnp.zeros((8, 3 * W_BR), F32)
        exts_s[0:8, :] = jnp.zeros((8, N_SMALL), F32)
        st_s[...] = jnp.zeros_like(st_s)

    p = p_ref[0]
    sm = small_ref[0]
    ext_s[8:8 + tb, :] = p
    exts_s[8:8 + tb, :] = sm
    prev = ext_s[7:7 + tb, :]
    prevs = exts_s[7:7 + tb, :]
    ext_s[0:8, :] = p[tb - 8:tb, :]
    exts_s[0:8, :] = sm[tb - 8:tb, :]
    ps = p + (prev - p) * mu_ref[...]
    pss = sm + (prevs - sm) * mus_ref[...]
    r = ps[:, 0:W_BR]
    k = ps[:, W_BR:2 * W_BR]
    v = ps[:, 2 * W_BR:3 * W_BR]
    par = par_ref[...]
    w0, a0, xi, alpha, rho, gn_g, gn_b = (par[i:i + 1, :] for i in range(7))
    d = w0 + _mm(jnp.tanh(pss), wup_ref[...], HI)
    logw = -RWKV_DECAY_SCALE * _sigmoid(d)
    a = _sigmoid(a0 + _mm(pss, aup_ref[...], HI))
    bd = _head_block_diag()
    kx = k * xi
    kk = kx * lax.rsqrt(_dot3_l(kx * kx, bd) + L2_EPS)
    k2 = k * (1.0 + (a - 1.0) * alpha)
    lc = _dot3_r(_chunk_tril(tb), logw)
    nb = -(a * kk)
    ah_s[...] = kk * jnp.exp(lc - logw)
    bh_s[...] = nb * jnp.exp(-lc)
    kh_s[...] = k2 * jnp.exp(-lc)
    rh_s[...] = r * jnp.exp(lc)
    v_s[...] = v
    rr = _iota2((tb, tb), 0)
    cc = _iota2((tb, tb), 1)
    last = jnp.where(cc == (rr // CHUNK) * CHUNK + (CHUNK - 1), 1.0, 0.0).astype(BF16)
    ll = _dot3_r(last, lc)
    be_s[...] = nb * jnp.exp(ll - lc)
    ke_s[...] = k2 * jnp.exp(ll - lc)
    wl_s[...] = jnp.exp(ll)

    eye, strict, incl, blk = _chunk_masks()
    incl2 = jnp.concatenate([incl, incl], axis=1)

    def chunk(ci, carry):
        r0 = pl.multiple_of(ci * CHUNK, CHUNK)
        rows = pl.ds(r0, CHUNK)
        ah_c, bh_c, kh_c, rh_c, v_c = ah_s[rows, :], bh_s[rows, :], kh_s[rows, :], rh_s[rows, :], v_s[rows, :]
        be_c, ke_c = be_s[rows, :], ke_s[rows, :]
        wl_c = wl_s[pl.ds(r0, 1), :]
        sts = [st_s[h] for h in range(N_HEADS)]
        heads = range(N_HEADS)
        lanes = [slice(h * HEAD_DIM, (h + 1) * HEAD_DIM) for h in heads]
        v_h = [v_c[:, hs] for hs in lanes]
        ar_s = [_split2(jnp.concatenate([ah_c[:, hs], rh_c[:, hs]], axis=0)) for hs in lanes]
        bk_s = [_split2(jnp.concatenate([bh_c[:, hs], kh_c[:, hs]], axis=0)) for hs in lanes]
        ends_s = [_split2(jnp.concatenate([be_c[:, hs], ke_c[:, hs]], axis=0)) for hs in lanes]
        gram = [_mm3s(ar_s[h], bk_s[h], _DN_NT) for h in heads]
        on_st = [_mm3s(ar_s[h], _split2(sts[h]), _DN_NT) for h in heads]
        a_ak_v = [_mm3(gram[h][0:CHUNK, CHUNK:2 * CHUNK] * strict, v_h[h]) for h in heads]
        uu = _tri_solve([-(gram[h][0:CHUNK, 0:CHUNK] * strict) for h in heads],
                        [on_st[h][0:CHUNK, :] + a_ak_v[h] for h in heads], eye, blk)
        uv_s = [_split2(jnp.concatenate([uu[h], v_h[h]], axis=0)) for h in heads]
        outs = [on_st[h][CHUNK:2 * CHUNK, :] + _mm3s(_split2(gram[h][CHUNK:2 * CHUNK, :] * incl2), uv_s[h])
                for h in heads]
        new_sts = [sts[h] * wl_c[:, lanes[h]] + _mm3s(uv_s[h], ends_s[h], _DN_TN) for h in heads]
        y_s[rows, :] = jnp.concatenate(outs, axis=1)
        for h in range(N_HEADS):
            st_s[h] = new_sts[h]
        return carry

    lax.fori_loop(0, tb // CHUNK, chunk, 0)

    y = y_s[...]
    mean = _dot3_l(y, bd) * (1.0 / HEAD_DIM)
    yc = y - mean
    var = _dot3_l(yc * yc, bd) * (1.0 / HEAD_DIM)
    yn = yc * lax.rsqrt(var + GN_EPS) * gn_g + gn_b
    bonus = _dot3_l(r * k2 * rho, bd) * v
    o_ref[0] = yn + bonus

    @pl.when(j == pl.num_programs(1) - 1)
    def _():
        s_out_ref[0] = st_s[...]


def _rwkv_prompt(main3, small3, mu, mus, wup, aup, par, tb):
    n, t, _ = main3.shape
    full = lambda shape: pl.BlockSpec(shape, lambda b, j: (0,) * len(shape))
    return pl.pallas_call(
        functools.partial(_rwkv_kernel, tb=tb),
        grid=(n, t // tb),
        in_specs=[pl.BlockSpec((1, tb, 3 * W_BR), lambda b, j: (b, j, C_RKV_B // (3 * W_BR))),
                  pl.BlockSpec((1, tb, N_SMALL), lambda b, j: (b, j, 0)),
                  full((1, 3 * W_BR)), full((1, N_SMALL)), full((N_SMALL, W_BR)), full((N_SMALL, W_BR)),
                  full((8, W_BR))],
        out_specs=[pl.BlockSpec((1, tb, W_BR), lambda b, j: (b, j, 0)),
                   pl.BlockSpec((1, N_HEADS, HEAD_DIM, HEAD_DIM), lambda b, j: (b, 0, 0, 0))],
        out_shape=[jax.ShapeDtypeStruct((n, t, W_BR), F32),
                   jax.ShapeDtypeStruct((n, N_HEADS, HEAD_DIM, HEAD_DIM), F32)],
        scratch_shapes=[pltpu.VMEM((tb + 8, 3 * W_BR), F32), pltpu.VMEM((tb + 8, N_SMALL), F32)]
                       + [pltpu.VMEM((tb, W_BR), F32)] * 9
                       + [pltpu.VMEM((N_HEADS, HEAD_DIM, HEAD_DIM), F32)],
        compiler_params=_cparams(("arbitrary", "arbitrary")),
        name="rwkv_prompt",
    )(main3, small3, mu, mus, wup, aup, par)


def _pool_kernel(u_ref, w_ref, scale_ref, o_ref, ext_s, *, tb):
    j = pl.program_id(1)

    @pl.when(j == 0)
    def _():
        ext_s[0:16, :] = jnp.zeros((16, W_BR), F32)

    u = u_ref[0]
    ext_s[16:16 + tb, :] = u
    pos = j * tb + _iota2((tb, POOL_GW), 0)
    outs = []
    for gi, wdw in enumerate(POOL_WINDOWS):
        ls = slice(gi * POOL_GW, (gi + 1) * POOL_GW)
        s = u[:, ls]
        for sh in range(1, wdw):
            s = s + ext_s[16 - sh:16 - sh + tb, ls]
        cnt = jnp.minimum(wdw, pos + 1).astype(F32)
        pooled = s / cnt - u[:, ls]
        outs.append(_mm(pooled.astype(BF16), w_ref[gi].astype(BF16)))
    ext_s[0:16, :] = u[tb - 16:tb, :]
    o_ref[0] = jnp.concatenate(outs, axis=1) * scale_ref[...]


def _pool_prompt(main3, pool_w, scale, tb):
    n, t, _ = main3.shape
    return pl.pallas_call(
        functools.partial(_pool_kernel, tb=tb),
        grid=(n, t // tb),
        in_specs=[pl.BlockSpec((1, tb, W_BR), lambda b, j: (b, j, C_U_D // W_BR)),
                  pl.BlockSpec((len(POOL_WINDOWS), POOL_GW, POOL_GW), lambda b, j: (0, 0, 0)),
                  pl.BlockSpec((1, W_BR), lambda b, j: (0, 0))],
        out_specs=pl.BlockSpec((1, tb, W_BR), lambda b, j: (b, j, 0)),
        out_shape=jax.ShapeDtypeStruct((n, t, W_BR), F32),
        scratch_shapes=[pltpu.VMEM((tb + 16, W_BR), F32)],
        compiler_params=_cparams(("arbitrary", "arbitrary")),
        name="pool_prompt",
    )(main3, pool_w, scale)


def _out_kernel(oa_ref, ob_ref, oc_ref, od_ref, za_ref, zb_ref, zc_ref, zd_ref, x_ref, w_ref, g_ref, b_ref, y_ref):
    acc = ALPHA_DN * x_ref[...]
    for i, (o_r, z_r) in enumerate(((oa_ref, za_ref), (ob_ref, zb_ref), (oc_ref, zc_ref), (od_ref, zd_ref))):
        gated = (o_r[...] * _silu(z_r[...])).astype(BF16)
        acc = acc + _mm(gated, w_ref[i * W_BR:(i + 1) * W_BR, :])
    mu = jnp.mean(acc, axis=-1, keepdims=True)
    xc = acc - mu
    var = jnp.mean(xc * xc, axis=-1, keepdims=True)
    y_ref[...] = xc * lax.rsqrt(var + LN_EPS) * g_ref[...] + b_ref[...]


def _out_proj(o_a, o_b, o_c, o_d, main2, x2d, w_out, ln_g, ln_b, tm):
    m = x2d.shape[0]
    ospec = pl.BlockSpec((tm, W_BR), lambda i: (i, 0))
    zspec = lambda col: pl.BlockSpec((tm, W_BR), lambda i: (i, col // W_BR))
    return pl.pallas_call(
        _out_kernel,
        grid=(m // tm,),
        in_specs=[ospec, ospec, ospec, ospec, zspec(C_Z_A), zspec(C_Z_B), zspec(C_Z_C), zspec(C_Z_D),
                  pl.BlockSpec((tm, D_MODEL), lambda i: (i, 0)),
                  pl.BlockSpec((D_MODEL, D_MODEL), lambda i: (0, 0)),
                  pl.BlockSpec((1, D_MODEL), lambda i: (0, 0)), pl.BlockSpec((1, D_MODEL), lambda i: (0, 0))],
        out_specs=pl.BlockSpec((tm, D_MODEL), lambda i: (i, 0)),
        out_shape=jax.ShapeDtypeStruct((m, D_MODEL), F32),
        compiler_params=_cparams(("arbitrary",)),
        name="out_proj",
    )(o_a, o_b, o_c, o_d, main2, main2, main2, main2, x2d, w_out, ln_g, ln_b)


def _rows8(row, nrows=1):
    return jnp.where(_iota2((8, row.shape[1]), 0) < nrows, jnp.broadcast_to(row, (8, row.shape[1])), 0.0)


def _dec_kernel(main_ref, small_ref, sa_ref, conv_ref, sb_ref, shift_ref, shifts_ref, dbuf_ref,
                convw_ref, gpar_ref, normg_ref, mu_ref, mus_ref, wup_ref, aup_ref, par_ref, poolw_ref, pscale_ref,
                oa_ref, ob_ref, od_ref, sa_out, conv_out, sb_out, dbuf_out, ext_s, *, pos):
    row = main_ref[0]
    sm = small_ref[0]
    bd = _head_block_diag()

    u = row[:, C_QKV_A:C_QKV_A + 3 * W_BR]
    buf = conv_ref[0]
    cw = convw_ref[...]
    c = buf[0:1, :] * cw[0:1, :] + buf[1:2, :] * cw[1:2, :] + buf[2:3, :] * cw[2:3, :] + u * cw[3:4, :]
    conv_out[0, 0:2, :] = buf[1:3, :]
    conv_out[0, 2:3, :] = u
    c = _silu(c)
    q = c[:, 0:W_BR]
    k = c[:, W_BR:2 * W_BR]
    v = c[:, 2 * W_BR:3 * W_BR]
    q = q * lax.rsqrt(_dot3_l(_rows8(q * q), bd)[0:1, :] + L2_EPS) * (HEAD_DIM ** -0.5)
    k = k * lax.rsqrt(_dot3_l(_rows8(k * k), bd)[0:1, :] + L2_EPS)
    gpar = gpar_ref[...]
    g = gpar[0:1, :] * _softplus(sm + gpar[1:2, :])
    beta = _sigmoid(sm)
    o_parts = []
    for h in range(N_HEADS):
        hs = slice(h * HEAD_DIM, (h + 1) * HEAD_DIM)
        eg = jnp.exp(g[:, S_A + h:S_A + h + 1])
        bt = beta[:, S_B + h:S_B + h + 1]
        st = sa_ref[0, h]
        kh = k[:, hs]
        k8 = _rows8(kh)
        ks = _mm(k8, st, HI)[0:1, :]
        v_new = bt * (v[:, hs] - eg * ks)
        st_new = st * eg + _mm_tn(k8, _rows8(v_new), HI)
        sa_out[0, h] = st_new
        o_parts.append(_mm(_rows8(q[:, hs]), st_new, HI)[0:1, :])
    o = jnp.concatenate(o_parts, axis=1)
    ms = _dot3_l(_rows8(o * o), bd)[0:1, :] * (1.0 / HEAD_DIM)
    oa_ref[0] = o * lax.rsqrt(ms + RMS_EPS) * normg_ref[...]

    p = row[:, C_RKV_B:C_RKV_B + 3 * W_BR]
    ps = p + (shift_ref[0] - p) * mu_ref[...]
    pss = sm + (shifts_ref[0] - sm) * mus_ref[...]
    r = ps[:, 0:W_BR]
    k = ps[:, W_BR:2 * W_BR]
    v = ps[:, 2 * W_BR:3 * W_BR]
    par = par_ref[...]
    w0, a0, xi, alpha, rho, gn_g, gn_b = (par[i:i + 1, :] for i in range(7))
    d = w0 + _mm(_rows8(jnp.tanh(pss)), wup_ref[...], HI)[0:1, :]
    decay = jnp.exp(-RWKV_DECAY_SCALE * _sigmoid(d))
    a = _sigmoid(a0 + _mm(_rows8(pss), aup_ref[...], HI)[0:1, :])
    kx = k * xi
    kk = kx * lax.rsqrt(_dot3_l(_rows8(kx * kx), bd)[0:1, :] + L2_EPS)
    k2 = k * (1.0 + (a - 1.0) * alpha)
    y_parts = []
    for h in range(N_HEADS):
        hs = slice(h * HEAD_DIM, (h + 1) * HEAD_DIM)
        st = sb_ref[0, h]
        kkh = kk[:, hs]
        s_kk = _mm_nt(_rows8(-kkh), st, HI)[0:1, :]
        lhs = jnp.where(_iota2((8, HEAD_DIM), 0) == 0, jnp.broadcast_to(s_kk, (8, HEAD_DIM)),
                        jnp.where(_iota2((8, HEAD_DIM), 0) == 1, jnp.broadcast_to(v[:, hs], (8, HEAD_DIM)), 0.0))
        rhs = jnp.where(_iota2((8, HEAD_DIM), 0) == 0, jnp.broadcast_to(kkh * a[:, hs], (8, HEAD_DIM)),
                        jnp.where(_iota2((8, HEAD_DIM), 0) == 1, jnp.broadcast_to(k2[:, hs], (8, HEAD_DIM)), 0.0))
        st_new = st * decay[:, hs] + _mm_tn(lhs, rhs, HI)
        sb_out[0, h] = st_new
        y_parts.append(_mm_nt(_rows8(r[:, hs]), st_new, HI)[0:1, :])
    y = jnp.concatenate(y_parts, axis=1)
    mean = _dot3_l(_rows8(y), bd)[0:1, :] * (1.0 / HEAD_DIM)
    yc = y - mean
    var = _dot3_l(_rows8(yc * yc), bd)[0:1, :] * (1.0 / HEAD_DIM)
    yn = yc * lax.rsqrt(var + GN_EPS) * gn_g + gn_b
    bonus = _dot3_l(_rows8(r * k2 * rho), bd)[0:1, :] * v
    ob_ref[0] = yn + bonus

    ud = row[:, C_U_D:C_U_D + W_BR]
    ext_s[0:POOL_BUF, :] = dbuf_ref[0]
    ext_s[POOL_BUF:POOL_BUF + 1, :] = ud
    dbuf_out[0] = ext_s[1:POOL_BUF + 1, :]
    outs = []
    for gi, wdw in enumerate(POOL_WINDOWS):
        ls = slice(gi * POOL_GW, (gi + 1) * POOL_GW)
        s = jnp.sum(ext_s[POOL_BUF + 1 - wdw:POOL_BUF + 1, ls], axis=0, keepdims=True)
        pooled = s / float(min(wdw, pos + 1)) - ud[:, ls]
        outs.append(_mm(_rows8(pooled).astype(BF16), poolw_ref[gi].astype(BF16))[0:1, :])
    od_ref[0] = jnp.concatenate(outs, axis=1) * pscale_ref[...]


def _decode_mixers(main_s, small_s, st_a, conv_a, st_b, shift_rkv, shift_small, dbuf,
                   conv_w, gpar, normg, mu, mus, wup, aup, par, pool_w, pscale, pos):
    nd = main_s.shape[0]
    per_seq = lambda shape: pl.BlockSpec((1,) + shape, lambda b: (b,) + (0,) * len(shape))
    full = lambda shape: pl.BlockSpec(shape, lambda b: (0,) * len(shape))
    hh = (N_HEADS, HEAD_DIM, HEAD_DIM)
    return pl.pallas_call(
        functools.partial(_dec_kernel, pos=pos),
        grid=(nd,),
        in_specs=[per_seq((1, N_MAIN)), per_seq((1, N_SMALL)), per_seq(hh), per_seq((CONV_W - 1, 3 * W_BR)),
                  per_seq(hh), per_seq((1, 3 * W_BR)), per_seq((1, N_SMALL)), per_seq((POOL_BUF, W_BR)),
                  full((CONV_W, 3 * W_BR)), full((8, N_SMALL)), full((1, W_BR)),
                  full((1, 3 * W_BR)), full((1, N_SMALL)), full((N_SMALL, W_BR)), full((N_SMALL, W_BR)),
                  full((8, W_BR)), full((len(POOL_WINDOWS), POOL_GW, POOL_GW)), full((1, W_BR))],
        out_specs=[per_seq((1, W_BR)), per_seq((1, W_BR)), per_seq((1, W_BR)), per_seq(hh),
                   per_seq((CONV_W - 1, 3 * W_BR)), per_seq(hh), per_seq((POOL_BUF, W_BR))],
        out_shape=[jax.ShapeDtypeStruct((nd, 1, W_BR), F32)] * 3
                  + [jax.ShapeDtypeStruct((nd,) + hh, F32), jax.ShapeDtypeStruct((nd, CONV_W - 1, 3 * W_BR), F32),
                     jax.ShapeDtypeStruct((nd,) + hh, F32), jax.ShapeDtypeStruct((nd, POOL_BUF, W_BR), F32)],
        scratch_shapes=[pltpu.VMEM((16, W_BR), F32)],
        compiler_params=_cparams(("arbitrary",)),
        name="decode_mixers",
    )(main_s.reshape(nd, 1, N_MAIN), small_s.reshape(nd, 1, N_SMALL), st_a, conv_a, st_b,
      shift_rkv.reshape(nd, 1, 3 * W_BR), shift_small.reshape(nd, 1, N_SMALL), dbuf,
      conv_w, gpar, normg, mu, mus, wup, aup, par, pool_w, pscale)


def _dfox_kernel(pt_ref, qkv_ref, qrep_ref, small_ref, bias_ref, *rest, n_grp):
    kt_refs, vt_refs, lf_refs = rest[0:n_grp], rest[n_grp:2 * n_grp], rest[2 * n_grp:3 * n_grp]
    o_ref, lf_out, m_s, l_s, w_s, r_s, acc_s = rest[3 * n_grp:]
    j = pl.program_id(1)
    scale = HEAD_DIM ** -0.5
    sub8 = _iota2((N_HEADS, LANES), 0)
    lane8 = _iota2((N_HEADS, LANES), 1)

    @pl.when(j == 0)
    def _():
        lf_new = _log_sigmoid(small_ref[0] + bias_ref[...])
        lf_out[0] = lf_new
        qb = (qkv_ref[0, 0] * scale).astype(BF16).astype(F32)
        kb = qkv_ref[0, 1].astype(BF16).astype(F32)
        m_s[...] = jnp.broadcast_to(jnp.sum(qb * kb, axis=1, keepdims=True), (N_HEADS, LANES))
        l_s[...] = jnp.ones_like(l_s)
        w_s[...] = jnp.ones_like(w_s)
        acc_s[...] = jnp.zeros_like(acc_s)
        mine = jnp.where(lane8 == sub8 + S_F, jnp.broadcast_to(lf_new, (N_HEADS, LANES)), 0.0)
        r_s[...] = jnp.broadcast_to(jnp.sum(mine, axis=1, keepdims=True), (N_HEADS, LANES))

    grp = range(n_grp)
    r_t = _iota2((PAGE_SIZE, 2 * LANES), 0)
    c_t = _iota2((PAGE_SIZE, 2 * LANES), 1)
    later_or_all = jnp.where((r_t > c_t) | (c_t >= LANES), 1.0, 0.0).astype(BF16)
    q_rep = qrep_ref[0]
    s_t = [jnp.sum(kt_refs[g][0, 0] * q_rep, axis=1) * scale for g in grp]
    gates = [_dot3_l(lf_refs[g][0, 0], later_or_all) for g in grp]
    logits, r_run = [], r_s[...]
    for g in grp:
        logits.append(s_t[g] + gates[g][:, 0:LANES] + r_run)
        r_run = r_run + gates[g][:, LANES:2 * LANES]
    m_old = m_s[...]
    m_grp = logits[0]
    for g in grp[1:]:
        m_grp = jnp.maximum(m_grp, logits[g])
    m_new = jnp.maximum(m_old, jnp.max(m_grp, axis=1, keepdims=True))
    alpha = jnp.exp(m_old - m_new)
    p = [jnp.exp(logits[g] - m_new) for g in grp]
    p_sum = p[0]
    for g in grp[1:]:
        p_sum = p_sum + p[g]
    for h in range(N_HEADS):
        acc = acc_s[h] * jnp.broadcast_to(alpha[h:h + 1, :], (HEAD_DIM, LANES))
        for g in grp:
            acc = acc + vt_refs[g][0, 0, h] * jnp.broadcast_to(p[g][h:h + 1, :], (HEAD_DIM, LANES))
        acc_s[h] = acc
    l_s[...] = alpha * l_s[...] + jnp.sum(p_sum, axis=1, keepdims=True)
    w_s[...] = alpha * w_s[...]
    m_s[...] = m_new
    r_s[...] = r_run

    @pl.when(j == pl.num_programs(1) - 1)
    def _():
        ones_b = jnp.ones((N_HEADS, LANES), BF16)
        sub = _iota2((N_HEADS, HEAD_DIM), 0)
        red = jnp.zeros((N_HEADS, HEAD_DIM), F32)
        for h in range(N_HEADS):
            hi, mid, lo = _split3(acc_s[h])
            tot = _mm_nt(ones_b, hi) + _mm_nt(ones_b, mid) + _mm_nt(ones_b, lo)
            red = red + jnp.where(sub == h, tot, 0.0)
        o_ref[0] = (red + w_s[...][:, 0:HEAD_DIM] * qkv_ref[0, 2]) / l_s[...][:, 0:HEAD_DIM]


def _fox_decode(page_table, qkv_s, small_s, bias_row, cache_k, cache_v, cache_lf, layer, n_grp):
    nd, n_pages = page_table.shape
    kt = jnp.transpose(cache_k, (0, 1, 3, 4, 2))
    vt = jnp.transpose(cache_v, (0, 1, 3, 4, 2))
    lft = jnp.transpose(cache_lf, (0, 1, 3, 2))
    qkv8 = qkv_s.reshape(nd, 3, N_HEADS, HEAD_DIM)
    q_rep = jnp.broadcast_to(qkv8[:, 0, :, :, None], (nd, N_HEADS, HEAD_DIM, LANES))
    page = lambda g: (lambda b, j, pt: (layer, pt[b, n_pages - 1 - (j * n_grp + g)], 0, 0, 0))
    page4 = lambda g: (lambda b, j, pt: (layer, pt[b, n_pages - 1 - (j * n_grp + g)], 0, 0))
    kv_spec = lambda g: pl.BlockSpec((1, 1, N_HEADS, HEAD_DIM, PAGE_SIZE), page(g))
    grid_spec = pltpu.PrefetchScalarGridSpec(
        num_scalar_prefetch=1,
        grid=(nd, n_pages // n_grp),
        in_specs=[pl.BlockSpec((1, 3, N_HEADS, HEAD_DIM), lambda b, j, pt: (b, 0, 0, 0)),
                  pl.BlockSpec((1, N_HEADS, HEAD_DIM, LANES), lambda b, j, pt: (b, 0, 0, 0)),
                  pl.BlockSpec((1, 1, N_SMALL), lambda b, j, pt: (b, 0, 0)),
                  pl.BlockSpec((1, N_SMALL), lambda b, j, pt: (0, 0))]
                 + [kv_spec(g) for g in range(n_grp)] + [kv_spec(g) for g in range(n_grp)]
                 + [pl.BlockSpec((1, 1, N_HEADS, PAGE_SIZE), page4(g)) for g in range(n_grp)],
        out_specs=[pl.BlockSpec((1, N_HEADS, HEAD_DIM), lambda b, j, pt: (b, 0, 0)),
                   pl.BlockSpec((1, 1, N_SMALL), lambda b, j, pt: (b, 0, 0))],
        scratch_shapes=[pltpu.VMEM((N_HEADS, LANES), F32)] * 4 + [pltpu.VMEM((N_HEADS, HEAD_DIM, LANES), F32)],
    )
    o, lf_new = pl.pallas_call(
        functools.partial(_dfox_kernel, n_grp=n_grp),
        grid_spec=grid_spec,
        out_shape=[jax.ShapeDtypeStruct((nd, N_HEADS, HEAD_DIM), F32), jax.ShapeDtypeStruct((nd, 1, N_SMALL), F32)],
        compiler_params=_cparams(("arbitrary", "arbitrary")),
        name="fox_decode",
    )(page_table, qkv8, q_rep, small_s.reshape(nd, 1, N_SMALL), bias_row,
      *([kt] * n_grp), *([vt] * n_grp), *([lft] * n_grp))
    return o.reshape(nd, W_BR), lf_new


_MAIN_SEGMENTS = ((O_QKV_A, 3 * W_BR), (O_P_B, 3 * W_BR), (O_QKV_C, 3 * W_BR), (O_Z_A, W_BR), (O_Z_B, W_BR),
                  (O_Z_C, W_BR), (O_U_D, W_BR), (O_Z_D, W_BR))
_SMALL_SEGMENTS = ((O_A_A, 2 * N_HEADS), (O_WL, 2 * LORA_B), (O_F_C, N_HEADS))


def _wprep_kernel(wt_ref, main_ref, small_ref):
    wt = wt_ref[0]
    main_ref[...] = jnp.concatenate([wt[a:a + n, :] for a, n in _MAIN_SEGMENTS], axis=0).T.astype(BF16)
    used = sum(n for _, n in _SMALL_SEGMENTS)
    small = [wt[a:a + n, :] for a, n in _SMALL_SEGMENTS] + [jnp.zeros((N_SMALL - used, wt.shape[1]), F32)]
    small_ref[...] = jnp.concatenate(small, axis=0).T.astype(BF16)


def _prep_w_in(w_in, layer, tr=256):
    _, d, d_in = w_in.shape
    return pl.pallas_call(
        _wprep_kernel,
        grid=(d // tr,),
        in_specs=[pl.BlockSpec((1, d_in, tr), lambda i: (layer, 0, i))],
        out_specs=[pl.BlockSpec((tr, N_MAIN), lambda i: (i, 0)), pl.BlockSpec((tr, N_SMALL), lambda i: (i, 0))],
        out_shape=[jax.ShapeDtypeStruct((d, N_MAIN), BF16), jax.ShapeDtypeStruct((d, N_SMALL), BF16)],
        compiler_params=_cparams(("arbitrary",)),
        name="w_in_prep",
    )(jnp.swapaxes(w_in, 1, 2))


def _small_row(pairs):
    row = jnp.zeros((N_SMALL,), F32)
    for off, vec in pairs:
        row = row.at[off:off + vec.shape[0]].set(vec.astype(F32))
    return row[None, :]


def _layer_params(l, w_in, conv_A, A_log, dt_bias, norm_A, mu_B, w0_B, w_up_B, a0_B, a_up_B, xi_B, alpha_B, rho_B,
                  gn_g_B, gn_b_B, b_f_C, pool_w_D, pool_scale_D, w_out, ln_g, ln_b):
    w_main, w_small = _prep_w_in(w_in, l)
    gpar = jnp.concatenate([_small_row([(S_A, -jnp.exp(A_log[l].astype(F32)))]), _small_row([(S_A, dt_bias[l])]),
                            jnp.zeros((6, N_SMALL), F32)], axis=0)
    head_of_lane = jnp.arange(W_BR) // HEAD_DIM
    lane = jnp.arange(N_SMALL)[:, None]
    eg = (lane == head_of_lane[None, :] + S_A).astype(BF16)
    eb = (lane == head_of_lane[None, :] + S_B).astype(BF16)
    mu = mu_B[l].astype(F32)
    wup = jnp.zeros((N_SMALL, W_BR), F32).at[S_WL:S_WL + LORA_B].set(w_up_B[l].astype(F32))
    aup = jnp.zeros((N_SMALL, W_BR), F32).at[S_AL:S_AL + LORA_B].set(a_up_B[l].astype(F32))
    par = jnp.stack([w0_B[l], a0_B[l], xi_B[l], alpha_B[l], rho_B[l], gn_g_B[l], gn_b_B[l],
                     jnp.zeros((W_BR,), F32)]).astype(F32)
    return dict(
        w_main=w_main, w_small=w_small, conv_w=conv_A[l].astype(F32), gpar=gpar, eg=eg, eb=eb,
        normg=jnp.tile(norm_A[l].astype(F32), N_HEADS)[None, :],
        mu=mu[None, 0:3 * W_BR], mus=_small_row([(S_WL, mu[3 * W_BR:])]), wup=wup, aup=aup, par=par,
        fbias=_small_row([(S_F, b_f_C[l])]), pool_w=pool_w_D[l].astype(F32), pscale=pool_scale_D[l].astype(F32)[None, :],
        w_out=w_out[l].astype(BF16), ln_g=ln_g[l].astype(F32)[None, :], ln_b=ln_b[l].astype(F32)[None, :])


def _prompt_layer(x3, P, tb_rec, tq, tb_gate, tb_pool, tm_proj, tm_out):
    n, t, _ = x3.shape
    x2 = x3.reshape(n * t, D_MODEL)
    main2, small2 = _proj(x2, P['w_main'], P['w_small'], tm_proj)
    main3 = main2.reshape(n, t, N_MAIN)
    small3 = small2.reshape(n, t, N_SMALL)
    o_a, s_a = _gdn_prompt(main3, small3, P['conv_w'], P['gpar'], P['eg'], P['eb'], P['normg'], tb_rec)
    o_b, s_b = _rwkv_prompt(main3, small3, P['mu'], P['mus'], P['wup'], P['aup'], P['par'], tb_rec)
    logf3, c3, ct3 = _gates(small3, P['fbias'], tb_gate)
    o_c = _fox_prompt(main3, c3, ct3, tq)
    o_d = _pool_prompt(main3, P['pool_w'], P['pscale'], tb_pool)
    flat = lambda a: a.reshape(n * t, W_BR)
    y2 = _out_proj(flat(o_a), flat(o_b), flat(o_c), flat(o_d), main2, x2, P['w_out'], P['ln_g'], P['ln_b'], tm_out)
    new = (s_a,
           main3[:, t - (CONV_W - 1):, C_QKV_A:C_QKV_A + 3 * W_BR],
           s_b,
           jnp.concatenate([main3[:, t - 1, C_RKV_B:C_RKV_B + 3 * W_BR], small3[:, t - 1, S_WL:S_WL + 2 * LORA_B]], axis=-1),
           main3[:, :, C_QKV_C + W_BR:C_QKV_C + 2 * W_BR].reshape(n, t, N_HEADS, HEAD_DIM),
           main3[:, :, C_QKV_C + 2 * W_BR:C_QKV_C + 3 * W_BR].reshape(n, t, N_HEADS, HEAD_DIM),
           logf3[:, :, S_F:S_F + N_HEADS],
           main3[:, t - POOL_BUF:, C_U_D:C_U_D + W_BR])
    return y2.reshape(n, t, D_MODEL), new


def _sample_layer(x3, st, cache, layer, page_table, P):
    nd = x3.shape[0]
    st_a, conv_a, st_b, shift_b, dbuf = st
    cache_k, cache_v, cache_lf = cache
    x2 = x3.reshape(nd, D_MODEL)
    main_s, small_s = _proj(x2, P['w_main'], P['w_small'], nd)
    shift_small = jnp.zeros((nd, N_SMALL), F32).at[:, S_WL:S_WL + 2 * LORA_B].set(shift_b[:, 3 * W_BR:].astype(F32))
    o_a, o_b, o_d, s_a, conv_n, s_b, dbuf_n = _decode_mixers(
        main_s, small_s, st_a, conv_a, st_b, shift_b[:, 0:3 * W_BR], shift_small, dbuf,
        P['conv_w'], P['gpar'], P['normg'], P['mu'], P['mus'], P['wup'], P['aup'], P['par'], P['pool_w'], P['pscale'],
        page_table.shape[1] * PAGE_SIZE)
    qkv_s = main_s[:, C_QKV_C:C_QKV_C + 3 * W_BR]
    n_grp = math.gcd(page_table.shape[1], 8)
    o_c, lf_new = _fox_decode(page_table, qkv_s, small_s, P['fbias'], cache_k, cache_v, cache_lf, layer, n_grp)
    flat = lambda a: a.reshape(nd, W_BR)
    y2 = _out_proj(flat(o_a), flat(o_b), flat(o_c), flat(o_d), main_s, x2, P['w_out'], P['ln_g'], P['ln_b'], nd)
    new = (s_a, conv_n, s_b,
           jnp.concatenate([main_s[:, C_RKV_B:C_RKV_B + 3 * W_BR], small_s[:, S_WL:S_WL + 2 * LORA_B]], axis=-1),
           qkv_s[:, W_BR:2 * W_BR].reshape(nd, 1, N_HEADS, HEAD_DIM),
           qkv_s[:, 2 * W_BR:3 * W_BR].reshape(nd, 1, N_HEADS, HEAD_DIM),
           lf_new[:, :, S_F:S_F + N_HEADS],
           dbuf_n)
    return y2.reshape(nd, 1, D_MODEL), new


def kernel(x_prompt, x_sample, state_A_S, state_A_conv, state_B_S, state_B_shift, cache_C_k, cache_C_v, cache_C_logf, state_D_buf, page_table, w_in, conv_A, A_log, dt_bias, norm_A, mu_B, w0_B, w_up_B, a0_B, a_up_B, xi_B, alpha_B, rho_B, gn_g_B, gn_b_B, b_f_C, pool_w_D, pool_scale_D, w_out, ln_g, ln_b):
    depth = w_in.shape[0]
    t = x_prompt.shape[1]
    tb_rec = min(256, t)
    tq = min(256, t)
    tb_gate = min(512, t)
    tb_pool = min(512, t)
    tm_proj = min(512, x_prompt.shape[0] * t)
    tm_out = min(256, x_prompt.shape[0] * t)
    y_p, y_s = x_prompt, x_sample
    prompt_new, sample_new = [], []
    for l in range(depth):
        P = _layer_params(l, w_in, conv_A, A_log, dt_bias, norm_A, mu_B, w0_B, w_up_B, a0_B, a_up_B, xi_B, alpha_B,
                          rho_B, gn_g_B, gn_b_B, b_f_C, pool_w_D, pool_scale_D, w_out, ln_g, ln_b)
        y_p, new_p = _prompt_layer(y_p, P, tb_rec, tq, tb_gate, tb_pool, tm_proj, tm_out)
        st = (state_A_S[l], state_A_conv[l], state_B_S[l], state_B_shift[l], state_D_buf[l])
        y_s, new_s = _sample_layer(y_s, st, (cache_C_k, cache_C_v, cache_C_logf), l, page_table, P)
        prompt_new.append(new_p)
        sample_new.append(new_s)
    p_out = [jnp.stack([n[i] for n in prompt_new]) for i in range(8)]
    s_out = [jnp.stack([n[i] for n in sample_new]) for i in range(8)]
    p_a_s, p_a_conv, p_b_s, p_b_shift, p_c_k, p_c_v, p_c_logf, p_d_buf = p_out
    s_a_s, s_a_conv, s_b_s, s_b_shift, s_c_k, s_c_v, s_c_logf, s_d_buf = s_out
    return (y_p, y_s, p_a_s, p_a_conv, p_b_s, p_b_shift, p_c_k, p_c_v, p_c_logf, p_d_buf,
            s_a_s, s_a_conv, s_b_s, s_b_shift, s_c_k, s_c_v, s_c_logf, s_d_buf)
```

```python
import functools
import math

import jax
import jax.numpy as jnp
from jax import lax
from jax.experimental import pallas as pl
from jax.experimental.pallas import tpu as pltpu

F32 = jnp.float32
BF16 = jnp.bfloat16
HI = lax.Precision.HIGHEST

D_MODEL = 2048
W_BR = 512
HEAD_DIM = 64
N_HEADS = W_BR // HEAD_DIM
CONV_W = 4
CHUNK = 64
LORA_B = 32
POOL_WINDOWS = (2, 4, 8, 16)
POOL_GW = W_BR // len(POOL_WINDOWS)
POOL_BUF = max(POOL_WINDOWS) - 1
PAGE_SIZE = 128
LANES = 128
DEPTH = 2
ALPHA_DN = (2.0 * DEPTH) ** 0.25
LN_EPS = 1e-5
GN_EPS = 64e-5
RMS_EPS = 1e-6
L2_EPS = 1e-6
RWKV_DECAY_SCALE = math.exp(-0.5)
NEG_INF = -1e30

C_QKV_A, C_RKV_B, C_QKV_C = 0, 1536, 3072
C_Z_A, C_Z_B, C_Z_C, C_U_D, C_Z_D = 4608, 5120, 5632, 6144, 6656
N_MAIN = 7168
S_A, S_B, S_WL, S_AL, S_F = 0, 8, 16, 48, 80
N_SMALL = 128
O_QKV_A, O_A_A, O_Z_A, O_P_B, O_WL, O_Z_B, O_QKV_C, O_F_C, O_Z_C, O_U_D, O_Z_D, D_IN = (
    0, 1536, 1552, 2064, 3600, 3664, 4176, 5712, 5720, 6232, 6744, 7256)

VMEM_LIMIT = 48 * 1024 * 1024


def _cparams(sem):
    return pltpu.CompilerParams(dimension_semantics=sem, vmem_limit_bytes=VMEM_LIMIT)


def _mm(a, b, prec=None):
    return jnp.dot(a, b, preferred_element_type=F32, precision=prec)


def _mm_nt(a, b, prec=None):
    return lax.dot_general(a, b, (((1,), (1,)), ((), ())), preferred_element_type=F32, precision=prec)


def _mm_tn(a, b, prec=None):
    return lax.dot_general(a, b, (((0,), (0,)), ((), ())), preferred_element_type=F32, precision=prec)


def _split3(x):
    hi = x.astype(BF16)
    r1 = x - hi.astype(F32)
    mid = r1.astype(BF16)
    lo = (r1 - mid.astype(F32)).astype(BF16)
    return hi, mid, lo


def _dot3_l(x, b_bf16):
    hi, mid, lo = _split3(x)
    return _mm(hi, b_bf16) + _mm(mid, b_bf16) + _mm(lo, b_bf16)


def _dot2_l(x, b_bf16):
    hi = x.astype(BF16)
    lo = (x - hi.astype(F32)).astype(BF16)
    return _mm(hi, b_bf16) + _mm(lo, b_bf16)


def _dot3_r(a_bf16, x):
    hi, mid, lo = _split3(x)
    return _mm(a_bf16, hi) + _mm(a_bf16, mid) + _mm(a_bf16, lo)


def _sigmoid(x):
    return 1.0 / (1.0 + jnp.exp(-x))


def _silu(x):
    return x * _sigmoid(x)


def _softplus(x):
    return jnp.maximum(x, 0.0) + jnp.log1p(jnp.exp(-jnp.abs(x)))


def _log_sigmoid(x):
    return -_softplus(-x)


def _iota2(shape, dim):
    return lax.broadcasted_iota(jnp.int32, shape, dim)


def _head_block_diag():
    r = _iota2((W_BR, W_BR), 0) // HEAD_DIM
    c = _iota2((W_BR, W_BR), 1) // HEAD_DIM
    return jnp.where(r == c, 1.0, 0.0).astype(BF16)


def _chunk_tril(n):
    r = _iota2((n, n), 0)
    c = _iota2((n, n), 1)
    return jnp.where((r >= c) & (r // CHUNK == c // CHUNK), 1.0, 0.0).astype(BF16)


_DN_NN = (((1,), (0,)), ((), ()))
_DN_NT = (((1,), (1,)), ((), ()))
_DN_TN = (((0,), (0,)), ((), ()))


def _bf(x):
    return x.astype(BF16)


def _mmb(a, b, dn=_DN_NN):
    return lax.dot_general(a, b, dn, preferred_element_type=F32)


def _tri_solve(a_list, rhs_list, blk):
    each = lambda f, *ls: [f(*xs) for xs in zip(*ls)]
    width = rhs_list[0].shape[1]
    ad = each(lambda a: a * blk, a_list)
    adb = each(_bf, ad)
    a2b = each(_bf, each(_mmb, adb, adb))
    a4b = each(_bf, each(_mmb, a2b, a2b))
    a8b = each(_bf, each(_mmb, a4b, a4b))
    z = each(lambda r, a, d: jnp.concatenate([r, a - d], axis=1), rhs_list, a_list, ad)
    z = each(lambda x, y: x - y, z, each(_mmb, adb, each(_bf, z)))
    for pw in (a2b, a4b, a8b):
        z = each(lambda x, y: x + y, z, each(_mmb, pw, each(_bf, z)))
    y = each(lambda x: x[:, 0:width], z)
    nb = each(lambda x: _bf(x[:, width:width + CHUNK]), z)
    n2b = each(_bf, each(_mmb, nb, nb))
    t = each(lambda x, u: x + u, y, each(_mmb, n2b, each(_bf, y)))
    return each(lambda x, u: x - u, t, each(_mmb, nb, each(_bf, t)))


def _chunk_masks():
    r = _iota2((CHUNK, CHUNK), 0)
    c = _iota2((CHUNK, CHUNK), 1)
    eye = jnp.where(r == c, 1.0, 0.0).astype(F32)
    strict = jnp.where(r > c, 1.0, 0.0).astype(F32)
    incl = jnp.where(r >= c, 1.0, 0.0).astype(F32)
    blk = jnp.where(r // 16 == c // 16, 1.0, 0.0).astype(F32)
    return eye, strict, incl, blk


def _proj_kernel(x_ref, w_ref, ws_ref, main_ref, small_ref, xb_ref):
    @pl.when(pl.program_id(1) == 0)
    def _():
        xb = x_ref[...].astype(BF16)
        xb_ref[...] = xb
        small_ref[...] = _mm(xb, ws_ref[...])

    main_ref[...] = _mm(xb_ref[...], w_ref[...])


def _proj(x2d, w_main, w_small, tm, tn=1024):
    m = x2d.shape[0]
    return pl.pallas_call(
        _proj_kernel,
        grid=(m // tm, N_MAIN // tn),
        in_specs=[pl.BlockSpec((tm, D_MODEL), lambda i, j: (i, 0)),
                  pl.BlockSpec((D_MODEL, tn), lambda i, j: (0, j)),
                  pl.BlockSpec((D_MODEL, N_SMALL), lambda i, j: (0, 0))],
        out_specs=[pl.BlockSpec((tm, tn), lambda i, j: (i, j)),
                   pl.BlockSpec((tm, N_SMALL), lambda i, j: (i, 0))],
        out_shape=[jax.ShapeDtypeStruct((m, N_MAIN), F32), jax.ShapeDtypeStruct((m, N_SMALL), F32)],
        scratch_shapes=[pltpu.VMEM((tm, D_MODEL), BF16)],
        compiler_params=_cparams(("arbitrary", "arbitrary")),
        name="proj",
    )(x2d, w_main, w_small)


def _gates_kernel(small_ref, bias_ref, logf_ref, c_ref, ct_ref, carry_ref, *, tb):
    @pl.when(pl.program_id(1) == 0)
    def _():
        carry_ref[...] = jnp.zeros_like(carry_ref)

    logf = _log_sigmoid(small_ref[0] + bias_ref[...])
    r = _iota2((tb, tb), 0)
    c = _iota2((tb, tb), 1)
    tril = jnp.where(r >= c, 1.0, 0.0).astype(BF16)
    cum = _dot3_r(tril, logf) + carry_ref[0:1, :]
    carry_ref[...] = jnp.broadcast_to(cum[tb - 1:tb, :], carry_ref.shape)
    logf_ref[0] = logf
    c_ref[0] = cum
    ct_ref[0] = cum.T[S_F:S_F + N_HEADS, :]


def _gates(small3, bias_row, tb):
    n, t, _ = small3.shape
    return pl.pallas_call(
        functools.partial(_gates_kernel, tb=tb),
        grid=(n, t // tb),
        in_specs=[pl.BlockSpec((1, tb, N_SMALL), lambda b, j: (b, j, 0)),
                  pl.BlockSpec((1, N_SMALL), lambda b, j: (0, 0))],
        out_specs=[pl.BlockSpec((1, tb, N_SMALL), lambda b, j: (b, j, 0)),
                   pl.BlockSpec((1, tb, N_SMALL), lambda b, j: (b, j, 0)),
                   pl.BlockSpec((1, N_HEADS, tb), lambda b, j: (b, 0, j))],
        out_shape=[jax.ShapeDtypeStruct((n, t, N_SMALL), F32), jax.ShapeDtypeStruct((n, t, N_SMALL), F32),
                   jax.ShapeDtypeStruct((n, N_HEADS, t), F32)],
        scratch_shapes=[pltpu.VMEM((8, N_SMALL), F32)],
        compiler_params=_cparams(("arbitrary", "arbitrary")),
        name="fox_gates",
    )(small3, bias_row)


def _fox_kernel(qi_ref, kj_ref, q_ref, k_ref, v_ref, c_ref, ct_ref, o_ref, m_s, l_s, acc_s, cq_s, *, tq):
    i = qi_ref[pl.program_id(1)]
    j = kj_ref[pl.program_id(1)]
    n_pairs = W_BR // LANES
    lo_half = _iota2((tq, LANES), 1) < HEAD_DIM
    pair_lanes = [slice(pr * LANES, (pr + 1) * LANES) for pr in range(n_pairs)]

    @pl.when(j == 0)
    def _():
        m_s[...] = jnp.full(m_s.shape, NEG_INF, F32)
        l_s[...] = jnp.zeros_like(l_s)
        acc_s[...] = jnp.zeros_like(acc_s)
        cq = c_ref[0]
        for h in range(N_HEADS):
            cq_s[h] = jnp.broadcast_to(cq[:, S_F + h:S_F + h + 1], (tq, LANES))

    def step(masked):
        q = q_ref[0] * (HEAD_DIM ** -0.5)
        k = k_ref[0].astype(BF16)
        v = v_ref[0]
        ct = ct_ref[0]
        wide = lambda x: jnp.concatenate([x] * (tq // LANES), axis=1)
        if masked:
            keep = _iota2((tq, tq), 0) >= _iota2((tq, tq), 1)
        s_all = []
        for ps in pair_lanes:
            qp, kp = q[:, ps], k[:, ps]
            s_all.append(_mm_nt(jnp.where(lo_half, qp, 0.0).astype(BF16), kp))
            s_all.append(_mm_nt(jnp.where(lo_half, 0.0, qp).astype(BF16), kp))
        p_all, alpha_all = [], []
        for h in range(N_HEADS):
            s = s_all[h] + wide(cq_s[h]) - ct[h:h + 1, :]
            if masked:
                s = jnp.where(keep, s, NEG_INF)
            m_old = m_s[h]
            m_new = jnp.maximum(m_old, jnp.max(s, axis=1, keepdims=True))
            alpha = jnp.exp(m_old - m_new)
            p = jnp.exp(s - wide(m_new))
            l_s[h] = alpha * l_s[h] + jnp.sum(p, axis=1, keepdims=True)
            m_s[h] = m_new
            p_all.append(p.astype(BF16))
            alpha_all.append(alpha)
        for pr, ps in enumerate(pair_lanes):
            vp = v[:, ps]
            v_bd = jnp.concatenate([jnp.where(lo_half, vp, 0.0), jnp.where(lo_half, 0.0, vp)], axis=0).astype(BF16)
            p_pair = jnp.concatenate([p_all[2 * pr], p_all[2 * pr + 1]], axis=1)
            alpha_p = jnp.where(lo_half, alpha_all[2 * pr], alpha_all[2 * pr + 1])
            acc_s[:, ps] = alpha_p * acc_s[:, ps] + _mm(p_pair, v_bd)

    @pl.when(j < i)
    def _():
        step(False)

    @pl.when(j == i)
    def _():
        step(True)
        for pr, ps in enumerate(pair_lanes):
            o_ref[0, :, ps] = acc_s[:, ps] / jnp.where(lo_half, l_s[2 * pr], l_s[2 * pr + 1])


def _fox_prompt(main3, c3, ct3, tq):
    n, t, _ = main3.shape
    nb = t // tq
    qb, kb, vb = C_QKV_C // W_BR, C_QKV_C // W_BR + 1, C_QKV_C // W_BR + 2
    pairs = [(i, j) for i in range(nb) for j in range(i + 1)]
    qi = jnp.asarray([p[0] for p in pairs], jnp.int32)
    kj = jnp.asarray([p[1] for p in pairs], jnp.int32)
    grid_spec = pltpu.PrefetchScalarGridSpec(
        num_scalar_prefetch=2,
        grid=(n, len(pairs)),
        in_specs=[pl.BlockSpec((1, tq, W_BR), lambda b, s, qi, kj: (b, qi[s], qb)),
                  pl.BlockSpec((1, tq, W_BR), lambda b, s, qi, kj: (b, kj[s], kb)),
                  pl.BlockSpec((1, tq, W_BR), lambda b, s, qi, kj: (b, kj[s], vb)),
                  pl.BlockSpec((1, tq, N_SMALL), lambda b, s, qi, kj: (b, qi[s], 0)),
                  pl.BlockSpec((1, N_HEADS, tq), lambda b, s, qi, kj: (b, 0, kj[s]))],
        out_specs=pl.BlockSpec((1, tq, W_BR), lambda b, s, qi, kj: (b, qi[s], 0)),
        scratch_shapes=[pltpu.VMEM((N_HEADS, tq, LANES), F32), pltpu.VMEM((N_HEADS, tq, LANES), F32),
                        pltpu.VMEM((tq, W_BR), F32), pltpu.VMEM((N_HEADS, tq, LANES), F32)],
    )
    return pl.pallas_call(
        functools.partial(_fox_kernel, tq=tq),
        grid_spec=grid_spec,
        out_shape=jax.ShapeDtypeStruct((n, t, W_BR), F32),
        compiler_params=_cparams(("arbitrary", "arbitrary")),
        name="fox_prompt",
    )(qi, kj, main3, main3, main3, c3, ct3)


def _gdn_kernel(qkv_ref, small_ref, convw_ref, gpar_ref, eg_ref, eb_ref, normg_ref, bd_ref,
                o_ref, s_out_ref, ext_s, q_s, k_s, v_s, b_s, g_s, o_s, st_s, *, tb):
    j = pl.program_id(1)

    @pl.when(j == 0)
    def _():
        ext_s[0:8, :] = jnp.zeros((8, 3 * W_BR), F32)
        st_s[...] = jnp.zeros_like(st_s)

    u = qkv_ref[0]
    ext_s[8:8 + tb, :] = u
    cw = convw_ref[...]
    c = (ext_s[5:5 + tb, :] * cw[0:1, :] + ext_s[6:6 + tb, :] * cw[1:2, :]
         + ext_s[7:7 + tb, :] * cw[2:3, :] + u * cw[3:4, :])
    ext_s[0:8, :] = u[tb - 8:tb, :]
    c = _silu(c)
    bd = bd_ref[...]
    q = c[:, 0:W_BR]
    k = c[:, W_BR:2 * W_BR]
    q_s[...] = q * lax.rsqrt(_dot2_l(q * q, bd) + L2_EPS) * (HEAD_DIM ** -0.5)
    k_s[...] = k * lax.rsqrt(_dot2_l(k * k, bd) + L2_EPS)
    v_s[...] = c[:, 2 * W_BR:3 * W_BR]

    sm = small_ref[0]
    gpar = gpar_ref[...]
    g = gpar[0:1, :] * _softplus(sm + gpar[1:2, :])
    beta = _sigmoid(sm)
    gcum = _dot3_r(_chunk_tril(tb), g)
    g_s[...] = _dot3_l(gcum, eg_ref[...])
    b_s[...] = _dot3_l(beta, eb_ref[...])

    _, strict, incl, blk = _chunk_masks()
    ones_b = jnp.ones((CHUNK, CHUNK), BF16)
    eye_heads = jnp.where(_iota2((CHUNK, W_BR), 0) == _iota2((CHUNK, W_BR), 1) % HEAD_DIM, 1.0, 0.0)

    def chunk(ci, carry):
        r0 = pl.multiple_of(ci * CHUNK, CHUNK)
        rows = pl.ds(r0, CHUNK)
        q_c, k_c, v_c, b_c, g_c = q_s[rows, :], k_s[rows, :], v_s[rows, :], b_s[rows, :], g_s[rows, :]
        sts = [st_s[h] for h in range(N_HEADS)]
        heads = range(N_HEADS)
        lanes = [slice(h * HEAD_DIM, (h + 1) * HEAD_DIM) for h in heads]
        q_h, k_h, v_h, b_h, g_h = ([x[:, hs] for hs in lanes] for x in (q_c, k_c, v_c, b_c, g_c))
        grow_c = _dot3_r(ones_b, g_c * eye_heads)
        decay = [jnp.exp(jnp.where(incl > 0, g_h[h] - grow_c[:, lanes[h]], NEG_INF)) for h in heads]
        kb = [k_h[h] * b_h[h] for h in heads]
        both = [_mm_nt(jnp.concatenate([kb[h], q_h[h]], axis=0).astype(BF16), k_h[h].astype(BF16))
                for h in heads]
        a = [both[h][0:CHUNK, :] * decay[h] * strict for h in heads]
        attn = [(both[h][CHUNK:2 * CHUNK, :] * decay[h]).astype(BF16) for h in heads]
        eg = [jnp.exp(g_h[h]) for h in heads]
        sol = _tri_solve(a, [jnp.concatenate([v_h[h] * b_h[h], kb[h] * eg[h]], axis=1) for h in heads], blk)
        stb = [sts[h].astype(BF16) for h in heads]
        v_new = [sol[h][:, 0:HEAD_DIM] - _mm(sol[h][:, HEAD_DIM:2 * HEAD_DIM].astype(BF16), stb[h]) for h in heads]
        vnb = [v_new[h].astype(BF16) for h in heads]
        outs = [_mm((q_h[h] * eg[h]).astype(BF16), stb[h]) + _mm(attn[h], vnb[h]) for h in heads]
        glast = [g_h[h][CHUNK - 1:CHUNK, :] for h in heads]
        new_sts = [sts[h] * jnp.exp(glast[h])
                   + _mm_tn((k_h[h] * jnp.exp(glast[h] - g_h[h])).astype(BF16), vnb[h]) for h in heads]
        o_s[rows, :] = jnp.concatenate(outs, axis=1)
        for h in range(N_HEADS):
            st_s[h] = new_sts[h]
        return carry

    lax.fori_loop(0, tb // CHUNK, chunk, 0)

    o = o_s[...]
    ms = _dot2_l(o * o, bd) * (1.0 / HEAD_DIM)
    o_ref[0] = o * lax.rsqrt(ms + RMS_EPS) * normg_ref[...]

    @pl.when(j == pl.num_programs(1) - 1)
    def _():
        s_out_ref[0] = st_s[...]


def _gdn_prompt(main3, small3, conv_w, gpar, eg, eb, normg, bd, tb):
    n, t, _ = main3.shape
    full = lambda shape: pl.BlockSpec(shape, lambda b, j: (0,) * len(shape))
    return pl.pallas_call(
        functools.partial(_gdn_kernel, tb=tb),
        grid=(n, t // tb),
        in_specs=[pl.BlockSpec((1, tb, 3 * W_BR), lambda b, j: (b, j, C_QKV_A // (3 * W_BR))),
                  pl.BlockSpec((1, tb, N_SMALL), lambda b, j: (b, j, 0)),
                  full((CONV_W, 3 * W_BR)), full((8, N_SMALL)), full((N_SMALL, W_BR)), full((N_SMALL, W_BR)),
                  full((1, W_BR)), full((W_BR, W_BR))],
        out_specs=[pl.BlockSpec((1, tb, W_BR), lambda b, j: (b, j, 0)),
                   pl.BlockSpec((1, N_HEADS, HEAD_DIM, HEAD_DIM), lambda b, j: (b, 0, 0, 0))],
        out_shape=[jax.ShapeDtypeStruct((n, t, W_BR), F32),
                   jax.ShapeDtypeStruct((n, N_HEADS, HEAD_DIM, HEAD_DIM), F32)],
        scratch_shapes=[pltpu.VMEM((tb + 8, 3 * W_BR), F32)] + [pltpu.VMEM((tb, W_BR), F32)] * 6
                       + [pltpu.VMEM((N_HEADS, HEAD_DIM, HEAD_DIM), F32)],
        compiler_params=_cparams(("arbitrary", "arbitrary")),
        name="gdn_prompt",
    )(main3, small3, conv_w, gpar, eg, eb, normg, bd)


def _rwkv_kernel(p_ref, small_ref, mu_ref, mus_ref, wup_ref, aup_ref, par_ref, bd_ref,
                 o_ref, s_out_ref, ext_s, exts_s, ah_s, bh_s, kh_s, rh_s, be_s, ke_s, v_s, wl_s, y_s, st_s, *, tb):
    j = pl.program_id(1)

    @pl.when(j == 0)
    def _():
        ext_s[0:8, :] = jnp.zeros((8, 3 * W_BR), F32)
        exts_s[0:8, :] = jnp.zeros((8, N_SMALL), F32)
        st_s[...] = jnp.zeros_like(st_s)

    p = p_ref[0]
    sm = small_ref[0]
    ext_s[8:8 + tb, :] = p
    exts_s[8:8 + tb, :] = sm
    prev = ext_s[7:7 + tb, :]
    prevs = exts_s[7:7 + tb, :]
    ext_s[0:8, :] = p[tb - 8:tb, :]
    exts_s[0:8, :] = sm[tb - 8:tb, :]
    ps = p + (prev - p) * mu_ref[...]
    pss = sm + (prevs - sm) * mus_ref[...]
    r = ps[:, 0:W_BR]
    k = ps[:, W_BR:2 * W_BR]
    v = ps[:, 2 * W_BR:3 * W_BR]
    par = par_ref[...]
    w0, a0, xi, alpha, rho, gn_g, gn_b = (par[i:i + 1, :] for i in range(7))
    d = w0 + _mm(_bf(jnp.tanh(pss)), wup_ref[...])
    logw = -RWKV_DECAY_SCALE * _sigmoid(d)
    a = _sigmoid(a0 + _mm(_bf(pss), aup_ref[...]))
    bd = bd_ref[...]
    kx = k * xi
    kk = kx * lax.rsqrt(_dot2_l(kx * kx, bd) + L2_EPS)
    k2 = k * (1.0 + (a - 1.0) * alpha)
    lc = _dot3_r(_chunk_tril(tb), logw)
    nb = -(a * kk)
    ah_s[...] = _bf(kk * jnp.exp(lc - logw))
    bh_s[...] = _bf(nb * jnp.exp(-lc))
    kh_s[...] = _bf(k2 * jnp.exp(-lc))
    rh_s[...] = _bf(r * jnp.exp(lc))
    v_s[...] = _bf(v)
    ll = jnp.concatenate([jnp.broadcast_to(lc[c * CHUNK + CHUNK - 1:(c + 1) * CHUNK, :], (CHUNK, W_BR))
                          for c in range(tb // CHUNK)], axis=0)
    to_end = jnp.exp(ll - lc)
    be_s[...] = _bf(nb * to_end)
    ke_s[...] = _bf(k2 * to_end)
    wl_s[...] = jnp.exp(ll)

    _, strict, incl, blk = _chunk_masks()
    incl2 = jnp.concatenate([incl, incl], axis=1)

    def chunk(ci, carry):
        r0 = pl.multiple_of(ci * CHUNK, CHUNK)
        rows = pl.ds(r0, CHUNK)
        ah_c, bh_c, kh_c, rh_c, v_c = ah_s[rows, :], bh_s[rows, :], kh_s[rows, :], rh_s[rows, :], v_s[rows, :]
        be_c, ke_c = be_s[rows, :], ke_s[rows, :]
        wl_c = wl_s[pl.ds(r0, 1), :]
        sts = [st_s[h] for h in range(N_HEADS)]
        heads = range(N_HEADS)
        lanes = [slice(h * HEAD_DIM, (h + 1) * HEAD_DIM) for h in heads]
        v_h = [v_c[:, hs] for hs in lanes]
        ar = [jnp.concatenate([ah_c[:, hs], rh_c[:, hs]], axis=0) for hs in lanes]
        bk = [jnp.concatenate([bh_c[:, hs], kh_c[:, hs]], axis=0) for hs in lanes]
        ends = [jnp.concatenate([be_c[:, hs], ke_c[:, hs]], axis=0) for hs in lanes]
        gram = [_mmb(ar[h], bk[h], _DN_NT) for h in heads]
        on_st = [_mmb(ar[h], _bf(sts[h]), _DN_NT) for h in heads]
        a_ak_v = [_mmb(_bf(gram[h][0:CHUNK, CHUNK:2 * CHUNK] * strict), v_h[h]) for h in heads]
        uu = _tri_solve([-(gram[h][0:CHUNK, 0:CHUNK] * strict) for h in heads],
                        [on_st[h][0:CHUNK, :] + a_ak_v[h] for h in heads], blk)
        uv = [jnp.concatenate([_bf(uu[h]), v_h[h]], axis=0) for h in heads]
        outs = [on_st[h][CHUNK:2 * CHUNK, :] + _mmb(_bf(gram[h][CHUNK:2 * CHUNK, :] * incl2), uv[h])
                for h in heads]
        new_sts = [sts[h] * wl_c[:, lanes[h]] + _mmb(uv[h], ends[h], _DN_TN) for h in heads]
        y_s[rows, :] = jnp.concatenate(outs, axis=1)
        for h in range(N_HEADS):
            st_s[h] = new_sts[h]
        return carry

    lax.fori_loop(0, tb // CHUNK, chunk, 0)

    y = y_s[...]
    mean = _dot2_l(y, bd) * (1.0 / HEAD_DIM)
    yc = y - mean
    var = _dot2_l(yc * yc, bd) * (1.0 / HEAD_DIM)
    yn = yc * lax.rsqrt(var + GN_EPS) * gn_g + gn_b
    bonus = _dot2_l(r * k2 * rho, bd) * v
    o_ref[0] = yn + bonus

    @pl.when(j == pl.num_programs(1) - 1)
    def _():
        s_out_ref[0] = st_s[...]


def _rwkv_prompt(main3, small3, mu, mus, wup, aup, par, bd, tb):
    n, t, _ = main3.shape
    full = lambda shape: pl.BlockSpec(shape, lambda b, j: (0,) * len(shape))
    return pl.pallas_call(
        functools.partial(_rwkv_kernel, tb=tb),
        grid=(n, t // tb),
        in_specs=[pl.BlockSpec((1, tb, 3 * W_BR), lambda b, j: (b, j, C_RKV_B // (3 * W_BR))),
                  pl.BlockSpec((1, tb, N_SMALL), lambda b, j: (b, j, 0)),
                  full((1, 3 * W_BR)), full((1, N_SMALL)), full((N_SMALL, W_BR)), full((N_SMALL, W_BR)),
                  full((8, W_BR)), full((W_BR, W_BR))],
        out_specs=[pl.BlockSpec((1, tb, W_BR), lambda b, j: (b, j, 0)),
                   pl.BlockSpec((1, N_HEADS, HEAD_DIM, HEAD_DIM), lambda b, j: (b, 0, 0, 0))],
        out_shape=[jax.ShapeDtypeStruct((n, t, W_BR), F32),
                   jax.ShapeDtypeStruct((n, N_HEADS, HEAD_DIM, HEAD_DIM), F32)],
        scratch_shapes=[pltpu.VMEM((tb + 8, 3 * W_BR), F32), pltpu.VMEM((tb + 8, N_SMALL), F32)]
                       + [pltpu.VMEM((tb, W_BR), BF16)] * 7 + [pltpu.VMEM((tb, W_BR), F32)] * 2
                       + [pltpu.VMEM((N_HEADS, HEAD_DIM, HEAD_DIM), F32)],
        compiler_params=_cparams(("arbitrary", "arbitrary")),
        name="rwkv_prompt",
    )(main3, small3, mu, mus, wup, aup, par, bd)


def _pool_kernel(u_ref, w_ref, scale_ref, o_ref, ext_s, *, tb):
    j = pl.program_id(1)

    @pl.when(j == 0)
    def _():
        ext_s[0:16, :] = jnp.zeros((16, W_BR), F32)

    u = u_ref[0]
    ext_s[16:16 + tb, :] = u
    pos = j * tb + _iota2((tb, POOL_GW), 0)
    outs = []
    for gi, wdw in enumerate(POOL_WINDOWS):
        ls = slice(gi * POOL_GW, (gi + 1) * POOL_GW)
        s = u[:, ls]
        for sh in range(1, wdw):
            s = s + ext_s[16 - sh:16 - sh + tb, ls]
        cnt = jnp.minimum(wdw, pos + 1).astype(F32)
        pooled = s / cnt - u[:, ls]
        outs.append(_mm(pooled.astype(BF16), w_ref[gi].astype(BF16)))
    ext_s[0:16, :] = u[tb - 16:tb, :]
    o_ref[0] = jnp.concatenate(outs, axis=1) * scale_ref[...]


def _pool_prompt(main3, pool_w, scale, tb):
    n, t, _ = main3.shape
    return pl.pallas_call(
        functools.partial(_pool_kernel, tb=tb),
        grid=(n, t // tb),
        in_specs=[pl.BlockSpec((1, tb, W_BR), lambda b, j: (b, j, C_U_D // W_BR)),
                  pl.BlockSpec((len(POOL_WINDOWS), POOL_GW, POOL_GW), lambda b, j: (0, 0, 0)),
                  pl.BlockSpec((1, W_BR), lambda b, j: (0, 0))],
        out_specs=pl.BlockSpec((1, tb, W_BR), lambda b, j: (b, j, 0)),
        out_shape=jax.ShapeDtypeStruct((n, t, W_BR), F32),
        scratch_shapes=[pltpu.VMEM((tb + 16, W_BR), F32)],
        compiler_params=_cparams(("arbitrary", "arbitrary")),
        name="pool_prompt",
    )(main3, pool_w, scale)


def _out_kernel(oa_ref, ob_ref, oc_ref, od_ref, za_ref, zb_ref, zc_ref, zd_ref, x_ref, w_ref, g_ref, b_ref, y_ref):
    acc = ALPHA_DN * x_ref[...]
    for i, (o_r, z_r) in enumerate(((oa_ref, za_ref), (ob_ref, zb_ref), (oc_ref, zc_ref), (od_ref, zd_ref))):
        gated = (o_r[...] * _silu(z_r[...])).astype(BF16)
        acc = acc + _mm(gated, w_ref[i * W_BR:(i + 1) * W_BR, :])
    mu = jnp.mean(acc, axis=-1, keepdims=True)
    xc = acc - mu
    var = jnp.mean(xc * xc, axis=-1, keepdims=True)
    y_ref[...] = xc * lax.rsqrt(var + LN_EPS) * g_ref[...] + b_ref[...]


def _out_proj(o_a, o_b, o_c, o_d, main2, x2d, w_out, ln_g, ln_b, tm):
    m = x2d.shape[0]
    ospec = pl.BlockSpec((tm, W_BR), lambda i: (i, 0))
    zspec = lambda col: pl.BlockSpec((tm, W_BR), lambda i: (i, col // W_BR))
    return pl.pallas_call(
        _out_kernel,
        grid=(m // tm,),
        in_specs=[ospec, ospec, ospec, ospec, zspec(C_Z_A), zspec(C_Z_B), zspec(C_Z_C), zspec(C_Z_D),
                  pl.BlockSpec((tm, D_MODEL), lambda i: (i, 0)),
                  pl.BlockSpec((D_MODEL, D_MODEL), lambda i: (0, 0)),
                  pl.BlockSpec((1, D_MODEL), lambda i: (0, 0)), pl.BlockSpec((1, D_MODEL), lambda i: (0, 0))],
        out_specs=pl.BlockSpec((tm, D_MODEL), lambda i: (i, 0)),
        out_shape=jax.ShapeDtypeStruct((m, D_MODEL), F32),
        compiler_params=_cparams(("arbitrary",)),
        name="out_proj",
    )(o_a, o_b, o_c, o_d, main2, main2, main2, main2, x2d, w_out, ln_g, ln_b)


def _rows8(row, nrows=1):
    return jnp.where(_iota2((8, row.shape[1]), 0) < nrows, jnp.broadcast_to(row, (8, row.shape[1])), 0.0)


def _dec_kernel(main_ref, small_ref, sa_ref, conv_ref, sb_ref, shift_ref, shifts_ref, dbuf_ref,
                convw_ref, gpar_ref, normg_ref, mu_ref, mus_ref, wup_ref, aup_ref, par_ref, poolw_ref, pscale_ref,
                oa_ref, ob_ref, od_ref, sa_out, conv_out, sb_out, dbuf_out, ext_s, *, pos):
    row = main_ref[0]
    sm = small_ref[0]
    bd = _head_block_diag()

    u = row[:, C_QKV_A:C_QKV_A + 3 * W_BR]
    buf = conv_ref[0]
    cw = convw_ref[...]
    c = buf[0:1, :] * cw[0:1, :] + buf[1:2, :] * cw[1:2, :] + buf[2:3, :] * cw[2:3, :] + u * cw[3:4, :]
    conv_out[0, 0:2, :] = buf[1:3, :]
    conv_out[0, 2:3, :] = u
    c = _silu(c)
    q = c[:, 0:W_BR]
    k = c[:, W_BR:2 * W_BR]
    v = c[:, 2 * W_BR:3 * W_BR]
    q = q * lax.rsqrt(_dot3_l(_rows8(q * q), bd)[0:1, :] + L2_EPS) * (HEAD_DIM ** -0.5)
    k = k * lax.rsqrt(_dot3_l(_rows8(k * k), bd)[0:1, :] + L2_EPS)
    gpar = gpar_ref[...]
    g = gpar[0:1, :] * _softplus(sm + gpar[1:2, :])
    beta = _sigmoid(sm)
    heads = range(N_HEADS)
    lanes = [slice(h * HEAD_DIM, (h + 1) * HEAD_DIM) for h in heads]
    eg = [jnp.exp(g[:, S_A + h:S_A + h + 1]) for h in heads]
    sts = [sa_ref[0, h] for h in heads]
    k8 = [_rows8(k[:, hs]) for hs in lanes]
    ks = [_mm(k8[h], sts[h], HI)[0:1, :] for h in heads]
    v_new = [beta[:, S_B + h:S_B + h + 1] * (v[:, lanes[h]] - eg[h] * ks[h]) for h in heads]
    st_new = [sts[h] * eg[h] + _mm_tn(k8[h], _rows8(v_new[h]), HI) for h in heads]
    o = jnp.concatenate([_mm(_rows8(q[:, lanes[h]]), st_new[h], HI)[0:1, :] for h in heads], axis=1)
    for h in heads:
        sa_out[0, h] = st_new[h]
    ms = _dot3_l(_rows8(o * o), bd)[0:1, :] * (1.0 / HEAD_DIM)
    oa_ref[0] = o * lax.rsqrt(ms + RMS_EPS) * normg_ref[...]

    p = row[:, C_RKV_B:C_RKV_B + 3 * W_BR]
    ps = p + (shift_ref[0] - p) * mu_ref[...]
    pss = sm + (shifts_ref[0] - sm) * mus_ref[...]
    r = ps[:, 0:W_BR]
    k = ps[:, W_BR:2 * W_BR]
    v = ps[:, 2 * W_BR:3 * W_BR]
    par = par_ref[...]
    w0, a0, xi, alpha, rho, gn_g, gn_b = (par[i:i + 1, :] for i in range(7))
    d = w0 + _mm(_rows8(jnp.tanh(pss)), wup_ref[...], HI)[0:1, :]
    decay = jnp.exp(-RWKV_DECAY_SCALE * _sigmoid(d))
    a = _sigmoid(a0 + _mm(_rows8(pss), aup_ref[...], HI)[0:1, :])
    kx = k * xi
    kk = kx * lax.rsqrt(_dot3_l(_rows8(kx * kx), bd)[0:1, :] + L2_EPS)
    k2 = k * (1.0 + (a - 1.0) * alpha)
    sub = _iota2((8, HEAD_DIM), 0)
    two_rows = lambda r0, r1: jnp.where(sub == 0, jnp.broadcast_to(r0, (8, HEAD_DIM)),
                                        jnp.where(sub == 1, jnp.broadcast_to(r1, (8, HEAD_DIM)), 0.0))
    sts = [sb_ref[0, h] for h in heads]
    s_kk = [_mm_nt(_rows8(-kk[:, lanes[h]]), sts[h], HI)[0:1, :] for h in heads]
    st_new = [sts[h] * decay[:, lanes[h]]
              + _mm_tn(two_rows(s_kk[h], v[:, lanes[h]]), two_rows(kk[:, lanes[h]] * a[:, lanes[h]], k2[:, lanes[h]]), HI)
              for h in heads]
    y = jnp.concatenate([_mm_nt(_rows8(r[:, lanes[h]]), st_new[h], HI)[0:1, :] for h in heads], axis=1)
    for h in heads:
        sb_out[0, h] = st_new[h]
    mean = _dot3_l(_rows8(y), bd)[0:1, :] * (1.0 / HEAD_DIM)
    yc = y - mean
    var = _dot3_l(_rows8(yc * yc), bd)[0:1, :] * (1.0 / HEAD_DIM)
    yn = yc * lax.rsqrt(var + GN_EPS) * gn_g + gn_b
    bonus = _dot3_l(_rows8(r * k2 * rho), bd)[0:1, :] * v
    ob_ref[0] = yn + bonus

    ud = row[:, C_U_D:C_U_D + W_BR]
    ext_s[0:POOL_BUF, :] = dbuf_ref[0]
    ext_s[POOL_BUF:POOL_BUF + 1, :] = ud
    dbuf_out[0] = ext_s[1:POOL_BUF + 1, :]
    outs = []
    for gi, wdw in enumerate(POOL_WINDOWS):
        ls = slice(gi * POOL_GW, (gi + 1) * POOL_GW)
        s = jnp.sum(ext_s[POOL_BUF + 1 - wdw:POOL_BUF + 1, ls], axis=0, keepdims=True)
        pooled = s / float(min(wdw, pos + 1)) - ud[:, ls]
        outs.append(_mm(_rows8(pooled).astype(BF16), poolw_ref[gi].astype(BF16))[0:1, :])
    od_ref[0] = jnp.concatenate(outs, axis=1) * pscale_ref[...]


def _decode_mixers(main_s, small_s, st_a, conv_a, st_b, shift_rkv, shift_small, dbuf,
                   conv_w, gpar, normg, mu, mus, wup, aup, par, pool_w, pscale, pos):
    nd = main_s.shape[0]
    per_seq = lambda shape: pl.BlockSpec((1,) + shape, lambda b: (b,) + (0,) * len(shape))
    full = lambda shape: pl.BlockSpec(shape, lambda b: (0,) * len(shape))
    hh = (N_HEADS, HEAD_DIM, HEAD_DIM)
    return pl.pallas_call(
        functools.partial(_dec_kernel, pos=pos),
        grid=(nd,),
        in_specs=[per_seq((1, N_MAIN)), per_seq((1, N_SMALL)), per_seq(hh), per_seq((CONV_W - 1, 3 * W_BR)),
                  per_seq(hh), per_seq((1, 3 * W_BR)), per_seq((1, N_SMALL)), per_seq((POOL_BUF, W_BR)),
                  full((CONV_W, 3 * W_BR)), full((8, N_SMALL)), full((1, W_BR)),
                  full((1, 3 * W_BR)), full((1, N_SMALL)), full((N_SMALL, W_BR)), full((N_SMALL, W_BR)),
                  full((8, W_BR)), full((len(POOL_WINDOWS), POOL_GW, POOL_GW)), full((1, W_BR))],
        out_specs=[per_seq((1, W_BR)), per_seq((1, W_BR)), per_seq((1, W_BR)), per_seq(hh),
                   per_seq((CONV_W - 1, 3 * W_BR)), per_seq(hh), per_seq((POOL_BUF, W_BR))],
        out_shape=[jax.ShapeDtypeStruct((nd, 1, W_BR), F32)] * 3
                  + [jax.ShapeDtypeStruct((nd,) + hh, F32), jax.ShapeDtypeStruct((nd, CONV_W - 1, 3 * W_BR), F32),
                     jax.ShapeDtypeStruct((nd,) + hh, F32), jax.ShapeDtypeStruct((nd, POOL_BUF, W_BR), F32)],
        scratch_shapes=[pltpu.VMEM((16, W_BR), F32)],
        compiler_params=_cparams(("arbitrary",)),
        name="decode_mixers",
    )(main_s.reshape(nd, 1, N_MAIN), small_s.reshape(nd, 1, N_SMALL), st_a, conv_a, st_b,
      shift_rkv.reshape(nd, 1, 3 * W_BR), shift_small.reshape(nd, 1, N_SMALL), dbuf,
      conv_w, gpar, normg, mu, mus, wup, aup, par, pool_w, pscale)


def _dfox_kernel(pt_ref, qkv_ref, qrep_ref, small_ref, bias_ref, *rest, n_grp):
    kt_refs, vt_refs, lf_refs = rest[0:n_grp], rest[n_grp:2 * n_grp], rest[2 * n_grp:3 * n_grp]
    o_ref, lf_out, m_s, l_s, w_s, r_s, acc_s = rest[3 * n_grp:]
    j = pl.program_id(1)
    scale = HEAD_DIM ** -0.5
    sub8 = _iota2((N_HEADS, LANES), 0)
    lane8 = _iota2((N_HEADS, LANES), 1)

    @pl.when(j == 0)
    def _():
        lf_new = _log_sigmoid(small_ref[0] + bias_ref[...])
        lf_out[0] = lf_new
        qb = (qkv_ref[0, 0] * scale).astype(BF16).astype(F32)
        kb = qkv_ref[0, 1].astype(BF16).astype(F32)
        m_s[...] = jnp.broadcast_to(jnp.sum(qb * kb, axis=1, keepdims=True), (N_HEADS, LANES))
        l_s[...] = jnp.ones_like(l_s)
        w_s[...] = jnp.ones_like(w_s)
        acc_s[...] = jnp.zeros_like(acc_s)
        mine = jnp.where(lane8 == sub8 + S_F, jnp.broadcast_to(lf_new, (N_HEADS, LANES)), 0.0)
        r_s[...] = jnp.broadcast_to(jnp.sum(mine, axis=1, keepdims=True), (N_HEADS, LANES))

    grp = range(n_grp)
    r_t = _iota2((PAGE_SIZE, 2 * LANES), 0)
    c_t = _iota2((PAGE_SIZE, 2 * LANES), 1)
    later_or_all = jnp.where((r_t > c_t) | (c_t >= LANES), 1.0, 0.0).astype(BF16)
    q_rep = qrep_ref[0]
    s_t = [jnp.sum(kt_refs[g][0, 0] * q_rep, axis=1) * scale for g in grp]
    gates = [_dot3_l(lf_refs[g][0, 0], later_or_all) for g in grp]
    logits, r_run = [], r_s[...]
    for g in grp:
        logits.append(s_t[g] + gates[g][:, 0:LANES] + r_run)
        r_run = r_run + gates[g][:, LANES:2 * LANES]
    m_old = m_s[...]
    m_grp = logits[0]
    for g in grp[1:]:
        m_grp = jnp.maximum(m_grp, logits[g])
    m_new = jnp.maximum(m_old, jnp.max(m_grp, axis=1, keepdims=True))
    alpha = jnp.exp(m_old - m_new)
    p = [jnp.exp(logits[g] - m_new) for g in grp]
    p_sum = p[0]
    for g in grp[1:]:
        p_sum = p_sum + p[g]
    for h in range(N_HEADS):
        acc = acc_s[h] * jnp.broadcast_to(alpha[h:h + 1, :], (HEAD_DIM, LANES))
        for g in grp:
            acc = acc + vt_refs[g][0, 0, h] * jnp.broadcast_to(p[g][h:h + 1, :], (HEAD_DIM, LANES))
        acc_s[h] = acc
    l_s[...] = alpha * l_s[...] + jnp.sum(p_sum, axis=1, keepdims=True)
    w_s[...] = alpha * w_s[...]
    m_s[...] = m_new
    r_s[...] = r_run

    @pl.when(j == pl.num_programs(1) - 1)
    def _():
        ones_b = jnp.ones((N_HEADS, LANES), BF16)
        sub = _iota2((N_HEADS, HEAD_DIM), 0)
        red = jnp.zeros((N_HEADS, HEAD_DIM), F32)
        for h in range(N_HEADS):
            hi, mid, lo = _split3(acc_s[h])
            tot = _mm_nt(ones_b, hi) + _mm_nt(ones_b, mid) + _mm_nt(ones_b, lo)
            red = red + jnp.where(sub == h, tot, 0.0)
        o_ref[0] = (red + w_s[...][:, 0:HEAD_DIM] * qkv_ref[0, 2]) / l_s[...][:, 0:HEAD_DIM]


def _fox_decode(page_table, qkv_s, small_s, bias_row, cache_k, cache_v, cache_lf, layer, n_grp):
    nd, n_pages = page_table.shape
    kt = jnp.transpose(cache_k, (0, 1, 3, 4, 2))
    vt = jnp.transpose(cache_v, (0, 1, 3, 4, 2))
    lft = jnp.transpose(cache_lf, (0, 1, 3, 2))
    qkv8 = qkv_s.reshape(nd, 3, N_HEADS, HEAD_DIM)
    q_rep = jnp.broadcast_to(qkv8[:, 0, :, :, None], (nd, N_HEADS, HEAD_DIM, LANES))
    page = lambda g: (lambda b, j, pt: (layer, pt[b, n_pages - 1 - (j * n_grp + g)], 0, 0, 0))
    page4 = lambda g: (lambda b, j, pt: (layer, pt[b, n_pages - 1 - (j * n_grp + g)], 0, 0))
    kv_spec = lambda g: pl.BlockSpec((1, 1, N_HEADS, HEAD_DIM, PAGE_SIZE), page(g))
    grid_spec = pltpu.PrefetchScalarGridSpec(
        num_scalar_prefetch=1,
        grid=(nd, n_pages // n_grp),
        in_specs=[pl.BlockSpec((1, 3, N_HEADS, HEAD_DIM), lambda b, j, pt: (b, 0, 0, 0)),
                  pl.BlockSpec((1, N_HEADS, HEAD_DIM, LANES), lambda b, j, pt: (b, 0, 0, 0)),
                  pl.BlockSpec((1, 1, N_SMALL), lambda b, j, pt: (b, 0, 0)),
                  pl.BlockSpec((1, N_SMALL), lambda b, j, pt: (0, 0))]
                 + [kv_spec(g) for g in range(n_grp)] + [kv_spec(g) for g in range(n_grp)]
                 + [pl.BlockSpec((1, 1, N_HEADS, PAGE_SIZE), page4(g)) for g in range(n_grp)],
        out_specs=[pl.BlockSpec((1, N_HEADS, HEAD_DIM), lambda b, j, pt: (b, 0, 0)),
                   pl.BlockSpec((1, 1, N_SMALL), lambda b, j, pt: (b, 0, 0))],
        scratch_shapes=[pltpu.VMEM((N_HEADS, LANES), F32)] * 4 + [pltpu.VMEM((N_HEADS, HEAD_DIM, LANES), F32)],
    )
    o, lf_new = pl.pallas_call(
        functools.partial(_dfox_kernel, n_grp=n_grp),
        grid_spec=grid_spec,
        out_shape=[jax.ShapeDtypeStruct((nd, N_HEADS, HEAD_DIM), F32), jax.ShapeDtypeStruct((nd, 1, N_SMALL), F32)],
        compiler_params=_cparams(("arbitrary", "arbitrary")),
        name="fox_decode",
    )(page_table, qkv8, q_rep, small_s.reshape(nd, 1, N_SMALL), bias_row,
      *([kt] * n_grp), *([vt] * n_grp), *([lft] * n_grp))
    return o.reshape(nd, W_BR), lf_new


_MAIN_SEGMENTS = ((O_QKV_A, 3 * W_BR), (O_P_B, 3 * W_BR), (O_QKV_C, 3 * W_BR), (O_Z_A, W_BR), (O_Z_B, W_BR),
                  (O_Z_C, W_BR), (O_U_D, W_BR), (O_Z_D, W_BR))
_SMALL_SEGMENTS = ((O_A_A, 2 * N_HEADS), (O_WL, 2 * LORA_B), (O_F_C, N_HEADS))


def _wprep_kernel(wt_ref, main_ref, small_ref):
    wt = wt_ref[0]
    main_ref[...] = jnp.concatenate([wt[a:a + n, :] for a, n in _MAIN_SEGMENTS], axis=0).T.astype(BF16)
    used = sum(n for _, n in _SMALL_SEGMENTS)
    small = [wt[a:a + n, :] for a, n in _SMALL_SEGMENTS] + [jnp.zeros((N_SMALL - used, wt.shape[1]), F32)]
    small_ref[...] = jnp.concatenate(small, axis=0).T.astype(BF16)


def _prep_w_in(w_in, layer, tr=256):
    _, d, d_in = w_in.shape
    return pl.pallas_call(
        _wprep_kernel,
        grid=(d // tr,),
        in_specs=[pl.BlockSpec((1, d_in, tr), lambda i: (layer, 0, i))],
        out_specs=[pl.BlockSpec((tr, N_MAIN), lambda i: (i, 0)), pl.BlockSpec((tr, N_SMALL), lambda i: (i, 0))],
        out_shape=[jax.ShapeDtypeStruct((d, N_MAIN), BF16), jax.ShapeDtypeStruct((d, N_SMALL), BF16)],
        compiler_params=_cparams(("arbitrary",)),
        name="w_in_prep",
    )(jnp.swapaxes(w_in, 1, 2))


def _small_row(pairs):
    row = jnp.zeros((N_SMALL,), F32)
    for off, vec in pairs:
        row = row.at[off:off + vec.shape[0]].set(vec.astype(F32))
    return row[None, :]


def _layer_params(l, w_in, conv_A, A_log, dt_bias, norm_A, mu_B, w0_B, w_up_B, a0_B, a_up_B, xi_B, alpha_B, rho_B,
                  gn_g_B, gn_b_B, b_f_C, pool_w_D, pool_scale_D, w_out, ln_g, ln_b):
    w_main, w_small = _prep_w_in(w_in, l)
    gpar = jnp.concatenate([_small_row([(S_A, -jnp.exp(A_log[l].astype(F32)))]), _small_row([(S_A, dt_bias[l])]),
                            jnp.zeros((6, N_SMALL), F32)], axis=0)
    head_of_lane = jnp.arange(W_BR) // HEAD_DIM
    lane = jnp.arange(N_SMALL)[:, None]
    eg = (lane == head_of_lane[None, :] + S_A).astype(BF16)
    eb = (lane == head_of_lane[None, :] + S_B).astype(BF16)
    mu = mu_B[l].astype(F32)
    wup = jnp.zeros((N_SMALL, W_BR), F32).at[S_WL:S_WL + LORA_B].set(w_up_B[l].astype(F32))
    aup = jnp.zeros((N_SMALL, W_BR), F32).at[S_AL:S_AL + LORA_B].set(a_up_B[l].astype(F32))
    bd = (head_of_lane[:, None] == head_of_lane[None, :]).astype(BF16)
    par = jnp.stack([w0_B[l], a0_B[l], xi_B[l], alpha_B[l], rho_B[l], gn_g_B[l], gn_b_B[l],
                     jnp.zeros((W_BR,), F32)]).astype(F32)
    return dict(
        w_main=w_main, w_small=w_small, bd=bd, conv_w=conv_A[l].astype(F32), gpar=gpar, eg=eg, eb=eb,
        normg=jnp.tile(norm_A[l].astype(F32), N_HEADS)[None, :],
        mu=mu[None, 0:3 * W_BR], mus=_small_row([(S_WL, mu[3 * W_BR:])]), wup=wup, aup=aup, par=par,
        fbias=_small_row([(S_F, b_f_C[l])]), pool_w=pool_w_D[l].astype(F32), pscale=pool_scale_D[l].astype(F32)[None, :],
        w_out=w_out[l].astype(BF16), ln_g=ln_g[l].astype(F32)[None, :], ln_b=ln_b[l].astype(F32)[None, :])


def _prompt_layer(x3, P, tb_rec, tq, tb_gate, tb_pool, tm_proj, tm_out):
    n, t, _ = x3.shape
    x2 = x3.reshape(n * t, D_MODEL)
    main2, small2 = _proj(x2, P['w_main'], P['w_small'], tm_proj)
    main3 = main2.reshape(n, t, N_MAIN)
    small3 = small2.reshape(n, t, N_SMALL)
    o_a, s_a = _gdn_prompt(main3, small3, P['conv_w'], P['gpar'], P['eg'], P['eb'], P['normg'], P['bd'], tb_rec)
    o_b, s_b = _rwkv_prompt(main3, small3, P['mu'], P['mus'], _bf(P['wup']), _bf(P['aup']), P['par'], P['bd'],
                            tb_rec)
    logf3, c3, ct3 = _gates(small3, P['fbias'], tb_gate)
    o_c = _fox_prompt(main3, c3, ct3, tq)
    o_d = _pool_prompt(main3, P['pool_w'], P['pscale'], tb_pool)
    flat = lambda a: a.reshape(n * t, W_BR)
    y2 = _out_proj(flat(o_a), flat(o_b), flat(o_c), flat(o_d), main2, x2, P['w_out'], P['ln_g'], P['ln_b'], tm_out)
    new = (s_a,
           main3[:, t - (CONV_W - 1):, C_QKV_A:C_QKV_A + 3 * W_BR],
           s_b,
           jnp.concatenate([main3[:, t - 1, C_RKV_B:C_RKV_B + 3 * W_BR], small3[:, t - 1, S_WL:S_WL + 2 * LORA_B]], axis=-1),
           main3[:, :, C_QKV_C + W_BR:C_QKV_C + 2 * W_BR].reshape(n, t, N_HEADS, HEAD_DIM),
           main3[:, :, C_QKV_C + 2 * W_BR:C_QKV_C + 3 * W_BR].reshape(n, t, N_HEADS, HEAD_DIM),
           logf3[:, :, S_F:S_F + N_HEADS],
           main3[:, t - POOL_BUF:, C_U_D:C_U_D + W_BR])
    return y2.reshape(n, t, D_MODEL), new


def _sample_layer(x3, st, cache, layer, page_table, P):
    nd = x3.shape[0]
    st_a, conv_a, st_b, shift_b, dbuf = st
    cache_k, cache_v, cache_lf = cache
    x2 = x3.reshape(nd, D_MODEL)
    main_s, small_s = _proj(x2, P['w_main'], P['w_small'], nd)
    shift_small = jnp.zeros((nd, N_SMALL), F32).at[:, S_WL:S_WL + 2 * LORA_B].set(shift_b[:, 3 * W_BR:].astype(F32))
    o_a, o_b, o_d, s_a, conv_n, s_b, dbuf_n = _decode_mixers(
        main_s, small_s, st_a, conv_a, st_b, shift_b[:, 0:3 * W_BR], shift_small, dbuf,
        P['conv_w'], P['gpar'], P['normg'], P['mu'], P['mus'], P['wup'], P['aup'], P['par'], P['pool_w'], P['pscale'],
        page_table.shape[1] * PAGE_SIZE)
    qkv_s = main_s[:, C_QKV_C:C_QKV_C + 3 * W_BR]
    n_grp = math.gcd(page_table.shape[1], 8)
    o_c, lf_new = _fox_decode(page_table, qkv_s, small_s, P['fbias'], cache_k, cache_v, cache_lf, layer, n_grp)
    flat = lambda a: a.reshape(nd, W_BR)
    y2 = _out_proj(flat(o_a), flat(o_b), flat(o_c), flat(o_d), main_s, x2, P['w_out'], P['ln_g'], P['ln_b'], nd)
    new = (s_a, conv_n, s_b,
           jnp.concatenate([main_s[:, C_RKV_B:C_RKV_B + 3 * W_BR], small_s[:, S_WL:S_WL + 2 * LORA_B]], axis=-1),
           qkv_s[:, W_BR:2 * W_BR].reshape(nd, 1, N_HEADS, HEAD_DIM),
           qkv_s[:, 2 * W_BR:3 * W_BR].reshape(nd, 1, N_HEADS, HEAD_DIM),
           lf_new[:, :, S_F:S_F + N_HEADS],
           dbuf_n)
    return y2.reshape(nd, 1, D_MODEL), new


def kernel(x_prompt, x_sample, state_A_S, state_A_conv, state_B_S, state_B_shift, cache_C_k, cache_C_v, cache_C_logf, state_D_buf, page_table, w_in, conv_A, A_log, dt_bias, norm_A, mu_B, w0_B, w_up_B, a0_B, a_up_B, xi_B, alpha_B, rho_B, gn_g_B, gn_b_B, b_f_C, pool_w_D, pool_scale_D, w_out, ln_g, ln_b):
    depth = w_in.shape[0]
    t = x_prompt.shape[1]
    tb_rec = min(256, t)
    tq = min(256, t)
    tb_gate = min(512, t)
    tb_pool = min(512, t)
    tm_proj = min(512, x_prompt.shape[0] * t)
    tm_out = min(256, x_prompt.shape[0] * t)
    y_p, y_s = x_prompt, x_sample
    prompt_new, sample_new = [], []
    for l in range(depth):
        P = _layer_params(l, w_in, conv_A, A_log, dt_bias, norm_A, mu_B, w0_B, w_up_B, a0_B, a_up_B, xi_B, alpha_B,
                          rho_B, gn_g_B, gn_b_B, b_f_C, pool_w_D, pool_scale_D, w_out, ln_g, ln_b)
        y_p, new_p = _prompt_layer(y_p, P, tb_rec, tq, tb_gate, tb_pool, tm_proj, tm_out)
        st = (state_A_S[l], state_A_conv[l], state_B_S[l], state_B_shift[l], state_D_buf[l])
        y_s, new_s = _sample_layer(y_s, st, (cache_C_k, cache_C_v, cache_C_logf), l, page_table, P)
        prompt_new.append(new_p)
        sample_new.append(new_s)
    p_out = [jnp.stack([n[i] for n in prompt_new]) for i in range(8)]
    s_out = [jnp.stack([n[i] for n in sample_new]) for i in range(8)]
    p_a_s, p_a_conv, p_b_s, p_b_shift, p_c_k, p_c_v, p_c_logf, p_d_buf = p_out
    s_a_s, s_a_conv, s_b_s, s_b_shift, s_c_k, s_c_v, s_c_logf, s_d_buf = s_out
    return (y_p, y_s, p_a_s, p_a_conv, p_b_s, p_b_shift, p_c_k, p_c_v, p_c_logf, p_d_buf,
            s_a_s, s_a_conv, s_b_s, s_b_shift, s_c_k, s_c_v, s_c_logf, s_d_buf)
```

```python
import functools
import math

import jax
import jax.numpy as jnp
from jax import lax
from jax.experimental import pallas as pl
from jax.experimental.pallas import tpu as pltpu

F32 = jnp.float32
BF16 = jnp.bfloat16
HI = lax.Precision.HIGHEST

D_MODEL = 2048
W_BR = 512
HEAD_DIM = 64
N_HEADS = W_BR // HEAD_DIM
CONV_W = 4
CHUNK = 64
LORA_B = 32
POOL_WINDOWS = (2, 4, 8, 16)
POOL_GW = W_BR // len(POOL_WINDOWS)
POOL_BUF = max(POOL_WINDOWS) - 1
PAGE_SIZE = 128
LANES = 128
DEPTH = 2
ALPHA_DN = (2.0 * DEPTH) ** 0.25
LN_EPS = 1e-5
GN_EPS = 64e-5
RMS_EPS = 1e-6
L2_EPS = 1e-6
RWKV_DECAY_SCALE = math.exp(-0.5)
NEG_INF = -1e30

C_QKV_A, C_RKV_B, C_QKV_C = 0, 1536, 3072
C_Z_A, C_Z_B, C_Z_C, C_U_D, C_Z_D = 4608, 5120, 5632, 6144, 6656
N_MAIN = 7168
S_A, S_B, S_WL, S_AL, S_F = 0, 8, 16, 48, 80
N_SMALL = 128
O_QKV_A, O_A_A, O_Z_A, O_P_B, O_WL, O_Z_B, O_QKV_C, O_F_C, O_Z_C, O_U_D, O_Z_D, D_IN = (
    0, 1536, 1552, 2064, 3600, 3664, 4176, 5712, 5720, 6232, 6744, 7256)

VMEM_LIMIT = 48 * 1024 * 1024


def _cparams(sem):
    return pltpu.CompilerParams(dimension_semantics=sem, vmem_limit_bytes=VMEM_LIMIT)


def _mm(a, b, prec=None):
    return jnp.dot(a, b, preferred_element_type=F32, precision=prec)


def _mm_nt(a, b, prec=None):
    return lax.dot_general(a, b, (((1,), (1,)), ((), ())), preferred_element_type=F32, precision=prec)


def _mm_tn(a, b, prec=None):
    return lax.dot_general(a, b, (((0,), (0,)), ((), ())), preferred_element_type=F32, precision=prec)


def _split3(x):
    hi = x.astype(BF16)
    r1 = x - hi.astype(F32)
    mid = r1.astype(BF16)
    lo = (r1 - mid.astype(F32)).astype(BF16)
    return hi, mid, lo


def _dot3_l(x, b_bf16):
    hi, mid, lo = _split3(x)
    return _mm(hi, b_bf16) + _mm(mid, b_bf16) + _mm(lo, b_bf16)


def _dot2_l(x, b_bf16):
    hi = x.astype(BF16)
    lo = (x - hi.astype(F32)).astype(BF16)
    return _mm(hi, b_bf16) + _mm(lo, b_bf16)


def _dot3_r(a_bf16, x):
    hi, mid, lo = _split3(x)
    return _mm(a_bf16, hi) + _mm(a_bf16, mid) + _mm(a_bf16, lo)


def _sigmoid(x):
    return 1.0 / (1.0 + jnp.exp(-x))


def _silu(x):
    return x * _sigmoid(x)


def _softplus(x):
    return jnp.maximum(x, 0.0) + jnp.log1p(jnp.exp(-jnp.abs(x)))


def _log_sigmoid(x):
    return -_softplus(-x)


def _iota2(shape, dim):
    return lax.broadcasted_iota(jnp.int32, shape, dim)


def _head_block_diag():
    r = _iota2((W_BR, W_BR), 0) // HEAD_DIM
    c = _iota2((W_BR, W_BR), 1) // HEAD_DIM
    return jnp.where(r == c, 1.0, 0.0).astype(BF16)


def _chunk_tril(n):
    r = _iota2((n, n), 0)
    c = _iota2((n, n), 1)
    return jnp.where((r >= c) & (r // CHUNK == c // CHUNK), 1.0, 0.0).astype(BF16)


_DN_NN = (((1,), (0,)), ((), ()))
_DN_NT = (((1,), (1,)), ((), ()))
_DN_TN = (((0,), (0,)), ((), ()))


def _bf(x):
    return x.astype(BF16)


def _mmb(a, b, dn=_DN_NN):
    return lax.dot_general(a, b, dn, preferred_element_type=F32)


def _tri_solve(a_list, rhs_list, blk):
    each = lambda f, *ls: [f(*xs) for xs in zip(*ls)]
    width = rhs_list[0].shape[1]
    ad = each(lambda a: a * blk, a_list)
    adb = each(_bf, ad)
    a2b = each(_bf, each(_mmb, adb, adb))
    a4b = each(_bf, each(_mmb, a2b, a2b))
    a8b = each(_bf, each(_mmb, a4b, a4b))
    z = each(lambda r, a, d: jnp.concatenate([r, a - d], axis=1), rhs_list, a_list, ad)
    z = each(lambda x, y: x - y, z, each(_mmb, adb, each(_bf, z)))
    for pw in (a2b, a4b, a8b):
        z = each(lambda x, y: x + y, z, each(_mmb, pw, each(_bf, z)))
    y = each(lambda x: x[:, 0:width], z)
    nb = each(lambda x: _bf(x[:, width:width + CHUNK]), z)
    n2b = each(_bf, each(_mmb, nb, nb))
    t = each(lambda x, u: x + u, y, each(_mmb, n2b, each(_bf, y)))
    return each(lambda x, u: x - u, t, each(_mmb, nb, each(_bf, t)))


def _chunk_masks():
    r = _iota2((CHUNK, CHUNK), 0)
    c = _iota2((CHUNK, CHUNK), 1)
    eye = jnp.where(r == c, 1.0, 0.0).astype(F32)
    strict = jnp.where(r > c, 1.0, 0.0).astype(F32)
    incl = jnp.where(r >= c, 1.0, 0.0).astype(F32)
    blk = jnp.where(r // 16 == c // 16, 1.0, 0.0).astype(F32)
    return eye, strict, incl, blk


def _proj_kernel(x_ref, w_ref, ws_ref, main_ref, small_ref, xb_ref):
    @pl.when(pl.program_id(1) == 0)
    def _():
        xb = x_ref[...].astype(BF16)
        xb_ref[...] = xb
        small_ref[...] = _mm(xb, ws_ref[...])

    main_ref[...] = _mm(xb_ref[...], w_ref[...])


def _proj(x2d, w_main, w_small, tm, tn=1024):
    m = x2d.shape[0]
    return pl.pallas_call(
        _proj_kernel,
        grid=(m // tm, N_MAIN // tn),
        in_specs=[pl.BlockSpec((tm, D_MODEL), lambda i, j: (i, 0)),
                  pl.BlockSpec((D_MODEL, tn), lambda i, j: (0, j)),
                  pl.BlockSpec((D_MODEL, N_SMALL), lambda i, j: (0, 0))],
        out_specs=[pl.BlockSpec((tm, tn), lambda i, j: (i, j)),
                   pl.BlockSpec((tm, N_SMALL), lambda i, j: (i, 0))],
        out_shape=[jax.ShapeDtypeStruct((m, N_MAIN), F32), jax.ShapeDtypeStruct((m, N_SMALL), F32)],
        scratch_shapes=[pltpu.VMEM((tm, D_MODEL), BF16)],
        compiler_params=_cparams(("arbitrary", "arbitrary")),
        name="proj",
    )(x2d, w_main, w_small)


def _gates_kernel(small_ref, bias_ref, logf_ref, c_ref, ct_ref, carry_ref, *, tb):
    @pl.when(pl.program_id(1) == 0)
    def _():
        carry_ref[...] = jnp.zeros_like(carry_ref)

    logf = _log_sigmoid(small_ref[0] + bias_ref[...])
    r = _iota2((tb, tb), 0)
    c = _iota2((tb, tb), 1)
    tril = jnp.where(r >= c, 1.0, 0.0).astype(BF16)
    cum = _dot3_r(tril, logf) + carry_ref[0:1, :]
    carry_ref[...] = jnp.broadcast_to(cum[tb - 1:tb, :], carry_ref.shape)
    logf_ref[0] = logf
    c_ref[0] = cum
    ct_ref[0] = cum.T[S_F:S_F + N_HEADS, :]


def _gates(small3, bias_row, tb):
    n, t, _ = small3.shape
    return pl.pallas_call(
        functools.partial(_gates_kernel, tb=tb),
        grid=(n, t // tb),
        in_specs=[pl.BlockSpec((1, tb, N_SMALL), lambda b, j: (b, j, 0)),
                  pl.BlockSpec((1, N_SMALL), lambda b, j: (0, 0))],
        out_specs=[pl.BlockSpec((1, tb, N_SMALL), lambda b, j: (b, j, 0)),
                   pl.BlockSpec((1, tb, N_SMALL), lambda b, j: (b, j, 0)),
                   pl.BlockSpec((1, N_HEADS, tb), lambda b, j: (b, 0, j))],
        out_shape=[jax.ShapeDtypeStruct((n, t, N_SMALL), F32), jax.ShapeDtypeStruct((n, t, N_SMALL), F32),
                   jax.ShapeDtypeStruct((n, N_HEADS, t), F32)],
        scratch_shapes=[pltpu.VMEM((8, N_SMALL), F32)],
        compiler_params=_cparams(("arbitrary", "arbitrary")),
        name="fox_gates",
    )(small3, bias_row)


def _fox_kernel(qi_ref, kj_ref, q_ref, k_ref, v_ref, c_ref, ct_ref, o_ref, m_s, l_s, acc_s, cq_s, *, tq):
    i = qi_ref[pl.program_id(1)]
    j = kj_ref[pl.program_id(1)]
    n_pairs = W_BR // LANES
    lo_half = _iota2((tq, LANES), 1) < HEAD_DIM
    pair_lanes = [slice(pr * LANES, (pr + 1) * LANES) for pr in range(n_pairs)]

    @pl.when(j == 0)
    def _():
        m_s[...] = jnp.full(m_s.shape, NEG_INF, F32)
        l_s[...] = jnp.zeros_like(l_s)
        acc_s[...] = jnp.zeros_like(acc_s)
        cq = c_ref[0]
        for h in range(N_HEADS):
            cq_s[h] = jnp.broadcast_to(cq[:, S_F + h:S_F + h + 1], (tq, LANES))

    def step(masked):
        q = q_ref[0] * (HEAD_DIM ** -0.5)
        k = k_ref[0].astype(BF16)
        v = v_ref[0]
        ct = ct_ref[0]
        wide = lambda x: jnp.concatenate([x] * (tq // LANES), axis=1)
        if masked:
            keep = _iota2((tq, tq), 0) >= _iota2((tq, tq), 1)
        s_all = []
        for ps in pair_lanes:
            qp, kp = q[:, ps], k[:, ps]
            s_all.append(_mm_nt(jnp.where(lo_half, qp, 0.0).astype(BF16), kp))
            s_all.append(_mm_nt(jnp.where(lo_half, 0.0, qp).astype(BF16), kp))
        p_all, alpha_all = [], []
        for h in range(N_HEADS):
            s = s_all[h] + wide(cq_s[h]) - ct[h:h + 1, :]
            if masked:
                s = jnp.where(keep, s, NEG_INF)
            m_old = m_s[h]
            m_new = jnp.maximum(m_old, jnp.max(s, axis=1, keepdims=True))
            alpha = jnp.exp(m_old - m_new)
            p = jnp.exp(s - wide(m_new))
            l_s[h] = alpha * l_s[h] + jnp.sum(p, axis=1, keepdims=True)
            m_s[h] = m_new
            p_all.append(p.astype(BF16))
            alpha_all.append(alpha)
        for pr, ps in enumerate(pair_lanes):
            vp = v[:, ps]
            v_bd = jnp.concatenate([jnp.where(lo_half, vp, 0.0), jnp.where(lo_half, 0.0, vp)], axis=0).astype(BF16)
            p_pair = jnp.concatenate([p_all[2 * pr], p_all[2 * pr + 1]], axis=1)
            alpha_p = jnp.where(lo_half, alpha_all[2 * pr], alpha_all[2 * pr + 1])
            acc_s[:, ps] = alpha_p * acc_s[:, ps] + _mm(p_pair, v_bd)

    @pl.when(j < i)
    def _():
        step(False)

    @pl.when(j == i)
    def _():
        step(True)
        for pr, ps in enumerate(pair_lanes):
            o_ref[0, :, ps] = acc_s[:, ps] / jnp.where(lo_half, l_s[2 * pr], l_s[2 * pr + 1])


def _fox_prompt(main3, c3, ct3, tq):
    n, t, _ = main3.shape
    nb = t // tq
    qb, kb, vb = C_QKV_C // W_BR, C_QKV_C // W_BR + 1, C_QKV_C // W_BR + 2
    pairs = [(i, j) for i in range(nb) for j in range(i + 1)]
    qi = jnp.asarray([p[0] for p in pairs], jnp.int32)
    kj = jnp.asarray([p[1] for p in pairs], jnp.int32)
    grid_spec = pltpu.PrefetchScalarGridSpec(
        num_scalar_prefetch=2,
        grid=(n, len(pairs)),
        in_specs=[pl.BlockSpec((1, tq, W_BR), lambda b, s, qi, kj: (b, qi[s], qb)),
                  pl.BlockSpec((1, tq, W_BR), lambda b, s, qi, kj: (b, kj[s], kb)),
                  pl.BlockSpec((1, tq, W_BR), lambda b, s, qi, kj: (b, kj[s], vb)),
                  pl.BlockSpec((1, tq, N_SMALL), lambda b, s, qi, kj: (b, qi[s], 0)),
                  pl.BlockSpec((1, N_HEADS, tq), lambda b, s, qi, kj: (b, 0, kj[s]))],
        out_specs=pl.BlockSpec((1, tq, W_BR), lambda b, s, qi, kj: (b, qi[s], 0)),
        scratch_shapes=[pltpu.VMEM((N_HEADS, tq, LANES), F32), pltpu.VMEM((N_HEADS, tq, LANES), F32),
                        pltpu.VMEM((tq, W_BR), F32), pltpu.VMEM((N_HEADS, tq, LANES), F32)],
    )
    return pl.pallas_call(
        functools.partial(_fox_kernel, tq=tq),
        grid_spec=grid_spec,
        out_shape=jax.ShapeDtypeStruct((n, t, W_BR), F32),
        compiler_params=_cparams(("arbitrary", "arbitrary")),
        name="fox_prompt",
    )(qi, kj, main3, main3, main3, c3, ct3)


def _gdn_kernel(qkv_ref, small_ref, convw_ref, gpar_ref, eg_ref, eb_ref, normg_ref, bd_ref,
                o_ref, s_out_ref, ext_s, q_s, k_s, v_s, b_s, g_s, o_s, st_s, *, tb):
    j = pl.program_id(1)

    @pl.when(j == 0)
    def _():
        ext_s[0:8, :] = jnp.zeros((8, 3 * W_BR), F32)
        st_s[...] = jnp.zeros_like(st_s)

    u = qkv_ref[0]
    ext_s[8:8 + tb, :] = u
    cw = convw_ref[...]
    c = (ext_s[5:5 + tb, :] * cw[0:1, :] + ext_s[6:6 + tb, :] * cw[1:2, :]
         + ext_s[7:7 + tb, :] * cw[2:3, :] + u * cw[3:4, :])
    ext_s[0:8, :] = u[tb - 8:tb, :]
    c = _silu(c)
    bd = bd_ref[...]
    q = c[:, 0:W_BR]
    k = c[:, W_BR:2 * W_BR]
    q_s[...] = q * lax.rsqrt(_dot2_l(q * q, bd) + L2_EPS) * (HEAD_DIM ** -0.5)
    k_s[...] = k * lax.rsqrt(_dot2_l(k * k, bd) + L2_EPS)
    v_s[...] = c[:, 2 * W_BR:3 * W_BR]

    sm = small_ref[0]
    gpar = gpar_ref[...]
    g = gpar[0:1, :] * _softplus(sm + gpar[1:2, :])
    beta = _sigmoid(sm)
    gcum = _dot3_r(_chunk_tril(tb), g)
    g_s[...] = _dot3_l(gcum, eg_ref[...])
    b_s[...] = _dot3_l(beta, eb_ref[...])

    _, strict, incl, blk = _chunk_masks()
    ones_b = jnp.ones((CHUNK, CHUNK), BF16)
    eye_heads = jnp.where(_iota2((CHUNK, W_BR), 0) == _iota2((CHUNK, W_BR), 1) % HEAD_DIM, 1.0, 0.0)

    heads = range(N_HEADS)
    lanes = [slice(h * HEAD_DIM, (h + 1) * HEAD_DIM) for h in heads]
    n_chunks = tb // CHUNK
    group = 2

    ops = {}
    for c0 in range(0, n_chunks, group):
        chunks = range(c0, min(c0 + group, n_chunks))
        items = [(c, h) for c in chunks for h in heads]
        rows = lambda c: slice(c * CHUNK, (c + 1) * CHUNK)
        tile = lambda ref: [ref[rows(c), lanes[h]] for c, h in items]
        q_h, k_h, v_h, b_h, g_h = tile(q_s), tile(k_s), tile(v_s), tile(b_s), tile(g_s)
        grow_c = {c: _dot3_r(ones_b, g_s[rows(c), :] * eye_heads) for c in chunks}
        grow = [grow_c[c][:, lanes[h]] for c, h in items]
        idx = range(len(items))
        decay = [jnp.exp(jnp.where(incl > 0, g_h[i] - grow[i], NEG_INF)) for i in idx]
        kb = [k_h[i] * b_h[i] for i in idx]
        both = [_mmb(_bf(jnp.concatenate([kb[i], q_h[i]], axis=0)), _bf(k_h[i]), _DN_NT) for i in idx]
        a = [both[i][0:CHUNK, :] * decay[i] * strict for i in idx]
        attn = [_bf(both[i][CHUNK:2 * CHUNK, :] * decay[i]) for i in idx]
        eg = [jnp.exp(g_h[i]) for i in idx]
        sol = _tri_solve(a, [jnp.concatenate([v_h[i] * b_h[i], kb[i] * eg[i]], axis=1) for i in idx], blk)
        ub = [_bf(sol[i][:, 0:HEAD_DIM]) for i in idx]
        wb = [_bf(sol[i][:, HEAD_DIM:2 * HEAD_DIM]) for i in idx]
        o1 = [_bf(q_h[i] * eg[i] - _mmb(attn[i], wb[i])) for i in idx]
        o2 = [_mmb(attn[i], ub[i]) for i in idx]
        glast = [g_h[i][CHUNK - 1:CHUNK, :] for i in idx]
        kdec = [_bf(k_h[i] * jnp.exp(glast[i] - g_h[i])) for i in idx]
        m = [_bf(_mmb(kdec[i], wb[i], _DN_TN)) for i in idx]
        cc = [_mmb(kdec[i], ub[i], _DN_TN) for i in idx]
        for i, key in enumerate(items):
            ops[key] = (o1[i], o2[i], m[i], cc[i], jnp.exp(glast[i]))

    sts = [st_s[h] for h in heads]
    for c in range(n_chunks):
        stb = [_bf(sts[h]) for h in heads]
        outs = [_mmb(ops[c, h][0], stb[h]) + ops[c, h][1] for h in heads]
        sts = [sts[h] * ops[c, h][4] - _mmb(ops[c, h][2], stb[h]) + ops[c, h][3] for h in heads]
        o_s[c * CHUNK:(c + 1) * CHUNK, :] = jnp.concatenate(outs, axis=1)
    for h in heads:
        st_s[h] = sts[h]

    o = o_s[...]
    ms = _dot2_l(o * o, bd) * (1.0 / HEAD_DIM)
    o_ref[0] = o * lax.rsqrt(ms + RMS_EPS) * normg_ref[...]

    @pl.when(j == pl.num_programs(1) - 1)
    def _():
        s_out_ref[0] = st_s[...]


def _gdn_prompt(main3, small3, conv_w, gpar, eg, eb, normg, bd, tb):
    n, t, _ = main3.shape
    full = lambda shape: pl.BlockSpec(shape, lambda b, j: (0,) * len(shape))
    return pl.pallas_call(
        functools.partial(_gdn_kernel, tb=tb),
        grid=(n, t // tb),
        in_specs=[pl.BlockSpec((1, tb, 3 * W_BR), lambda b, j: (b, j, C_QKV_A // (3 * W_BR))),
                  pl.BlockSpec((1, tb, N_SMALL), lambda b, j: (b, j, 0)),
                  full((CONV_W, 3 * W_BR)), full((8, N_SMALL)), full((N_SMALL, W_BR)), full((N_SMALL, W_BR)),
                  full((1, W_BR)), full((W_BR, W_BR))],
        out_specs=[pl.BlockSpec((1, tb, W_BR), lambda b, j: (b, j, 0)),
                   pl.BlockSpec((1, N_HEADS, HEAD_DIM, HEAD_DIM), lambda b, j: (b, 0, 0, 0))],
        out_shape=[jax.ShapeDtypeStruct((n, t, W_BR), F32),
                   jax.ShapeDtypeStruct((n, N_HEADS, HEAD_DIM, HEAD_DIM), F32)],
        scratch_shapes=[pltpu.VMEM((tb + 8, 3 * W_BR), F32)] + [pltpu.VMEM((tb, W_BR), F32)] * 6
                       + [pltpu.VMEM((N_HEADS, HEAD_DIM, HEAD_DIM), F32)],
        compiler_params=_cparams(("arbitrary", "arbitrary")),
        name="gdn_prompt",
    )(main3, small3, conv_w, gpar, eg, eb, normg, bd)


def _rwkv_kernel(p_ref, small_ref, mu_ref, mus_ref, wup_ref, aup_ref, par_ref, bd_ref,
                 o_ref, s_out_ref, ext_s, exts_s, ah_s, bh_s, kh_s, rh_s, be_s, ke_s, v_s, wl_s, y_s, st_s, *, tb):
    j = pl.program_id(1)

    @pl.when(j == 0)
    def _():
        ext_s[0:8, :] = jnp.zeros((8, 3 * W_BR), F32)
        exts_s[0:8, :] = jnp.zeros((8, N_SMALL), F32)
        st_s[...] = jnp.zeros_like(st_s)

    p = p_ref[0]
    sm = small_ref[0]
    ext_s[8:8 + tb, :] = p
    exts_s[8:8 + tb, :] = sm
    prev = ext_s[7:7 + tb, :]
    prevs = exts_s[7:7 + tb, :]
    ext_s[0:8, :] = p[tb - 8:tb, :]
    exts_s[0:8, :] = sm[tb - 8:tb, :]
    ps = p + (prev - p) * mu_ref[...]
    pss = sm + (prevs - sm) * mus_ref[...]
    r = ps[:, 0:W_BR]
    k = ps[:, W_BR:2 * W_BR]
    v = ps[:, 2 * W_BR:3 * W_BR]
    par = par_ref[...]
    w0, a0, xi, alpha, rho, gn_g, gn_b = (par[i:i + 1, :] for i in range(7))
    d = w0 + _mm(_bf(jnp.tanh(pss)), wup_ref[...])
    logw = -RWKV_DECAY_SCALE * _sigmoid(d)
    a = _sigmoid(a0 + _mm(_bf(pss), aup_ref[...]))
    bd = bd_ref[...]
    kx = k * xi
    kk = kx * lax.rsqrt(_dot2_l(kx * kx, bd) + L2_EPS)
    k2 = k * (1.0 + (a - 1.0) * alpha)
    lc = _dot3_r(_chunk_tril(tb), logw)
    nb = -(a * kk)
    ah_s[...] = _bf(kk * jnp.exp(lc - logw))
    bh_s[...] = _bf(nb * jnp.exp(-lc))
    kh_s[...] = _bf(k2 * jnp.exp(-lc))
    rh_s[...] = _bf(r * jnp.exp(lc))
    v_s[...] = _bf(v)
    ll = jnp.concatenate([jnp.broadcast_to(lc[c * CHUNK + CHUNK - 1:(c + 1) * CHUNK, :], (CHUNK, W_BR))
                          for c in range(tb // CHUNK)], axis=0)
    to_end = jnp.exp(ll - lc)
    be_s[...] = _bf(nb * to_end)
    ke_s[...] = _bf(k2 * to_end)
    wl_s[...] = jnp.exp(ll)

    _, strict, incl, blk = _chunk_masks()
    incl2 = jnp.concatenate([incl, incl], axis=1)

    heads = range(N_HEADS)
    lanes = [slice(h * HEAD_DIM, (h + 1) * HEAD_DIM) for h in heads]
    n_chunks = tb // CHUNK
    group = 2

    ops = {}
    for c0 in range(0, n_chunks, group):
        chunks = range(c0, min(c0 + group, n_chunks))
        items = [(c, h) for c in chunks for h in heads]
        rows = lambda c: slice(c * CHUNK, (c + 1) * CHUNK)
        tile = lambda ref: [ref[rows(c), lanes[h]] for c, h in items]
        a_h, b_h, k_h, r_h, v_h, be_h, ke_h = (tile(x) for x in (ah_s, bh_s, kh_s, rh_s, v_s, be_s, ke_s))
        idx = range(len(items))
        gram = [_mmb(jnp.concatenate([a_h[i], r_h[i]], axis=0), jnp.concatenate([b_h[i], k_h[i]], axis=0), _DN_NT)
                for i in idx]
        akv = [_mmb(_bf(gram[i][0:CHUNK, CHUNK:2 * CHUNK] * strict), v_h[i]) for i in idx]
        sol = _tri_solve([-(gram[i][0:CHUNK, 0:CHUNK] * strict) for i in idx],
                         [jnp.concatenate([a_h[i].astype(F32), akv[i]], axis=1) for i in idx], blk)
        w1b = [_bf(sol[i][:, 0:HEAD_DIM]) for i in idx]
        w2v = [jnp.concatenate([_bf(sol[i][:, HEAD_DIM:2 * HEAD_DIM]), v_h[i]], axis=0) for i in idx]
        r_bk = [_bf(gram[i][CHUNK:2 * CHUNK, :] * incl2) for i in idx]
        g1 = [_bf(r_h[i].astype(F32) + _mmb(r_bk[i][:, 0:CHUNK], w1b[i])) for i in idx]
        g2 = [_mmb(r_bk[i], w2v[i]) for i in idx]
        m1 = [_bf(_mmb(w1b[i], be_h[i], _DN_TN)) for i in idx]
        c2 = [_mmb(w2v[i], jnp.concatenate([be_h[i], ke_h[i]], axis=0), _DN_TN) for i in idx]
        for i, (c, h) in enumerate(items):
            ops[c, h] = (g1[i], g2[i], m1[i], c2[i], wl_s[c * CHUNK:c * CHUNK + 1, lanes[h]])

    sts = [st_s[h] for h in heads]
    for c in range(n_chunks):
        stb = [_bf(sts[h]) for h in heads]
        outs = [_mmb(ops[c, h][0], stb[h], _DN_NT) + ops[c, h][1] for h in heads]
        sts = [sts[h] * ops[c, h][4] + _mmb(stb[h], ops[c, h][2]) + ops[c, h][3] for h in heads]
        y_s[c * CHUNK:(c + 1) * CHUNK, :] = jnp.concatenate(outs, axis=1)
    for h in heads:
        st_s[h] = sts[h]

    y = y_s[...]
    mean = _dot2_l(y, bd) * (1.0 / HEAD_DIM)
    yc = y - mean
    var = _dot2_l(yc * yc, bd) * (1.0 / HEAD_DIM)
    yn = yc * lax.rsqrt(var + GN_EPS) * gn_g + gn_b
    bonus = _dot2_l(r * k2 * rho, bd) * v
    o_ref[0] = yn + bonus

    @pl.when(j == pl.num_programs(1) - 1)
    def _():
        s_out_ref[0] = st_s[...]


def _rwkv_prompt(main3, small3, mu, mus, wup, aup, par, bd, tb):
    n, t, _ = main3.shape
    full = lambda shape: pl.BlockSpec(shape, lambda b, j: (0,) * len(shape))
    return pl.pallas_call(
        functools.partial(_rwkv_kernel, tb=tb),
        grid=(n, t // tb),
        in_specs=[pl.BlockSpec((1, tb, 3 * W_BR), lambda b, j: (b, j, C_RKV_B // (3 * W_BR))),
                  pl.BlockSpec((1, tb, N_SMALL), lambda b, j: (b, j, 0)),
                  full((1, 3 * W_BR)), full((1, N_SMALL)), full((N_SMALL, W_BR)), full((N_SMALL, W_BR)),
                  full((8, W_BR)), full((W_BR, W_BR))],
        out_specs=[pl.BlockSpec((1, tb, W_BR), lambda b, j: (b, j, 0)),
                   pl.BlockSpec((1, N_HEADS, HEAD_DIM, HEAD_DIM), lambda b, j: (b, 0, 0, 0))],
        out_shape=[jax.ShapeDtypeStruct((n, t, W_BR), F32),
                   jax.ShapeDtypeStruct((n, N_HEADS, HEAD_DIM, HEAD_DIM), F32)],
        scratch_shapes=[pltpu.VMEM((tb + 8, 3 * W_BR), F32), pltpu.VMEM((tb + 8, N_SMALL), F32)]
                       + [pltpu.VMEM((tb, W_BR), BF16)] * 7 + [pltpu.VMEM((tb, W_BR), F32)] * 2
                       + [pltpu.VMEM((N_HEADS, HEAD_DIM, HEAD_DIM), F32)],
        compiler_params=_cparams(("arbitrary", "arbitrary")),
        name="rwkv_prompt",
    )(main3, small3, mu, mus, wup, aup, par, bd)


def _pool_kernel(u_ref, w_ref, scale_ref, o_ref, ext_s, *, tb):
    j = pl.program_id(1)

    @pl.when(j == 0)
    def _():
        ext_s[0:16, :] = jnp.zeros((16, W_BR), F32)

    u = u_ref[0]
    ext_s[16:16 + tb, :] = u
    pos = j * tb + _iota2((tb, POOL_GW), 0)
    outs = []
    for gi, wdw in enumerate(POOL_WINDOWS):
        ls = slice(gi * POOL_GW, (gi + 1) * POOL_GW)
        s = u[:, ls]
        for sh in range(1, wdw):
            s = s + ext_s[16 - sh:16 - sh + tb, ls]
        cnt = jnp.minimum(wdw, pos + 1).astype(F32)
        pooled = s / cnt - u[:, ls]
        outs.append(_mm(pooled.astype(BF16), w_ref[gi].astype(BF16)))
    ext_s[0:16, :] = u[tb - 16:tb, :]
    o_ref[0] = jnp.concatenate(outs, axis=1) * scale_ref[...]


def _pool_prompt(main3, pool_w, scale, tb):
    n, t, _ = main3.shape
    return pl.pallas_call(
        functools.partial(_pool_kernel, tb=tb),
        grid=(n, t // tb),
        in_specs=[pl.BlockSpec((1, tb, W_BR), lambda b, j: (b, j, C_U_D // W_BR)),
                  pl.BlockSpec((len(POOL_WINDOWS), POOL_GW, POOL_GW), lambda b, j: (0, 0, 0)),
                  pl.BlockSpec((1, W_BR), lambda b, j: (0, 0))],
        out_specs=pl.BlockSpec((1, tb, W_BR), lambda b, j: (b, j, 0)),
        out_shape=jax.ShapeDtypeStruct((n, t, W_BR), F32),
        scratch_shapes=[pltpu.VMEM((tb + 16, W_BR), F32)],
        compiler_params=_cparams(("arbitrary", "arbitrary")),
        name="pool_prompt",
    )(main3, pool_w, scale)


def _out_kernel(oa_ref, ob_ref, oc_ref, od_ref, za_ref, zb_ref, zc_ref, zd_ref, x_ref, w_ref, g_ref, b_ref, y_ref):
    acc = ALPHA_DN * x_ref[...]
    for i, (o_r, z_r) in enumerate(((oa_ref, za_ref), (ob_ref, zb_ref), (oc_ref, zc_ref), (od_ref, zd_ref))):
        gated = (o_r[...] * _silu(z_r[...])).astype(BF16)
        acc = acc + _mm(gated, w_ref[i * W_BR:(i + 1) * W_BR, :])
    mu = jnp.mean(acc, axis=-1, keepdims=True)
    xc = acc - mu
    var = jnp.mean(xc * xc, axis=-1, keepdims=True)
    y_ref[...] = xc * lax.rsqrt(var + LN_EPS) * g_ref[...] + b_ref[...]


def _out_proj(o_a, o_b, o_c, o_d, main2, x2d, w_out, ln_g, ln_b, tm):
    m = x2d.shape[0]
    ospec = pl.BlockSpec((tm, W_BR), lambda i: (i, 0))
    zspec = lambda col: pl.BlockSpec((tm, W_BR), lambda i: (i, col // W_BR))
    return pl.pallas_call(
        _out_kernel,
        grid=(m // tm,),
        in_specs=[ospec, ospec, ospec, ospec, zspec(C_Z_A), zspec(C_Z_B), zspec(C_Z_C), zspec(C_Z_D),
                  pl.BlockSpec((tm, D_MODEL), lambda i: (i, 0)),
                  pl.BlockSpec((D_MODEL, D_MODEL), lambda i: (0, 0)),
                  pl.BlockSpec((1, D_MODEL), lambda i: (0, 0)), pl.BlockSpec((1, D_MODEL), lambda i: (0, 0))],
        out_specs=pl.BlockSpec((tm, D_MODEL), lambda i: (i, 0)),
        out_shape=jax.ShapeDtypeStruct((m, D_MODEL), F32),
        compiler_params=_cparams(("arbitrary",)),
        name="out_proj",
    )(o_a, o_b, o_c, o_d, main2, main2, main2, main2, x2d, w_out, ln_g, ln_b)


def _rows8(row, nrows=1):
    return jnp.where(_iota2((8, row.shape[1]), 0) < nrows, jnp.broadcast_to(row, (8, row.shape[1])), 0.0)


def _dec_kernel(main_ref, small_ref, sa_ref, conv_ref, sb_ref, shift_ref, shifts_ref, dbuf_ref,
                convw_ref, gpar_ref, normg_ref, mu_ref, mus_ref, wup_ref, aup_ref, par_ref, poolw_ref, pscale_ref,
                oa_ref, ob_ref, od_ref, sa_out, conv_out, sb_out, dbuf_out, ext_s, *, pos):
    row = main_ref[0]
    sm = small_ref[0]
    bd = _head_block_diag()

    u = row[:, C_QKV_A:C_QKV_A + 3 * W_BR]
    buf = conv_ref[0]
    cw = convw_ref[...]
    c = buf[0:1, :] * cw[0:1, :] + buf[1:2, :] * cw[1:2, :] + buf[2:3, :] * cw[2:3, :] + u * cw[3:4, :]
    conv_out[0, 0:2, :] = buf[1:3, :]
    conv_out[0, 2:3, :] = u
    c = _silu(c)
    q = c[:, 0:W_BR]
    k = c[:, W_BR:2 * W_BR]
    v = c[:, 2 * W_BR:3 * W_BR]
    q = q * lax.rsqrt(_dot3_l(_rows8(q * q), bd)[0:1, :] + L2_EPS) * (HEAD_DIM ** -0.5)
    k = k * lax.rsqrt(_dot3_l(_rows8(k * k), bd)[0:1, :] + L2_EPS)
    gpar = gpar_ref[...]
    g = gpar[0:1, :] * _softplus(sm + gpar[1:2, :])
    beta = _sigmoid(sm)
    heads = range(N_HEADS)
    lanes = [slice(h * HEAD_DIM, (h + 1) * HEAD_DIM) for h in heads]
    eg = [jnp.exp(g[:, S_A + h:S_A + h + 1]) for h in heads]
    sts = [sa_ref[0, h] for h in heads]
    k8 = [_rows8(k[:, hs]) for hs in lanes]
    ks = [_mm(k8[h], sts[h], HI)[0:1, :] for h in heads]
    v_new = [beta[:, S_B + h:S_B + h + 1] * (v[:, lanes[h]] - eg[h] * ks[h]) for h in heads]
    st_new = [sts[h] * eg[h] + _mm_tn(k8[h], _rows8(v_new[h]), HI) for h in heads]
    o = jnp.concatenate([_mm(_rows8(q[:, lanes[h]]), st_new[h], HI)[0:1, :] for h in heads], axis=1)
    for h in heads:
        sa_out[0, h] = st_new[h]
    ms = _dot3_l(_rows8(o * o), bd)[0:1, :] * (1.0 / HEAD_DIM)
    oa_ref[0] = o * lax.rsqrt(ms + RMS_EPS) * normg_ref[...]

    p = row[:, C_RKV_B:C_RKV_B + 3 * W_BR]
    ps = p + (shift_ref[0] - p) * mu_ref[...]
    pss = sm + (shifts_ref[0] - sm) * mus_ref[...]
    r = ps[:, 0:W_BR]
    k = ps[:, W_BR:2 * W_BR]
    v = ps[:, 2 * W_BR:3 * W_BR]
    par = par_ref[...]
    w0, a0, xi, alpha, rho, gn_g, gn_b = (par[i:i + 1, :] for i in range(7))
    d = w0 + _mm(_rows8(jnp.tanh(pss)), wup_ref[...], HI)[0:1, :]
    decay = jnp.exp(-RWKV_DECAY_SCALE * _sigmoid(d))
    a = _sigmoid(a0 + _mm(_rows8(pss), aup_ref[...], HI)[0:1, :])
    kx = k * xi
    kk = kx * lax.rsqrt(_dot3_l(_rows8(kx * kx), bd)[0:1, :] + L2_EPS)
    k2 = k * (1.0 + (a - 1.0) * alpha)
    sub = _iota2((8, HEAD_DIM), 0)
    two_rows = lambda r0, r1: jnp.where(sub == 0, jnp.broadcast_to(r0, (8, HEAD_DIM)),
                                        jnp.where(sub == 1, jnp.broadcast_to(r1, (8, HEAD_DIM)), 0.0))
    sts = [sb_ref[0, h] for h in heads]
    s_kk = [_mm_nt(_rows8(-kk[:, lanes[h]]), sts[h], HI)[0:1, :] for h in heads]
    st_new = [sts[h] * decay[:, lanes[h]]
              + _mm_tn(two_rows(s_kk[h], v[:, lanes[h]]), two_rows(kk[:, lanes[h]] * a[:, lanes[h]], k2[:, lanes[h]]), HI)
              for h in heads]
    y = jnp.concatenate([_mm_nt(_rows8(r[:, lanes[h]]), st_new[h], HI)[0:1, :] for h in heads], axis=1)
    for h in heads:
        sb_out[0, h] = st_new[h]
    mean = _dot3_l(_rows8(y), bd)[0:1, :] * (1.0 / HEAD_DIM)
    yc = y - mean
    var = _dot3_l(_rows8(yc * yc), bd)[0:1, :] * (1.0 / HEAD_DIM)
    yn = yc * lax.rsqrt(var + GN_EPS) * gn_g + gn_b
    bonus = _dot3_l(_rows8(r * k2 * rho), bd)[0:1, :] * v
    ob_ref[0] = yn + bonus

    ud = row[:, C_U_D:C_U_D + W_BR]
    ext_s[0:POOL_BUF, :] = dbuf_ref[0]
    ext_s[POOL_BUF:POOL_BUF + 1, :] = ud
    dbuf_out[0] = ext_s[1:POOL_BUF + 1, :]
    outs = []
    for gi, wdw in enumerate(POOL_WINDOWS):
        ls = slice(gi * POOL_GW, (gi + 1) * POOL_GW)
        s = jnp.sum(ext_s[POOL_BUF + 1 - wdw:POOL_BUF + 1, ls], axis=0, keepdims=True)
        pooled = s / float(min(wdw, pos + 1)) - ud[:, ls]
        outs.append(_mm(_rows8(pooled).astype(BF16), poolw_ref[gi].astype(BF16))[0:1, :])
    od_ref[0] = jnp.concatenate(outs, axis=1) * pscale_ref[...]


def _decode_mixers(main_s, small_s, st_a, conv_a, st_b, shift_rkv, shift_small, dbuf,
                   conv_w, gpar, normg, mu, mus, wup, aup, par, pool_w, pscale, pos):
    nd = main_s.shape[0]
    per_seq = lambda shape: pl.BlockSpec((1,) + shape, lambda b: (b,) + (0,) * len(shape))
    full = lambda shape: pl.BlockSpec(shape, lambda b: (0,) * len(shape))
    hh = (N_HEADS, HEAD_DIM, HEAD_DIM)
    return pl.pallas_call(
        functools.partial(_dec_kernel, pos=pos),
        grid=(nd,),
        in_specs=[per_seq((1, N_MAIN)), per_seq((1, N_SMALL)), per_seq(hh), per_seq((CONV_W - 1, 3 * W_BR)),
                  per_seq(hh), per_seq((1, 3 * W_BR)), per_seq((1, N_SMALL)), per_seq((POOL_BUF, W_BR)),
                  full((CONV_W, 3 * W_BR)), full((8, N_SMALL)), full((1, W_BR)),
                  full((1, 3 * W_BR)), full((1, N_SMALL)), full((N_SMALL, W_BR)), full((N_SMALL, W_BR)),
                  full((8, W_BR)), full((len(POOL_WINDOWS), POOL_GW, POOL_GW)), full((1, W_BR))],
        out_specs=[per_seq((1, W_BR)), per_seq((1, W_BR)), per_seq((1, W_BR)), per_seq(hh),
                   per_seq((CONV_W - 1, 3 * W_BR)), per_seq(hh), per_seq((POOL_BUF, W_BR))],
        out_shape=[jax.ShapeDtypeStruct((nd, 1, W_BR), F32)] * 3
                  + [jax.ShapeDtypeStruct((nd,) + hh, F32), jax.ShapeDtypeStruct((nd, CONV_W - 1, 3 * W_BR), F32),
                     jax.ShapeDtypeStruct((nd,) + hh, F32), jax.ShapeDtypeStruct((nd, POOL_BUF, W_BR), F32)],
        scratch_shapes=[pltpu.VMEM((16, W_BR), F32)],
        compiler_params=_cparams(("arbitrary",)),
        name="decode_mixers",
    )(main_s.reshape(nd, 1, N_MAIN), small_s.reshape(nd, 1, N_SMALL), st_a, conv_a, st_b,
      shift_rkv.reshape(nd, 1, 3 * W_BR), shift_small.reshape(nd, 1, N_SMALL), dbuf,
      conv_w, gpar, normg, mu, mus, wup, aup, par, pool_w, pscale)


def _dfox_kernel(pt_ref, qkv_ref, qrep_ref, small_ref, bias_ref, *rest, n_grp):
    kt_refs, vt_refs, lf_refs = rest[0:n_grp], rest[n_grp:2 * n_grp], rest[2 * n_grp:3 * n_grp]
    o_ref, lf_out, m_s, l_s, w_s, r_s, acc_s = rest[3 * n_grp:]
    j = pl.program_id(1)
    scale = HEAD_DIM ** -0.5
    sub8 = _iota2((N_HEADS, LANES), 0)
    lane8 = _iota2((N_HEADS, LANES), 1)

    @pl.when(j == 0)
    def _():
        lf_new = _log_sigmoid(small_ref[0] + bias_ref[...])
        lf_out[0] = lf_new
        qb = (qkv_ref[0, 0] * scale).astype(BF16).astype(F32)
        kb = qkv_ref[0, 1].astype(BF16).astype(F32)
        m_s[...] = jnp.broadcast_to(jnp.sum(qb * kb, axis=1, keepdims=True), (N_HEADS, LANES))
        l_s[...] = jnp.ones_like(l_s)
        w_s[...] = jnp.ones_like(w_s)
        acc_s[...] = jnp.zeros_like(acc_s)
        mine = jnp.where(lane8 == sub8 + S_F, jnp.broadcast_to(lf_new, (N_HEADS, LANES)), 0.0)
        r_s[...] = jnp.broadcast_to(jnp.sum(mine, axis=1, keepdims=True), (N_HEADS, LANES))

    grp = range(n_grp)
    r_t = _iota2((PAGE_SIZE, 2 * LANES), 0)
    c_t = _iota2((PAGE_SIZE, 2 * LANES), 1)
    later_or_all = jnp.where((r_t > c_t) | (c_t >= LANES), 1.0, 0.0).astype(BF16)
    q_rep = qrep_ref[0]
    s_t = [jnp.sum(kt_refs[g][0, 0] * q_rep, axis=1) * scale for g in grp]
    gates = [_dot3_l(lf_refs[g][0, 0], later_or_all) for g in grp]
    logits, r_run = [], r_s[...]
    for g in grp:
        logits.append(s_t[g] + gates[g][:, 0:LANES] + r_run)
        r_run = r_run + gates[g][:, LANES:2 * LANES]
    m_old = m_s[...]
    m_grp = logits[0]
    for g in grp[1:]:
        m_grp = jnp.maximum(m_grp, logits[g])
    m_new = jnp.maximum(m_old, jnp.max(m_grp, axis=1, keepdims=True))
    alpha = jnp.exp(m_old - m_new)
    p = [jnp.exp(logits[g] - m_new) for g in grp]
    p_sum = p[0]
    for g in grp[1:]:
        p_sum = p_sum + p[g]
    for h in range(N_HEADS):
        acc = acc_s[h] * jnp.broadcast_to(alpha[h:h + 1, :], (HEAD_DIM, LANES))
        for g in grp:
            acc = acc + vt_refs[g][0, 0, h] * jnp.broadcast_to(p[g][h:h + 1, :], (HEAD_DIM, LANES))
        acc_s[h] = acc
    l_s[...] = alpha * l_s[...] + jnp.sum(p_sum, axis=1, keepdims=True)
    w_s[...] = alpha * w_s[...]
    m_s[...] = m_new
    r_s[...] = r_run

    @pl.when(j == pl.num_programs(1) - 1)
    def _():
        ones_b = jnp.ones((N_HEADS, LANES), BF16)
        sub = _iota2((N_HEADS, HEAD_DIM), 0)
        red = jnp.zeros((N_HEADS, HEAD_DIM), F32)
        for h in range(N_HEADS):
            hi, mid, lo = _split3(acc_s[h])
            tot = _mm_nt(ones_b, hi) + _mm_nt(ones_b, mid) + _mm_nt(ones_b, lo)
            red = red + jnp.where(sub == h, tot, 0.0)
        o_ref[0] = (red + w_s[...][:, 0:HEAD_DIM] * qkv_ref[0, 2]) / l_s[...][:, 0:HEAD_DIM]


def _fox_decode(page_table, qkv_s, small_s, bias_row, cache_k, cache_v, cache_lf, layer, n_grp):
    nd, n_pages = page_table.shape
    kt = jnp.transpose(cache_k, (0, 1, 3, 4, 2))
    vt = jnp.transpose(cache_v, (0, 1, 3, 4, 2))
    lft = jnp.transpose(cache_lf, (0, 1, 3, 2))
    qkv8 = qkv_s.reshape(nd, 3, N_HEADS, HEAD_DIM)
    q_rep = jnp.broadcast_to(qkv8[:, 0, :, :, None], (nd, N_HEADS, HEAD_DIM, LANES))
    page = lambda g: (lambda b, j, pt: (layer, pt[b, n_pages - 1 - (j * n_grp + g)], 0, 0, 0))
    page4 = lambda g: (lambda b, j, pt: (layer, pt[b, n_pages - 1 - (j * n_grp + g)], 0, 0))
    kv_spec = lambda g: pl.BlockSpec((1, 1, N_HEADS, HEAD_DIM, PAGE_SIZE), page(g))
    grid_spec = pltpu.PrefetchScalarGridSpec(
        num_scalar_prefetch=1,
        grid=(nd, n_pages // n_grp),
        in_specs=[pl.BlockSpec((1, 3, N_HEADS, HEAD_DIM), lambda b, j, pt: (b, 0, 0, 0)),
                  pl.BlockSpec((1, N_HEADS, HEAD_DIM, LANES), lambda b, j, pt: (b, 0, 0, 0)),
                  pl.BlockSpec((1, 1, N_SMALL), lambda b, j, pt: (b, 0, 0)),
                  pl.BlockSpec((1, N_SMALL), lambda b, j, pt: (0, 0))]
                 + [kv_spec(g) for g in range(n_grp)] + [kv_spec(g) for g in range(n_grp)]
                 + [pl.BlockSpec((1, 1, N_HEADS, PAGE_SIZE), page4(g)) for g in range(n_grp)],
        out_specs=[pl.BlockSpec((1, N_HEADS, HEAD_DIM), lambda b, j, pt: (b, 0, 0)),
                   pl.BlockSpec((1, 1, N_SMALL), lambda b, j, pt: (b, 0, 0))],
        scratch_shapes=[pltpu.VMEM((N_HEADS, LANES), F32)] * 4 + [pltpu.VMEM((N_HEADS, HEAD_DIM, LANES), F32)],
    )
    o, lf_new = pl.pallas_call(
        functools.partial(_dfox_kernel, n_grp=n_grp),
        grid_spec=grid_spec,
        out_shape=[jax.ShapeDtypeStruct((nd, N_HEADS, HEAD_DIM), F32), jax.ShapeDtypeStruct((nd, 1, N_SMALL), F32)],
        compiler_params=_cparams(("arbitrary", "arbitrary")),
        name="fox_decode",
    )(page_table, qkv8, q_rep, small_s.reshape(nd, 1, N_SMALL), bias_row,
      *([kt] * n_grp), *([vt] * n_grp), *([lft] * n_grp))
    return o.reshape(nd, W_BR), lf_new


_MAIN_SEGMENTS = ((O_QKV_A, 3 * W_BR), (O_P_B, 3 * W_BR), (O_QKV_C, 3 * W_BR), (O_Z_A, W_BR), (O_Z_B, W_BR),
                  (O_Z_C, W_BR), (O_U_D, W_BR), (O_Z_D, W_BR))
_SMALL_SEGMENTS = ((O_A_A, 2 * N_HEADS), (O_WL, 2 * LORA_B), (O_F_C, N_HEADS))


def _wprep_kernel(wt_ref, main_ref, small_ref):
    wt = wt_ref[0]
    main_ref[...] = jnp.concatenate([wt[a:a + n, :] for a, n in _MAIN_SEGMENTS], axis=0).T.astype(BF16)
    used = sum(n for _, n in _SMALL_SEGMENTS)
    small = [wt[a:a + n, :] for a, n in _SMALL_SEGMENTS] + [jnp.zeros((N_SMALL - used, wt.shape[1]), F32)]
    small_ref[...] = jnp.concatenate(small, axis=0).T.astype(BF16)


def _prep_w_in(w_in, layer, tr=256):
    _, d, d_in = w_in.shape
    return pl.pallas_call(
        _wprep_kernel,
        grid=(d // tr,),
        in_specs=[pl.BlockSpec((1, d_in, tr), lambda i: (layer, 0, i))],
        out_specs=[pl.BlockSpec((tr, N_MAIN), lambda i: (i, 0)), pl.BlockSpec((tr, N_SMALL), lambda i: (i, 0))],
        out_shape=[jax.ShapeDtypeStruct((d, N_MAIN), BF16), jax.ShapeDtypeStruct((d, N_SMALL), BF16)],
        compiler_params=_cparams(("arbitrary",)),
        name="w_in_prep",
    )(jnp.swapaxes(w_in, 1, 2))


def _small_row(pairs):
    row = jnp.zeros((N_SMALL,), F32)
    for off, vec in pairs:
        row = row.at[off:off + vec.shape[0]].set(vec.astype(F32))
    return row[None, :]


def _layer_params(l, w_in, conv_A, A_log, dt_bias, norm_A, mu_B, w0_B, w_up_B, a0_B, a_up_B, xi_B, alpha_B, rho_B,
                  gn_g_B, gn_b_B, b_f_C, pool_w_D, pool_scale_D, w_out, ln_g, ln_b):
    w_main, w_small = _prep_w_in(w_in, l)
    gpar = jnp.concatenate([_small_row([(S_A, -jnp.exp(A_log[l].astype(F32)))]), _small_row([(S_A, dt_bias[l])]),
                            jnp.zeros((6, N_SMALL), F32)], axis=0)
    head_of_lane = jnp.arange(W_BR) // HEAD_DIM
    lane = jnp.arange(N_SMALL)[:, None]
    eg = (lane == head_of_lane[None, :] + S_A).astype(BF16)
    eb = (lane == head_of_lane[None, :] + S_B).astype(BF16)
    mu = mu_B[l].astype(F32)
    wup = jnp.zeros((N_SMALL, W_BR), F32).at[S_WL:S_WL + LORA_B].set(w_up_B[l].astype(F32))
    aup = jnp.zeros((N_SMALL, W_BR), F32).at[S_AL:S_AL + LORA_B].set(a_up_B[l].astype(F32))
    bd = (head_of_lane[:, None] == head_of_lane[None, :]).astype(BF16)
    par = jnp.stack([w0_B[l], a0_B[l], xi_B[l], alpha_B[l], rho_B[l], gn_g_B[l], gn_b_B[l],
                     jnp.zeros((W_BR,), F32)]).astype(F32)
    return dict(
        w_main=w_main, w_small=w_small, bd=bd, conv_w=conv_A[l].astype(F32), gpar=gpar, eg=eg, eb=eb,
        normg=jnp.tile(norm_A[l].astype(F32), N_HEADS)[None, :],
        mu=mu[None, 0:3 * W_BR], mus=_small_row([(S_WL, mu[3 * W_BR:])]), wup=wup, aup=aup, par=par,
        fbias=_small_row([(S_F, b_f_C[l])]), pool_w=pool_w_D[l].astype(F32), pscale=pool_scale_D[l].astype(F32)[None, :],
        w_out=w_out[l].astype(BF16), ln_g=ln_g[l].astype(F32)[None, :], ln_b=ln_b[l].astype(F32)[None, :])


def _prompt_layer(x3, P, tb_rec, tq, tb_gate, tb_pool, tm_proj, tm_out):
    n, t, _ = x3.shape
    x2 = x3.reshape(n * t, D_MODEL)
    main2, small2 = _proj(x2, P['w_main'], P['w_small'], tm_proj)
    main3 = main2.reshape(n, t, N_MAIN)
    small3 = small2.reshape(n, t, N_SMALL)
    o_a, s_a = _gdn_prompt(main3, small3, P['conv_w'], P['gpar'], P['eg'], P['eb'], P['normg'], P['bd'], tb_rec)
    o_b, s_b = _rwkv_prompt(main3, small3, P['mu'], P['mus'], _bf(P['wup']), _bf(P['aup']), P['par'], P['bd'],
                            tb_rec)
    logf3, c3, ct3 = _gates(small3, P['fbias'], tb_gate)
    o_c = _fox_prompt(main3, c3, ct3, tq)
    o_d = _pool_prompt(main3, P['pool_w'], P['pscale'], tb_pool)
    flat = lambda a: a.reshape(n * t, W_BR)
    y2 = _out_proj(flat(o_a), flat(o_b), flat(o_c), flat(o_d), main2, x2, P['w_out'], P['ln_g'], P['ln_b'], tm_out)
    new = (s_a,
           main3[:, t - (CONV_W - 1):, C_QKV_A:C_QKV_A + 3 * W_BR],
           s_b,
           jnp.concatenate([main3[:, t - 1, C_RKV_B:C_RKV_B + 3 * W_BR], small3[:, t - 1, S_WL:S_WL + 2 * LORA_B]], axis=-1),
           main3[:, :, C_QKV_C + W_BR:C_QKV_C + 2 * W_BR].reshape(n, t, N_HEADS, HEAD_DIM),
           main3[:, :, C_QKV_C + 2 * W_BR:C_QKV_C + 3 * W_BR].reshape(n, t, N_HEADS, HEAD_DIM),
           logf3[:, :, S_F:S_F + N_HEADS],
           main3[:, t - POOL_BUF:, C_U_D:C_U_D + W_BR])
    return y2.reshape(n, t, D_MODEL), new


def _sample_layer(x3, st, cache, layer, page_table, P):
    nd = x3.shape[0]
    st_a, conv_a, st_b, shift_b, dbuf = st
    cache_k, cache_v, cache_lf = cache
    x2 = x3.reshape(nd, D_MODEL)
    main_s, small_s = _proj(x2, P['w_main'], P['w_small'], nd)
    shift_small = jnp.zeros((nd, N_SMALL), F32).at[:, S_WL:S_WL + 2 * LORA_B].set(shift_b[:, 3 * W_BR:].astype(F32))
    o_a, o_b, o_d, s_a, conv_n, s_b, dbuf_n = _decode_mixers(
        main_s, small_s, st_a, conv_a, st_b, shift_b[:, 0:3 * W_BR], shift_small, dbuf,
        P['conv_w'], P['gpar'], P['normg'], P['mu'], P['mus'], P['wup'], P['aup'], P['par'], P['pool_w'], P['pscale'],
        page_table.shape[1] * PAGE_SIZE)
    qkv_s = main_s[:, C_QKV_C:C_QKV_C + 3 * W_BR]
    n_grp = math.gcd(page_table.shape[1], 8)
    o_c, lf_new = _fox_decode(page_table, qkv_s, small_s, P['fbias'], cache_k, cache_v, cache_lf, layer, n_grp)
    flat = lambda a: a.reshape(nd, W_BR)
    y2 = _out_proj(flat(o_a), flat(o_b), flat(o_c), flat(o_d), main_s, x2, P['w_out'], P['ln_g'], P['ln_b'], nd)
    new = (s_a, conv_n, s_b,
           jnp.concatenate([main_s[:, C_RKV_B:C_RKV_B + 3 * W_BR], small_s[:, S_WL:S_WL + 2 * LORA_B]], axis=-1),
           qkv_s[:, W_BR:2 * W_BR].reshape(nd, 1, N_HEADS, HEAD_DIM),
           qkv_s[:, 2 * W_BR:3 * W_BR].reshape(nd, 1, N_HEADS, HEAD_DIM),
           lf_new[:, :, S_F:S_F + N_HEADS],
           dbuf_n)
    return y2.reshape(nd, 1, D_MODEL), new


def kernel(x_prompt, x_sample, state_A_S, state_A_conv, state_B_S, state_B_shift, cache_C_k, cache_C_v, cache_C_logf, state_D_buf, page_table, w_in, conv_A, A_log, dt_bias, norm_A, mu_B, w0_B, w_up_B, a0_B, a_up_B, xi_B, alpha_B, rho_B, gn_g_B, gn_b_B, b_f_C, pool_w_D, pool_scale_D, w_out, ln_g, ln_b):
    depth = w_in.shape[0]
    t = x_prompt.shape[1]
    tb_rec = min(256, t)
    tq = min(256, t)
    tb_gate = min(512, t)
    tb_pool = min(512, t)
    tm_proj = min(1024, x_prompt.shape[0] * t)
    tm_out = min(256, x_prompt.shape[0] * t)
    y_p, y_s = x_prompt, x_sample
    prompt_new, sample_new = [], []
    for l in range(depth):
        P = _layer_params(l, w_in, conv_A, A_log, dt_bias, norm_A, mu_B, w0_B, w_up_B, a0_B, a_up_B, xi_B, alpha_B,
                          rho_B, gn_g_B, gn_b_B, b_f_C, pool_w_D, pool_scale_D, w_out, ln_g, ln_b)
        y_p, new_p = _prompt_layer(y_p, P, tb_rec, tq, tb_gate, tb_pool, tm_proj, tm_out)
        st = (state_A_S[l], state_A_conv[l], state_B_S[l], state_B_shift[l], state_D_buf[l])
        y_s, new_s = _sample_layer(y_s, st, (cache_C_k, cache_C_v, cache_C_logf), l, page_table, P)
        prompt_new.append(new_p)
        sample_new.append(new_s)
    p_out = [jnp.stack([n[i] for n in prompt_new]) for i in range(8)]
    s_out = [jnp.stack([n[i] for n in sample_new]) for i in range(8)]
    p_a_s, p_a_conv, p_b_s, p_b_shift, p_c_k, p_c_v, p_c_logf, p_d_buf = p_out
    s_a_s, s_a_conv, s_b_s, s_b_shift, s_c_k, s_c_v, s_c_logf, s_d_buf = s_out
    return (y_p, y_s, p_a_s, p_a_conv, p_b_s, p_b_shift, p_c_k, p_c_v, p_c_logf, p_d_buf,
            s_a_s, s_a_conv, s_b_s, s_b_shift, s_c_k, s_c_v, s_c_logf, s_d_buf)
```

```python
import functools
import math

import jax
import jax.numpy as jnp
from jax import lax
from jax.experimental import pallas as pl
from jax.experimental.pallas import tpu as pltpu

F32 = jnp.float32
BF16 = jnp.bfloat16
HI = lax.Precision.HIGHEST

D_MODEL = 2048
W_BR = 512
HEAD_DIM = 64
N_HEADS = W_BR // HEAD_DIM
CONV_W = 4
CHUNK = 64
LORA_B = 32
POOL_WINDOWS = (2, 4, 8, 16)
POOL_GW = W_BR // len(POOL_WINDOWS)
POOL_BUF = max(POOL_WINDOWS) - 1
PAGE_SIZE = 128
LANES = 128
DEPTH = 2
ALPHA_DN = (2.0 * DEPTH) ** 0.25
LN_EPS = 1e-5
GN_EPS = 64e-5
RMS_EPS = 1e-6
L2_EPS = 1e-6
RWKV_DECAY_SCALE = math.exp(-0.5)
NEG_INF = -1e30

C_QKV_A, C_RKV_B, C_QKV_C = 0, 1536, 3072
C_Z_A, C_Z_B, C_Z_C, C_U_D, C_Z_D = 4608, 5120, 5632, 6144, 6656
N_MAIN = 7168
S_A, S_B, S_WL, S_AL, S_F = 0, 8, 16, 48, 80
N_SMALL = 128
O_QKV_A, O_A_A, O_Z_A, O_P_B, O_WL, O_Z_B, O_QKV_C, O_F_C, O_Z_C, O_U_D, O_Z_D, D_IN = (
    0, 1536, 1552, 2064, 3600, 3664, 4176, 5712, 5720, 6232, 6744, 7256)

VMEM_LIMIT = 48 * 1024 * 1024


VMEM_LIMIT_PROJ = 56 * 1024 * 1024


def _cparams(sem, vmem_limit=VMEM_LIMIT):
    return pltpu.CompilerParams(dimension_semantics=sem, vmem_limit_bytes=vmem_limit)


def _mm(a, b, prec=None):
    return jnp.dot(a, b, preferred_element_type=F32, precision=prec)


def _mm_nt(a, b, prec=None):
    return lax.dot_general(a, b, (((1,), (1,)), ((), ())), preferred_element_type=F32, precision=prec)


def _mm_tn(a, b, prec=None):
    return lax.dot_general(a, b, (((0,), (0,)), ((), ())), preferred_element_type=F32, precision=prec)


def _split3(x):
    hi = x.astype(BF16)
    r1 = x - hi.astype(F32)
    mid = r1.astype(BF16)
    lo = (r1 - mid.astype(F32)).astype(BF16)
    return hi, mid, lo


def _dot3_l(x, b_bf16):
    hi, mid, lo = _split3(x)
    return _mm(hi, b_bf16) + _mm(mid, b_bf16) + _mm(lo, b_bf16)


def _dot2_l(x, b_bf16):
    hi = x.astype(BF16)
    lo = (x - hi.astype(F32)).astype(BF16)
    return _mm(hi, b_bf16) + _mm(lo, b_bf16)


def _dot3_r(a_bf16, x):
    hi, mid, lo = _split3(x)
    return _mm(a_bf16, hi) + _mm(a_bf16, mid) + _mm(a_bf16, lo)


def _sigmoid(x):
    return 1.0 / (1.0 + jnp.exp(-x))


def _silu(x):
    return x * _sigmoid(x)


def _softplus(x):
    return jnp.maximum(x, 0.0) + jnp.log1p(jnp.exp(-jnp.abs(x)))


def _log_sigmoid(x):
    return -_softplus(-x)


def _iota2(shape, dim):
    return lax.broadcasted_iota(jnp.int32, shape, dim)


def _head_block_diag():
    r = _iota2((W_BR, W_BR), 0) // HEAD_DIM
    c = _iota2((W_BR, W_BR), 1) // HEAD_DIM
    return jnp.where(r == c, 1.0, 0.0).astype(BF16)


def _chunk_tril(n):
    r = _iota2((n, n), 0)
    c = _iota2((n, n), 1)
    return jnp.where((r >= c) & (r // CHUNK == c // CHUNK), 1.0, 0.0).astype(BF16)


_DN_NN = (((1,), (0,)), ((), ()))
_DN_NT = (((1,), (1,)), ((), ()))
_DN_TN = (((0,), (0,)), ((), ()))


def _bf(x):
    return x.astype(BF16)


def _mmb(a, b, dn=_DN_NN):
    return lax.dot_general(a, b, dn, preferred_element_type=F32)


def _tri_solve(a_list, rhs_list, blk):
    each = lambda f, *ls: [f(*xs) for xs in zip(*ls)]
    width = rhs_list[0].shape[1]
    ad = each(lambda a: a * blk, a_list)
    adb = each(_bf, ad)
    a2b = each(_bf, each(_mmb, adb, adb))
    a4b = each(_bf, each(_mmb, a2b, a2b))
    a8b = each(_bf, each(_mmb, a4b, a4b))
    z = each(lambda r, a, d: jnp.concatenate([r, a - d], axis=1), rhs_list, a_list, ad)
    z = each(lambda x, y: x - y, z, each(_mmb, adb, each(_bf, z)))
    for pw in (a2b, a4b, a8b):
        z = each(lambda x, y: x + y, z, each(_mmb, pw, each(_bf, z)))
    y = each(lambda x: x[:, 0:width], z)
    nb = each(lambda x: _bf(x[:, width:width + CHUNK]), z)
    n2b = each(_bf, each(_mmb, nb, nb))
    t = each(lambda x, u: x + u, y, each(_mmb, n2b, each(_bf, y)))
    return each(lambda x, u: x - u, t, each(_mmb, nb, each(_bf, t)))


def _chunk_masks():
    r = _iota2((CHUNK, CHUNK), 0)
    c = _iota2((CHUNK, CHUNK), 1)
    eye = jnp.where(r == c, 1.0, 0.0).astype(F32)
    strict = jnp.where(r > c, 1.0, 0.0).astype(F32)
    incl = jnp.where(r >= c, 1.0, 0.0).astype(F32)
    blk = jnp.where(r // 16 == c // 16, 1.0, 0.0).astype(F32)
    return eye, strict, incl, blk


def _proj_kernel(x_ref, w_ref, ws_ref, kin_ref, vin_ref, main_ref, small_ref, k_ref, v_ref, xb_ref, *, tn,
                 token_minor):
    j = pl.program_id(1)

    @pl.when(j == 0)
    def _():
        xb = x_ref[...].astype(BF16)
        xb_ref[...] = xb
        small_ref[...] = _mm(xb, ws_ref[...])

    acc = _mm(xb_ref[...], w_ref[...])
    main_ref[...] = acc
    for col, out_ref in ((C_QKV_C + W_BR, k_ref), (C_QKV_C + 2 * W_BR, v_ref)):
        @pl.when(j == col // tn)
        def _(col=col, out_ref=out_ref):
            kv = acc[:, col % tn:col % tn + W_BR]
            if token_minor:
                out_ref[0, 0] = kv.T
            else:
                out_ref[0] = kv


def _proj(x2d, w_main, w_small, k_all, v_all, layer, tm, tn=1024):
    m = x2d.shape[0]
    assert (C_QKV_C + W_BR) % tn + W_BR <= tn and (C_QKV_C + 2 * W_BR) % tn + W_BR <= tn
    token_minor = k_all.ndim == 4
    if token_minor:
        per_seq = k_all.shape[3] // tm
        kv_spec = pl.BlockSpec((1, 1, W_BR, tm), lambda i, j: (layer, i // per_seq, 0, i % per_seq))
    else:
        kv_spec = pl.BlockSpec((1, tm, W_BR), lambda i, j: (layer, i, 0))
    return pl.pallas_call(
        functools.partial(_proj_kernel, tn=tn, token_minor=token_minor),
        grid=(m // tm, N_MAIN // tn),
        in_specs=[pl.BlockSpec((tm, D_MODEL), lambda i, j: (i, 0)),
                  pl.BlockSpec((D_MODEL, tn), lambda i, j: (0, j)),
                  pl.BlockSpec((D_MODEL, N_SMALL), lambda i, j: (0, 0)),
                  pl.BlockSpec(memory_space=pl.ANY), pl.BlockSpec(memory_space=pl.ANY)],
        out_specs=[pl.BlockSpec((tm, tn), lambda i, j: (i, j)),
                   pl.BlockSpec((tm, N_SMALL), lambda i, j: (i, 0)), kv_spec, kv_spec],
        out_shape=[jax.ShapeDtypeStruct((m, N_MAIN), F32), jax.ShapeDtypeStruct((m, N_SMALL), F32),
                   jax.ShapeDtypeStruct(k_all.shape, F32), jax.ShapeDtypeStruct(v_all.shape, F32)],
        input_output_aliases={3: 2, 4: 3},
        scratch_shapes=[pltpu.VMEM((tm, D_MODEL), BF16)],
        compiler_params=_cparams(("arbitrary", "arbitrary"), VMEM_LIMIT_PROJ),
        name="proj",
    )(x2d, w_main, w_small, k_all, v_all)


def _gates_kernel(small_ref, bias_ref, logf_ref, c_ref, ct_ref, carry_ref, *, tb):
    @pl.when(pl.program_id(1) == 0)
    def _():
        carry_ref[...] = jnp.zeros_like(carry_ref)

    logf = _log_sigmoid(small_ref[0] + bias_ref[...])
    r = _iota2((tb, tb), 0)
    c = _iota2((tb, tb), 1)
    tril = jnp.where(r >= c, 1.0, 0.0).astype(BF16)
    cum = _dot3_r(tril, logf) + carry_ref[0:1, :]
    carry_ref[...] = jnp.broadcast_to(cum[tb - 1:tb, :], carry_ref.shape)
    logf_ref[0] = logf
    c_ref[0] = cum
    ct_ref[0] = cum.T[S_F:S_F + N_HEADS, :]


def _gates(small3, bias_row, tb):
    n, t, _ = small3.shape
    return pl.pallas_call(
        functools.partial(_gates_kernel, tb=tb),
        grid=(n, t // tb),
        in_specs=[pl.BlockSpec((1, tb, N_SMALL), lambda b, j: (b, j, 0)),
                  pl.BlockSpec((1, N_SMALL), lambda b, j: (0, 0))],
        out_specs=[pl.BlockSpec((1, tb, N_SMALL), lambda b, j: (b, j, 0)),
                   pl.BlockSpec((1, tb, N_SMALL), lambda b, j: (b, j, 0)),
                   pl.BlockSpec((1, N_HEADS, tb), lambda b, j: (b, 0, j))],
        out_shape=[jax.ShapeDtypeStruct((n, t, N_SMALL), F32), jax.ShapeDtypeStruct((n, t, N_SMALL), F32),
                   jax.ShapeDtypeStruct((n, N_HEADS, t), F32)],
        scratch_shapes=[pltpu.VMEM((8, N_SMALL), F32)],
        compiler_params=_cparams(("arbitrary", "arbitrary")),
        name="fox_gates",
    )(small3, bias_row)


def _fox_kernel(qi_ref, kj_ref, q_ref, k_ref, v_ref, c_ref, ct_ref, o_ref, m_s, l_s, acc_s, cq_s, *, tq):
    i = qi_ref[pl.program_id(1)]
    j = kj_ref[pl.program_id(1)]
    n_pairs = W_BR // LANES
    lo_half = _iota2((tq, LANES), 1) < HEAD_DIM
    pair_lanes = [slice(pr * LANES, (pr + 1) * LANES) for pr in range(n_pairs)]

    @pl.when(j == 0)
    def _():
        m_s[...] = jnp.full(m_s.shape, NEG_INF, F32)
        l_s[...] = jnp.zeros_like(l_s)
        acc_s[...] = jnp.zeros_like(acc_s)
        cq = c_ref[0]
        for h in range(N_HEADS):
            cq_s[h] = jnp.broadcast_to(cq[:, S_F + h:S_F + h + 1], (tq, LANES))

    def step(masked):
        q = q_ref[0] * (HEAD_DIM ** -0.5)
        k = k_ref[0].astype(BF16)
        v = v_ref[0]
        ct = ct_ref[0]
        wide = lambda x: jnp.concatenate([x] * (tq // LANES), axis=1)
        if masked:
            keep = _iota2((tq, tq), 0) >= _iota2((tq, tq), 1)
        s_all, m_all = [], []
        for pr, ps in enumerate(pair_lanes):
            qp, kp = q[:, ps], k[:, ps]
            for e, q_half in enumerate((jnp.where(lo_half, qp, 0.0), jnp.where(lo_half, 0.0, qp))):
                h = 2 * pr + e
                s = _mm_nt(q_half.astype(BF16), kp) + wide(cq_s[h]) - ct[h:h + 1, :]
                if masked:
                    s = jnp.where(keep, s, NEG_INF)
                s_all.append(s)
                m_all.append(jnp.maximum(m_s[h], jnp.max(s, axis=1, keepdims=True)))
        p_all, alpha_all = [], []
        for h in range(N_HEADS):
            m_new = m_all[h]
            alpha = jnp.exp(m_s[h] - m_new)
            p = jnp.exp(s_all[h] - wide(m_new))
            l_s[h] = alpha * l_s[h] + jnp.sum(p, axis=1, keepdims=True)
            m_s[h] = m_new
            p_all.append(p.astype(BF16))
            alpha_all.append(alpha)
        for pr, ps in enumerate(pair_lanes):
            vp = v[:, ps]
            v_bd = jnp.concatenate([jnp.where(lo_half, vp, 0.0), jnp.where(lo_half, 0.0, vp)], axis=0).astype(BF16)
            p_pair = jnp.concatenate([p_all[2 * pr], p_all[2 * pr + 1]], axis=1)
            alpha_p = jnp.where(lo_half, alpha_all[2 * pr], alpha_all[2 * pr + 1])
            acc_s[:, ps] = alpha_p * acc_s[:, ps] + _mm(p_pair, v_bd)

    @pl.when(j < i)
    def _():
        step(False)

    @pl.when(j == i)
    def _():
        step(True)
        for pr, ps in enumerate(pair_lanes):
            o_ref[0, :, ps] = acc_s[:, ps] / jnp.where(lo_half, l_s[2 * pr], l_s[2 * pr + 1])


def _fox_prompt(main3, c3, ct3, tq):
    n, t, _ = main3.shape
    nb = t // tq
    qb, kb, vb = C_QKV_C // W_BR, C_QKV_C // W_BR + 1, C_QKV_C // W_BR + 2
    pairs = [(i, j) for i in range(nb) for j in range(i + 1)]
    qi = jnp.asarray([p[0] for p in pairs], jnp.int32)
    kj = jnp.asarray([p[1] for p in pairs], jnp.int32)
    grid_spec = pltpu.PrefetchScalarGridSpec(
        num_scalar_prefetch=2,
        grid=(n, len(pairs)),
        in_specs=[pl.BlockSpec((1, tq, W_BR), lambda b, s, qi, kj: (b, qi[s], qb)),
                  pl.BlockSpec((1, tq, W_BR), lambda b, s, qi, kj: (b, kj[s], kb)),
                  pl.BlockSpec((1, tq, W_BR), lambda b, s, qi, kj: (b, kj[s], vb)),
                  pl.BlockSpec((1, tq, N_SMALL), lambda b, s, qi, kj: (b, qi[s], 0)),
                  pl.BlockSpec((1, N_HEADS, tq), lambda b, s, qi, kj: (b, 0, kj[s]))],
        out_specs=pl.BlockSpec((1, tq, W_BR), lambda b, s, qi, kj: (b, qi[s], 0)),
        scratch_shapes=[pltpu.VMEM((N_HEADS, tq, LANES), F32), pltpu.VMEM((N_HEADS, tq, LANES), F32),
                        pltpu.VMEM((tq, W_BR), F32), pltpu.VMEM((N_HEADS, tq, LANES), F32)],
    )
    return pl.pallas_call(
        functools.partial(_fox_kernel, tq=tq),
        grid_spec=grid_spec,
        out_shape=jax.ShapeDtypeStruct((n, t, W_BR), F32),
        compiler_params=_cparams(("arbitrary", "arbitrary")),
        name="fox_prompt",
    )(qi, kj, main3, main3, main3, c3, ct3)


def _gdn_kernel(qkv_ref, small_ref, convw_ref, gpar_ref, eg_ref, eb_ref, normg_ref, bd_ref,
                o_ref, s_out_ref, ext_s, q_s, k_s, v_s, b_s, g_s, o_s, st_s, *, tb):
    j = pl.program_id(1)

    @pl.when(j == 0)
    def _():
        ext_s[0:8, :] = jnp.zeros((8, 3 * W_BR), F32)
        st_s[...] = jnp.zeros_like(st_s)

    u = qkv_ref[0]
    ext_s[8:8 + tb, :] = u
    cw = convw_ref[...]
    c = (ext_s[5:5 + tb, :] * cw[0:1, :] + ext_s[6:6 + tb, :] * cw[1:2, :]
         + ext_s[7:7 + tb, :] * cw[2:3, :] + u * cw[3:4, :])
    ext_s[0:8, :] = u[tb - 8:tb, :]
    c = _silu(c)
    bd = bd_ref[...]
    q = c[:, 0:W_BR]
    k = c[:, W_BR:2 * W_BR]
    q_s[...] = q * lax.rsqrt(_dot2_l(q * q, bd) + L2_EPS) * (HEAD_DIM ** -0.5)
    k_s[...] = k * lax.rsqrt(_dot2_l(k * k, bd) + L2_EPS)
    v_s[...] = c[:, 2 * W_BR:3 * W_BR]

    sm = small_ref[0]
    gpar = gpar_ref[...]
    g = gpar[0:1, :] * _softplus(sm + gpar[1:2, :])
    beta = _sigmoid(sm)
    gcum = _dot3_r(_chunk_tril(tb), g)
    g_s[...] = _dot3_l(gcum, eg_ref[...])
    b_s[...] = _dot3_l(beta, eb_ref[...])

    _, strict, incl, blk = _chunk_masks()
    ones_b = jnp.ones((CHUNK, CHUNK), BF16)
    eye_heads = jnp.where(_iota2((CHUNK, W_BR), 0) == _iota2((CHUNK, W_BR), 1) % HEAD_DIM, 1.0, 0.0)

    heads = range(N_HEADS)
    lanes = [slice(h * HEAD_DIM, (h + 1) * HEAD_DIM) for h in heads]
    n_chunks = tb // CHUNK
    group = 4

    ops = {}
    for c0 in range(0, n_chunks, group):
        chunks = range(c0, min(c0 + group, n_chunks))
        items = [(c, h) for c in chunks for h in heads]
        rows = lambda c: slice(c * CHUNK, (c + 1) * CHUNK)
        tile = lambda ref: [ref[rows(c), lanes[h]] for c, h in items]
        q_h, k_h, v_h, b_h, g_h = tile(q_s), tile(k_s), tile(v_s), tile(b_s), tile(g_s)
        grow_c = {c: _dot3_r(ones_b, g_s[rows(c), :] * eye_heads) for c in chunks}
        grow = [grow_c[c][:, lanes[h]] for c, h in items]
        idx = range(len(items))
        decay = [jnp.exp(jnp.where(incl > 0, g_h[i] - grow[i], NEG_INF)) for i in idx]
        kb = [k_h[i] * b_h[i] for i in idx]
        both = [_mmb(_bf(jnp.concatenate([kb[i], q_h[i]], axis=0)), _bf(k_h[i]), _DN_NT) for i in idx]
        a = [both[i][0:CHUNK, :] * decay[i] * strict for i in idx]
        attn = [_bf(both[i][CHUNK:2 * CHUNK, :] * decay[i]) for i in idx]
        eg = [jnp.exp(g_h[i]) for i in idx]
        sol = _tri_solve(a, [jnp.concatenate([v_h[i] * b_h[i], kb[i] * eg[i]], axis=1) for i in idx], blk)
        ub = [_bf(sol[i][:, 0:HEAD_DIM]) for i in idx]
        wb = [_bf(sol[i][:, HEAD_DIM:2 * HEAD_DIM]) for i in idx]
        o1 = [_bf(q_h[i] * eg[i] - _mmb(attn[i], wb[i])) for i in idx]
        o2 = [_mmb(attn[i], ub[i]) for i in idx]
        glast = [g_h[i][CHUNK - 1:CHUNK, :] for i in idx]
        kdec = [_bf(k_h[i] * jnp.exp(glast[i] - g_h[i])) for i in idx]
        m = [_bf(_mmb(kdec[i], wb[i], _DN_TN)) for i in idx]
        cc = [_mmb(kdec[i], ub[i], _DN_TN) for i in idx]
        for i, key in enumerate(items):
            ops[key] = (o1[i], o2[i], m[i], cc[i], jnp.exp(glast[i]))

    sts = [st_s[h] for h in heads]
    for c in range(n_chunks):
        stb = [_bf(sts[h]) for h in heads]
        outs = [_mmb(ops[c, h][0], stb[h]) + ops[c, h][1] for h in heads]
        sts = [sts[h] * ops[c, h][4] - _mmb(ops[c, h][2], stb[h]) + ops[c, h][3] for h in heads]
        o_s[c * CHUNK:(c + 1) * CHUNK, :] = jnp.concatenate(outs, axis=1)
    for h in heads:
        st_s[h] = sts[h]

    o = o_s[...]
    ms = _dot2_l(o * o, bd) * (1.0 / HEAD_DIM)
    o_ref[0] = o * lax.rsqrt(ms + RMS_EPS) * normg_ref[...]

    @pl.when(j == pl.num_programs(1) - 1)
    def _():
        s_out_ref[0] = st_s[...]


def _gdn_prompt(main3, small3, conv_w, gpar, eg, eb, normg, bd, tb):
    n, t, _ = main3.shape
    full = lambda shape: pl.BlockSpec(shape, lambda b, j: (0,) * len(shape))
    return pl.pallas_call(
        functools.partial(_gdn_kernel, tb=tb),
        grid=(n, t // tb),
        in_specs=[pl.BlockSpec((1, tb, 3 * W_BR), lambda b, j: (b, j, C_QKV_A // (3 * W_BR))),
                  pl.BlockSpec((1, tb, N_SMALL), lambda b, j: (b, j, 0)),
                  full((CONV_W, 3 * W_BR)), full((8, N_SMALL)), full((N_SMALL, W_BR)), full((N_SMALL, W_BR)),
                  full((1, W_BR)), full((W_BR, W_BR))],
        out_specs=[pl.BlockSpec((1, tb, W_BR), lambda b, j: (b, j, 0)),
                   pl.BlockSpec((1, N_HEADS, HEAD_DIM, HEAD_DIM), lambda b, j: (b, 0, 0, 0))],
        out_shape=[jax.ShapeDtypeStruct((n, t, W_BR), F32),
                   jax.ShapeDtypeStruct((n, N_HEADS, HEAD_DIM, HEAD_DIM), F32)],
        scratch_shapes=[pltpu.VMEM((tb + 8, 3 * W_BR), F32)] + [pltpu.VMEM((tb, W_BR), F32)] * 6
                       + [pltpu.VMEM((N_HEADS, HEAD_DIM, HEAD_DIM), F32)],
        compiler_params=_cparams(("arbitrary", "arbitrary")),
        name="gdn_prompt",
    )(main3, small3, conv_w, gpar, eg, eb, normg, bd)


def _rwkv_kernel(p_ref, small_ref, mu_ref, mus_ref, wup_ref, aup_ref, par_ref, bd_ref,
                 o_ref, s_out_ref, ext_s, exts_s, ah_s, bh_s, kh_s, rh_s, be_s, ke_s, v_s, wl_s, y_s, st_s, *, tb):
    j = pl.program_id(1)

    @pl.when(j == 0)
    def _():
        ext_s[0:8, :] = jnp.zeros((8, 3 * W_BR), F32)
        exts_s[0:8, :] = jnp.zeros((8, N_SMALL), F32)
        st_s[...] = jnp.zeros_like(st_s)

    p = p_ref[0]
    sm = small_ref[0]
    ext_s[8:8 + tb, :] = p
    exts_s[8:8 + tb, :] = sm
    prev = ext_s[7:7 + tb, :]
    prevs = exts_s[7:7 + tb, :]
    ext_s[0:8, :] = p[tb - 8:tb, :]
    exts_s[0:8, :] = sm[tb - 8:tb, :]
    ps = p + (prev - p) * mu_ref[...]
    pss = sm + (prevs - sm) * mus_ref[...]
    r = ps[:, 0:W_BR]
    k = ps[:, W_BR:2 * W_BR]
    v = ps[:, 2 * W_BR:3 * W_BR]
    par = par_ref[...]
    w0, a0, xi, alpha, rho, gn_g, gn_b = (par[i:i + 1, :] for i in range(7))
    d = w0 + _mm(_bf(jnp.tanh(pss)), wup_ref[...])
    logw = -RWKV_DECAY_SCALE * _sigmoid(d)
    a = _sigmoid(a0 + _mm(_bf(pss), aup_ref[...]))
    bd = bd_ref[...]
    kx = k * xi
    kk = kx * lax.rsqrt(_dot2_l(kx * kx, bd) + L2_EPS)
    k2 = k * (1.0 + (a - 1.0) * alpha)
    lc = _dot3_r(_chunk_tril(tb), logw)
    nb = -(a * kk)
    ah_s[...] = _bf(kk * jnp.exp(lc - logw))
    bh_s[...] = _bf(nb * jnp.exp(-lc))
    kh_s[...] = _bf(k2 * jnp.exp(-lc))
    rh_s[...] = _bf(r * jnp.exp(lc))
    v_s[...] = _bf(v)
    ll = jnp.concatenate([jnp.broadcast_to(lc[c * CHUNK + CHUNK - 1:(c + 1) * CHUNK, :], (CHUNK, W_BR))
                          for c in range(tb // CHUNK)], axis=0)
    to_end = jnp.exp(ll - lc)
    be_s[...] = _bf(nb * to_end)
    ke_s[...] = _bf(k2 * to_end)
    wl_s[...] = jnp.exp(ll)

    _, strict, incl, blk = _chunk_masks()
    incl2 = jnp.concatenate([incl, incl], axis=1)

    heads = range(N_HEADS)
    lanes = [slice(h * HEAD_DIM, (h + 1) * HEAD_DIM) for h in heads]
    n_chunks = tb // CHUNK
    group = 4

    ops = {}
    for c0 in range(0, n_chunks, group):
        chunks = range(c0, min(c0 + group, n_chunks))
        items = [(c, h) for c in chunks for h in heads]
        rows = lambda c: slice(c * CHUNK, (c + 1) * CHUNK)
        tile = lambda ref: [ref[rows(c), lanes[h]] for c, h in items]
        a_h, b_h, k_h, r_h, v_h, be_h, ke_h = (tile(x) for x in (ah_s, bh_s, kh_s, rh_s, v_s, be_s, ke_s))
        idx = range(len(items))
        gram = [_mmb(jnp.concatenate([a_h[i], r_h[i]], axis=0), jnp.concatenate([b_h[i], k_h[i]], axis=0), _DN_NT)
                for i in idx]
        akv = [_mmb(_bf(gram[i][0:CHUNK, CHUNK:2 * CHUNK] * strict), v_h[i]) for i in idx]
        sol = _tri_solve([-(gram[i][0:CHUNK, 0:CHUNK] * strict) for i in idx],
                         [jnp.concatenate([a_h[i].astype(F32), akv[i]], axis=1) for i in idx], blk)
        w1b = [_bf(sol[i][:, 0:HEAD_DIM]) for i in idx]
        w2v = [jnp.concatenate([_bf(sol[i][:, HEAD_DIM:2 * HEAD_DIM]), v_h[i]], axis=0) for i in idx]
        r_bk = [_bf(gram[i][CHUNK:2 * CHUNK, :] * incl2) for i in idx]
        g1 = [_bf(r_h[i].astype(F32) + _mmb(r_bk[i][:, 0:CHUNK], w1b[i])) for i in idx]
        g2 = [_mmb(r_bk[i], w2v[i]) for i in idx]
        m1 = [_bf(_mmb(w1b[i], be_h[i], _DN_TN)) for i in idx]
        c2 = [_mmb(w2v[i], jnp.concatenate([be_h[i], ke_h[i]], axis=0), _DN_TN) for i in idx]
        for i, (c, h) in enumerate(items):
            ops[c, h] = (g1[i], g2[i], m1[i], c2[i], wl_s[c * CHUNK:c * CHUNK + 1, lanes[h]])

    sts = [st_s[h] for h in heads]
    for c in range(n_chunks):
        stb = [_bf(sts[h]) for h in heads]
        outs = [_mmb(ops[c, h][0], stb[h], _DN_NT) + ops[c, h][1] for h in heads]
        sts = [sts[h] * ops[c, h][4] + _mmb(stb[h], ops[c, h][2]) + ops[c, h][3] for h in heads]
        y_s[c * CHUNK:(c + 1) * CHUNK, :] = jnp.concatenate(outs, axis=1)
    for h in heads:
        st_s[h] = sts[h]

    y = y_s[...]
    mean = _dot2_l(y, bd) * (1.0 / HEAD_DIM)
    yc = y - mean
    var = _dot2_l(yc * yc, bd) * (1.0 / HEAD_DIM)
    yn = yc * lax.rsqrt(var + GN_EPS) * gn_g + gn_b
    bonus = _dot2_l(r * k2 * rho, bd) * v
    o_ref[0] = yn + bonus

    @pl.when(j == pl.num_programs(1) - 1)
    def _():
        s_out_ref[0] = st_s[...]


def _rwkv_prompt(main3, small3, mu, mus, wup, aup, par, bd, tb):
    n, t, _ = main3.shape
    full = lambda shape: pl.BlockSpec(shape, lambda b, j: (0,) * len(shape))
    return pl.pallas_call(
        functools.partial(_rwkv_kernel, tb=tb),
        grid=(n, t // tb),
        in_specs=[pl.BlockSpec((1, tb, 3 * W_BR), lambda b, j: (b, j, C_RKV_B // (3 * W_BR))),
                  pl.BlockSpec((1, tb, N_SMALL), lambda b, j: (b, j, 0)),
                  full((1, 3 * W_BR)), full((1, N_SMALL)), full((N_SMALL, W_BR)), full((N_SMALL, W_BR)),
                  full((8, W_BR)), full((W_BR, W_BR))],
        out_specs=[pl.BlockSpec((1, tb, W_BR), lambda b, j: (b, j, 0)),
                   pl.BlockSpec((1, N_HEADS, HEAD_DIM, HEAD_DIM), lambda b, j: (b, 0, 0, 0))],
        out_shape=[jax.ShapeDtypeStruct((n, t, W_BR), F32),
                   jax.ShapeDtypeStruct((n, N_HEADS, HEAD_DIM, HEAD_DIM), F32)],
        scratch_shapes=[pltpu.VMEM((tb + 8, 3 * W_BR), F32), pltpu.VMEM((tb + 8, N_SMALL), F32)]
                       + [pltpu.VMEM((tb, W_BR), BF16)] * 7 + [pltpu.VMEM((tb, W_BR), F32)] * 2
                       + [pltpu.VMEM((N_HEADS, HEAD_DIM, HEAD_DIM), F32)],
        compiler_params=_cparams(("arbitrary", "arbitrary")),
        name="rwkv_prompt",
    )(main3, small3, mu, mus, wup, aup, par, bd)


def _pool_kernel(u_ref, w_ref, scale_ref, o_ref, ext_s, *, tb):
    j = pl.program_id(1)

    @pl.when(j == 0)
    def _():
        ext_s[0:16, :] = jnp.zeros((16, W_BR), F32)

    u = u_ref[0]
    ext_s[16:16 + tb, :] = u
    pos = j * tb + _iota2((tb, POOL_GW), 0)
    outs = []
    for gi, wdw in enumerate(POOL_WINDOWS):
        ls = slice(gi * POOL_GW, (gi + 1) * POOL_GW)
        s = u[:, ls]
        for sh in range(1, wdw):
            s = s + ext_s[16 - sh:16 - sh + tb, ls]
        cnt = jnp.minimum(wdw, pos + 1).astype(F32)
        pooled = s / cnt - u[:, ls]
        outs.append(_mm(pooled.astype(BF16), w_ref[gi].astype(BF16)))
    ext_s[0:16, :] = u[tb - 16:tb, :]
    o_ref[0] = jnp.concatenate(outs, axis=1) * scale_ref[...]


def _pool_prompt(main3, pool_w, scale, tb):
    n, t, _ = main3.shape
    return pl.pallas_call(
        functools.partial(_pool_kernel, tb=tb),
        grid=(n, t // tb),
        in_specs=[pl.BlockSpec((1, tb, W_BR), lambda b, j: (b, j, C_U_D // W_BR)),
                  pl.BlockSpec((len(POOL_WINDOWS), POOL_GW, POOL_GW), lambda b, j: (0, 0, 0)),
                  pl.BlockSpec((1, W_BR), lambda b, j: (0, 0))],
        out_specs=pl.BlockSpec((1, tb, W_BR), lambda b, j: (b, j, 0)),
        out_shape=jax.ShapeDtypeStruct((n, t, W_BR), F32),
        scratch_shapes=[pltpu.VMEM((tb + 16, W_BR), F32)],
        compiler_params=_cparams(("arbitrary", "arbitrary")),
        name="pool_prompt",
    )(main3, pool_w, scale)


def _out_kernel(oa_ref, ob_ref, oc_ref, od_ref, za_ref, zb_ref, zc_ref, zd_ref, x_ref, w_ref, g_ref, b_ref, y_ref):
    acc = ALPHA_DN * x_ref[...]
    for i, (o_r, z_r) in enumerate(((oa_ref, za_ref), (ob_ref, zb_ref), (oc_ref, zc_ref), (od_ref, zd_ref))):
        gated = (o_r[...] * _silu(z_r[...])).astype(BF16)
        acc = acc + _mm(gated, w_ref[i * W_BR:(i + 1) * W_BR, :])
    mu = jnp.mean(acc, axis=-1, keepdims=True)
    xc = acc - mu
    var = jnp.mean(xc * xc, axis=-1, keepdims=True)
    y_ref[...] = xc * lax.rsqrt(var + LN_EPS) * g_ref[...] + b_ref[...]


def _out_proj(o_a, o_b, o_c, o_d, main2, x2d, w_out, ln_g, ln_b, tm):
    m = x2d.shape[0]
    ospec = pl.BlockSpec((tm, W_BR), lambda i: (i, 0))
    zspec = lambda col: pl.BlockSpec((tm, W_BR), lambda i: (i, col // W_BR))
    return pl.pallas_call(
        _out_kernel,
        grid=(m // tm,),
        in_specs=[ospec, ospec, ospec, ospec, zspec(C_Z_A), zspec(C_Z_B), zspec(C_Z_C), zspec(C_Z_D),
                  pl.BlockSpec((tm, D_MODEL), lambda i: (i, 0)),
                  pl.BlockSpec((D_MODEL, D_MODEL), lambda i: (0, 0)),
                  pl.BlockSpec((1, D_MODEL), lambda i: (0, 0)), pl.BlockSpec((1, D_MODEL), lambda i: (0, 0))],
        out_specs=pl.BlockSpec((tm, D_MODEL), lambda i: (i, 0)),
        out_shape=jax.ShapeDtypeStruct((m, D_MODEL), F32),
        compiler_params=_cparams(("arbitrary",)),
        name="out_proj",
    )(o_a, o_b, o_c, o_d, main2, main2, main2, main2, x2d, w_out, ln_g, ln_b)


def _rows8(row, nrows=1):
    return jnp.where(_iota2((8, row.shape[1]), 0) < nrows, jnp.broadcast_to(row, (8, row.shape[1])), 0.0)


def _dec_kernel(main_ref, small_ref, sa_ref, conv_ref, sb_ref, shift_ref, shifts_ref, dbuf_ref,
                convw_ref, gpar_ref, normg_ref, mu_ref, mus_ref, wup_ref, aup_ref, par_ref, poolw_ref, pscale_ref,
                oa_ref, ob_ref, od_ref, sa_out, conv_out, sb_out, dbuf_out, ext_s, *, pos):
    row = main_ref[0]
    sm = small_ref[0]
    bd = _head_block_diag()

    u = row[:, C_QKV_A:C_QKV_A + 3 * W_BR]
    buf = conv_ref[0]
    cw = convw_ref[...]
    c = buf[0:1, :] * cw[0:1, :] + buf[1:2, :] * cw[1:2, :] + buf[2:3, :] * cw[2:3, :] + u * cw[3:4, :]
    conv_out[0, 0:2, :] = buf[1:3, :]
    conv_out[0, 2:3, :] = u
    c = _silu(c)
    q = c[:, 0:W_BR]
    k = c[:, W_BR:2 * W_BR]
    v = c[:, 2 * W_BR:3 * W_BR]
    q = q * lax.rsqrt(_dot3_l(_rows8(q * q), bd)[0:1, :] + L2_EPS) * (HEAD_DIM ** -0.5)
    k = k * lax.rsqrt(_dot3_l(_rows8(k * k), bd)[0:1, :] + L2_EPS)
    gpar = gpar_ref[...]
    g = gpar[0:1, :] * _softplus(sm + gpar[1:2, :])
    beta = _sigmoid(sm)
    heads = range(N_HEADS)
    lanes = [slice(h * HEAD_DIM, (h + 1) * HEAD_DIM) for h in heads]
    eg = [jnp.exp(g[:, S_A + h:S_A + h + 1]) for h in heads]
    sts = [sa_ref[0, h] for h in heads]
    k8 = [_rows8(k[:, hs]) for hs in lanes]
    ks = [_mm(k8[h], sts[h], HI)[0:1, :] for h in heads]
    v_new = [beta[:, S_B + h:S_B + h + 1] * (v[:, lanes[h]] - eg[h] * ks[h]) for h in heads]
    st_new = [sts[h] * eg[h] + _mm_tn(k8[h], _rows8(v_new[h]), HI) for h in heads]
    o = jnp.concatenate([_mm(_rows8(q[:, lanes[h]]), st_new[h], HI)[0:1, :] for h in heads], axis=1)
    for h in heads:
        sa_out[0, h] = st_new[h]
    ms = _dot3_l(_rows8(o * o), bd)[0:1, :] * (1.0 / HEAD_DIM)
    oa_ref[0] = o * lax.rsqrt(ms + RMS_EPS) * normg_ref[...]

    p = row[:, C_RKV_B:C_RKV_B + 3 * W_BR]
    ps = p + (shift_ref[0] - p) * mu_ref[...]
    pss = sm + (shifts_ref[0] - sm) * mus_ref[...]
    r = ps[:, 0:W_BR]
    k = ps[:, W_BR:2 * W_BR]
    v = ps[:, 2 * W_BR:3 * W_BR]
    par = par_ref[...]
    w0, a0, xi, alpha, rho, gn_g, gn_b = (par[i:i + 1, :] for i in range(7))
    d = w0 + _mm(_rows8(jnp.tanh(pss)), wup_ref[...], HI)[0:1, :]
    decay = jnp.exp(-RWKV_DECAY_SCALE * _sigmoid(d))
    a = _sigmoid(a0 + _mm(_rows8(pss), aup_ref[...], HI)[0:1, :])
    kx = k * xi
    kk = kx * lax.rsqrt(_dot3_l(_rows8(kx * kx), bd)[0:1, :] + L2_EPS)
    k2 = k * (1.0 + (a - 1.0) * alpha)
    sub = _iota2((8, HEAD_DIM), 0)
    two_rows = lambda r0, r1: jnp.where(sub == 0, jnp.broadcast_to(r0, (8, HEAD_DIM)),
                                        jnp.where(sub == 1, jnp.broadcast_to(r1, (8, HEAD_DIM)), 0.0))
    sts = [sb_ref[0, h] for h in heads]
    s_kk = [_mm_nt(_rows8(-kk[:, lanes[h]]), sts[h], HI)[0:1, :] for h in heads]
    st_new = [sts[h] * decay[:, lanes[h]]
              + _mm_tn(two_rows(s_kk[h], v[:, lanes[h]]), two_rows(kk[:, lanes[h]] * a[:, lanes[h]], k2[:, lanes[h]]), HI)
              for h in heads]
    y = jnp.concatenate([_mm_nt(_rows8(r[:, lanes[h]]), st_new[h], HI)[0:1, :] for h in heads], axis=1)
    for h in heads:
        sb_out[0, h] = st_new[h]
    mean = _dot3_l(_rows8(y), bd)[0:1, :] * (1.0 / HEAD_DIM)
    yc = y - mean
    var = _dot3_l(_rows8(yc * yc), bd)[0:1, :] * (1.0 / HEAD_DIM)
    yn = yc * lax.rsqrt(var + GN_EPS) * gn_g + gn_b
    bonus = _dot3_l(_rows8(r * k2 * rho), bd)[0:1, :] * v
    ob_ref[0] = yn + bonus

    ud = row[:, C_U_D:C_U_D + W_BR]
    ext_s[0:POOL_BUF, :] = dbuf_ref[0]
    ext_s[POOL_BUF:POOL_BUF + 1, :] = ud
    dbuf_out[0] = ext_s[1:POOL_BUF + 1, :]
    outs = []
    for gi, wdw in enumerate(POOL_WINDOWS):
        ls = slice(gi * POOL_GW, (gi + 1) * POOL_GW)
        s = jnp.sum(ext_s[POOL_BUF + 1 - wdw:POOL_BUF + 1, ls], axis=0, keepdims=True)
        pooled = s / float(min(wdw, pos + 1)) - ud[:, ls]
        outs.append(_mm(_rows8(pooled).astype(BF16), poolw_ref[gi].astype(BF16))[0:1, :])
    od_ref[0] = jnp.concatenate(outs, axis=1) * pscale_ref[...]


def _decode_mixers(main_s, small_s, st_a, conv_a, st_b, shift_rkv, shift_small, dbuf,
                   conv_w, gpar, normg, mu, mus, wup, aup, par, pool_w, pscale, pos):
    nd = main_s.shape[0]
    per_seq = lambda shape: pl.BlockSpec((1,) + shape, lambda b: (b,) + (0,) * len(shape))
    full = lambda shape: pl.BlockSpec(shape, lambda b: (0,) * len(shape))
    hh = (N_HEADS, HEAD_DIM, HEAD_DIM)
    return pl.pallas_call(
        functools.partial(_dec_kernel, pos=pos),
        grid=(nd,),
        in_specs=[per_seq((1, N_MAIN)), per_seq((1, N_SMALL)), per_seq(hh), per_seq((CONV_W - 1, 3 * W_BR)),
                  per_seq(hh), per_seq((1, 3 * W_BR)), per_seq((1, N_SMALL)), per_seq((POOL_BUF, W_BR)),
                  full((CONV_W, 3 * W_BR)), full((8, N_SMALL)), full((1, W_BR)),
                  full((1, 3 * W_BR)), full((1, N_SMALL)), full((N_SMALL, W_BR)), full((N_SMALL, W_BR)),
                  full((8, W_BR)), full((len(POOL_WINDOWS), POOL_GW, POOL_GW)), full((1, W_BR))],
        out_specs=[per_seq((1, W_BR)), per_seq((1, W_BR)), per_seq((1, W_BR)), per_seq(hh),
                   per_seq((CONV_W - 1, 3 * W_BR)), per_seq(hh), per_seq((POOL_BUF, W_BR))],
        out_shape=[jax.ShapeDtypeStruct((nd, 1, W_BR), F32)] * 3
                  + [jax.ShapeDtypeStruct((nd,) + hh, F32), jax.ShapeDtypeStruct((nd, CONV_W - 1, 3 * W_BR), F32),
                     jax.ShapeDtypeStruct((nd,) + hh, F32), jax.ShapeDtypeStruct((nd, POOL_BUF, W_BR), F32)],
        scratch_shapes=[pltpu.VMEM((16, W_BR), F32)],
        compiler_params=_cparams(("arbitrary",)),
        name="decode_mixers",
    )(main_s.reshape(nd, 1, N_MAIN), small_s.reshape(nd, 1, N_SMALL), st_a, conv_a, st_b,
      shift_rkv.reshape(nd, 1, 3 * W_BR), shift_small.reshape(nd, 1, N_SMALL), dbuf,
      conv_w, gpar, normg, mu, mus, wup, aup, par, pool_w, pscale)


def _dfox_kernel(pt_ref, qkv_ref, qrep_ref, small_ref, bias_ref, *rest, n_grp):
    kt_refs, vt_refs, lf_refs = rest[0:n_grp], rest[n_grp:2 * n_grp], rest[2 * n_grp:3 * n_grp]
    o_ref, lf_out, m_s, l_s, w_s, r_s, acc_s = rest[3 * n_grp:]
    j = pl.program_id(1)
    scale = HEAD_DIM ** -0.5
    sub8 = _iota2((N_HEADS, LANES), 0)
    lane8 = _iota2((N_HEADS, LANES), 1)

    @pl.when(j == 0)
    def _():
        lf_new = _log_sigmoid(small_ref[0] + bias_ref[...])
        lf_out[0] = lf_new
        qb = (qkv_ref[0, 0] * scale).astype(BF16).astype(F32)
        kb = qkv_ref[0, 1].astype(BF16).astype(F32)
        m_s[...] = jnp.broadcast_to(jnp.sum(qb * kb, axis=1, keepdims=True), (N_HEADS, LANES))
        l_s[...] = jnp.ones_like(l_s)
        w_s[...] = jnp.ones_like(w_s)
        acc_s[...] = jnp.zeros_like(acc_s)
        mine = jnp.where(lane8 == sub8 + S_F, jnp.broadcast_to(lf_new, (N_HEADS, LANES)), 0.0)
        r_s[...] = jnp.broadcast_to(jnp.sum(mine, axis=1, keepdims=True), (N_HEADS, LANES))

    grp = range(n_grp)
    r_t = _iota2((PAGE_SIZE, 2 * LANES), 0)
    c_t = _iota2((PAGE_SIZE, 2 * LANES), 1)
    later_or_all = jnp.where((r_t > c_t) | (c_t >= LANES), 1.0, 0.0).astype(BF16)
    s_t = [jnp.zeros((N_HEADS, LANES), F32) for _ in grp]
    for h in range(N_HEADS):
        q_h = qrep_ref[0, h]
        for g in grp:
            row = jnp.sum(kt_refs[g][0, 0, h] * q_h, axis=0, keepdims=True) * scale
            s_t[g] = jnp.where(sub8 == h, jnp.broadcast_to(row, (N_HEADS, LANES)), s_t[g])
    gates = [_dot3_l(lf_refs[g][0, 0], later_or_all) for g in grp]
    logits, r_run = [], r_s[...]
    for g in grp:
        logits.append(s_t[g] + gates[g][:, 0:LANES] + r_run)
        r_run = r_run + gates[g][:, LANES:2 * LANES]
    m_old = m_s[...]
    m_grp = logits[0]
    for g in grp[1:]:
        m_grp = jnp.maximum(m_grp, logits[g])
    m_new = jnp.maximum(m_old, jnp.max(m_grp, axis=1, keepdims=True))
    alpha = jnp.exp(m_old - m_new)
    p = [jnp.exp(logits[g] - m_new) for g in grp]
    p_sum = p[0]
    for g in grp[1:]:
        p_sum = p_sum + p[g]
    for h in range(N_HEADS):
        acc = acc_s[h] * jnp.broadcast_to(alpha[h:h + 1, :], (HEAD_DIM, LANES))
        for g in grp:
            acc = acc + vt_refs[g][0, 0, h] * jnp.broadcast_to(p[g][h:h + 1, :], (HEAD_DIM, LANES))
        acc_s[h] = acc
    l_s[...] = alpha * l_s[...] + jnp.sum(p_sum, axis=1, keepdims=True)
    w_s[...] = alpha * w_s[...]
    m_s[...] = m_new
    r_s[...] = r_run

    @pl.when(j == pl.num_programs(1) - 1)
    def _():
        ones_b = jnp.ones((N_HEADS, LANES), BF16)
        sub = _iota2((N_HEADS, HEAD_DIM), 0)
        red = jnp.zeros((N_HEADS, HEAD_DIM), F32)
        for h in range(N_HEADS):
            hi, mid, lo = _split3(acc_s[h])
            tot = _mm_nt(ones_b, hi) + _mm_nt(ones_b, mid) + _mm_nt(ones_b, lo)
            red = red + jnp.where(sub == h, tot, 0.0)
        o_ref[0] = (red + w_s[...][:, 0:HEAD_DIM] * qkv_ref[0, 2]) / l_s[...][:, 0:HEAD_DIM]


def _fox_decode(page_table, qkv_s, small_s, bias_row, cache_k, cache_v, cache_lf, layer, n_grp):
    nd, n_pages = page_table.shape
    kt = jnp.transpose(cache_k, (0, 1, 3, 4, 2))
    vt = jnp.transpose(cache_v, (0, 1, 3, 4, 2))
    lft = jnp.transpose(cache_lf, (0, 1, 3, 2))
    qkv8 = qkv_s.reshape(nd, 3, N_HEADS, HEAD_DIM)
    q_rep = jnp.broadcast_to(qkv8[:, 0, :, :, None], (nd, N_HEADS, HEAD_DIM, LANES))
    page = lambda g: (lambda b, j, pt: (layer, pt[b, n_pages - 1 - (j * n_grp + g)], 0, 0, 0))
    page4 = lambda g: (lambda b, j, pt: (layer, pt[b, n_pages - 1 - (j * n_grp + g)], 0, 0))
    kv_spec = lambda g: pl.BlockSpec((1, 1, N_HEADS, HEAD_DIM, PAGE_SIZE), page(g))
    grid_spec = pltpu.PrefetchScalarGridSpec(
        num_scalar_prefetch=1,
        grid=(nd, n_pages // n_grp),
        in_specs=[pl.BlockSpec((1, 3, N_HEADS, HEAD_DIM), lambda b, j, pt: (b, 0, 0, 0)),
                  pl.BlockSpec((1, N_HEADS, HEAD_DIM, LANES), lambda b, j, pt: (b, 0, 0, 0)),
                  pl.BlockSpec((1, 1, N_SMALL), lambda b, j, pt: (b, 0, 0)),
                  pl.BlockSpec((1, N_SMALL), lambda b, j, pt: (0, 0))]
                 + [kv_spec(g) for g in range(n_grp)] + [kv_spec(g) for g in range(n_grp)]
                 + [pl.BlockSpec((1, 1, N_HEADS, PAGE_SIZE), page4(g)) for g in range(n_grp)],
        out_specs=[pl.BlockSpec((1, N_HEADS, HEAD_DIM), lambda b, j, pt: (b, 0, 0)),
                   pl.BlockSpec((1, 1, N_SMALL), lambda b, j, pt: (b, 0, 0))],
        scratch_shapes=[pltpu.VMEM((N_HEADS, LANES), F32)] * 4 + [pltpu.VMEM((N_HEADS, HEAD_DIM, LANES), F32)],
    )
    o, lf_new = pl.pallas_call(
        functools.partial(_dfox_kernel, n_grp=n_grp),
        grid_spec=grid_spec,
        out_shape=[jax.ShapeDtypeStruct((nd, N_HEADS, HEAD_DIM), F32), jax.ShapeDtypeStruct((nd, 1, N_SMALL), F32)],
        compiler_params=_cparams(("arbitrary", "arbitrary")),
        name="fox_decode",
    )(page_table, qkv8, q_rep, small_s.reshape(nd, 1, N_SMALL), bias_row,
      *([kt] * n_grp), *([vt] * n_grp), *([lft] * n_grp))
    return o.reshape(nd, W_BR), lf_new


_MAIN_SEGMENTS = ((O_QKV_A, 3 * W_BR), (O_P_B, 3 * W_BR), (O_QKV_C, 3 * W_BR), (O_Z_A, W_BR), (O_Z_B, W_BR),
                  (O_Z_C, W_BR), (O_U_D, W_BR), (O_Z_D, W_BR))
_SMALL_SEGMENTS = ((O_A_A, 2 * N_HEADS), (O_WL, 2 * LORA_B), (O_F_C, N_HEADS))


def _wprep_kernel(wt_ref, main_ref, small_ref):
    wt = wt_ref[0]
    main_ref[...] = jnp.concatenate([wt[a:a + n, :] for a, n in _MAIN_SEGMENTS], axis=0).T.astype(BF16)
    used = sum(n for _, n in _SMALL_SEGMENTS)
    small = [wt[a:a + n, :] for a, n in _SMALL_SEGMENTS] + [jnp.zeros((N_SMALL - used, wt.shape[1]), F32)]
    small_ref[...] = jnp.concatenate(small, axis=0).T.astype(BF16)


def _prep_w_in(w_in, layer, tr=256):
    _, d, d_in = w_in.shape
    return pl.pallas_call(
        _wprep_kernel,
        grid=(d // tr,),
        in_specs=[pl.BlockSpec((1, d_in, tr), lambda i: (layer, 0, i))],
        out_specs=[pl.BlockSpec((tr, N_MAIN), lambda i: (i, 0)), pl.BlockSpec((tr, N_SMALL), lambda i: (i, 0))],
        out_shape=[jax.ShapeDtypeStruct((d, N_MAIN), BF16), jax.ShapeDtypeStruct((d, N_SMALL), BF16)],
        compiler_params=_cparams(("arbitrary",)),
        name="w_in_prep",
    )(jnp.swapaxes(w_in, 1, 2))


def _small_row(pairs):
    row = jnp.zeros((N_SMALL,), F32)
    for off, vec in pairs:
        row = row.at[off:off + vec.shape[0]].set(vec.astype(F32))
    return row[None, :]


def _layer_params(l, w_in, conv_A, A_log, dt_bias, norm_A, mu_B, w0_B, w_up_B, a0_B, a_up_B, xi_B, alpha_B, rho_B,
                  gn_g_B, gn_b_B, b_f_C, pool_w_D, pool_scale_D, w_out, ln_g, ln_b):
    w_main, w_small = _prep_w_in(w_in, l)
    gpar = jnp.concatenate([_small_row([(S_A, -jnp.exp(A_log[l].astype(F32)))]), _small_row([(S_A, dt_bias[l])]),
                            jnp.zeros((6, N_SMALL), F32)], axis=0)
    head_of_lane = jnp.arange(W_BR) // HEAD_DIM
    lane = jnp.arange(N_SMALL)[:, None]
    eg = (lane == head_of_lane[None, :] + S_A).astype(BF16)
    eb = (lane == head_of_lane[None, :] + S_B).astype(BF16)
    mu = mu_B[l].astype(F32)
    wup = jnp.zeros((N_SMALL, W_BR), F32).at[S_WL:S_WL + LORA_B].set(w_up_B[l].astype(F32))
    aup = jnp.zeros((N_SMALL, W_BR), F32).at[S_AL:S_AL + LORA_B].set(a_up_B[l].astype(F32))
    bd = (head_of_lane[:, None] == head_of_lane[None, :]).astype(BF16)
    par = jnp.stack([w0_B[l], a0_B[l], xi_B[l], alpha_B[l], rho_B[l], gn_g_B[l], gn_b_B[l],
                     jnp.zeros((W_BR,), F32)]).astype(F32)
    return dict(
        w_main=w_main, w_small=w_small, bd=bd, conv_w=conv_A[l].astype(F32), gpar=gpar, eg=eg, eb=eb,
        normg=jnp.tile(norm_A[l].astype(F32), N_HEADS)[None, :],
        mu=mu[None, 0:3 * W_BR], mus=_small_row([(S_WL, mu[3 * W_BR:])]), wup=wup, aup=aup, par=par,
        fbias=_small_row([(S_F, b_f_C[l])]), pool_w=pool_w_D[l].astype(F32), pscale=pool_scale_D[l].astype(F32)[None, :],
        w_out=w_out[l].astype(BF16), ln_g=ln_g[l].astype(F32)[None, :], ln_b=ln_b[l].astype(F32)[None, :])


def _prompt_layer(x3, P, layer, k_all, v_all, tb_rec, tq, tb_gate, tb_pool, tm_proj, tm_out):
    n, t, _ = x3.shape
    x2 = x3.reshape(n * t, D_MODEL)
    main2, small2, k_all, v_all = _proj(x2, P['w_main'], P['w_small'], k_all, v_all, layer, tm_proj)
    main3 = main2.reshape(n, t, N_MAIN)
    small3 = small2.reshape(n, t, N_SMALL)
    o_a, s_a = _gdn_prompt(main3, small3, P['conv_w'], P['gpar'], P['eg'], P['eb'], P['normg'], P['bd'], tb_rec)
    o_b, s_b = _rwkv_prompt(main3, small3, P['mu'], P['mus'], _bf(P['wup']), _bf(P['aup']), P['par'], P['bd'],
                            tb_rec)
    logf3, c3, ct3 = _gates(small3, P['fbias'], tb_gate)
    o_c = _fox_prompt(main3, c3, ct3, tq)
    o_d = _pool_prompt(main3, P['pool_w'], P['pscale'], tb_pool)
    flat = lambda a: a.reshape(n * t, W_BR)
    y2 = _out_proj(flat(o_a), flat(o_b), flat(o_c), flat(o_d), main2, x2, P['w_out'], P['ln_g'], P['ln_b'], tm_out)
    new = (s_a,
           main3[:, t - (CONV_W - 1):, C_QKV_A:C_QKV_A + 3 * W_BR],
           s_b,
           jnp.concatenate([main3[:, t - 1, C_RKV_B:C_RKV_B + 3 * W_BR], small3[:, t - 1, S_WL:S_WL + 2 * LORA_B]], axis=-1),
           logf3[:, :, S_F:S_F + N_HEADS],
           main3[:, t - POOL_BUF:, C_U_D:C_U_D + W_BR])
    return y2.reshape(n, t, D_MODEL), new, k_all, v_all


def _sample_layer(x3, st, cache, layer, k_all, v_all, page_table, P):
    nd = x3.shape[0]
    st_a, conv_a, st_b, shift_b, dbuf = st
    cache_k, cache_v, cache_lf = cache
    x2 = x3.reshape(nd, D_MODEL)
    main_s, small_s, k_all, v_all = _proj(x2, P['w_main'], P['w_small'], k_all, v_all, layer, nd)
    shift_small = jnp.zeros((nd, N_SMALL), F32).at[:, S_WL:S_WL + 2 * LORA_B].set(shift_b[:, 3 * W_BR:].astype(F32))
    o_a, o_b, o_d, s_a, conv_n, s_b, dbuf_n = _decode_mixers(
        main_s, small_s, st_a, conv_a, st_b, shift_b[:, 0:3 * W_BR], shift_small, dbuf,
        P['conv_w'], P['gpar'], P['normg'], P['mu'], P['mus'], P['wup'], P['aup'], P['par'], P['pool_w'], P['pscale'],
        page_table.shape[1] * PAGE_SIZE)
    qkv_s = main_s[:, C_QKV_C:C_QKV_C + 3 * W_BR]
    n_grp = math.gcd(page_table.shape[1], 16)
    o_c, lf_new = _fox_decode(page_table, qkv_s, small_s, P['fbias'], cache_k, cache_v, cache_lf, layer, n_grp)
    flat = lambda a: a.reshape(nd, W_BR)
    y2 = _out_proj(flat(o_a), flat(o_b), flat(o_c), flat(o_d), main_s, x2, P['w_out'], P['ln_g'], P['ln_b'], nd)
    new = (s_a, conv_n, s_b,
           jnp.concatenate([main_s[:, C_RKV_B:C_RKV_B + 3 * W_BR], small_s[:, S_WL:S_WL + 2 * LORA_B]], axis=-1),
           lf_new[:, :, S_F:S_F + N_HEADS],
           dbuf_n)
    return y2.reshape(nd, 1, D_MODEL), new, k_all, v_all


def kernel(x_prompt, x_sample, state_A_S, state_A_conv, state_B_S, state_B_shift, cache_C_k, cache_C_v, cache_C_logf, state_D_buf, page_table, w_in, conv_A, A_log, dt_bias, norm_A, mu_B, w0_B, w_up_B, a0_B, a_up_B, xi_B, alpha_B, rho_B, gn_g_B, gn_b_B, b_f_C, pool_w_D, pool_scale_D, w_out, ln_g, ln_b):
    depth = w_in.shape[0]
    t = x_prompt.shape[1]
    tb_rec = min(256, t)
    tq = min(256, t)
    tb_gate = min(512, t)
    tb_pool = min(512, t)
    tm_proj = min(1024, t)
    tm_out = min(256, x_prompt.shape[0] * t)
    y_p, y_s = x_prompt, x_sample
    prompt_new, sample_new = [], []
    nb, nd = x_prompt.shape[0], x_sample.shape[0]
    pk, pv = jnp.zeros((depth, nb, W_BR, t), F32), jnp.zeros((depth, nb, W_BR, t), F32)
    sk, sv = jnp.zeros((depth, nd, W_BR), F32), jnp.zeros((depth, nd, W_BR), F32)
    for l in range(depth):
        P = _layer_params(l, w_in, conv_A, A_log, dt_bias, norm_A, mu_B, w0_B, w_up_B, a0_B, a_up_B, xi_B, alpha_B,
                          rho_B, gn_g_B, gn_b_B, b_f_C, pool_w_D, pool_scale_D, w_out, ln_g, ln_b)
        y_p, new_p, pk, pv = _prompt_layer(y_p, P, l, pk, pv, tb_rec, tq, tb_gate, tb_pool, tm_proj, tm_out)
        st = (state_A_S[l], state_A_conv[l], state_B_S[l], state_B_shift[l], state_D_buf[l])
        y_s, new_s, sk, sv = _sample_layer(y_s, st, (cache_C_k, cache_C_v, cache_C_logf), l, sk, sv, page_table, P)
        prompt_new.append(new_p)
        sample_new.append(new_s)
    p_a_s, p_a_conv, p_b_s, p_b_shift, p_c_logf, p_d_buf = [jnp.stack([n[i] for n in prompt_new]) for i in range(6)]
    s_a_s, s_a_conv, s_b_s, s_b_shift, s_c_logf, s_d_buf = [jnp.stack([n[i] for n in sample_new]) for i in range(6)]
    p_c_k, p_c_v = (jnp.transpose(a.reshape(depth, nb, N_HEADS, HEAD_DIM, t), (0, 1, 4, 2, 3)) for a in (pk, pv))
    s_c_k, s_c_v = (a.reshape(depth, nd, 1, N_HEADS, HEAD_DIM) for a in (sk, sv))
    return (y_p, y_s, p_a_s, p_a_conv, p_b_s, p_b_shift, p_c_k, p_c_v, p_c_logf, p_d_buf,
            s_a_s, s_a_conv, s_b_s, s_b_shift, s_c_k, s_c_v, s_c_logf, s_d_buf)
```

```python
import functools
import math

import jax
import jax.numpy as jnp
from jax import lax
from jax.experimental import pallas as pl
from jax.experimental.pallas import tpu as pltpu

F32 = jnp.float32
BF16 = jnp.bfloat16
HI = lax.Precision.HIGHEST

D_MODEL = 2048
W_BR = 512
HEAD_DIM = 64
N_HEADS = W_BR // HEAD_DIM
CONV_W = 4
CHUNK = 64
LORA_B = 32
POOL_WINDOWS = (2, 4, 8, 16)
POOL_GW = W_BR // len(POOL_WINDOWS)
POOL_BUF = max(POOL_WINDOWS) - 1
PAGE_SIZE = 128
LANES = 128
FOX_ROW_SPLIT = 2
DEPTH = 2
ALPHA_DN = (2.0 * DEPTH) ** 0.25
LN_EPS = 1e-5
GN_EPS = 64e-5
RMS_EPS = 1e-6
L2_EPS = 1e-6
RWKV_DECAY_SCALE = math.exp(-0.5)
NEG_INF = -1e30
LOG2_E = 1.4426950408889634

C_QKV_A, C_RKV_B, C_QKV_C = 0, 1536, 3072
C_Z_A, C_Z_B, C_Z_C, C_U_D, C_Z_D = 4608, 5120, 5632, 6144, 6656
N_MAIN = 7168
S_A, S_B, S_WL, S_AL, S_F = 0, 8, 16, 48, 80
N_SMALL = 128
O_QKV_A, O_A_A, O_Z_A, O_P_B, O_WL, O_Z_B, O_QKV_C, O_F_C, O_Z_C, O_U_D, O_Z_D, D_IN = (
    0, 1536, 1552, 2064, 3600, 3664, 4176, 5712, 5720, 6232, 6744, 7256)

VMEM_LIMIT = 48 * 1024 * 1024


VMEM_LIMIT_PROJ = 56 * 1024 * 1024


def _cparams(sem, vmem_limit=VMEM_LIMIT):
    return pltpu.CompilerParams(dimension_semantics=sem, vmem_limit_bytes=vmem_limit)


def _mm(a, b, prec=None):
    return jnp.dot(a, b, preferred_element_type=F32, precision=prec)


def _mm_nt(a, b, prec=None):
    return lax.dot_general(a, b, (((1,), (1,)), ((), ())), preferred_element_type=F32, precision=prec)


def _mm_tn(a, b, prec=None):
    return lax.dot_general(a, b, (((0,), (0,)), ((), ())), preferred_element_type=F32, precision=prec)


def _split3(x):
    hi = x.astype(BF16)
    r1 = x - hi.astype(F32)
    mid = r1.astype(BF16)
    lo = (r1 - mid.astype(F32)).astype(BF16)
    return hi, mid, lo


def _dot3_l(x, b_bf16):
    hi, mid, lo = _split3(x)
    return _mm(hi, b_bf16) + _mm(mid, b_bf16) + _mm(lo, b_bf16)


def _head_sums(x, same_head_bf16):
    return _mm(x.astype(BF16), same_head_bf16)


def _dot3_r(a_bf16, x):
    hi, mid, lo = _split3(x)
    return _mm(a_bf16, hi) + _mm(a_bf16, mid) + _mm(a_bf16, lo)


def _sigmoid(x):
    return 1.0 / (1.0 + jnp.exp(-x))


def _silu(x):
    return x * _sigmoid(x)


def _softplus(x):
    return jnp.maximum(x, 0.0) + jnp.log1p(jnp.exp(-jnp.abs(x)))


def _log_sigmoid(x):
    return -_softplus(-x)


def _iota2(shape, dim):
    return lax.broadcasted_iota(jnp.int32, shape, dim)


def _head_block_diag():
    r = _iota2((W_BR, W_BR), 0) // HEAD_DIM
    c = _iota2((W_BR, W_BR), 1) // HEAD_DIM
    return jnp.where(r == c, 1.0, 0.0).astype(BF16)


def _chunk_tril(n):
    r = _iota2((n, n), 0)
    c = _iota2((n, n), 1)
    return jnp.where((r >= c) & (r // CHUNK == c // CHUNK), 1.0, 0.0).astype(BF16)


_DN_NN = (((1,), (0,)), ((), ()))
_DN_NT = (((1,), (1,)), ((), ()))
_DN_TN = (((0,), (0,)), ((), ()))


def _bf(x):
    return x.astype(BF16)


def _mmb(a, b, dn=_DN_NN):
    return lax.dot_general(a, b, dn, preferred_element_type=F32)


def _tri_solve(a_list, rhs_list, blk):
    each = lambda f, *ls: [f(*xs) for xs in zip(*ls)]
    width = rhs_list[0].shape[1]
    ad = each(lambda a: a * blk, a_list)
    adb = each(_bf, ad)
    a2b = each(_bf, each(_mmb, adb, adb))
    a4b = each(_bf, each(_mmb, a2b, a2b))
    a8b = each(_bf, each(_mmb, a4b, a4b))
    z = each(lambda r, a, d: jnp.concatenate([r, a - d], axis=1), rhs_list, a_list, ad)
    z = each(lambda x, y: x - y, z, each(_mmb, adb, each(_bf, z)))
    for pw in (a2b, a4b, a8b):
        z = each(lambda x, y: x + y, z, each(_mmb, pw, each(_bf, z)))
    y = each(lambda x: x[:, 0:width], z)
    nb = each(lambda x: _bf(x[:, width:width + CHUNK]), z)
    n2b = each(_bf, each(_mmb, nb, nb))
    t = each(lambda x, u: x + u, y, each(_mmb, n2b, each(_bf, y)))
    return each(lambda x, u: x - u, t, each(_mmb, nb, each(_bf, t)))


def _chunk_masks():
    r = _iota2((CHUNK, CHUNK), 0)
    c = _iota2((CHUNK, CHUNK), 1)
    eye = jnp.where(r == c, 1.0, 0.0).astype(F32)
    strict = jnp.where(r > c, 1.0, 0.0).astype(F32)
    incl = jnp.where(r >= c, 1.0, 0.0).astype(F32)
    blk = jnp.where(r // 16 == c // 16, 1.0, 0.0).astype(F32)
    return eye, strict, incl, blk


def _proj_kernel(x_ref, w_ref, ws_ref, kin_ref, vin_ref, main_ref, small_ref, k_ref, v_ref, xb_ref, *, tn,
                 token_minor):
    j = pl.program_id(1)

    @pl.when(j == 0)
    def _():
        xb = x_ref[...].astype(BF16)
        xb_ref[...] = xb
        small_ref[...] = _mm(xb, ws_ref[...])

    acc = _mm(xb_ref[...], w_ref[...])
    main_ref[...] = acc
    for col, out_ref in ((C_QKV_C + W_BR, k_ref), (C_QKV_C + 2 * W_BR, v_ref)):
        @pl.when(j == col // tn)
        def _(col=col, out_ref=out_ref):
            kv = acc[:, col % tn:col % tn + W_BR]
            if token_minor:
                out_ref[0, 0] = kv.T
            else:
                out_ref[0] = kv


def _proj(x2d, w_main, w_small, k_all, v_all, layer, tm, tn=1024):
    m = x2d.shape[0]
    assert (C_QKV_C + W_BR) % tn + W_BR <= tn and (C_QKV_C + 2 * W_BR) % tn + W_BR <= tn
    token_minor = k_all.ndim == 4
    if token_minor:
        per_seq = k_all.shape[3] // tm
        kv_spec = pl.BlockSpec((1, 1, W_BR, tm), lambda i, j: (layer, i // per_seq, 0, i % per_seq))
    else:
        kv_spec = pl.BlockSpec((1, tm, W_BR), lambda i, j: (layer, i, 0))
    return pl.pallas_call(
        functools.partial(_proj_kernel, tn=tn, token_minor=token_minor),
        grid=(m // tm, N_MAIN // tn),
        in_specs=[pl.BlockSpec((tm, D_MODEL), lambda i, j: (i, 0)),
                  pl.BlockSpec((D_MODEL, tn), lambda i, j: (0, j)),
                  pl.BlockSpec((D_MODEL, N_SMALL), lambda i, j: (0, 0)),
                  pl.BlockSpec(memory_space=pl.ANY), pl.BlockSpec(memory_space=pl.ANY)],
        out_specs=[pl.BlockSpec((tm, tn), lambda i, j: (i, j)),
                   pl.BlockSpec((tm, N_SMALL), lambda i, j: (i, 0)), kv_spec, kv_spec],
        out_shape=[jax.ShapeDtypeStruct((m, N_MAIN), F32), jax.ShapeDtypeStruct((m, N_SMALL), F32),
                   jax.ShapeDtypeStruct(k_all.shape, F32), jax.ShapeDtypeStruct(v_all.shape, F32)],
        input_output_aliases={3: 2, 4: 3},
        scratch_shapes=[pltpu.VMEM((tm, D_MODEL), BF16)],
        compiler_params=_cparams(("arbitrary", "arbitrary"), VMEM_LIMIT_PROJ),
        name="proj",
    )(x2d, w_main, w_small, k_all, v_all)


def _gates_kernel(small_ref, bias_ref, logf_ref, c_ref, ct_ref, carry_ref, *, tb):
    @pl.when(pl.program_id(1) == 0)
    def _():
        carry_ref[...] = jnp.zeros_like(carry_ref)

    logf = _log_sigmoid(small_ref[0] + bias_ref[...])
    r = _iota2((tb, tb), 0)
    c = _iota2((tb, tb), 1)
    tril = jnp.where(r >= c, 1.0, 0.0).astype(BF16)
    cum = _dot3_r(tril, logf) + carry_ref[0:1, :]
    carry_ref[...] = jnp.broadcast_to(cum[tb - 1:tb, :], carry_ref.shape)
    logf_ref[0] = logf
    c_ref[0] = cum
    ct_ref[0] = cum.T[S_F:S_F + N_HEADS, :]


def _gates(small3, bias_row, tb):
    n, t, _ = small3.shape
    return pl.pallas_call(
        functools.partial(_gates_kernel, tb=tb),
        grid=(n, t // tb),
        in_specs=[pl.BlockSpec((1, tb, N_SMALL), lambda b, j: (b, j, 0)),
                  pl.BlockSpec((1, N_SMALL), lambda b, j: (0, 0))],
        out_specs=[pl.BlockSpec((1, tb, N_SMALL), lambda b, j: (b, j, 0)),
                   pl.BlockSpec((1, tb, N_SMALL), lambda b, j: (b, j, 0)),
                   pl.BlockSpec((1, N_HEADS, tb), lambda b, j: (b, 0, j))],
        out_shape=[jax.ShapeDtypeStruct((n, t, N_SMALL), F32), jax.ShapeDtypeStruct((n, t, N_SMALL), F32),
                   jax.ShapeDtypeStruct((n, N_HEADS, t), F32)],
        scratch_shapes=[pltpu.VMEM((8, N_SMALL), F32)],
        compiler_params=_cparams(("arbitrary", "arbitrary")),
        name="fox_gates",
    )(small3, bias_row)


def _fox_kernel(qi_ref, kj_ref, q_ref, k_ref, v_ref, c_ref, ct_ref, o_ref, m_s, l_s, acc_s, cq_s, *, tq, n_sub):
    i = qi_ref[pl.program_id(1)]
    j = kj_ref[pl.program_id(1)]
    n_pairs = W_BR // LANES
    lo_half = _iota2((tq, LANES), 1) < HEAD_DIM
    pair_lanes = [slice(pr * LANES, (pr + 1) * LANES) for pr in range(n_pairs)]

    @pl.when(j == 0)
    def _():
        m_s[...] = jnp.full(m_s.shape, NEG_INF, F32)
        l_s[...] = jnp.zeros_like(l_s)
        acc_s[...] = jnp.zeros_like(acc_s)
        cq = c_ref[0]
        for h in range(N_HEADS):
            cq_s[h] = jnp.broadcast_to(cq[:, S_F + h:S_F + h + 1] * LOG2_E, (tq, LANES))

    def step(masked):
        k = k_ref[0].astype(BF16)
        v = v_ref[0]
        ct = ct_ref[0] * LOG2_E
        v_bd = [jnp.concatenate([jnp.where(lo_half, v[:, ps], 0.0), jnp.where(lo_half, 0.0, v[:, ps])],
                                axis=0).astype(BF16) for ps in pair_lanes]
        rq = tq // n_sub
        lo_q = _iota2((rq, LANES), 1) < HEAD_DIM
        wide = lambda x: jnp.concatenate([x] * (tq // LANES), axis=1)
        for r in range(n_sub):
            rs = slice(r * rq, (r + 1) * rq)
            q = q_ref[0, rs, :] * (HEAD_DIM ** -0.5 * LOG2_E)
            if masked:
                keep = _iota2((rq, tq), 0) + r * rq >= _iota2((rq, tq), 1)
            s_all, m_all = [], []
            for pr, ps in enumerate(pair_lanes):
                qp, kp = q[:, ps], k[:, ps]
                for e, q_half in enumerate((jnp.where(lo_q, qp, 0.0), jnp.where(lo_q, 0.0, qp))):
                    h = 2 * pr + e
                    s = _mm_nt(q_half.astype(BF16), kp) + wide(cq_s[h, rs, :]) - ct[h:h + 1, :]
                    if masked:
                        s = jnp.where(keep, s, NEG_INF)
                    s_all.append(s)
                    m_all.append(jnp.maximum(m_s[h, rs, :], jnp.max(s, axis=1, keepdims=True)))
            p_all, alpha_all = [], []
            for h in range(N_HEADS):
                m_new = m_all[h]
                alpha = jnp.exp2(m_s[h, rs, :] - m_new)
                p = jnp.exp2(s_all[h] - wide(m_new))
                l_s[h, rs, :] = alpha * l_s[h, rs, :] + jnp.sum(p, axis=1, keepdims=True)
                m_s[h, rs, :] = m_new
                p_all.append(p.astype(BF16))
                alpha_all.append(alpha)
            for pr, ps in enumerate(pair_lanes):
                p_pair = jnp.concatenate([p_all[2 * pr], p_all[2 * pr + 1]], axis=1)
                alpha_p = jnp.where(lo_q, alpha_all[2 * pr], alpha_all[2 * pr + 1])
                acc_s[rs, ps] = alpha_p * acc_s[rs, ps] + _mm(p_pair, v_bd[pr])

    @pl.when(j < i)
    def _():
        step(False)

    @pl.when(j == i)
    def _():
        step(True)
        for pr, ps in enumerate(pair_lanes):
            o_ref[0, :, ps] = acc_s[:, ps] / jnp.where(lo_half, l_s[2 * pr], l_s[2 * pr + 1])


def _fox_prompt(main3, c3, ct3, tq):
    n, t, _ = main3.shape
    nb = t // tq
    qb, kb, vb = C_QKV_C // W_BR, C_QKV_C // W_BR + 1, C_QKV_C // W_BR + 2
    pairs = [(i, j) for i in range(nb) for j in range(i + 1)]
    qi = jnp.asarray([p[0] for p in pairs], jnp.int32)
    kj = jnp.asarray([p[1] for p in pairs], jnp.int32)
    grid_spec = pltpu.PrefetchScalarGridSpec(
        num_scalar_prefetch=2,
        grid=(n, len(pairs)),
        in_specs=[pl.BlockSpec((1, tq, W_BR), lambda b, s, qi, kj: (b, qi[s], qb)),
                  pl.BlockSpec((1, tq, W_BR), lambda b, s, qi, kj: (b, kj[s], kb)),
                  pl.BlockSpec((1, tq, W_BR), lambda b, s, qi, kj: (b, kj[s], vb)),
                  pl.BlockSpec((1, tq, N_SMALL), lambda b, s, qi, kj: (b, qi[s], 0)),
                  pl.BlockSpec((1, N_HEADS, tq), lambda b, s, qi, kj: (b, 0, kj[s]))],
        out_specs=pl.BlockSpec((1, tq, W_BR), lambda b, s, qi, kj: (b, qi[s], 0)),
        scratch_shapes=[pltpu.VMEM((N_HEADS, tq, LANES), F32), pltpu.VMEM((N_HEADS, tq, LANES), F32),
                        pltpu.VMEM((tq, W_BR), F32), pltpu.VMEM((N_HEADS, tq, LANES), F32)],
    )
    return pl.pallas_call(
        functools.partial(_fox_kernel, tq=tq, n_sub=FOX_ROW_SPLIT),
        grid_spec=grid_spec,
        out_shape=jax.ShapeDtypeStruct((n, t, W_BR), F32),
        compiler_params=_cparams(("arbitrary", "arbitrary")),
        name="fox_prompt",
    )(qi, kj, main3, main3, main3, c3, ct3)


def _gdn_kernel(qkv_ref, small_ref, convw_ref, gpar_ref, eg_ref, eb_ref, normg_ref, bd_ref,
                o_ref, s_out_ref, ext_s, q_s, k_s, v_s, b_s, g_s, o_s, st_s, *, tb):
    j = pl.program_id(1)

    @pl.when(j == 0)
    def _():
        ext_s[0:8, :] = jnp.zeros((8, 3 * W_BR), F32)
        st_s[...] = jnp.zeros_like(st_s)

    u = qkv_ref[0]
    ext_s[8:8 + tb, :] = u
    cw = convw_ref[...]
    c = (ext_s[5:5 + tb, :] * cw[0:1, :] + ext_s[6:6 + tb, :] * cw[1:2, :]
         + ext_s[7:7 + tb, :] * cw[2:3, :] + u * cw[3:4, :])
    ext_s[0:8, :] = u[tb - 8:tb, :]
    c = _silu(c)
    bd = bd_ref[...]
    q = c[:, 0:W_BR]
    k = c[:, W_BR:2 * W_BR]
    q_s[...] = q * lax.rsqrt(_head_sums(q * q, bd) + L2_EPS) * (HEAD_DIM ** -0.5)
    k_s[...] = k * lax.rsqrt(_head_sums(k * k, bd) + L2_EPS)
    v_s[...] = c[:, 2 * W_BR:3 * W_BR]

    sm = small_ref[0]
    gpar = gpar_ref[...]
    g = gpar[0:1, :] * _softplus(sm + gpar[1:2, :])
    beta = _sigmoid(sm)
    gcum = _dot3_r(_chunk_tril(tb), g)
    g_s[...] = _dot3_l(gcum, eg_ref[...])
    b_s[...] = _dot3_l(beta, eb_ref[...])

    _, strict, incl, blk = _chunk_masks()
    ones_b = jnp.ones((CHUNK, CHUNK), BF16)
    eye_heads = jnp.where(_iota2((CHUNK, W_BR), 0) == _iota2((CHUNK, W_BR), 1) % HEAD_DIM, 1.0, 0.0)

    heads = range(N_HEADS)
    lanes = [slice(h * HEAD_DIM, (h + 1) * HEAD_DIM) for h in heads]
    n_chunks = tb // CHUNK
    group = 4

    ops = {}
    for c0 in range(0, n_chunks, group):
        chunks = range(c0, min(c0 + group, n_chunks))
        items = [(c, h) for c in chunks for h in heads]
        rows = lambda c: slice(c * CHUNK, (c + 1) * CHUNK)
        tile = lambda ref: [ref[rows(c), lanes[h]] for c, h in items]
        q_h, k_h, v_h, b_h, g_h = tile(q_s), tile(k_s), tile(v_s), tile(b_s), tile(g_s)
        grow_c = {c: _dot3_r(ones_b, g_s[rows(c), :] * eye_heads) for c in chunks}
        grow = [grow_c[c][:, lanes[h]] for c, h in items]
        idx = range(len(items))
        decay = [jnp.exp(jnp.where(incl > 0, g_h[i] - grow[i], NEG_INF)) for i in idx]
        kb = [k_h[i] * b_h[i] for i in idx]
        both = [_mmb(_bf(jnp.concatenate([kb[i], q_h[i]], axis=0)), _bf(k_h[i]), _DN_NT) for i in idx]
        a = [both[i][0:CHUNK, :] * decay[i] * strict for i in idx]
        attn = [_bf(both[i][CHUNK:2 * CHUNK, :] * decay[i]) for i in idx]
        eg = [jnp.exp(g_h[i]) for i in idx]
        sol = _tri_solve(a, [jnp.concatenate([v_h[i] * b_h[i], kb[i] * eg[i]], axis=1) for i in idx], blk)
        ub = [_bf(sol[i][:, 0:HEAD_DIM]) for i in idx]
        wb = [_bf(sol[i][:, HEAD_DIM:2 * HEAD_DIM]) for i in idx]
        o1 = [_bf(q_h[i] * eg[i] - _mmb(attn[i], wb[i])) for i in idx]
        o2 = [_mmb(attn[i], ub[i]) for i in idx]
        glast = [g_h[i][CHUNK - 1:CHUNK, :] for i in idx]
        kdec = [_bf(k_h[i] * jnp.exp(glast[i] - g_h[i])) for i in idx]
        m = [_bf(_mmb(kdec[i], wb[i], _DN_TN)) for i in idx]
        cc = [_mmb(kdec[i], ub[i], _DN_TN) for i in idx]
        for i, key in enumerate(items):
            ops[key] = (o1[i], o2[i], m[i], cc[i], jnp.exp(glast[i]))

    sts = [st_s[h] for h in heads]
    for c in range(n_chunks):
        stb = [_bf(sts[h]) for h in heads]
        outs = [_mmb(ops[c, h][0], stb[h]) + ops[c, h][1] for h in heads]
        sts = [sts[h] * ops[c, h][4] - _mmb(ops[c, h][2], stb[h]) + ops[c, h][3] for h in heads]
        o_s[c * CHUNK:(c + 1) * CHUNK, :] = jnp.concatenate(outs, axis=1)
    for h in heads:
        st_s[h] = sts[h]

    o = o_s[...]
    ms = _head_sums(o * o, bd) * (1.0 / HEAD_DIM)
    o_ref[0] = o * lax.rsqrt(ms + RMS_EPS) * normg_ref[...]

    @pl.when(j == pl.num_programs(1) - 1)
    def _():
        s_out_ref[0] = st_s[...]


def _gdn_prompt(main3, small3, conv_w, gpar, eg, eb, normg, bd, tb):
    n, t, _ = main3.shape
    full = lambda shape: pl.BlockSpec(shape, lambda b, j: (0,) * len(shape))
    return pl.pallas_call(
        functools.partial(_gdn_kernel, tb=tb),
        grid=(n, t // tb),
        in_specs=[pl.BlockSpec((1, tb, 3 * W_BR), lambda b, j: (b, j, C_QKV_A // (3 * W_BR))),
                  pl.BlockSpec((1, tb, N_SMALL), lambda b, j: (b, j, 0)),
                  full((CONV_W, 3 * W_BR)), full((8, N_SMALL)), full((N_SMALL, W_BR)), full((N_SMALL, W_BR)),
                  full((1, W_BR)), full((W_BR, W_BR))],
        out_specs=[pl.BlockSpec((1, tb, W_BR), lambda b, j: (b, j, 0)),
                   pl.BlockSpec((1, N_HEADS, HEAD_DIM, HEAD_DIM), lambda b, j: (b, 0, 0, 0))],
        out_shape=[jax.ShapeDtypeStruct((n, t, W_BR), F32),
                   jax.ShapeDtypeStruct((n, N_HEADS, HEAD_DIM, HEAD_DIM), F32)],
        scratch_shapes=[pltpu.VMEM((tb + 8, 3 * W_BR), F32)] + [pltpu.VMEM((tb, W_BR), F32)] * 6
                       + [pltpu.VMEM((N_HEADS, HEAD_DIM, HEAD_DIM), F32)],
        compiler_params=_cparams(("arbitrary", "arbitrary")),
        name="gdn_prompt",
    )(main3, small3, conv_w, gpar, eg, eb, normg, bd)


def _rwkv_kernel(p_ref, small_ref, mu_ref, mus_ref, wup_ref, aup_ref, par_ref, bd_ref,
                 o_ref, s_out_ref, ext_s, exts_s, ah_s, bh_s, kh_s, rh_s, be_s, ke_s, v_s, wl_s, y_s, st_s, *, tb):
    j = pl.program_id(1)

    @pl.when(j == 0)
    def _():
        ext_s[0:8, :] = jnp.zeros((8, 3 * W_BR), F32)
        exts_s[0:8, :] = jnp.zeros((8, N_SMALL), F32)
        st_s[...] = jnp.zeros_like(st_s)

    p = p_ref[0]
    sm = small_ref[0]
    ext_s[8:8 + tb, :] = p
    exts_s[8:8 + tb, :] = sm
    prev = ext_s[7:7 + tb, :]
    prevs = exts_s[7:7 + tb, :]
    ext_s[0:8, :] = p[tb - 8:tb, :]
    exts_s[0:8, :] = sm[tb - 8:tb, :]
    ps = p + (prev - p) * mu_ref[...]
    pss = sm + (prevs - sm) * mus_ref[...]
    r = ps[:, 0:W_BR]
    k = ps[:, W_BR:2 * W_BR]
    v = ps[:, 2 * W_BR:3 * W_BR]
    par = par_ref[...]
    w0, a0, xi, alpha, rho, gn_g, gn_b = (par[i:i + 1, :] for i in range(7))
    d = w0 + _mm(_bf(jnp.tanh(pss)), wup_ref[...])
    logw = -RWKV_DECAY_SCALE * _sigmoid(d)
    a = _sigmoid(a0 + _mm(_bf(pss), aup_ref[...]))
    bd = bd_ref[...]
    kx = k * xi
    kk = kx * lax.rsqrt(_head_sums(kx * kx, bd) + L2_EPS)
    k2 = k * (1.0 + (a - 1.0) * alpha)
    lc = _dot3_r(_chunk_tril(tb), logw)
    nb = -(a * kk)
    ah_s[...] = _bf(kk * jnp.exp(lc - logw))
    bh_s[...] = _bf(nb * jnp.exp(-lc))
    kh_s[...] = _bf(k2 * jnp.exp(-lc))
    rh_s[...] = _bf(r * jnp.exp(lc))
    v_s[...] = _bf(v)
    ll = jnp.concatenate([jnp.broadcast_to(lc[c * CHUNK + CHUNK - 1:(c + 1) * CHUNK, :], (CHUNK, W_BR))
                          for c in range(tb // CHUNK)], axis=0)
    to_end = jnp.exp(ll - lc)
    be_s[...] = _bf(nb * to_end)
    ke_s[...] = _bf(k2 * to_end)
    wl_s[...] = jnp.exp(ll)

    _, strict, incl, blk = _chunk_masks()
    incl2 = jnp.concatenate([incl, incl], axis=1)

    heads = range(N_HEADS)
    lanes = [slice(h * HEAD_DIM, (h + 1) * HEAD_DIM) for h in heads]
    n_chunks = tb // CHUNK
    group = 4

    ops = {}
    for c0 in range(0, n_chunks, group):
        chunks = range(c0, min(c0 + group, n_chunks))
        items = [(c, h) for c in chunks for h in heads]
        rows = lambda c: slice(c * CHUNK, (c + 1) * CHUNK)
        tile = lambda ref: [ref[rows(c), lanes[h]] for c, h in items]
        a_h, b_h, k_h, r_h, v_h, be_h, ke_h = (tile(x) for x in (ah_s, bh_s, kh_s, rh_s, v_s, be_s, ke_s))
        idx = range(len(items))
        gram = [_mmb(jnp.concatenate([a_h[i], r_h[i]], axis=0), jnp.concatenate([b_h[i], k_h[i]], axis=0), _DN_NT)
                for i in idx]
        akv = [_mmb(_bf(gram[i][0:CHUNK, CHUNK:2 * CHUNK] * strict), v_h[i]) for i in idx]
        sol = _tri_solve([-(gram[i][0:CHUNK, 0:CHUNK] * strict) for i in idx],
                         [jnp.concatenate([a_h[i].astype(F32), akv[i]], axis=1) for i in idx], blk)
        w1b = [_bf(sol[i][:, 0:HEAD_DIM]) for i in idx]
        w2v = [jnp.concatenate([_bf(sol[i][:, HEAD_DIM:2 * HEAD_DIM]), v_h[i]], axis=0) for i in idx]
        r_bk = [_bf(gram[i][CHUNK:2 * CHUNK, :] * incl2) for i in idx]
        g1 = [_bf(r_h[i].astype(F32) + _mmb(r_bk[i][:, 0:CHUNK], w1b[i])) for i in idx]
        g2 = [_mmb(r_bk[i], w2v[i]) for i in idx]
        m1 = [_bf(_mmb(w1b[i], be_h[i], _DN_TN)) for i in idx]
        c2 = [_mmb(w2v[i], jnp.concatenate([be_h[i], ke_h[i]], axis=0), _DN_TN) for i in idx]
        for i, (c, h) in enumerate(items):
            ops[c, h] = (g1[i], g2[i], m1[i], c2[i], wl_s[c * CHUNK:c * CHUNK + 1, lanes[h]])

    sts = [st_s[h] for h in heads]
    for c in range(n_chunks):
        stb = [_bf(sts[h]) for h in heads]
        outs = [_mmb(ops[c, h][0], stb[h], _DN_NT) + ops[c, h][1] for h in heads]
        sts = [sts[h] * ops[c, h][4] + _mmb(stb[h], ops[c, h][2]) + ops[c, h][3] for h in heads]
        y_s[c * CHUNK:(c + 1) * CHUNK, :] = jnp.concatenate(outs, axis=1)
    for h in heads:
        st_s[h] = sts[h]

    y = y_s[...]
    mean = _head_sums(y, bd) * (1.0 / HEAD_DIM)
    yc = y - mean
    var = _head_sums(yc * yc, bd) * (1.0 / HEAD_DIM)
    yn = yc * lax.rsqrt(var + GN_EPS) * gn_g + gn_b
    bonus = _head_sums(r * k2 * rho, bd) * v
    o_ref[0] = yn + bonus

    @pl.when(j == pl.num_programs(1) - 1)
    def _():
        s_out_ref[0] = st_s[...]


def _rwkv_prompt(main3, small3, mu, mus, wup, aup, par, bd, tb):
    n, t, _ = main3.shape
    full = lambda shape: pl.BlockSpec(shape, lambda b, j: (0,) * len(shape))
    return pl.pallas_call(
        functools.partial(_rwkv_kernel, tb=tb),
        grid=(n, t // tb),
        in_specs=[pl.BlockSpec((1, tb, 3 * W_BR), lambda b, j: (b, j, C_RKV_B // (3 * W_BR))),
                  pl.BlockSpec((1, tb, N_SMALL), lambda b, j: (b, j, 0)),
                  full((1, 3 * W_BR)), full((1, N_SMALL)), full((N_SMALL, W_BR)), full((N_SMALL, W_BR)),
                  full((8, W_BR)), full((W_BR, W_BR))],
        out_specs=[pl.BlockSpec((1, tb, W_BR), lambda b, j: (b, j, 0)),
                   pl.BlockSpec((1, N_HEADS, HEAD_DIM, HEAD_DIM), lambda b, j: (b, 0, 0, 0))],
        out_shape=[jax.ShapeDtypeStruct((n, t, W_BR), F32),
                   jax.ShapeDtypeStruct((n, N_HEADS, HEAD_DIM, HEAD_DIM), F32)],
        scratch_shapes=[pltpu.VMEM((tb + 8, 3 * W_BR), F32), pltpu.VMEM((tb + 8, N_SMALL), F32)]
                       + [pltpu.VMEM((tb, W_BR), BF16)] * 7 + [pltpu.VMEM((tb, W_BR), F32)] * 2
                       + [pltpu.VMEM((N_HEADS, HEAD_DIM, HEAD_DIM), F32)],
        compiler_params=_cparams(("arbitrary", "arbitrary")),
        name="rwkv_prompt",
    )(main3, small3, mu, mus, wup, aup, par, bd)


def _pool_kernel(u_ref, w_ref, scale_ref, o_ref, ext_s, *, tb):
    j = pl.program_id(1)

    @pl.when(j == 0)
    def _():
        ext_s[0:16, :] = jnp.zeros((16, W_BR), F32)

    u = u_ref[0]
    ext_s[16:16 + tb, :] = u
    pos = j * tb + _iota2((tb, POOL_GW), 0)
    outs = []
    for gi, wdw in enumerate(POOL_WINDOWS):
        ls = slice(gi * POOL_GW, (gi + 1) * POOL_GW)
        s = u[:, ls]
        for sh in range(1, wdw):
            s = s + ext_s[16 - sh:16 - sh + tb, ls]
        cnt = jnp.minimum(wdw, pos + 1).astype(F32)
        pooled = s / cnt - u[:, ls]
        outs.append(_mm(pooled.astype(BF16), w_ref[gi].astype(BF16)))
    ext_s[0:16, :] = u[tb - 16:tb, :]
    o_ref[0] = jnp.concatenate(outs, axis=1) * scale_ref[...]


def _pool_prompt(main3, pool_w, scale, tb):
    n, t, _ = main3.shape
    return pl.pallas_call(
        functools.partial(_pool_kernel, tb=tb),
        grid=(n, t // tb),
        in_specs=[pl.BlockSpec((1, tb, W_BR), lambda b, j: (b, j, C_U_D // W_BR)),
                  pl.BlockSpec((len(POOL_WINDOWS), POOL_GW, POOL_GW), lambda b, j: (0, 0, 0)),
                  pl.BlockSpec((1, W_BR), lambda b, j: (0, 0))],
        out_specs=pl.BlockSpec((1, tb, W_BR), lambda b, j: (b, j, 0)),
        out_shape=jax.ShapeDtypeStruct((n, t, W_BR), F32),
        scratch_shapes=[pltpu.VMEM((tb + 16, W_BR), F32)],
        compiler_params=_cparams(("arbitrary", "arbitrary")),
        name="pool_prompt",
    )(main3, pool_w, scale)


def _out_kernel(oa_ref, ob_ref, oc_ref, od_ref, za_ref, zb_ref, zc_ref, zd_ref, x_ref, w_ref, g_ref, b_ref, y_ref):
    acc = ALPHA_DN * x_ref[...]
    for i, (o_r, z_r) in enumerate(((oa_ref, za_ref), (ob_ref, zb_ref), (oc_ref, zc_ref), (od_ref, zd_ref))):
        gated = (o_r[...] * _silu(z_r[...])).astype(BF16)
        acc = acc + _mm(gated, w_ref[i * W_BR:(i + 1) * W_BR, :])
    mu = jnp.mean(acc, axis=-1, keepdims=True)
    xc = acc - mu
    var = jnp.mean(xc * xc, axis=-1, keepdims=True)
    y_ref[...] = xc * lax.rsqrt(var + LN_EPS) * g_ref[...] + b_ref[...]


def _out_proj(o_a, o_b, o_c, o_d, main2, x2d, w_out, ln_g, ln_b, tm):
    m = x2d.shape[0]
    ospec = pl.BlockSpec((tm, W_BR), lambda i: (i, 0))
    zspec = lambda col: pl.BlockSpec((tm, W_BR), lambda i: (i, col // W_BR))
    return pl.pallas_call(
        _out_kernel,
        grid=(m // tm,),
        in_specs=[ospec, ospec, ospec, ospec, zspec(C_Z_A), zspec(C_Z_B), zspec(C_Z_C), zspec(C_Z_D),
                  pl.BlockSpec((tm, D_MODEL), lambda i: (i, 0)),
                  pl.BlockSpec((D_MODEL, D_MODEL), lambda i: (0, 0)),
                  pl.BlockSpec((1, D_MODEL), lambda i: (0, 0)), pl.BlockSpec((1, D_MODEL), lambda i: (0, 0))],
        out_specs=pl.BlockSpec((tm, D_MODEL), lambda i: (i, 0)),
        out_shape=jax.ShapeDtypeStruct((m, D_MODEL), F32),
        compiler_params=_cparams(("arbitrary",)),
        name="out_proj",
    )(o_a, o_b, o_c, o_d, main2, main2, main2, main2, x2d, w_out, ln_g, ln_b)


def _rows8(row, nrows=1):
    return jnp.where(_iota2((8, row.shape[1]), 0) < nrows, jnp.broadcast_to(row, (8, row.shape[1])), 0.0)


def _dec_kernel(main_ref, small_ref, sa_ref, conv_ref, sb_ref, shift_ref, shifts_ref, dbuf_ref,
                convw_ref, gpar_ref, normg_ref, mu_ref, mus_ref, wup_ref, aup_ref, par_ref, poolw_ref, pscale_ref,
                oa_ref, ob_ref, od_ref, sa_out, conv_out, sb_out, dbuf_out, ext_s, *, pos):
    row = main_ref[0]
    sm = small_ref[0]
    bd = _head_block_diag()

    u = row[:, C_QKV_A:C_QKV_A + 3 * W_BR]
    buf = conv_ref[0]
    cw = convw_ref[...]
    c = buf[0:1, :] * cw[0:1, :] + buf[1:2, :] * cw[1:2, :] + buf[2:3, :] * cw[2:3, :] + u * cw[3:4, :]
    conv_out[0, 0:2, :] = buf[1:3, :]
    conv_out[0, 2:3, :] = u
    c = _silu(c)
    q = c[:, 0:W_BR]
    k = c[:, W_BR:2 * W_BR]
    v = c[:, 2 * W_BR:3 * W_BR]
    q = q * lax.rsqrt(_dot3_l(_rows8(q * q), bd)[0:1, :] + L2_EPS) * (HEAD_DIM ** -0.5)
    k = k * lax.rsqrt(_dot3_l(_rows8(k * k), bd)[0:1, :] + L2_EPS)
    gpar = gpar_ref[...]
    g = gpar[0:1, :] * _softplus(sm + gpar[1:2, :])
    beta = _sigmoid(sm)
    heads = range(N_HEADS)
    lanes = [slice(h * HEAD_DIM, (h + 1) * HEAD_DIM) for h in heads]
    eg = [jnp.exp(g[:, S_A + h:S_A + h + 1]) for h in heads]
    sts = [sa_ref[0, h] for h in heads]
    k8 = [_rows8(k[:, hs]) for hs in lanes]
    ks = [_mm(k8[h], sts[h], HI)[0:1, :] for h in heads]
    v_new = [beta[:, S_B + h:S_B + h + 1] * (v[:, lanes[h]] - eg[h] * ks[h]) for h in heads]
    st_new = [sts[h] * eg[h] + _mm_tn(k8[h], _rows8(v_new[h]), HI) for h in heads]
    o = jnp.concatenate([_mm(_rows8(q[:, lanes[h]]), st_new[h], HI)[0:1, :] for h in heads], axis=1)
    for h in heads:
        sa_out[0, h] = st_new[h]
    ms = _dot3_l(_rows8(o * o), bd)[0:1, :] * (1.0 / HEAD_DIM)
    oa_ref[0] = o * lax.rsqrt(ms + RMS_EPS) * normg_ref[...]

    p = row[:, C_RKV_B:C_RKV_B + 3 * W_BR]
    ps = p + (shift_ref[0] - p) * mu_ref[...]
    pss = sm + (shifts_ref[0] - sm) * mus_ref[...]
    r = ps[:, 0:W_BR]
    k = ps[:, W_BR:2 * W_BR]
    v = ps[:, 2 * W_BR:3 * W_BR]
    par = par_ref[...]
    w0, a0, xi, alpha, rho, gn_g, gn_b = (par[i:i + 1, :] for i in range(7))
    d = w0 + _mm(_rows8(jnp.tanh(pss)), wup_ref[...], HI)[0:1, :]
    decay = jnp.exp(-RWKV_DECAY_SCALE * _sigmoid(d))
    a = _sigmoid(a0 + _mm(_rows8(pss), aup_ref[...], HI)[0:1, :])
    kx = k * xi
    kk = kx * lax.rsqrt(_dot3_l(_rows8(kx * kx), bd)[0:1, :] + L2_EPS)
    k2 = k * (1.0 + (a - 1.0) * alpha)
    sub = _iota2((8, HEAD_DIM), 0)
    two_rows = lambda r0, r1: jnp.where(sub == 0, jnp.broadcast_to(r0, (8, HEAD_DIM)),
                                        jnp.where(sub == 1, jnp.broadcast_to(r1, (8, HEAD_DIM)), 0.0))
    sts = [sb_ref[0, h] for h in heads]
    s_kk = [_mm_nt(_rows8(-kk[:, lanes[h]]), sts[h], HI)[0:1, :] for h in heads]
    st_new = [sts[h] * decay[:, lanes[h]]
              + _mm_tn(two_rows(s_kk[h], v[:, lanes[h]]), two_rows(kk[:, lanes[h]] * a[:, lanes[h]], k2[:, lanes[h]]), HI)
              for h in heads]
    y = jnp.concatenate([_mm_nt(_rows8(r[:, lanes[h]]), st_new[h], HI)[0:1, :] for h in heads], axis=1)
    for h in heads:
        sb_out[0, h] = st_new[h]
    mean = _dot3_l(_rows8(y), bd)[0:1, :] * (1.0 / HEAD_DIM)
    yc = y - mean
    var = _dot3_l(_rows8(yc * yc), bd)[0:1, :] * (1.0 / HEAD_DIM)
    yn = yc * lax.rsqrt(var + GN_EPS) * gn_g + gn_b
    bonus = _dot3_l(_rows8(r * k2 * rho), bd)[0:1, :] * v
    ob_ref[0] = yn + bonus

    ud = row[:, C_U_D:C_U_D + W_BR]
    ext_s[0:POOL_BUF, :] = dbuf_ref[0]
    ext_s[POOL_BUF:POOL_BUF + 1, :] = ud
    dbuf_out[0] = ext_s[1:POOL_BUF + 1, :]
    outs = []
    for gi, wdw in enumerate(POOL_WINDOWS):
        ls = slice(gi * POOL_GW, (gi + 1) * POOL_GW)
        s = jnp.sum(ext_s[POOL_BUF + 1 - wdw:POOL_BUF + 1, ls], axis=0, keepdims=True)
        pooled = s / float(min(wdw, pos + 1)) - ud[:, ls]
        outs.append(_mm(_rows8(pooled).astype(BF16), poolw_ref[gi].astype(BF16))[0:1, :])
    od_ref[0] = jnp.concatenate(outs, axis=1) * pscale_ref[...]


def _decode_mixers(main_s, small_s, st_a, conv_a, st_b, shift_rkv, shift_small, dbuf,
                   conv_w, gpar, normg, mu, mus, wup, aup, par, pool_w, pscale, pos):
    nd = main_s.shape[0]
    per_seq = lambda shape: pl.BlockSpec((1,) + shape, lambda b: (b,) + (0,) * len(shape))
    full = lambda shape: pl.BlockSpec(shape, lambda b: (0,) * len(shape))
    hh = (N_HEADS, HEAD_DIM, HEAD_DIM)
    return pl.pallas_call(
        functools.partial(_dec_kernel, pos=pos),
        grid=(nd,),
        in_specs=[per_seq((1, N_MAIN)), per_seq((1, N_SMALL)), per_seq(hh), per_seq((CONV_W - 1, 3 * W_BR)),
                  per_seq(hh), per_seq((1, 3 * W_BR)), per_seq((1, N_SMALL)), per_seq((POOL_BUF, W_BR)),
                  full((CONV_W, 3 * W_BR)), full((8, N_SMALL)), full((1, W_BR)),
                  full((1, 3 * W_BR)), full((1, N_SMALL)), full((N_SMALL, W_BR)), full((N_SMALL, W_BR)),
                  full((8, W_BR)), full((len(POOL_WINDOWS), POOL_GW, POOL_GW)), full((1, W_BR))],
        out_specs=[per_seq((1, W_BR)), per_seq((1, W_BR)), per_seq((1, W_BR)), per_seq(hh),
                   per_seq((CONV_W - 1, 3 * W_BR)), per_seq(hh), per_seq((POOL_BUF, W_BR))],
        out_shape=[jax.ShapeDtypeStruct((nd, 1, W_BR), F32)] * 3
                  + [jax.ShapeDtypeStruct((nd,) + hh, F32), jax.ShapeDtypeStruct((nd, CONV_W - 1, 3 * W_BR), F32),
                     jax.ShapeDtypeStruct((nd,) + hh, F32), jax.ShapeDtypeStruct((nd, POOL_BUF, W_BR), F32)],
        scratch_shapes=[pltpu.VMEM((16, W_BR), F32)],
        compiler_params=_cparams(("arbitrary",)),
        name="decode_mixers",
    )(main_s.reshape(nd, 1, N_MAIN), small_s.reshape(nd, 1, N_SMALL), st_a, conv_a, st_b,
      shift_rkv.reshape(nd, 1, 3 * W_BR), shift_small.reshape(nd, 1, N_SMALL), dbuf,
      conv_w, gpar, normg, mu, mus, wup, aup, par, pool_w, pscale)


def _dfox_kernel(pt_ref, qkv_ref, qrep_ref, small_ref, bias_ref, *rest, n_grp):
    kt_refs, vt_refs, lf_refs = rest[0:n_grp], rest[n_grp:2 * n_grp], rest[2 * n_grp:3 * n_grp]
    o_ref, lf_out, m_s, l_s, w_s, r_s, acc_s = rest[3 * n_grp:]
    j = pl.program_id(1)
    scale = HEAD_DIM ** -0.5
    sub8 = _iota2((N_HEADS, LANES), 0)
    lane8 = _iota2((N_HEADS, LANES), 1)

    @pl.when(j == 0)
    def _():
        lf_new = _log_sigmoid(small_ref[0] + bias_ref[...])
        lf_out[0] = lf_new
        qb = (qkv_ref[0, 0] * scale).astype(BF16).astype(F32)
        kb = qkv_ref[0, 1].astype(BF16).astype(F32)
        m_s[...] = jnp.broadcast_to(jnp.sum(qb * kb, axis=1, keepdims=True), (N_HEADS, LANES))
        l_s[...] = jnp.ones_like(l_s)
        w_s[...] = jnp.ones_like(w_s)
        acc_s[...] = jnp.zeros_like(acc_s)
        mine = jnp.where(lane8 == sub8 + S_F, jnp.broadcast_to(lf_new, (N_HEADS, LANES)), 0.0)
        r_s[...] = jnp.broadcast_to(jnp.sum(mine, axis=1, keepdims=True), (N_HEADS, LANES))

    grp = range(n_grp)
    r_t = _iota2((PAGE_SIZE, 2 * LANES), 0)
    c_t = _iota2((PAGE_SIZE, 2 * LANES), 1)
    later_or_all = jnp.where((r_t > c_t) | (c_t >= LANES), 1.0, 0.0).astype(BF16)
    s_t = [jnp.zeros((N_HEADS, LANES), F32) for _ in grp]
    for h in range(N_HEADS):
        q_h = qrep_ref[0, h]
        for g in grp:
            row = jnp.sum(kt_refs[g][0, 0, h] * q_h, axis=0, keepdims=True) * scale
            s_t[g] = jnp.where(sub8 == h, jnp.broadcast_to(row, (N_HEADS, LANES)), s_t[g])
    gates = [_dot3_l(lf_refs[g][0, 0], later_or_all) for g in grp]
    logits, r_run = [], r_s[...]
    for g in grp:
        logits.append(s_t[g] + gates[g][:, 0:LANES] + r_run)
        r_run = r_run + gates[g][:, LANES:2 * LANES]
    m_old = m_s[...]
    m_grp = logits[0]
    for g in grp[1:]:
        m_grp = jnp.maximum(m_grp, logits[g])
    m_new = jnp.maximum(m_old, jnp.max(m_grp, axis=1, keepdims=True))
    alpha = jnp.exp(m_old - m_new)
    p = [jnp.exp(logits[g] - m_new) for g in grp]
    p_sum = p[0]
    for g in grp[1:]:
        p_sum = p_sum + p[g]
    for h in range(N_HEADS):
        acc = acc_s[h] * jnp.broadcast_to(alpha[h:h + 1, :], (HEAD_DIM, LANES))
        for g in grp:
            acc = acc + vt_refs[g][0, 0, h] * jnp.broadcast_to(p[g][h:h + 1, :], (HEAD_DIM, LANES))
        acc_s[h] = acc
    l_s[...] = alpha * l_s[...] + jnp.sum(p_sum, axis=1, keepdims=True)
    w_s[...] = alpha * w_s[...]
    m_s[...] = m_new
    r_s[...] = r_run

    @pl.when(j == pl.num_programs(1) - 1)
    def _():
        ones_b = jnp.ones((N_HEADS, LANES), BF16)
        sub = _iota2((N_HEADS, HEAD_DIM), 0)
        red = jnp.zeros((N_HEADS, HEAD_DIM), F32)
        for h in range(N_HEADS):
            hi, mid, lo = _split3(acc_s[h])
            tot = _mm_nt(ones_b, hi) + _mm_nt(ones_b, mid) + _mm_nt(ones_b, lo)
            red = red + jnp.where(sub == h, tot, 0.0)
        o_ref[0] = (red + w_s[...][:, 0:HEAD_DIM] * qkv_ref[0, 2]) / l_s[...][:, 0:HEAD_DIM]


def _fox_decode(page_table, qkv_s, small_s, bias_row, cache_k, cache_v, cache_lf, layer, n_grp):
    nd, n_pages = page_table.shape
    kt = jnp.transpose(cache_k, (0, 1, 3, 4, 2))
    vt = jnp.transpose(cache_v, (0, 1, 3, 4, 2))
    lft = jnp.transpose(cache_lf, (0, 1, 3, 2))
    qkv8 = qkv_s.reshape(nd, 3, N_HEADS, HEAD_DIM)
    q_rep = jnp.broadcast_to(qkv8[:, 0, :, :, None], (nd, N_HEADS, HEAD_DIM, LANES))
    page = lambda g: (lambda b, j, pt: (layer, pt[b, n_pages - 1 - (j * n_grp + g)], 0, 0, 0))
    page4 = lambda g: (lambda b, j, pt: (layer, pt[b, n_pages - 1 - (j * n_grp + g)], 0, 0))
    kv_spec = lambda g: pl.BlockSpec((1, 1, N_HEADS, HEAD_DIM, PAGE_SIZE), page(g))
    grid_spec = pltpu.PrefetchScalarGridSpec(
        num_scalar_prefetch=1,
        grid=(nd, n_pages // n_grp),
        in_specs=[pl.BlockSpec((1, 3, N_HEADS, HEAD_DIM), lambda b, j, pt: (b, 0, 0, 0)),
                  pl.BlockSpec((1, N_HEADS, HEAD_DIM, LANES), lambda b, j, pt: (b, 0, 0, 0)),
                  pl.BlockSpec((1, 1, N_SMALL), lambda b, j, pt: (b, 0, 0)),
                  pl.BlockSpec((1, N_SMALL), lambda b, j, pt: (0, 0))]
                 + [kv_spec(g) for g in range(n_grp)] + [kv_spec(g) for g in range(n_grp)]
                 + [pl.BlockSpec((1, 1, N_HEADS, PAGE_SIZE), page4(g)) for g in range(n_grp)],
        out_specs=[pl.BlockSpec((1, N_HEADS, HEAD_DIM), lambda b, j, pt: (b, 0, 0)),
                   pl.BlockSpec((1, 1, N_SMALL), lambda b, j, pt: (b, 0, 0))],
        scratch_shapes=[pltpu.VMEM((N_HEADS, LANES), F32)] * 4 + [pltpu.VMEM((N_HEADS, HEAD_DIM, LANES), F32)],
    )
    o, lf_new = pl.pallas_call(
        functools.partial(_dfox_kernel, n_grp=n_grp),
        grid_spec=grid_spec,
        out_shape=[jax.ShapeDtypeStruct((nd, N_HEADS, HEAD_DIM), F32), jax.ShapeDtypeStruct((nd, 1, N_SMALL), F32)],
        compiler_params=_cparams(("arbitrary", "arbitrary")),
        name="fox_decode",
    )(page_table, qkv8, q_rep, small_s.reshape(nd, 1, N_SMALL), bias_row,
      *([kt] * n_grp), *([vt] * n_grp), *([lft] * n_grp))
    return o.reshape(nd, W_BR), lf_new


_MAIN_SEGMENTS = ((O_QKV_A, 3 * W_BR), (O_P_B, 3 * W_BR), (O_QKV_C, 3 * W_BR), (O_Z_A, W_BR), (O_Z_B, W_BR),
                  (O_Z_C, W_BR), (O_U_D, W_BR), (O_Z_D, W_BR))
_SMALL_SEGMENTS = ((O_A_A, 2 * N_HEADS), (O_WL, 2 * LORA_B), (O_F_C, N_HEADS))


def _wprep_kernel(wt_ref, main_ref, small_ref):
    wt = wt_ref[0]
    main_ref[...] = jnp.concatenate([wt[a:a + n, :] for a, n in _MAIN_SEGMENTS], axis=0).T.astype(BF16)
    used = sum(n for _, n in _SMALL_SEGMENTS)
    small = [wt[a:a + n, :] for a, n in _SMALL_SEGMENTS] + [jnp.zeros((N_SMALL - used, wt.shape[1]), F32)]
    small_ref[...] = jnp.concatenate(small, axis=0).T.astype(BF16)


def _prep_w_in(w_in, layer, tr=256):
    _, d, d_in = w_in.shape
    return pl.pallas_call(
        _wprep_kernel,
        grid=(d // tr,),
        in_specs=[pl.BlockSpec((1, d_in, tr), lambda i: (layer, 0, i))],
        out_specs=[pl.BlockSpec((tr, N_MAIN), lambda i: (i, 0)), pl.BlockSpec((tr, N_SMALL), lambda i: (i, 0))],
        out_shape=[jax.ShapeDtypeStruct((d, N_MAIN), BF16), jax.ShapeDtypeStruct((d, N_SMALL), BF16)],
        compiler_params=_cparams(("arbitrary",)),
        name="w_in_prep",
    )(jnp.swapaxes(w_in, 1, 2))


def _small_row(pairs):
    row = jnp.zeros((N_SMALL,), F32)
    for off, vec in pairs:
        row = row.at[off:off + vec.shape[0]].set(vec.astype(F32))
    return row[None, :]


def _layer_params(l, w_in, conv_A, A_log, dt_bias, norm_A, mu_B, w0_B, w_up_B, a0_B, a_up_B, xi_B, alpha_B, rho_B,
                  gn_g_B, gn_b_B, b_f_C, pool_w_D, pool_scale_D, w_out, ln_g, ln_b):
    w_main, w_small = _prep_w_in(w_in, l)
    gpar = jnp.concatenate([_small_row([(S_A, -jnp.exp(A_log[l].astype(F32)))]), _small_row([(S_A, dt_bias[l])]),
                            jnp.zeros((6, N_SMALL), F32)], axis=0)
    head_of_lane = jnp.arange(W_BR) // HEAD_DIM
    lane = jnp.arange(N_SMALL)[:, None]
    eg = (lane == head_of_lane[None, :] + S_A).astype(BF16)
    eb = (lane == head_of_lane[None, :] + S_B).astype(BF16)
    mu = mu_B[l].astype(F32)
    wup = jnp.zeros((N_SMALL, W_BR), F32).at[S_WL:S_WL + LORA_B].set(w_up_B[l].astype(F32))
    aup = jnp.zeros((N_SMALL, W_BR), F32).at[S_AL:S_AL + LORA_B].set(a_up_B[l].astype(F32))
    bd = (head_of_lane[:, None] == head_of_lane[None, :]).astype(BF16)
    par = jnp.stack([w0_B[l], a0_B[l], xi_B[l], alpha_B[l], rho_B[l], gn_g_B[l], gn_b_B[l],
                     jnp.zeros((W_BR,), F32)]).astype(F32)
    return dict(
        w_main=w_main, w_small=w_small, bd=bd, conv_w=conv_A[l].astype(F32), gpar=gpar, eg=eg, eb=eb,
        normg=jnp.tile(norm_A[l].astype(F32), N_HEADS)[None, :],
        mu=mu[None, 0:3 * W_BR], mus=_small_row([(S_WL, mu[3 * W_BR:])]), wup=wup, aup=aup, par=par,
        fbias=_small_row([(S_F, b_f_C[l])]), pool_w=pool_w_D[l].astype(F32), pscale=pool_scale_D[l].astype(F32)[None, :],
        w_out=w_out[l].astype(BF16), ln_g=ln_g[l].astype(F32)[None, :], ln_b=ln_b[l].astype(F32)[None, :])


def _prompt_layer(x3, P, layer, k_all, v_all, tb_rec, tq, tb_gate, tb_pool, tm_proj, tm_out):
    n, t, _ = x3.shape
    x2 = x3.reshape(n * t, D_MODEL)
    main2, small2, k_all, v_all = _proj(x2, P['w_main'], P['w_small'], k_all, v_all, layer, tm_proj)
    main3 = main2.reshape(n, t, N_MAIN)
    small3 = small2.reshape(n, t, N_SMALL)
    o_a, s_a = _gdn_prompt(main3, small3, P['conv_w'], P['gpar'], P['eg'], P['eb'], P['normg'], P['bd'], tb_rec)
    o_b, s_b = _rwkv_prompt(main3, small3, P['mu'], P['mus'], _bf(P['wup']), _bf(P['aup']), P['par'], P['bd'],
                            tb_rec)
    logf3, c3, ct3 = _gates(small3, P['fbias'], tb_gate)
    o_c = _fox_prompt(main3, c3, ct3, tq)
    o_d = _pool_prompt(main3, P['pool_w'], P['pscale'], tb_pool)
    flat = lambda a: a.reshape(n * t, W_BR)
    y2 = _out_proj(flat(o_a), flat(o_b), flat(o_c), flat(o_d), main2, x2, P['w_out'], P['ln_g'], P['ln_b'], tm_out)
    new = (s_a,
           main3[:, t - (CONV_W - 1):, C_QKV_A:C_QKV_A + 3 * W_BR],
           s_b,
           jnp.concatenate([main3[:, t - 1, C_RKV_B:C_RKV_B + 3 * W_BR], small3[:, t - 1, S_WL:S_WL + 2 * LORA_B]], axis=-1),
           logf3[:, :, S_F:S_F + N_HEADS],
           main3[:, t - POOL_BUF:, C_U_D:C_U_D + W_BR])
    return y2.reshape(n, t, D_MODEL), new, k_all, v_all


def _sample_layer(x3, st, cache, layer, k_all, v_all, page_table, P):
    nd = x3.shape[0]
    st_a, conv_a, st_b, shift_b, dbuf = st
    cache_k, cache_v, cache_lf = cache
    x2 = x3.reshape(nd, D_MODEL)
    main_s, small_s, k_all, v_all = _proj(x2, P['w_main'], P['w_small'], k_all, v_all, layer, nd)
    shift_small = jnp.zeros((nd, N_SMALL), F32).at[:, S_WL:S_WL + 2 * LORA_B].set(shift_b[:, 3 * W_BR:].astype(F32))
    o_a, o_b, o_d, s_a, conv_n, s_b, dbuf_n = _decode_mixers(
        main_s, small_s, st_a, conv_a, st_b, shift_b[:, 0:3 * W_BR], shift_small, dbuf,
        P['conv_w'], P['gpar'], P['normg'], P['mu'], P['mus'], P['wup'], P['aup'], P['par'], P['pool_w'], P['pscale'],
        page_table.shape[1] * PAGE_SIZE)
    qkv_s = main_s[:, C_QKV_C:C_QKV_C + 3 * W_BR]
    n_grp = math.gcd(page_table.shape[1], 16)
    o_c, lf_new = _fox_decode(page_table, qkv_s, small_s, P['fbias'], cache_k, cache_v, cache_lf, layer, n_grp)
    flat = lambda a: a.reshape(nd, W_BR)
    y2 = _out_proj(flat(o_a), flat(o_b), flat(o_c), flat(o_d), main_s, x2, P['w_out'], P['ln_g'], P['ln_b'], nd)
    new = (s_a, conv_n, s_b,
           jnp.concatenate([main_s[:, C_RKV_B:C_RKV_B + 3 * W_BR], small_s[:, S_WL:S_WL + 2 * LORA_B]], axis=-1),
           lf_new[:, :, S_F:S_F + N_HEADS],
           dbuf_n)
    return y2.reshape(nd, 1, D_MODEL), new, k_all, v_all


def kernel(x_prompt, x_sample, state_A_S, state_A_conv, state_B_S, state_B_shift, cache_C_k, cache_C_v, cache_C_logf, state_D_buf, page_table, w_in, conv_A, A_log, dt_bias, norm_A, mu_B, w0_B, w_up_B, a0_B, a_up_B, xi_B, alpha_B, rho_B, gn_g_B, gn_b_B, b_f_C, pool_w_D, pool_scale_D, w_out, ln_g, ln_b):
    depth = w_in.shape[0]
    t = x_prompt.shape[1]
    tb_rec = min(256, t)
    tq = min(256, t)
    tb_gate = min(512, t)
    tb_pool = min(512, t)
    tm_proj = min(1024, t)
    tm_out = min(256, x_prompt.shape[0] * t)
    y_p, y_s = x_prompt, x_sample
    prompt_new, sample_new = [], []
    nb, nd = x_prompt.shape[0], x_sample.shape[0]
    pk, pv = jnp.zeros((depth, nb, W_BR, t), F32), jnp.zeros((depth, nb, W_BR, t), F32)
    sk, sv = jnp.zeros((depth, nd, W_BR), F32), jnp.zeros((depth, nd, W_BR), F32)
    for l in range(depth):
        P = _layer_params(l, w_in, conv_A, A_log, dt_bias, norm_A, mu_B, w0_B, w_up_B, a0_B, a_up_B, xi_B, alpha_B,
                          rho_B, gn_g_B, gn_b_B, b_f_C, pool_w_D, pool_scale_D, w_out, ln_g, ln_b)
        y_p, new_p, pk, pv = _prompt_layer(y_p, P, l, pk, pv, tb_rec, tq, tb_gate, tb_pool, tm_proj, tm_out)
        st = (state_A_S[l], state_A_conv[l], state_B_S[l], state_B_shift[l], state_D_buf[l])
        y_s, new_s, sk, sv = _sample_layer(y_s, st, (cache_C_k, cache_C_v, cache_C_logf), l, sk, sv, page_table, P)
        prompt_new.append(new_p)
        sample_new.append(new_s)
    p_a_s, p_a_conv, p_b_s, p_b_shift, p_c_logf, p_d_buf = [jnp.stack([n[i] for n in prompt_new]) for i in range(6)]
    s_a_s, s_a_conv, s_b_s, s_b_shift, s_c_logf, s_d_buf = [jnp.stack([n[i] for n in sample_new]) for i in range(6)]
    p_c_k, p_c_v = (jnp.transpose(a.reshape(depth, nb, N_HEADS, HEAD_DIM, t), (0, 1, 4, 2, 3)) for a in (pk, pv))
    s_c_k, s_c_v = (a.reshape(depth, nd, 1, N_HEADS, HEAD_DIM) for a in (sk, sv))
    return (y_p, y_s, p_a_s, p_a_conv, p_b_s, p_b_shift, p_c_k, p_c_v, p_c_logf, p_d_buf,
            s_a_s, s_a_conv, s_b_s, s_b_shift, s_c_k, s_c_v, s_c_logf, s_d_buf)
```

```python
import functools
import math

import jax
import jax.numpy as jnp
from jax import lax
from jax.experimental import pallas as pl
from jax.experimental.pallas import tpu as pltpu

F32 = jnp.float32
BF16 = jnp.bfloat16
HI = lax.Precision.HIGHEST

D_MODEL = 2048
W_BR = 512
HEAD_DIM = 64
N_HEADS = W_BR // HEAD_DIM
CONV_W = 4
CHUNK = 64
LORA_B = 32
POOL_WINDOWS = (2, 4, 8, 16)
POOL_GW = W_BR // len(POOL_WINDOWS)
POOL_BUF = max(POOL_WINDOWS) - 1
PAGE_SIZE = 128
LANES = 128
FOX_ROW_SPLIT = 2
DEPTH = 2
ALPHA_DN = (2.0 * DEPTH) ** 0.25
LN_EPS = 1e-5
GN_EPS = 64e-5
RMS_EPS = 1e-6
L2_EPS = 1e-6
RWKV_DECAY_SCALE = math.exp(-0.5)
NEG_INF = -1e30
LOG2_E = 1.4426950408889634

C_QKV_A, C_RKV_B, C_QKV_C = 0, 1536, 3072
C_Z_A, C_Z_B, C_Z_C, C_U_D, C_Z_D = 4608, 5120, 5632, 6144, 6656
N_MAIN = 7168
S_A, S_B, S_WL, S_AL, S_F = 0, 8, 16, 48, 80
N_SMALL = 128
O_QKV_A, O_A_A, O_Z_A, O_P_B, O_WL, O_Z_B, O_QKV_C, O_F_C, O_Z_C, O_U_D, O_Z_D, D_IN = (
    0, 1536, 1552, 2064, 3600, 3664, 4176, 5712, 5720, 6232, 6744, 7256)

VMEM_LIMIT = 48 * 1024 * 1024


VMEM_LIMIT_PROJ = 56 * 1024 * 1024


def _cparams(sem, vmem_limit=VMEM_LIMIT):
    return pltpu.CompilerParams(dimension_semantics=sem, vmem_limit_bytes=vmem_limit)


def _mm(a, b, prec=None):
    return jnp.dot(a, b, preferred_element_type=F32, precision=prec)


def _mm_nt(a, b, prec=None):
    return lax.dot_general(a, b, (((1,), (1,)), ((), ())), preferred_element_type=F32, precision=prec)


def _mm_tn(a, b, prec=None):
    return lax.dot_general(a, b, (((0,), (0,)), ((), ())), preferred_element_type=F32, precision=prec)


def _split3(x):
    hi = x.astype(BF16)
    r1 = x - hi.astype(F32)
    mid = r1.astype(BF16)
    lo = (r1 - mid.astype(F32)).astype(BF16)
    return hi, mid, lo


def _dot3_l(x, b_bf16):
    hi, mid, lo = _split3(x)
    return _mm(hi, b_bf16) + _mm(mid, b_bf16) + _mm(lo, b_bf16)


def _head_sums(x, same_head_bf16):
    return _mm(x.astype(BF16), same_head_bf16)


def _dot3_r(a_bf16, x):
    hi, mid, lo = _split3(x)
    return _mm(a_bf16, hi) + _mm(a_bf16, mid) + _mm(a_bf16, lo)


def _sigmoid(x):
    return 1.0 / (1.0 + jnp.exp(-x))


def _silu(x):
    return x * _sigmoid(x)


def _softplus(x):
    return jnp.maximum(x, 0.0) + jnp.log1p(jnp.exp(-jnp.abs(x)))


def _log_sigmoid(x):
    return -_softplus(-x)


def _iota2(shape, dim):
    return lax.broadcasted_iota(jnp.int32, shape, dim)


def _head_block_diag():
    r = _iota2((W_BR, W_BR), 0) // HEAD_DIM
    c = _iota2((W_BR, W_BR), 1) // HEAD_DIM
    return jnp.where(r == c, 1.0, 0.0).astype(BF16)


def _chunk_tril(n):
    r = _iota2((n, n), 0)
    c = _iota2((n, n), 1)
    return jnp.where((r >= c) & (r // CHUNK == c // CHUNK), 1.0, 0.0).astype(BF16)


_DN_NN = (((1,), (0,)), ((), ()))
_DN_NT = (((1,), (1,)), ((), ()))
_DN_TN = (((0,), (0,)), ((), ()))


def _bf(x):
    return x.astype(BF16)


def _mmb(a, b, dn=_DN_NN):
    return lax.dot_general(a, b, dn, preferred_element_type=F32)


def _tri_solve(a_list, rhs_list, blk):
    each = lambda f, *ls: [f(*xs) for xs in zip(*ls)]
    width = rhs_list[0].shape[1]
    ad = each(lambda a: a * blk, a_list)
    adb = each(_bf, ad)
    a2b = each(_bf, each(_mmb, adb, adb))
    a4b = each(_bf, each(_mmb, a2b, a2b))
    a8b = each(_bf, each(_mmb, a4b, a4b))
    z = each(lambda r, a, d: jnp.concatenate([r, a - d], axis=1), rhs_list, a_list, ad)
    z = each(lambda x, y: x - y, z, each(_mmb, adb, each(_bf, z)))
    for pw in (a2b, a4b, a8b):
        z = each(lambda x, y: x + y, z, each(_mmb, pw, each(_bf, z)))
    y = each(lambda x: x[:, 0:width], z)
    nb = each(lambda x: _bf(x[:, width:width + CHUNK]), z)
    n2b = each(_bf, each(_mmb, nb, nb))
    t = each(lambda x, u: x + u, y, each(_mmb, n2b, each(_bf, y)))
    return each(lambda x, u: x - u, t, each(_mmb, nb, each(_bf, t)))


def _chunk_masks():
    r = _iota2((CHUNK, CHUNK), 0)
    c = _iota2((CHUNK, CHUNK), 1)
    eye = jnp.where(r == c, 1.0, 0.0).astype(F32)
    strict = jnp.where(r > c, 1.0, 0.0).astype(F32)
    incl = jnp.where(r >= c, 1.0, 0.0).astype(F32)
    blk = jnp.where(r // 16 == c // 16, 1.0, 0.0).astype(F32)
    return eye, strict, incl, blk


def _proj_kernel(x_ref, w_ref, ws_ref, kin_ref, vin_ref, main_ref, small_ref, k_ref, v_ref, xb_ref, *, tn,
                 token_minor):
    j = pl.program_id(1)

    @pl.when(j == 0)
    def _():
        xb = x_ref[...].astype(BF16)
        xb_ref[...] = xb
        small_ref[...] = _mm(xb, ws_ref[...])

    acc = _mm(xb_ref[...], w_ref[...])
    main_ref[...] = acc
    for col, out_ref in ((C_QKV_C + W_BR, k_ref), (C_QKV_C + 2 * W_BR, v_ref)):
        @pl.when(j == col // tn)
        def _(col=col, out_ref=out_ref):
            kv = acc[:, col % tn:col % tn + W_BR]
            if token_minor:
                out_ref[0, 0] = kv.T
            else:
                out_ref[0] = kv


def _proj(x2d, w_main, w_small, k_all, v_all, layer, tm, tn=1024):
    m = x2d.shape[0]
    assert (C_QKV_C + W_BR) % tn + W_BR <= tn and (C_QKV_C + 2 * W_BR) % tn + W_BR <= tn
    token_minor = k_all.ndim == 4
    if token_minor:
        per_seq = k_all.shape[3] // tm
        kv_spec = pl.BlockSpec((1, 1, W_BR, tm), lambda i, j: (layer, i // per_seq, 0, i % per_seq))
    else:
        kv_spec = pl.BlockSpec((1, tm, W_BR), lambda i, j: (layer, i, 0))
    return pl.pallas_call(
        functools.partial(_proj_kernel, tn=tn, token_minor=token_minor),
        grid=(m // tm, N_MAIN // tn),
        in_specs=[pl.BlockSpec((tm, D_MODEL), lambda i, j: (i, 0)),
                  pl.BlockSpec((D_MODEL, tn), lambda i, j: (0, j)),
                  pl.BlockSpec((D_MODEL, N_SMALL), lambda i, j: (0, 0)),
                  pl.BlockSpec(memory_space=pl.ANY), pl.BlockSpec(memory_space=pl.ANY)],
        out_specs=[pl.BlockSpec((tm, tn), lambda i, j: (i, j)),
                   pl.BlockSpec((tm, N_SMALL), lambda i, j: (i, 0)), kv_spec, kv_spec],
        out_shape=[jax.ShapeDtypeStruct((m, N_MAIN), F32), jax.ShapeDtypeStruct((m, N_SMALL), F32),
                   jax.ShapeDtypeStruct(k_all.shape, F32), jax.ShapeDtypeStruct(v_all.shape, F32)],
        input_output_aliases={3: 2, 4: 3},
        scratch_shapes=[pltpu.VMEM((tm, D_MODEL), BF16)],
        compiler_params=_cparams(("arbitrary", "arbitrary"), VMEM_LIMIT_PROJ),
        name="proj",
    )(x2d, w_main, w_small, k_all, v_all)


def _gates_kernel(small_ref, bias_ref, logf_ref, c_ref, ct_ref, carry_ref, *, tb):
    @pl.when(pl.program_id(1) == 0)
    def _():
        carry_ref[...] = jnp.zeros_like(carry_ref)

    logf = _log_sigmoid(small_ref[0] + bias_ref[...])
    r = _iota2((tb, tb), 0)
    c = _iota2((tb, tb), 1)
    tril = jnp.where(r >= c, 1.0, 0.0).astype(BF16)
    cum = _dot3_r(tril, logf) + carry_ref[0:1, :]
    carry_ref[...] = jnp.broadcast_to(cum[tb - 1:tb, :], carry_ref.shape)
    logf_ref[0] = logf
    c_ref[0] = cum
    ct_ref[0] = cum.T[S_F:S_F + N_HEADS, :]


def _gates(small3, bias_row, tb):
    n, t, _ = small3.shape
    return pl.pallas_call(
        functools.partial(_gates_kernel, tb=tb),
        grid=(n, t // tb),
        in_specs=[pl.BlockSpec((1, tb, N_SMALL), lambda b, j: (b, j, 0)),
                  pl.BlockSpec((1, N_SMALL), lambda b, j: (0, 0))],
        out_specs=[pl.BlockSpec((1, tb, N_SMALL), lambda b, j: (b, j, 0)),
                   pl.BlockSpec((1, tb, N_SMALL), lambda b, j: (b, j, 0)),
                   pl.BlockSpec((1, N_HEADS, tb), lambda b, j: (b, 0, j))],
        out_shape=[jax.ShapeDtypeStruct((n, t, N_SMALL), F32), jax.ShapeDtypeStruct((n, t, N_SMALL), F32),
                   jax.ShapeDtypeStruct((n, N_HEADS, t), F32)],
        scratch_shapes=[pltpu.VMEM((8, N_SMALL), F32)],
        compiler_params=_cparams(("arbitrary", "arbitrary")),
        name="fox_gates",
    )(small3, bias_row)


def _fox_kernel(qi_ref, kj_ref, q_ref, k_ref, v_ref, c_ref, ct_ref, o_ref, m_s, l_s, acc_s, cq_s, *, tq, n_sub):
    i = qi_ref[pl.program_id(1)]
    j = kj_ref[pl.program_id(1)]
    n_pairs = W_BR // LANES
    lo_half = _iota2((tq, LANES), 1) < HEAD_DIM
    pair_lanes = [slice(pr * LANES, (pr + 1) * LANES) for pr in range(n_pairs)]

    @pl.when(j == 0)
    def _():
        m_s[...] = jnp.full(m_s.shape, NEG_INF, F32)
        l_s[...] = jnp.zeros_like(l_s)
        acc_s[...] = jnp.zeros_like(acc_s)
        cq = c_ref[0]
        for h in range(N_HEADS):
            cq_s[h] = jnp.broadcast_to(cq[:, S_F + h:S_F + h + 1] * LOG2_E, (tq, LANES))

    def step(masked):
        k = k_ref[0].astype(BF16)
        v = v_ref[0]
        ct = ct_ref[0] * LOG2_E
        v_bd = [jnp.concatenate([jnp.where(lo_half, v[:, ps], 0.0), jnp.where(lo_half, 0.0, v[:, ps])],
                                axis=0).astype(BF16) for ps in pair_lanes]
        rq = tq // n_sub
        lo_q = _iota2((rq, LANES), 1) < HEAD_DIM
        wide = lambda x: jnp.concatenate([x] * (tq // LANES), axis=1)
        for r in range(n_sub):
            rs = slice(r * rq, (r + 1) * rq)
            q = q_ref[0, rs, :] * (HEAD_DIM ** -0.5 * LOG2_E)
            if masked:
                keep = _iota2((rq, tq), 0) + r * rq >= _iota2((rq, tq), 1)
            s_all, m_all = [], []
            for pr, ps in enumerate(pair_lanes):
                qp, kp = q[:, ps], k[:, ps]
                for e, q_half in enumerate((jnp.where(lo_q, qp, 0.0), jnp.where(lo_q, 0.0, qp))):
                    h = 2 * pr + e
                    s = _mm_nt(q_half.astype(BF16), kp) + wide(cq_s[h, rs, :]) - ct[h:h + 1, :]
                    if masked:
                        s = jnp.where(keep, s, NEG_INF)
                    s_all.append(s)
                    m_all.append(jnp.maximum(m_s[h, rs, :], jnp.max(s, axis=1, keepdims=True)))
            p_all, alpha_all = [], []
            for h in range(N_HEADS):
                m_new = m_all[h]
                alpha = jnp.exp2(m_s[h, rs, :] - m_new)
                p = jnp.exp2(s_all[h] - wide(m_new))
                l_s[h, rs, :] = alpha * l_s[h, rs, :] + jnp.sum(p, axis=1, keepdims=True)
                m_s[h, rs, :] = m_new
                p_all.append(p.astype(BF16))
                alpha_all.append(alpha)
            for pr, ps in enumerate(pair_lanes):
                p_pair = jnp.concatenate([p_all[2 * pr], p_all[2 * pr + 1]], axis=1)
                alpha_p = jnp.where(lo_q, alpha_all[2 * pr], alpha_all[2 * pr + 1])
                acc_s[rs, ps] = alpha_p * acc_s[rs, ps] + _mm(p_pair, v_bd[pr])

    @pl.when(j < i)
    def _():
        step(False)

    @pl.when(j == i)
    def _():
        step(True)
        for pr, ps in enumerate(pair_lanes):
            o_ref[0, :, ps] = acc_s[:, ps] / jnp.where(lo_half, l_s[2 * pr], l_s[2 * pr + 1])


def _fox_prompt(main3, c3, ct3, tq):
    n, t, _ = main3.shape
    nb = t // tq
    qb, kb, vb = C_QKV_C // W_BR, C_QKV_C // W_BR + 1, C_QKV_C // W_BR + 2
    pairs = [(i, j) for i in range(nb) for j in range(i + 1)]
    qi = jnp.asarray([p[0] for p in pairs], jnp.int32)
    kj = jnp.asarray([p[1] for p in pairs], jnp.int32)
    grid_spec = pltpu.PrefetchScalarGridSpec(
        num_scalar_prefetch=2,
        grid=(n, len(pairs)),
        in_specs=[pl.BlockSpec((1, tq, W_BR), lambda b, s, qi, kj: (b, qi[s], qb)),
                  pl.BlockSpec((1, tq, W_BR), lambda b, s, qi, kj: (b, kj[s], kb)),
                  pl.BlockSpec((1, tq, W_BR), lambda b, s, qi, kj: (b, kj[s], vb)),
                  pl.BlockSpec((1, tq, N_SMALL), lambda b, s, qi, kj: (b, qi[s], 0)),
                  pl.BlockSpec((1, N_HEADS, tq), lambda b, s, qi, kj: (b, 0, kj[s]))],
        out_specs=pl.BlockSpec((1, tq, W_BR), lambda b, s, qi, kj: (b, qi[s], 0)),
        scratch_shapes=[pltpu.VMEM((N_HEADS, tq, LANES), F32), pltpu.VMEM((N_HEADS, tq, LANES), F32),
                        pltpu.VMEM((tq, W_BR), F32), pltpu.VMEM((N_HEADS, tq, LANES), F32)],
    )
    return pl.pallas_call(
        functools.partial(_fox_kernel, tq=tq, n_sub=FOX_ROW_SPLIT),
        grid_spec=grid_spec,
        out_shape=jax.ShapeDtypeStruct((n, t, W_BR), F32),
        compiler_params=_cparams(("arbitrary", "arbitrary")),
        name="fox_prompt",
    )(qi, kj, main3, main3, main3, c3, ct3)


def _gdn_kernel(qkv_ref, small_ref, convw_ref, gpar_ref, eg_ref, eb_ref, normg_ref, bd_ref,
                o_ref, s_out_ref, ext_s, q_s, k_s, v_s, b_s, g_s, o_s, st_s, *, tb):
    j = pl.program_id(1)

    @pl.when(j == 0)
    def _():
        ext_s[0:8, :] = jnp.zeros((8, 3 * W_BR), F32)
        st_s[...] = jnp.zeros_like(st_s)

    u = qkv_ref[0]
    ext_s[8:8 + tb, :] = u
    cw = convw_ref[...]
    c = (ext_s[5:5 + tb, :] * cw[0:1, :] + ext_s[6:6 + tb, :] * cw[1:2, :]
         + ext_s[7:7 + tb, :] * cw[2:3, :] + u * cw[3:4, :])
    ext_s[0:8, :] = u[tb - 8:tb, :]
    c = _silu(c)
    bd = bd_ref[...]
    q = c[:, 0:W_BR]
    k = c[:, W_BR:2 * W_BR]
    q_s[...] = q * lax.rsqrt(_head_sums(q * q, bd) + L2_EPS) * (HEAD_DIM ** -0.5)
    k_s[...] = k * lax.rsqrt(_head_sums(k * k, bd) + L2_EPS)
    v_s[...] = c[:, 2 * W_BR:3 * W_BR]

    sm = small_ref[0]
    gpar = gpar_ref[...]
    g = gpar[0:1, :] * _softplus(sm + gpar[1:2, :])
    beta = _sigmoid(sm)
    gcum = _dot3_r(_chunk_tril(tb), g)
    g_s[...] = _dot3_l(gcum, eg_ref[...])
    b_s[...] = _dot3_l(beta, eb_ref[...])

    gcum_t = gcum.T[S_A:S_A + N_HEADS, :]

    _, strict, incl, blk = _chunk_masks()

    heads = range(N_HEADS)
    lanes = [slice(h * HEAD_DIM, (h + 1) * HEAD_DIM) for h in heads]
    n_chunks = tb // CHUNK
    group = 4

    ops = {}
    for c0 in range(0, n_chunks, group):
        chunks = range(c0, min(c0 + group, n_chunks))
        items = [(c, h) for c in chunks for h in heads]
        rows = lambda c: slice(c * CHUNK, (c + 1) * CHUNK)
        tile = lambda ref: [ref[rows(c), lanes[h]] for c, h in items]
        q_h, k_h, v_h, b_h, g_h = tile(q_s), tile(k_s), tile(v_s), tile(b_s), tile(g_s)
        grow = [jnp.broadcast_to(gcum_t[h:h + 1, c * CHUNK:(c + 1) * CHUNK], (CHUNK, CHUNK))
                for c, h in items]
        idx = range(len(items))
        decay = [jnp.exp(jnp.where(incl > 0, g_h[i] - grow[i], NEG_INF)) for i in idx]
        kb = [k_h[i] * b_h[i] for i in idx]
        both = [_mmb(_bf(jnp.concatenate([kb[i], q_h[i]], axis=0)), _bf(k_h[i]), _DN_NT) for i in idx]
        a = [both[i][0:CHUNK, :] * decay[i] * strict for i in idx]
        attn = [_bf(both[i][CHUNK:2 * CHUNK, :] * decay[i]) for i in idx]
        eg = [jnp.exp(g_h[i]) for i in idx]
        sol = _tri_solve(a, [jnp.concatenate([v_h[i] * b_h[i], kb[i] * eg[i]], axis=1) for i in idx], blk)
        ub = [_bf(sol[i][:, 0:HEAD_DIM]) for i in idx]
        wb = [_bf(sol[i][:, HEAD_DIM:2 * HEAD_DIM]) for i in idx]
        o1 = [_bf(q_h[i] * eg[i] - _mmb(attn[i], wb[i])) for i in idx]
        o2 = [_mmb(attn[i], ub[i]) for i in idx]
        glast = [g_h[i][CHUNK - 1:CHUNK, :] for i in idx]
        kdec = [_bf(k_h[i] * jnp.exp(glast[i] - g_h[i])) for i in idx]
        m = [_bf(_mmb(kdec[i], wb[i], _DN_TN)) for i in idx]
        cc = [_mmb(kdec[i], ub[i], _DN_TN) for i in idx]
        for i, key in enumerate(items):
            ops[key] = (o1[i], o2[i], m[i], cc[i], jnp.exp(glast[i]))

    sts = [st_s[h] for h in heads]
    for c in range(n_chunks):
        stb = [_bf(sts[h]) for h in heads]
        outs = [_mmb(ops[c, h][0], stb[h]) + ops[c, h][1] for h in heads]
        sts = [sts[h] * ops[c, h][4] - _mmb(ops[c, h][2], stb[h]) + ops[c, h][3] for h in heads]
        o_s[c * CHUNK:(c + 1) * CHUNK, :] = jnp.concatenate(outs, axis=1)
    for h in heads:
        st_s[h] = sts[h]

    o = o_s[...]
    ms = _head_sums(o * o, bd) * (1.0 / HEAD_DIM)
    o_ref[0] = o * lax.rsqrt(ms + RMS_EPS) * normg_ref[...]

    @pl.when(j == pl.num_programs(1) - 1)
    def _():
        s_out_ref[0] = st_s[...]


def _gdn_prompt(main3, small3, conv_w, gpar, eg, eb, normg, bd, tb):
    n, t, _ = main3.shape
    full = lambda shape: pl.BlockSpec(shape, lambda b, j: (0,) * len(shape))
    return pl.pallas_call(
        functools.partial(_gdn_kernel, tb=tb),
        grid=(n, t // tb),
        in_specs=[pl.BlockSpec((1, tb, 3 * W_BR), lambda b, j: (b, j, C_QKV_A // (3 * W_BR))),
                  pl.BlockSpec((1, tb, N_SMALL), lambda b, j: (b, j, 0)),
                  full((CONV_W, 3 * W_BR)), full((8, N_SMALL)), full((N_SMALL, W_BR)), full((N_SMALL, W_BR)),
                  full((1, W_BR)), full((W_BR, W_BR))],
        out_specs=[pl.BlockSpec((1, tb, W_BR), lambda b, j: (b, j, 0)),
                   pl.BlockSpec((1, N_HEADS, HEAD_DIM, HEAD_DIM), lambda b, j: (b, 0, 0, 0))],
        out_shape=[jax.ShapeDtypeStruct((n, t, W_BR), F32),
                   jax.ShapeDtypeStruct((n, N_HEADS, HEAD_DIM, HEAD_DIM), F32)],
        scratch_shapes=[pltpu.VMEM((tb + 8, 3 * W_BR), F32)] + [pltpu.VMEM((tb, W_BR), F32)] * 6
                       + [pltpu.VMEM((N_HEADS, HEAD_DIM, HEAD_DIM), F32)],
        compiler_params=_cparams(("arbitrary", "arbitrary")),
        name="gdn_prompt",
    )(main3, small3, conv_w, gpar, eg, eb, normg, bd)


def _rwkv_kernel(p_ref, small_ref, mu_ref, mus_ref, wup_ref, aup_ref, par_ref, bd_ref,
                 o_ref, s_out_ref, ext_s, exts_s, ah_s, bh_s, kh_s, rh_s, be_s, ke_s, v_s, wl_s, y_s, st_s, *, tb):
    j = pl.program_id(1)

    @pl.when(j == 0)
    def _():
        ext_s[0:8, :] = jnp.zeros((8, 3 * W_BR), F32)
        exts_s[0:8, :] = jnp.zeros((8, N_SMALL), F32)
        st_s[...] = jnp.zeros_like(st_s)

    p = p_ref[0]
    sm = small_ref[0]
    ext_s[8:8 + tb, :] = p
    exts_s[8:8 + tb, :] = sm
    prev = ext_s[7:7 + tb, :]
    prevs = exts_s[7:7 + tb, :]
    ext_s[0:8, :] = p[tb - 8:tb, :]
    exts_s[0:8, :] = sm[tb - 8:tb, :]
    ps = p + (prev - p) * mu_ref[...]
    pss = sm + (prevs - sm) * mus_ref[...]
    r = ps[:, 0:W_BR]
    k = ps[:, W_BR:2 * W_BR]
    v = ps[:, 2 * W_BR:3 * W_BR]
    par = par_ref[...]
    w0, a0, xi, alpha, rho, gn_g, gn_b = (par[i:i + 1, :] for i in range(7))
    d = w0 + _mm(_bf(jnp.tanh(pss)), wup_ref[...])
    logw = -RWKV_DECAY_SCALE * _sigmoid(d)
    a = _sigmoid(a0 + _mm(_bf(pss), aup_ref[...]))
    bd = bd_ref[...]
    kx = k * xi
    kk = kx * lax.rsqrt(_head_sums(kx * kx, bd) + L2_EPS)
    k2 = k * (1.0 + (a - 1.0) * alpha)
    lc = _dot3_r(_chunk_tril(tb), logw)
    nb = -(a * kk)
    ah_s[...] = _bf(kk * jnp.exp(lc - logw))
    bh_s[...] = _bf(nb * jnp.exp(-lc))
    kh_s[...] = _bf(k2 * jnp.exp(-lc))
    rh_s[...] = _bf(r * jnp.exp(lc))
    v_s[...] = _bf(v)
    ll = jnp.concatenate([jnp.broadcast_to(lc[c * CHUNK + CHUNK - 1:(c + 1) * CHUNK, :], (CHUNK, W_BR))
                          for c in range(tb // CHUNK)], axis=0)
    to_end = jnp.exp(ll - lc)
    be_s[...] = _bf(nb * to_end)
    ke_s[...] = _bf(k2 * to_end)
    wl_s[...] = jnp.exp(ll)

    _, strict, incl, blk = _chunk_masks()
    incl2 = jnp.concatenate([incl, incl], axis=1)

    heads = range(N_HEADS)
    lanes = [slice(h * HEAD_DIM, (h + 1) * HEAD_DIM) for h in heads]
    n_chunks = tb // CHUNK
    group = 4

    ops = {}
    for c0 in range(0, n_chunks, group):
        chunks = range(c0, min(c0 + group, n_chunks))
        items = [(c, h) for c in chunks for h in heads]
        rows = lambda c: slice(c * CHUNK, (c + 1) * CHUNK)
        tile = lambda ref: [ref[rows(c), lanes[h]] for c, h in items]
        a_h, b_h, k_h, r_h, v_h, be_h, ke_h = (tile(x) for x in (ah_s, bh_s, kh_s, rh_s, v_s, be_s, ke_s))
        idx = range(len(items))
        gram = [_mmb(jnp.concatenate([a_h[i], r_h[i]], axis=0), jnp.concatenate([b_h[i], k_h[i]], axis=0), _DN_NT)
                for i in idx]
        akv = [_mmb(_bf(gram[i][0:CHUNK, CHUNK:2 * CHUNK] * strict), v_h[i]) for i in idx]
        sol = _tri_solve([-(gram[i][0:CHUNK, 0:CHUNK] * strict) for i in idx],
                         [jnp.concatenate([a_h[i].astype(F32), akv[i]], axis=1) for i in idx], blk)
        w1b = [_bf(sol[i][:, 0:HEAD_DIM]) for i in idx]
        w2v = [jnp.concatenate([_bf(sol[i][:, HEAD_DIM:2 * HEAD_DIM]), v_h[i]], axis=0) for i in idx]
        r_bk = [_bf(gram[i][CHUNK:2 * CHUNK, :] * incl2) for i in idx]
        g1 = [_bf(r_h[i].astype(F32) + _mmb(r_bk[i][:, 0:CHUNK], w1b[i])) for i in idx]
        g2 = [_mmb(r_bk[i], w2v[i]) for i in idx]
        m1 = [_bf(_mmb(w1b[i], be_h[i], _DN_TN)) for i in idx]
        c2 = [_mmb(w2v[i], jnp.concatenate([be_h[i], ke_h[i]], axis=0), _DN_TN) for i in idx]
        for i, (c, h) in enumerate(items):
            ops[c, h] = (g1[i], g2[i], m1[i], c2[i], wl_s[c * CHUNK:c * CHUNK + 1, lanes[h]])

    sts = [st_s[h] for h in heads]
    for c in range(n_chunks):
        stb = [_bf(sts[h]) for h in heads]
        outs = [_mmb(ops[c, h][0], stb[h], _DN_NT) + ops[c, h][1] for h in heads]
        sts = [sts[h] * ops[c, h][4] + _mmb(stb[h], ops[c, h][2]) + ops[c, h][3] for h in heads]
        y_s[c * CHUNK:(c + 1) * CHUNK, :] = jnp.concatenate(outs, axis=1)
    for h in heads:
        st_s[h] = sts[h]

    y = y_s[...]
    mean = _head_sums(y, bd) * (1.0 / HEAD_DIM)
    yc = y - mean
    var = _head_sums(yc * yc, bd) * (1.0 / HEAD_DIM)
    yn = yc * lax.rsqrt(var + GN_EPS) * gn_g + gn_b
    bonus = _head_sums(r * k2 * rho, bd) * v
    o_ref[0] = yn + bonus

    @pl.when(j == pl.num_programs(1) - 1)
    def _():
        s_out_ref[0] = st_s[...]


def _rwkv_prompt(main3, small3, mu, mus, wup, aup, par, bd, tb):
    n, t, _ = main3.shape
    full = lambda shape: pl.BlockSpec(shape, lambda b, j: (0,) * len(shape))
    return pl.pallas_call(
        functools.partial(_rwkv_kernel, tb=tb),
        grid=(n, t // tb),
        in_specs=[pl.BlockSpec((1, tb, 3 * W_BR), lambda b, j: (b, j, C_RKV_B // (3 * W_BR))),
                  pl.BlockSpec((1, tb, N_SMALL), lambda b, j: (b, j, 0)),
                  full((1, 3 * W_BR)), full((1, N_SMALL)), full((N_SMALL, W_BR)), full((N_SMALL, W_BR)),
                  full((8, W_BR)), full((W_BR, W_BR))],
        out_specs=[pl.BlockSpec((1, tb, W_BR), lambda b, j: (b, j, 0)),
                   pl.BlockSpec((1, N_HEADS, HEAD_DIM, HEAD_DIM), lambda b, j: (b, 0, 0, 0))],
        out_shape=[jax.ShapeDtypeStruct((n, t, W_BR), F32),
                   jax.ShapeDtypeStruct((n, N_HEADS, HEAD_DIM, HEAD_DIM), F32)],
        scratch_shapes=[pltpu.VMEM((tb + 8, 3 * W_BR), F32), pltpu.VMEM((tb + 8, N_SMALL), F32)]
                       + [pltpu.VMEM((tb, W_BR), BF16)] * 7 + [pltpu.VMEM((tb, W_BR), F32)] * 2
                       + [pltpu.VMEM((N_HEADS, HEAD_DIM, HEAD_DIM), F32)],
        compiler_params=_cparams(("arbitrary", "arbitrary")),
        name="rwkv_prompt",
    )(main3, small3, mu, mus, wup, aup, par, bd)


def _pool_kernel(u_ref, w_ref, scale_ref, o_ref, ext_s, *, tb):
    j = pl.program_id(1)

    @pl.when(j == 0)
    def _():
        ext_s[0:16, :] = jnp.zeros((16, W_BR), F32)

    u = u_ref[0]
    ext_s[16:16 + tb, :] = u
    pos = j * tb + _iota2((tb, POOL_GW), 0)
    outs = []
    for gi, wdw in enumerate(POOL_WINDOWS):
        ls = slice(gi * POOL_GW, (gi + 1) * POOL_GW)
        s = u[:, ls]
        for sh in range(1, wdw):
            s = s + ext_s[16 - sh:16 - sh + tb, ls]
        cnt = jnp.minimum(wdw, pos + 1).astype(F32)
        pooled = s / cnt - u[:, ls]
        outs.append(_mm(pooled.astype(BF16), w_ref[gi].astype(BF16)))
    ext_s[0:16, :] = u[tb - 16:tb, :]
    o_ref[0] = jnp.concatenate(outs, axis=1) * scale_ref[...]


def _pool_prompt(main3, pool_w, scale, tb):
    n, t, _ = main3.shape
    return pl.pallas_call(
        functools.partial(_pool_kernel, tb=tb),
        grid=(n, t // tb),
        in_specs=[pl.BlockSpec((1, tb, W_BR), lambda b, j: (b, j, C_U_D // W_BR)),
                  pl.BlockSpec((len(POOL_WINDOWS), POOL_GW, POOL_GW), lambda b, j: (0, 0, 0)),
                  pl.BlockSpec((1, W_BR), lambda b, j: (0, 0))],
        out_specs=pl.BlockSpec((1, tb, W_BR), lambda b, j: (b, j, 0)),
        out_shape=jax.ShapeDtypeStruct((n, t, W_BR), F32),
        scratch_shapes=[pltpu.VMEM((tb + 16, W_BR), F32)],
        compiler_params=_cparams(("arbitrary", "arbitrary")),
        name="pool_prompt",
    )(main3, pool_w, scale)


def _out_kernel(oa_ref, ob_ref, oc_ref, od_ref, za_ref, zb_ref, zc_ref, zd_ref, x_ref, w_ref, g_ref, b_ref, y_ref):
    tm = x_ref.shape[0]
    n_sub = 2 if tm % 16 == 0 else 1
    for r in range(n_sub):
        rows = slice(r * tm // n_sub, (r + 1) * tm // n_sub)
        acc = ALPHA_DN * x_ref[rows, :]
        for i, (o_r, z_r) in enumerate(((oa_ref, za_ref), (ob_ref, zb_ref), (oc_ref, zc_ref), (od_ref, zd_ref))):
            gated = (o_r[rows, :] * _silu(z_r[rows, :])).astype(BF16)
            acc = acc + _mm(gated, w_ref[i * W_BR:(i + 1) * W_BR, :])
        mu = jnp.mean(acc, axis=-1, keepdims=True)
        xc = acc - mu
        var = jnp.mean(xc * xc, axis=-1, keepdims=True)
        y_ref[rows, :] = xc * lax.rsqrt(var + LN_EPS) * g_ref[...] + b_ref[...]


def _out_proj(o_a, o_b, o_c, o_d, main2, x2d, w_out, ln_g, ln_b, tm):
    m = x2d.shape[0]
    ospec = pl.BlockSpec((tm, W_BR), lambda i: (i, 0))
    zspec = lambda col: pl.BlockSpec((tm, W_BR), lambda i: (i, col // W_BR))
    return pl.pallas_call(
        _out_kernel,
        grid=(m // tm,),
        in_specs=[ospec, ospec, ospec, ospec, zspec(C_Z_A), zspec(C_Z_B), zspec(C_Z_C), zspec(C_Z_D),
                  pl.BlockSpec((tm, D_MODEL), lambda i: (i, 0)),
                  pl.BlockSpec((D_MODEL, D_MODEL), lambda i: (0, 0)),
                  pl.BlockSpec((1, D_MODEL), lambda i: (0, 0)), pl.BlockSpec((1, D_MODEL), lambda i: (0, 0))],
        out_specs=pl.BlockSpec((tm, D_MODEL), lambda i: (i, 0)),
        out_shape=jax.ShapeDtypeStruct((m, D_MODEL), F32),
        compiler_params=_cparams(("arbitrary",)),
        name="out_proj",
    )(o_a, o_b, o_c, o_d, main2, main2, main2, main2, x2d, w_out, ln_g, ln_b)


def _rows8(row, nrows=1):
    return jnp.where(_iota2((8, row.shape[1]), 0) < nrows, jnp.broadcast_to(row, (8, row.shape[1])), 0.0)


def _dec_kernel(main_ref, small_ref, sa_ref, conv_ref, sb_ref, shift_ref, shifts_ref, dbuf_ref,
                convw_ref, gpar_ref, normg_ref, mu_ref, mus_ref, wup_ref, aup_ref, par_ref, poolw_ref, pscale_ref,
                oa_ref, ob_ref, od_ref, sa_out, conv_out, sb_out, dbuf_out, ext_s, *, pos):
    row = main_ref[0]
    sm = small_ref[0]
    bd = _head_block_diag()
    heads = range(N_HEADS)
    lanes = [slice(h * HEAD_DIM, (h + 1) * HEAD_DIM) for h in heads]
    hsum = lambda x: _dot3_l(_rows8(x), bd)[0:1, :]

    u = row[:, C_QKV_A:C_QKV_A + 3 * W_BR]
    buf = conv_ref[0]
    cw = convw_ref[...]
    c = _silu(buf[0:1, :] * cw[0:1, :] + buf[1:2, :] * cw[1:2, :] + buf[2:3, :] * cw[2:3, :] + u * cw[3:4, :])
    conv_out[0, 0:2, :] = buf[1:3, :]
    conv_out[0, 2:3, :] = u
    q_a, k_a, v_a = c[:, 0:W_BR], c[:, W_BR:2 * W_BR], c[:, 2 * W_BR:3 * W_BR]
    gpar = gpar_ref[...]
    g = gpar[0:1, :] * _softplus(sm + gpar[1:2, :])
    beta = _sigmoid(sm)
    eg = [jnp.exp(g[:, S_A + h:S_A + h + 1]) for h in heads]

    p = row[:, C_RKV_B:C_RKV_B + 3 * W_BR]
    ps = p + (shift_ref[0] - p) * mu_ref[...]
    pss = sm + (shifts_ref[0] - sm) * mus_ref[...]
    r, k_b, v_b = ps[:, 0:W_BR], ps[:, W_BR:2 * W_BR], ps[:, 2 * W_BR:3 * W_BR]
    par = par_ref[...]
    w0, a0, xi, alpha, rho, gn_g, gn_b = (par[i:i + 1, :] for i in range(7))
    kx = k_b * xi
    d = w0 + _mm(_rows8(jnp.tanh(pss)), wup_ref[...], HI)[0:1, :]
    a = _sigmoid(a0 + _mm(_rows8(pss), aup_ref[...], HI)[0:1, :])
    q_a = q_a * lax.rsqrt(hsum(q_a * q_a) + L2_EPS) * (HEAD_DIM ** -0.5)
    k_a = k_a * lax.rsqrt(hsum(k_a * k_a) + L2_EPS)
    kk = kx * lax.rsqrt(hsum(kx * kx) + L2_EPS)
    decay = jnp.exp(-RWKV_DECAY_SCALE * _sigmoid(d))
    k2 = k_b * (1.0 + (a - 1.0) * alpha)
    bonus = hsum(r * k2 * rho) * v_b

    sts_a = [sa_ref[0, h] for h in heads]
    sts_b = [sb_ref[0, h] for h in heads]
    k8 = [_rows8(k_a[:, hs]) for hs in lanes]
    ks = [_mm(k8[h], sts_a[h], HI)[0:1, :] for h in heads]
    s_kk = [_mm_nt(_rows8(-kk[:, lanes[h]]), sts_b[h], HI)[0:1, :] for h in heads]

    sub = _iota2((8, HEAD_DIM), 0)
    two_rows = lambda r0, r1: jnp.where(sub == 0, jnp.broadcast_to(r0, (8, HEAD_DIM)),
                                        jnp.where(sub == 1, jnp.broadcast_to(r1, (8, HEAD_DIM)), 0.0))
    v_new = [beta[:, S_B + h:S_B + h + 1] * (v_a[:, lanes[h]] - eg[h] * ks[h]) for h in heads]
    new_a = [sts_a[h] * eg[h] + _mm_tn(k8[h], _rows8(v_new[h]), HI) for h in heads]
    new_b = [sts_b[h] * decay[:, lanes[h]]
             + _mm_tn(two_rows(s_kk[h], v_b[:, lanes[h]]), two_rows(kk[:, lanes[h]] * a[:, lanes[h]], k2[:, lanes[h]]),
                      HI) for h in heads]

    o = jnp.concatenate([_mm(_rows8(q_a[:, lanes[h]]), new_a[h], HI)[0:1, :] for h in heads], axis=1)
    y = jnp.concatenate([_mm_nt(_rows8(r[:, lanes[h]]), new_b[h], HI)[0:1, :] for h in heads], axis=1)
    for h in heads:
        sa_out[0, h] = new_a[h]
        sb_out[0, h] = new_b[h]
    ms = hsum(o * o) * (1.0 / HEAD_DIM)
    mean = hsum(y) * (1.0 / HEAD_DIM)
    oa_ref[0] = o * lax.rsqrt(ms + RMS_EPS) * normg_ref[...]
    yc = y - mean
    var = hsum(yc * yc) * (1.0 / HEAD_DIM)
    ob_ref[0] = yc * lax.rsqrt(var + GN_EPS) * gn_g + gn_b + bonus

    ud = row[:, C_U_D:C_U_D + W_BR]
    ext_s[0:POOL_BUF, :] = dbuf_ref[0]
    ext_s[POOL_BUF:POOL_BUF + 1, :] = ud
    dbuf_out[0] = ext_s[1:POOL_BUF + 1, :]
    outs = []
    for gi, wdw in enumerate(POOL_WINDOWS):
        ls = slice(gi * POOL_GW, (gi + 1) * POOL_GW)
        s = jnp.sum(ext_s[POOL_BUF + 1 - wdw:POOL_BUF + 1, ls], axis=0, keepdims=True)
        pooled = s / float(min(wdw, pos + 1)) - ud[:, ls]
        outs.append(_mm(_rows8(pooled).astype(BF16), poolw_ref[gi].astype(BF16))[0:1, :])
    od_ref[0] = jnp.concatenate(outs, axis=1) * pscale_ref[...]


def _decode_mixers(main_s, small_s, st_a, conv_a, st_b, shift_rkv, shift_small, dbuf,
                   conv_w, gpar, normg, mu, mus, wup, aup, par, pool_w, pscale, pos):
    nd = main_s.shape[0]
    per_seq = lambda shape: pl.BlockSpec((1,) + shape, lambda b: (b,) + (0,) * len(shape))
    full = lambda shape: pl.BlockSpec(shape, lambda b: (0,) * len(shape))
    hh = (N_HEADS, HEAD_DIM, HEAD_DIM)
    return pl.pallas_call(
        functools.partial(_dec_kernel, pos=pos),
        grid=(nd,),
        in_specs=[per_seq((1, N_MAIN)), per_seq((1, N_SMALL)), per_seq(hh), per_seq((CONV_W - 1, 3 * W_BR)),
                  per_seq(hh), per_seq((1, 3 * W_BR)), per_seq((1, N_SMALL)), per_seq((POOL_BUF, W_BR)),
                  full((CONV_W, 3 * W_BR)), full((8, N_SMALL)), full((1, W_BR)),
                  full((1, 3 * W_BR)), full((1, N_SMALL)), full((N_SMALL, W_BR)), full((N_SMALL, W_BR)),
                  full((8, W_BR)), full((len(POOL_WINDOWS), POOL_GW, POOL_GW)), full((1, W_BR))],
        out_specs=[per_seq((1, W_BR)), per_seq((1, W_BR)), per_seq((1, W_BR)), per_seq(hh),
                   per_seq((CONV_W - 1, 3 * W_BR)), per_seq(hh), per_seq((POOL_BUF, W_BR))],
        out_shape=[jax.ShapeDtypeStruct((nd, 1, W_BR), F32)] * 3
                  + [jax.ShapeDtypeStruct((nd,) + hh, F32), jax.ShapeDtypeStruct((nd, CONV_W - 1, 3 * W_BR), F32),
                     jax.ShapeDtypeStruct((nd,) + hh, F32), jax.ShapeDtypeStruct((nd, POOL_BUF, W_BR), F32)],
        scratch_shapes=[pltpu.VMEM((16, W_BR), F32)],
        compiler_params=_cparams(("arbitrary",)),
        name="decode_mixers",
    )(main_s.reshape(nd, 1, N_MAIN), small_s.reshape(nd, 1, N_SMALL), st_a, conv_a, st_b,
      shift_rkv.reshape(nd, 1, 3 * W_BR), shift_small.reshape(nd, 1, N_SMALL), dbuf,
      conv_w, gpar, normg, mu, mus, wup, aup, par, pool_w, pscale)


def _dfox_kernel(pt_ref, qkv_ref, qrep_ref, small_ref, bias_ref, *rest, n_grp):
    kt_refs, vt_refs, lf_refs = rest[0:n_grp], rest[n_grp:2 * n_grp], rest[2 * n_grp:3 * n_grp]
    o_ref, lf_out, m_s, l_s, w_s, r_s, acc_s = rest[3 * n_grp:]
    j = pl.program_id(1)
    scale = HEAD_DIM ** -0.5
    sub8 = _iota2((N_HEADS, LANES), 0)
    lane8 = _iota2((N_HEADS, LANES), 1)

    @pl.when(j == 0)
    def _():
        lf_new = _log_sigmoid(small_ref[0] + bias_ref[...])
        lf_out[0] = lf_new
        qb = (qkv_ref[0, 0] * scale).astype(BF16).astype(F32)
        kb = qkv_ref[0, 1].astype(BF16).astype(F32)
        m_s[...] = jnp.broadcast_to(jnp.sum(qb * kb, axis=1, keepdims=True), (N_HEADS, LANES))
        l_s[...] = jnp.ones_like(l_s)
        w_s[...] = jnp.ones_like(w_s)
        acc_s[...] = jnp.zeros_like(acc_s)
        mine = jnp.where(lane8 == sub8 + S_F, jnp.broadcast_to(lf_new, (N_HEADS, LANES)), 0.0)
        r_s[...] = jnp.broadcast_to(jnp.sum(mine, axis=1, keepdims=True), (N_HEADS, LANES))

    grp = range(n_grp)
    r_t = _iota2((PAGE_SIZE, 2 * LANES), 0)
    c_t = _iota2((PAGE_SIZE, 2 * LANES), 1)
    later_or_all = jnp.where((r_t > c_t) | (c_t >= LANES), 1.0, 0.0).astype(BF16)
    s_t = [jnp.zeros((N_HEADS, LANES), F32) for _ in grp]
    for h in range(N_HEADS):
        q_h = qrep_ref[0, h]
        for g in grp:
            row = jnp.sum(kt_refs[g][0, 0, h] * q_h, axis=0, keepdims=True) * scale
            s_t[g] = jnp.where(sub8 == h, jnp.broadcast_to(row, (N_HEADS, LANES)), s_t[g])
    gates = [_dot3_l(lf_refs[g][0, 0], later_or_all) for g in grp]
    logits, r_run = [], r_s[...]
    for g in grp:
        logits.append(s_t[g] + gates[g][:, 0:LANES] + r_run)
        r_run = r_run + gates[g][:, LANES:2 * LANES]
    m_old = m_s[...]
    m_grp = logits[0]
    for g in grp[1:]:
        m_grp = jnp.maximum(m_grp, logits[g])
    m_new = jnp.maximum(m_old, jnp.max(m_grp, axis=1, keepdims=True))
    alpha = jnp.exp(m_old - m_new)
    p = [jnp.exp(logits[g] - m_new) for g in grp]
    p_sum = p[0]
    for g in grp[1:]:
        p_sum = p_sum + p[g]
    for h in range(N_HEADS):
        acc = acc_s[h] * jnp.broadcast_to(alpha[h:h + 1, :], (HEAD_DIM, LANES))
        for g in grp:
            acc = acc + vt_refs[g][0, 0, h] * jnp.broadcast_to(p[g][h:h + 1, :], (HEAD_DIM, LANES))
        acc_s[h] = acc
    l_s[...] = alpha * l_s[...] + jnp.sum(p_sum, axis=1, keepdims=True)
    w_s[...] = alpha * w_s[...]
    m_s[...] = m_new
    r_s[...] = r_run

    @pl.when(j == pl.num_programs(1) - 1)
    def _():
        ones_b = jnp.ones((N_HEADS, LANES), BF16)
        sub = _iota2((N_HEADS, HEAD_DIM), 0)
        red = jnp.zeros((N_HEADS, HEAD_DIM), F32)
        for h in range(N_HEADS):
            hi, mid, lo = _split3(acc_s[h])
            tot = _mm_nt(ones_b, hi) + _mm_nt(ones_b, mid) + _mm_nt(ones_b, lo)
            red = red + jnp.where(sub == h, tot, 0.0)
        o_ref[0] = (red + w_s[...][:, 0:HEAD_DIM] * qkv_ref[0, 2]) / l_s[...][:, 0:HEAD_DIM]


def _fox_decode(page_table, qkv_s, small_s, bias_row, cache_k, cache_v, cache_lf, layer, n_grp):
    nd, n_pages = page_table.shape
    kt = jnp.transpose(cache_k, (0, 1, 3, 4, 2))
    vt = jnp.transpose(cache_v, (0, 1, 3, 4, 2))
    lft = jnp.transpose(cache_lf, (0, 1, 3, 2))
    qkv8 = qkv_s.reshape(nd, 3, N_HEADS, HEAD_DIM)
    q_rep = jnp.broadcast_to(qkv8[:, 0, :, :, None], (nd, N_HEADS, HEAD_DIM, LANES))
    page = lambda g: (lambda b, j, pt: (layer, pt[b, n_pages - 1 - (j * n_grp + g)], 0, 0, 0))
    page4 = lambda g: (lambda b, j, pt: (layer, pt[b, n_pages - 1 - (j * n_grp + g)], 0, 0))
    kv_spec = lambda g: pl.BlockSpec((1, 1, N_HEADS, HEAD_DIM, PAGE_SIZE), page(g))
    grid_spec = pltpu.PrefetchScalarGridSpec(
        num_scalar_prefetch=1,
        grid=(nd, n_pages // n_grp),
        in_specs=[pl.BlockSpec((1, 3, N_HEADS, HEAD_DIM), lambda b, j, pt: (b, 0, 0, 0)),
                  pl.BlockSpec((1, N_HEADS, HEAD_DIM, LANES), lambda b, j, pt: (b, 0, 0, 0)),
                  pl.BlockSpec((1, 1, N_SMALL), lambda b, j, pt: (b, 0, 0)),
                  pl.BlockSpec((1, N_SMALL), lambda b, j, pt: (0, 0))]
                 + [kv_spec(g) for g in range(n_grp)] + [kv_spec(g) for g in range(n_grp)]
                 + [pl.BlockSpec((1, 1, N_HEADS, PAGE_SIZE), page4(g)) for g in range(n_grp)],
        out_specs=[pl.BlockSpec((1, N_HEADS, HEAD_DIM), lambda b, j, pt: (b, 0, 0)),
                   pl.BlockSpec((1, 1, N_SMALL), lambda b, j, pt: (b, 0, 0))],
        scratch_shapes=[pltpu.VMEM((N_HEADS, LANES), F32)] * 4 + [pltpu.VMEM((N_HEADS, HEAD_DIM, LANES), F32)],
    )
    o, lf_new = pl.pallas_call(
        functools.partial(_dfox_kernel, n_grp=n_grp),
        grid_spec=grid_spec,
        out_shape=[jax.ShapeDtypeStruct((nd, N_HEADS, HEAD_DIM), F32), jax.ShapeDtypeStruct((nd, 1, N_SMALL), F32)],
        compiler_params=_cparams(("arbitrary", "arbitrary")),
        name="fox_decode",
    )(page_table, qkv8, q_rep, small_s.reshape(nd, 1, N_SMALL), bias_row,
      *([kt] * n_grp), *([vt] * n_grp), *([lft] * n_grp))
    return o.reshape(nd, W_BR), lf_new


_MAIN_SEGMENTS = ((O_QKV_A, 3 * W_BR), (O_P_B, 3 * W_BR), (O_QKV_C, 3 * W_BR), (O_Z_A, W_BR), (O_Z_B, W_BR),
                  (O_Z_C, W_BR), (O_U_D, W_BR), (O_Z_D, W_BR))
_SMALL_SEGMENTS = ((O_A_A, 2 * N_HEADS), (O_WL, 2 * LORA_B), (O_F_C, N_HEADS))


def _wprep_kernel(wt_ref, main_ref, small_ref):
    wt = wt_ref[0]
    main_ref[...] = jnp.concatenate([wt[a:a + n, :] for a, n in _MAIN_SEGMENTS], axis=0).T.astype(BF16)
    used = sum(n for _, n in _SMALL_SEGMENTS)
    small = [wt[a:a + n, :] for a, n in _SMALL_SEGMENTS] + [jnp.zeros((N_SMALL - used, wt.shape[1]), F32)]
    small_ref[...] = jnp.concatenate(small, axis=0).T.astype(BF16)


def _prep_w_in(w_in, layer, tr=256):
    _, d, d_in = w_in.shape
    return pl.pallas_call(
        _wprep_kernel,
        grid=(d // tr,),
        in_specs=[pl.BlockSpec((1, d_in, tr), lambda i: (layer, 0, i))],
        out_specs=[pl.BlockSpec((tr, N_MAIN), lambda i: (i, 0)), pl.BlockSpec((tr, N_SMALL), lambda i: (i, 0))],
        out_shape=[jax.ShapeDtypeStruct((d, N_MAIN), BF16), jax.ShapeDtypeStruct((d, N_SMALL), BF16)],
        compiler_params=_cparams(("arbitrary",)),
        name="w_in_prep",
    )(jnp.swapaxes(w_in, 1, 2))


def _small_row(pairs):
    row = jnp.zeros((N_SMALL,), F32)
    for off, vec in pairs:
        row = row.at[off:off + vec.shape[0]].set(vec.astype(F32))
    return row[None, :]


def _layer_params(l, w_in, conv_A, A_log, dt_bias, norm_A, mu_B, w0_B, w_up_B, a0_B, a_up_B, xi_B, alpha_B, rho_B,
                  gn_g_B, gn_b_B, b_f_C, pool_w_D, pool_scale_D, w_out, ln_g, ln_b):
    w_main, w_small = _prep_w_in(w_in, l)
    gpar = jnp.concatenate([_small_row([(S_A, -jnp.exp(A_log[l].astype(F32)))]), _small_row([(S_A, dt_bias[l])]),
                            jnp.zeros((6, N_SMALL), F32)], axis=0)
    head_of_lane = jnp.arange(W_BR) // HEAD_DIM
    lane = jnp.arange(N_SMALL)[:, None]
    eg = (lane == head_of_lane[None, :] + S_A).astype(BF16)
    eb = (lane == head_of_lane[None, :] + S_B).astype(BF16)
    mu = mu_B[l].astype(F32)
    wup = jnp.zeros((N_SMALL, W_BR), F32).at[S_WL:S_WL + LORA_B].set(w_up_B[l].astype(F32))
    aup = jnp.zeros((N_SMALL, W_BR), F32).at[S_AL:S_AL + LORA_B].set(a_up_B[l].astype(F32))
    bd = (head_of_lane[:, None] == head_of_lane[None, :]).astype(BF16)
    par = jnp.stack([w0_B[l], a0_B[l], xi_B[l], alpha_B[l], rho_B[l], gn_g_B[l], gn_b_B[l],
                     jnp.zeros((W_BR,), F32)]).astype(F32)
    return dict(
        w_main=w_main, w_small=w_small, bd=bd, conv_w=conv_A[l].astype(F32), gpar=gpar, eg=eg, eb=eb,
        normg=jnp.tile(norm_A[l].astype(F32), N_HEADS)[None, :],
        mu=mu[None, 0:3 * W_BR], mus=_small_row([(S_WL, mu[3 * W_BR:])]), wup=wup, aup=aup, par=par,
        fbias=_small_row([(S_F, b_f_C[l])]), pool_w=pool_w_D[l].astype(F32), pscale=pool_scale_D[l].astype(F32)[None, :],
        w_out=w_out[l].astype(BF16), ln_g=ln_g[l].astype(F32)[None, :], ln_b=ln_b[l].astype(F32)[None, :])


def _prompt_layer(x3, P, layer, k_all, v_all, tb_rec, tq, tb_gate, tb_pool, tm_proj, tm_out):
    n, t, _ = x3.shape
    x2 = x3.reshape(n * t, D_MODEL)
    main2, small2, k_all, v_all = _proj(x2, P['w_main'], P['w_small'], k_all, v_all, layer, tm_proj)
    main3 = main2.reshape(n, t, N_MAIN)
    small3 = small2.reshape(n, t, N_SMALL)
    o_a, s_a = _gdn_prompt(main3, small3, P['conv_w'], P['gpar'], P['eg'], P['eb'], P['normg'], P['bd'], tb_rec)
    o_b, s_b = _rwkv_prompt(main3, small3, P['mu'], P['mus'], _bf(P['wup']), _bf(P['aup']), P['par'], P['bd'],
                            tb_rec)
    logf3, c3, ct3 = _gates(small3, P['fbias'], tb_gate)
    o_c = _fox_prompt(main3, c3, ct3, tq)
    o_d = _pool_prompt(main3, P['pool_w'], P['pscale'], tb_pool)
    flat = lambda a: a.reshape(n * t, W_BR)
    y2 = _out_proj(flat(o_a), flat(o_b), flat(o_c), flat(o_d), main2, x2, P['w_out'], P['ln_g'], P['ln_b'], tm_out)
    new = (s_a,
           main3[:, t - (CONV_W - 1):, C_QKV_A:C_QKV_A + 3 * W_BR],
           s_b,
           jnp.concatenate([main3[:, t - 1, C_RKV_B:C_RKV_B + 3 * W_BR], small3[:, t - 1, S_WL:S_WL + 2 * LORA_B]], axis=-1),
           logf3[:, :, S_F:S_F + N_HEADS],
           main3[:, t - POOL_BUF:, C_U_D:C_U_D + W_BR])
    return y2.reshape(n, t, D_MODEL), new, k_all, v_all


def _sample_layer(x3, st, cache, layer, k_all, v_all, page_table, P):
    nd = x3.shape[0]
    st_a, conv_a, st_b, shift_b, dbuf = st
    cache_k, cache_v, cache_lf = cache
    x2 = x3.reshape(nd, D_MODEL)
    main_s, small_s, k_all, v_all = _proj(x2, P['w_main'], P['w_small'], k_all, v_all, layer, nd)
    shift_small = jnp.zeros((nd, N_SMALL), F32).at[:, S_WL:S_WL + 2 * LORA_B].set(shift_b[:, 3 * W_BR:].astype(F32))
    o_a, o_b, o_d, s_a, conv_n, s_b, dbuf_n = _decode_mixers(
        main_s, small_s, st_a, conv_a, st_b, shift_b[:, 0:3 * W_BR], shift_small, dbuf,
        P['conv_w'], P['gpar'], P['normg'], P['mu'], P['mus'], P['wup'], P['aup'], P['par'], P['pool_w'], P['pscale'],
        page_table.shape[1] * PAGE_SIZE)
    qkv_s = main_s[:, C_QKV_C:C_QKV_C + 3 * W_BR]
    n_grp = math.gcd(page_table.shape[1], 16)
    o_c, lf_new = _fox_decode(page_table, qkv_s, small_s, P['fbias'], cache_k, cache_v, cache_lf, layer, n_grp)
    flat = lambda a: a.reshape(nd, W_BR)
    y2 = _out_proj(flat(o_a), flat(o_b), flat(o_c), flat(o_d), main_s, x2, P['w_out'], P['ln_g'], P['ln_b'], nd)
    new = (s_a, conv_n, s_b,
           jnp.concatenate([main_s[:, C_RKV_B:C_RKV_B + 3 * W_BR], small_s[:, S_WL:S_WL + 2 * LORA_B]], axis=-1),
           lf_new[:, :, S_F:S_F + N_HEADS],
           dbuf_n)
    return y2.reshape(nd, 1, D_MODEL), new, k_all, v_all


def kernel(x_prompt, x_sample, state_A_S, state_A_conv, state_B_S, state_B_shift, cache_C_k, cache_C_v, cache_C_logf, state_D_buf, page_table, w_in, conv_A, A_log, dt_bias, norm_A, mu_B, w0_B, w_up_B, a0_B, a_up_B, xi_B, alpha_B, rho_B, gn_g_B, gn_b_B, b_f_C, pool_w_D, pool_scale_D, w_out, ln_g, ln_b):
    depth = w_in.shape[0]
    t = x_prompt.shape[1]
    tb_rec = min(256, t)
    tq = min(256, t)
    tb_gate = min(512, t)
    tb_pool = min(512, t)
    tm_proj = min(1024, t)
    tm_out = min(256, x_prompt.shape[0] * t)
    y_p, y_s = x_prompt, x_sample
    prompt_new, sample_new = [], []
    nb, nd = x_prompt.shape[0], x_sample.shape[0]
    pk, pv = jnp.zeros((depth, nb, W_BR, t), F32), jnp.zeros((depth, nb, W_BR, t), F32)
    sk, sv = jnp.zeros((depth, nd, W_BR), F32), jnp.zeros((depth, nd, W_BR), F32)
    for l in range(depth):
        P = _layer_params(l, w_in, conv_A, A_log, dt_bias, norm_A, mu_B, w0_B, w_up_B, a0_B, a_up_B, xi_B, alpha_B,
                          rho_B, gn_g_B, gn_b_B, b_f_C, pool_w_D, pool_scale_D, w_out, ln_g, ln_b)
        y_p, new_p, pk, pv = _prompt_layer(y_p, P, l, pk, pv, tb_rec, tq, tb_gate, tb_pool, tm_proj, tm_out)
        st = (state_A_S[l], state_A_conv[l], state_B_S[l], state_B_shift[l], state_D_buf[l])
        y_s, new_s, sk, sv = _sample_layer(y_s, st, (cache_C_k, cache_C_v, cache_C_logf), l, sk, sv, page_table, P)
        prompt_new.append(new_p)
        sample_new.append(new_s)
    p_a_s, p_a_conv, p_b_s, p_b_shift, p_c_logf, p_d_buf = [jnp.stack([n[i] for n in prompt_new]) for i in range(6)]
    s_a_s, s_a_conv, s_b_s, s_b_shift, s_c_logf, s_d_buf = [jnp.stack([n[i] for n in sample_new]) for i in range(6)]
    p_c_k, p_c_v = (jnp.transpose(a.reshape(depth, nb, N_HEADS, HEAD_DIM, t), (0, 1, 4, 2, 3)) for a in (pk, pv))
    s_c_k, s_c_v = (a.reshape(depth, nd, 1, N_HEADS, HEAD_DIM) for a in (sk, sv))
    return (y_p, y_s, p_a_s, p_a_conv, p_b_s, p_b_shift, p_c_k, p_c_v, p_c_logf, p_d_buf,
            s_a_s, s_a_conv, s_b_s, s_b_shift, s_c_k, s_c_v, s_c_logf, s_d_buf)
```

```python
import functools
import math

import jax
import jax.numpy as jnp
from jax import lax
from jax.experimental import pallas as pl
from jax.experimental.pallas import tpu as pltpu

F32 = jnp.float32
BF16 = jnp.bfloat16
HI = lax.Precision.HIGHEST

D_MODEL = 2048
W_BR = 512
HEAD_DIM = 64
N_HEADS = W_BR // HEAD_DIM
CONV_W = 4
CHUNK = 64
LORA_B = 32
POOL_WINDOWS = (2, 4, 8, 16)
POOL_GW = W_BR // len(POOL_WINDOWS)
POOL_BUF = max(POOL_WINDOWS) - 1
PAGE_SIZE = 128
LANES = 128
FOX_ROW_SPLIT = 2
DEPTH = 2
ALPHA_DN = (2.0 * DEPTH) ** 0.25
LN_EPS = 1e-5
GN_EPS = 64e-5
RMS_EPS = 1e-6
L2_EPS = 1e-6
RWKV_DECAY_SCALE = math.exp(-0.5)
NEG_INF = -1e30
LOG2_E = 1.4426950408889634

C_QKV_A, C_RKV_B, C_QKV_C = 0, 1536, 3072
C_Z_A, C_Z_B, C_Z_C, C_U_D, C_Z_D = 4608, 5120, 5632, 6144, 6656
N_MAIN = 7168
S_A, S_B, S_WL, S_AL, S_F = 0, 8, 16, 48, 80
N_SMALL = 128
O_QKV_A, O_A_A, O_Z_A, O_P_B, O_WL, O_Z_B, O_QKV_C, O_F_C, O_Z_C, O_U_D, O_Z_D, D_IN = (
    0, 1536, 1552, 2064, 3600, 3664, 4176, 5712, 5720, 6232, 6744, 7256)

VMEM_LIMIT = 48 * 1024 * 1024


VMEM_LIMIT_PROJ = 56 * 1024 * 1024


def _cparams(sem, vmem_limit=VMEM_LIMIT):
    return pltpu.CompilerParams(dimension_semantics=sem, vmem_limit_bytes=vmem_limit)


def _mm(a, b, prec=None):
    return jnp.dot(a, b, preferred_element_type=F32, precision=prec)


def _mm_nt(a, b, prec=None):
    return lax.dot_general(a, b, (((1,), (1,)), ((), ())), preferred_element_type=F32, precision=prec)


def _mm_tn(a, b, prec=None):
    return lax.dot_general(a, b, (((0,), (0,)), ((), ())), preferred_element_type=F32, precision=prec)


def _split3(x):
    hi = x.astype(BF16)
    r1 = x - hi.astype(F32)
    mid = r1.astype(BF16)
    lo = (r1 - mid.astype(F32)).astype(BF16)
    return hi, mid, lo


def _dot3_l(x, b_bf16):
    hi, mid, lo = _split3(x)
    return _mm(hi, b_bf16) + _mm(mid, b_bf16) + _mm(lo, b_bf16)


def _head_sums(x, same_head_bf16):
    return _mm(x.astype(BF16), same_head_bf16)


def _dot3_r(a_bf16, x):
    hi, mid, lo = _split3(x)
    return _mm(a_bf16, hi) + _mm(a_bf16, mid) + _mm(a_bf16, lo)


def _sigmoid(x):
    return 1.0 / (1.0 + jnp.exp(-x))


def _silu(x):
    return x * _sigmoid(x)


def _softplus(x):
    return jnp.maximum(x, 0.0) + jnp.log1p(jnp.exp(-jnp.abs(x)))


def _log_sigmoid(x):
    return -_softplus(-x)


def _iota2(shape, dim):
    return lax.broadcasted_iota(jnp.int32, shape, dim)


def _head_block_diag():
    r = _iota2((W_BR, W_BR), 0) // HEAD_DIM
    c = _iota2((W_BR, W_BR), 1) // HEAD_DIM
    return jnp.where(r == c, 1.0, 0.0).astype(BF16)


def _chunk_tril(n):
    r = _iota2((n, n), 0)
    c = _iota2((n, n), 1)
    return jnp.where((r >= c) & (r // CHUNK == c // CHUNK), 1.0, 0.0).astype(BF16)


_DN_NN = (((1,), (0,)), ((), ()))
_DN_NT = (((1,), (1,)), ((), ()))
_DN_TN = (((0,), (0,)), ((), ()))


def _bf(x):
    return x.astype(BF16)


def _mmb(a, b, dn=_DN_NN):
    return lax.dot_general(a, b, dn, preferred_element_type=F32)


def _tri_solve(a_list, rhs_list, blk):
    each = lambda f, *ls: [f(*xs) for xs in zip(*ls)]
    width = rhs_list[0].shape[1]
    ad = each(lambda a: a * blk, a_list)
    adb = each(_bf, ad)
    a2b = each(_bf, each(_mmb, adb, adb))
    a4b = each(_bf, each(_mmb, a2b, a2b))
    a8b = each(_bf, each(_mmb, a4b, a4b))
    z = each(lambda r, a, d: jnp.concatenate([r, a - d], axis=1), rhs_list, a_list, ad)
    z = each(lambda x, y: x - y, z, each(_mmb, adb, each(_bf, z)))
    for pw in (a2b, a4b, a8b):
        z = each(lambda x, y: x + y, z, each(_mmb, pw, each(_bf, z)))
    y = each(lambda x: x[:, 0:width], z)
    nb = each(lambda x: _bf(x[:, width:width + CHUNK]), z)
    n2b = each(_bf, each(_mmb, nb, nb))
    t = each(lambda x, u: x + u, y, each(_mmb, n2b, each(_bf, y)))
    return each(lambda x, u: x - u, t, each(_mmb, nb, each(_bf, t)))


def _chunk_masks():
    r = _iota2((CHUNK, CHUNK), 0)
    c = _iota2((CHUNK, CHUNK), 1)
    eye = jnp.where(r == c, 1.0, 0.0).astype(F32)
    strict = jnp.where(r > c, 1.0, 0.0).astype(F32)
    incl = jnp.where(r >= c, 1.0, 0.0).astype(F32)
    blk = jnp.where(r // 16 == c // 16, 1.0, 0.0).astype(F32)
    return eye, strict, incl, blk


def _proj_kernel(x_ref, w_ref, ws_ref, kin_ref, vin_ref, main_ref, small_ref, k_ref, v_ref, xb_ref, *, tn,
                 token_minor):
    j = pl.program_id(1)

    @pl.when(j == 0)
    def _():
        xb = x_ref[...].astype(BF16)
        xb_ref[...] = xb
        small_ref[...] = _mm(xb, ws_ref[...])

    acc = _mm(xb_ref[...], w_ref[...])
    main_ref[...] = acc
    for col, out_ref in ((C_QKV_C + W_BR, k_ref), (C_QKV_C + 2 * W_BR, v_ref)):
        @pl.when(j == col // tn)
        def _(col=col, out_ref=out_ref):
            kv = acc[:, col % tn:col % tn + W_BR]
            if token_minor:
                out_ref[0, 0] = kv.T
            else:
                out_ref[0] = kv


def _proj(x2d, w_main, w_small, k_all, v_all, layer, tm, tn=1024):
    m = x2d.shape[0]
    assert (C_QKV_C + W_BR) % tn + W_BR <= tn and (C_QKV_C + 2 * W_BR) % tn + W_BR <= tn
    token_minor = k_all.ndim == 4
    if token_minor:
        per_seq = k_all.shape[3] // tm
        kv_spec = pl.BlockSpec((1, 1, W_BR, tm), lambda i, j: (layer, i // per_seq, 0, i % per_seq))
    else:
        kv_spec = pl.BlockSpec((1, tm, W_BR), lambda i, j: (layer, i, 0))
    return pl.pallas_call(
        functools.partial(_proj_kernel, tn=tn, token_minor=token_minor),
        grid=(m // tm, N_MAIN // tn),
        in_specs=[pl.BlockSpec((tm, D_MODEL), lambda i, j: (i, 0)),
                  pl.BlockSpec((D_MODEL, tn), lambda i, j: (0, j)),
                  pl.BlockSpec((D_MODEL, N_SMALL), lambda i, j: (0, 0)),
                  pl.BlockSpec(memory_space=pl.ANY), pl.BlockSpec(memory_space=pl.ANY)],
        out_specs=[pl.BlockSpec((tm, tn), lambda i, j: (i, j)),
                   pl.BlockSpec((tm, N_SMALL), lambda i, j: (i, 0)), kv_spec, kv_spec],
        out_shape=[jax.ShapeDtypeStruct((m, N_MAIN), F32), jax.ShapeDtypeStruct((m, N_SMALL), F32),
                   jax.ShapeDtypeStruct(k_all.shape, F32), jax.ShapeDtypeStruct(v_all.shape, F32)],
        input_output_aliases={3: 2, 4: 3},
        scratch_shapes=[pltpu.VMEM((tm, D_MODEL), BF16)],
        compiler_params=_cparams(("arbitrary", "arbitrary"), VMEM_LIMIT_PROJ),
        name="proj",
    )(x2d, w_main, w_small, k_all, v_all)


def _gates_kernel(small_ref, bias_ref, logf_ref, c_ref, ct_ref, carry_ref, *, tb):
    @pl.when(pl.program_id(1) == 0)
    def _():
        carry_ref[...] = jnp.zeros_like(carry_ref)

    logf = _log_sigmoid(small_ref[0] + bias_ref[...])
    r = _iota2((tb, tb), 0)
    c = _iota2((tb, tb), 1)
    tril = jnp.where(r >= c, 1.0, 0.0).astype(BF16)
    cum = _dot3_r(tril, logf) + carry_ref[0:1, :]
    carry_ref[...] = jnp.broadcast_to(cum[tb - 1:tb, :], carry_ref.shape)
    logf_ref[0] = logf
    c_ref[0] = cum
    ct_ref[0] = cum.T[S_F:S_F + N_HEADS, :]


def _gates(small3, bias_row, tb):
    n, t, _ = small3.shape
    return pl.pallas_call(
        functools.partial(_gates_kernel, tb=tb),
        grid=(n, t // tb),
        in_specs=[pl.BlockSpec((1, tb, N_SMALL), lambda b, j: (b, j, 0)),
                  pl.BlockSpec((1, N_SMALL), lambda b, j: (0, 0))],
        out_specs=[pl.BlockSpec((1, tb, N_SMALL), lambda b, j: (b, j, 0)),
                   pl.BlockSpec((1, tb, N_SMALL), lambda b, j: (b, j, 0)),
                   pl.BlockSpec((1, N_HEADS, tb), lambda b, j: (b, 0, j))],
        out_shape=[jax.ShapeDtypeStruct((n, t, N_SMALL), F32), jax.ShapeDtypeStruct((n, t, N_SMALL), F32),
                   jax.ShapeDtypeStruct((n, N_HEADS, t), F32)],
        scratch_shapes=[pltpu.VMEM((8, N_SMALL), F32)],
        compiler_params=_cparams(("arbitrary", "arbitrary")),
        name="fox_gates",
    )(small3, bias_row)


def _fox_kernel(qi_ref, kj_ref, q_ref, k_ref, v_ref, c_ref, ct_ref, o_ref, m_s, l_s, acc_s, cq_s, *, tq, n_sub):
    i = qi_ref[pl.program_id(1)]
    j = kj_ref[pl.program_id(1)]
    n_pairs = W_BR // LANES
    lo_half = _iota2((tq, LANES), 1) < HEAD_DIM
    pair_lanes = [slice(pr * LANES, (pr + 1) * LANES) for pr in range(n_pairs)]

    @pl.when(j == 0)
    def _():
        m_s[...] = jnp.full(m_s.shape, NEG_INF, F32)
        l_s[...] = jnp.zeros_like(l_s)
        acc_s[...] = jnp.zeros_like(acc_s)
        cq = c_ref[0]
        for h in range(N_HEADS):
            cq_s[h] = jnp.broadcast_to(cq[:, S_F + h:S_F + h + 1] * LOG2_E, (tq, LANES))

    def step(masked):
        k = k_ref[0].astype(BF16)
        v = v_ref[0]
        ct = ct_ref[0] * LOG2_E
        ones_bd = jnp.concatenate([jnp.where(lo_half, 1.0, 0.0), jnp.where(lo_half, 0.0, 1.0)], axis=0)
        v_bd = [jnp.concatenate(
            [jnp.concatenate([jnp.where(lo_half, v[:, ps], 0.0), jnp.where(lo_half, 0.0, v[:, ps])], axis=0), ones_bd],
            axis=1).astype(BF16) for ps in pair_lanes]
        rq = tq // n_sub
        lo_q = _iota2((rq, LANES), 1) < HEAD_DIM
        wide = lambda x: jnp.concatenate([x] * (tq // LANES), axis=1)
        for r in range(n_sub):
            rs = slice(r * rq, (r + 1) * rq)
            q = q_ref[0, rs, :] * (HEAD_DIM ** -0.5 * LOG2_E)
            if masked:
                keep = _iota2((rq, tq), 0) + r * rq >= _iota2((rq, tq), 1)
            s_all, m_all = [], []
            for pr, ps in enumerate(pair_lanes):
                qp, kp = q[:, ps], k[:, ps]
                for e, q_half in enumerate((jnp.where(lo_q, qp, 0.0), jnp.where(lo_q, 0.0, qp))):
                    h = 2 * pr + e
                    s = _mm_nt(q_half.astype(BF16), kp) + wide(cq_s[h, rs, :]) - ct[h:h + 1, :]
                    if masked:
                        s = jnp.where(keep, s, NEG_INF)
                    s_all.append(s)
                    m_all.append(jnp.maximum(m_s[h, rs, :], jnp.max(s, axis=1, keepdims=True)))
            p_all, alpha_all = [], []
            for h in range(N_HEADS):
                m_new = m_all[h]
                alpha = jnp.exp2(m_s[h, rs, :] - m_new)
                p = jnp.exp2(s_all[h] - wide(m_new))
                m_s[h, rs, :] = m_new
                p_all.append(p.astype(BF16))
                alpha_all.append(alpha)
            for pr, ps in enumerate(pair_lanes):
                p_pair = jnp.concatenate([p_all[2 * pr], p_all[2 * pr + 1]], axis=1)
                alpha_p = jnp.where(lo_q, alpha_all[2 * pr], alpha_all[2 * pr + 1])
                pv_l = _mm(p_pair, v_bd[pr])
                acc_s[rs, ps] = alpha_p * acc_s[rs, ps] + pv_l[:, 0:LANES]
                l_s[pr, rs, :] = alpha_p * l_s[pr, rs, :] + pv_l[:, LANES:2 * LANES]

    @pl.when(j < i)
    def _():
        step(False)

    @pl.when(j == i)
    def _():
        step(True)
        for pr, ps in enumerate(pair_lanes):
            o_ref[0, :, ps] = acc_s[:, ps] / l_s[pr]


def _fox_prompt(main3, c3, ct3, tq):
    n, t, _ = main3.shape
    nb = t // tq
    qb, kb, vb = C_QKV_C // W_BR, C_QKV_C // W_BR + 1, C_QKV_C // W_BR + 2
    pairs = [(i, j) for i in range(nb) for j in range(i + 1)]
    qi = jnp.asarray([p[0] for p in pairs], jnp.int32)
    kj = jnp.asarray([p[1] for p in pairs], jnp.int32)
    grid_spec = pltpu.PrefetchScalarGridSpec(
        num_scalar_prefetch=2,
        grid=(n, len(pairs)),
        in_specs=[pl.BlockSpec((1, tq, W_BR), lambda b, s, qi, kj: (b, qi[s], qb)),
                  pl.BlockSpec((1, tq, W_BR), lambda b, s, qi, kj: (b, kj[s], kb)),
                  pl.BlockSpec((1, tq, W_BR), lambda b, s, qi, kj: (b, kj[s], vb)),
                  pl.BlockSpec((1, tq, N_SMALL), lambda b, s, qi, kj: (b, qi[s], 0)),
                  pl.BlockSpec((1, N_HEADS, tq), lambda b, s, qi, kj: (b, 0, kj[s]))],
        out_specs=pl.BlockSpec((1, tq, W_BR), lambda b, s, qi, kj: (b, qi[s], 0)),
        scratch_shapes=[pltpu.VMEM((N_HEADS, tq, LANES), F32), pltpu.VMEM((W_BR // LANES, tq, LANES), F32),
                        pltpu.VMEM((tq, W_BR), F32), pltpu.VMEM((N_HEADS, tq, LANES), F32)],
    )
    return pl.pallas_call(
        functools.partial(_fox_kernel, tq=tq, n_sub=FOX_ROW_SPLIT),
        grid_spec=grid_spec,
        out_shape=jax.ShapeDtypeStruct((n, t, W_BR), F32),
        compiler_params=_cparams(("arbitrary", "arbitrary")),
        name="fox_prompt",
    )(qi, kj, main3, main3, main3, c3, ct3)


def _gdn_kernel(qkv_ref, small_ref, convw_ref, gpar_ref, eg_ref, eb_ref, normg_ref, bd_ref,
                o_ref, s_out_ref, ext_s, q_s, k_s, v_s, b_s, g_s, o_s, st_s, *, tb):
    j = pl.program_id(1)

    @pl.when(j == 0)
    def _():
        ext_s[0:8, :] = jnp.zeros((8, 3 * W_BR), F32)
        st_s[...] = jnp.zeros_like(st_s)

    u = qkv_ref[0]
    ext_s[8:8 + tb, :] = u
    cw = convw_ref[...]
    c = (ext_s[5:5 + tb, :] * cw[0:1, :] + ext_s[6:6 + tb, :] * cw[1:2, :]
         + ext_s[7:7 + tb, :] * cw[2:3, :] + u * cw[3:4, :])
    ext_s[0:8, :] = u[tb - 8:tb, :]
    c = _silu(c)
    bd = bd_ref[...]
    q = c[:, 0:W_BR]
    k = c[:, W_BR:2 * W_BR]
    q_s[...] = q * lax.rsqrt(_head_sums(q * q, bd) + L2_EPS) * (HEAD_DIM ** -0.5)
    k_s[...] = k * lax.rsqrt(_head_sums(k * k, bd) + L2_EPS)
    v_s[...] = c[:, 2 * W_BR:3 * W_BR]

    sm = small_ref[0]
    gpar = gpar_ref[...]
    g = gpar[0:1, :] * _softplus(sm + gpar[1:2, :])
    beta = _sigmoid(sm)
    gcum = _dot3_r(_chunk_tril(tb), g)
    g_s[...] = _dot3_l(gcum, eg_ref[...])
    b_s[...] = _dot3_l(beta, eb_ref[...])

    gcum_t = gcum.T[S_A:S_A + N_HEADS, :]

    _, strict, incl, blk = _chunk_masks()

    heads = range(N_HEADS)
    lanes = [slice(h * HEAD_DIM, (h + 1) * HEAD_DIM) for h in heads]
    n_chunks = tb // CHUNK
    group = 4

    ops = {}
    for c0 in range(0, n_chunks, group):
        chunks = range(c0, min(c0 + group, n_chunks))
        items = [(c, h) for c in chunks for h in heads]
        rows = lambda c: slice(c * CHUNK, (c + 1) * CHUNK)
        tile = lambda ref: [ref[rows(c), lanes[h]] for c, h in items]
        q_h, k_h, v_h, b_h, g_h = tile(q_s), tile(k_s), tile(v_s), tile(b_s), tile(g_s)
        grow = [jnp.broadcast_to(gcum_t[h:h + 1, c * CHUNK:(c + 1) * CHUNK], (CHUNK, CHUNK))
                for c, h in items]
        idx = range(len(items))
        decay = [jnp.exp(jnp.where(incl > 0, g_h[i] - grow[i], NEG_INF)) for i in idx]
        kb = [k_h[i] * b_h[i] for i in idx]
        both = [_mmb(_bf(jnp.concatenate([kb[i], q_h[i]], axis=0)), _bf(k_h[i]), _DN_NT) for i in idx]
        a = [both[i][0:CHUNK, :] * decay[i] * strict for i in idx]
        attn = [_bf(both[i][CHUNK:2 * CHUNK, :] * decay[i]) for i in idx]
        eg = [jnp.exp(g_h[i]) for i in idx]
        sol = _tri_solve(a, [jnp.concatenate([v_h[i] * b_h[i], kb[i] * eg[i]], axis=1) for i in idx], blk)
        ub = [_bf(sol[i][:, 0:HEAD_DIM]) for i in idx]
        wb = [_bf(sol[i][:, HEAD_DIM:2 * HEAD_DIM]) for i in idx]
        o1 = [_bf(q_h[i] * eg[i] - _mmb(attn[i], wb[i])) for i in idx]
        o2 = [_mmb(attn[i], ub[i]) for i in idx]
        glast = [g_h[i][CHUNK - 1:CHUNK, :] for i in idx]
        kdec = [_bf(k_h[i] * jnp.exp(glast[i] - g_h[i])) for i in idx]
        m = [_bf(_mmb(kdec[i], wb[i], _DN_TN)) for i in idx]
        cc = [_mmb(kdec[i], ub[i], _DN_TN) for i in idx]
        for i, key in enumerate(items):
            ops[key] = (o1[i], o2[i], m[i], cc[i], jnp.exp(glast[i]))

    sts = [st_s[h] for h in heads]
    for c in range(n_chunks):
        stb = [_bf(sts[h]) for h in heads]
        outs = [_mmb(ops[c, h][0], stb[h]) + ops[c, h][1] for h in heads]
        sts = [sts[h] * ops[c, h][4] - _mmb(ops[c, h][2], stb[h]) + ops[c, h][3] for h in heads]
        o_s[c * CHUNK:(c + 1) * CHUNK, :] = jnp.concatenate(outs, axis=1)
    for h in heads:
        st_s[h] = sts[h]

    o = o_s[...]
    ms = _head_sums(o * o, bd) * (1.0 / HEAD_DIM)
    o_ref[0] = o * lax.rsqrt(ms + RMS_EPS) * normg_ref[...]

    @pl.when(j == pl.num_programs(1) - 1)
    def _():
        s_out_ref[0] = st_s[...]


def _gdn_prompt(main3, small3, conv_w, gpar, eg, eb, normg, bd, tb):
    n, t, _ = main3.shape
    full = lambda shape: pl.BlockSpec(shape, lambda b, j: (0,) * len(shape))
    return pl.pallas_call(
        functools.partial(_gdn_kernel, tb=tb),
        grid=(n, t // tb),
        in_specs=[pl.BlockSpec((1, tb, 3 * W_BR), lambda b, j: (b, j, C_QKV_A // (3 * W_BR))),
                  pl.BlockSpec((1, tb, N_SMALL), lambda b, j: (b, j, 0)),
                  full((CONV_W, 3 * W_BR)), full((8, N_SMALL)), full((N_SMALL, W_BR)), full((N_SMALL, W_BR)),
                  full((1, W_BR)), full((W_BR, W_BR))],
        out_specs=[pl.BlockSpec((1, tb, W_BR), lambda b, j: (b, j, 0)),
                   pl.BlockSpec((1, N_HEADS, HEAD_DIM, HEAD_DIM), lambda b, j: (b, 0, 0, 0))],
        out_shape=[jax.ShapeDtypeStruct((n, t, W_BR), F32),
                   jax.ShapeDtypeStruct((n, N_HEADS, HEAD_DIM, HEAD_DIM), F32)],
        scratch_shapes=[pltpu.VMEM((tb + 8, 3 * W_BR), F32)] + [pltpu.VMEM((tb, W_BR), F32)] * 6
                       + [pltpu.VMEM((N_HEADS, HEAD_DIM, HEAD_DIM), F32)],
        compiler_params=_cparams(("arbitrary", "arbitrary")),
        name="gdn_prompt",
    )(main3, small3, conv_w, gpar, eg, eb, normg, bd)


def _rwkv_kernel(p_ref, small_ref, mu_ref, mus_ref, wup_ref, aup_ref, par_ref, bd_ref,
                 o_ref, s_out_ref, ext_s, exts_s, ah_s, bh_s, kh_s, rh_s, be_s, ke_s, v_s, wl_s, y_s, st_s, *, tb):
    j = pl.program_id(1)

    @pl.when(j == 0)
    def _():
        ext_s[0:8, :] = jnp.zeros((8, 3 * W_BR), F32)
        exts_s[0:8, :] = jnp.zeros((8, N_SMALL), F32)
        st_s[...] = jnp.zeros_like(st_s)

    p = p_ref[0]
    sm = small_ref[0]
    ext_s[8:8 + tb, :] = p
    exts_s[8:8 + tb, :] = sm
    prev = ext_s[7:7 + tb, :]
    prevs = exts_s[7:7 + tb, :]
    ext_s[0:8, :] = p[tb - 8:tb, :]
    exts_s[0:8, :] = sm[tb - 8:tb, :]
    ps = p + (prev - p) * mu_ref[...]
    pss = sm + (prevs - sm) * mus_ref[...]
    r = ps[:, 0:W_BR]
    k = ps[:, W_BR:2 * W_BR]
    v = ps[:, 2 * W_BR:3 * W_BR]
    par = par_ref[...]
    w0, a0, xi, alpha, rho, gn_g, gn_b = (par[i:i + 1, :] for i in range(7))
    d = w0 + _mm(_bf(jnp.tanh(pss)), wup_ref[...])
    logw = -RWKV_DECAY_SCALE * _sigmoid(d)
    a = _sigmoid(a0 + _mm(_bf(pss), aup_ref[...]))
    bd = bd_ref[...]
    kx = k * xi
    kk = kx * lax.rsqrt(_head_sums(kx * kx, bd) + L2_EPS)
    k2 = k * (1.0 + (a - 1.0) * alpha)
    lc = _dot3_r(_chunk_tril(tb), logw)
    nb = -(a * kk)
    ah_s[...] = _bf(kk * jnp.exp(lc - logw))
    bh_s[...] = _bf(nb * jnp.exp(-lc))
    kh_s[...] = _bf(k2 * jnp.exp(-lc))
    rh_s[...] = _bf(r * jnp.exp(lc))
    v_s[...] = _bf(v)
    ll = jnp.concatenate([jnp.broadcast_to(lc[c * CHUNK + CHUNK - 1:(c + 1) * CHUNK, :], (CHUNK, W_BR))
                          for c in range(tb // CHUNK)], axis=0)
    to_end = jnp.exp(ll - lc)
    be_s[...] = _bf(nb * to_end)
    ke_s[...] = _bf(k2 * to_end)
    wl_s[...] = jnp.exp(ll)

    _, strict, incl, blk = _chunk_masks()
    incl2 = jnp.concatenate([incl, incl], axis=1)

    heads = range(N_HEADS)
    lanes = [slice(h * HEAD_DIM, (h + 1) * HEAD_DIM) for h in heads]
    n_chunks = tb // CHUNK
    group = 4

    ops = {}
    for c0 in range(0, n_chunks, group):
        chunks = range(c0, min(c0 + group, n_chunks))
        items = [(c, h) for c in chunks for h in heads]
        rows = lambda c: slice(c * CHUNK, (c + 1) * CHUNK)
        tile = lambda ref: [ref[rows(c), lanes[h]] for c, h in items]
        a_h, b_h, k_h, r_h, v_h, be_h, ke_h = (tile(x) for x in (ah_s, bh_s, kh_s, rh_s, v_s, be_s, ke_s))
        idx = range(len(items))
        gram = [_mmb(jnp.concatenate([a_h[i], r_h[i]], axis=0), jnp.concatenate([b_h[i], k_h[i]], axis=0), _DN_NT)
                for i in idx]
        akv = [_mmb(_bf(gram[i][0:CHUNK, CHUNK:2 * CHUNK] * strict), v_h[i]) for i in idx]
        sol = _tri_solve([-(gram[i][0:CHUNK, 0:CHUNK] * strict) for i in idx],
                         [jnp.concatenate([a_h[i].astype(F32), akv[i]], axis=1) for i in idx], blk)
        w1b = [_bf(sol[i][:, 0:HEAD_DIM]) for i in idx]
        w2v = [jnp.concatenate([_bf(sol[i][:, HEAD_DIM:2 * HEAD_DIM]), v_h[i]], axis=0) for i in idx]
        r_bk = [_bf(gram[i][CHUNK:2 * CHUNK, :] * incl2) for i in idx]
        g1 = [_bf(r_h[i].astype(F32) + _mmb(r_bk[i][:, 0:CHUNK], w1b[i])) for i in idx]
        g2 = [_mmb(r_bk[i], w2v[i]) for i in idx]
        m1 = [_bf(_mmb(w1b[i], be_h[i], _DN_TN)) for i in idx]
        c2 = [_mmb(w2v[i], jnp.concatenate([be_h[i], ke_h[i]], axis=0), _DN_TN) for i in idx]
        for i, (c, h) in enumerate(items):
            ops[c, h] = (g1[i], g2[i], m1[i], c2[i], wl_s[c * CHUNK:c * CHUNK + 1, lanes[h]])

    sts = [st_s[h] for h in heads]
    for c in range(n_chunks):
        stb = [_bf(sts[h]) for h in heads]
        outs = [_mmb(ops[c, h][0], stb[h], _DN_NT) + ops[c, h][1] for h in heads]
        sts = [sts[h] * ops[c, h][4] + _mmb(stb[h], ops[c, h][2]) + ops[c, h][3] for h in heads]
        y_s[c * CHUNK:(c + 1) * CHUNK, :] = jnp.concatenate(outs, axis=1)
    for h in heads:
        st_s[h] = sts[h]

    y = y_s[...]
    mean = _head_sums(y, bd) * (1.0 / HEAD_DIM)
    yc = y - mean
    var = _head_sums(yc * yc, bd) * (1.0 / HEAD_DIM)
    yn = yc * lax.rsqrt(var + GN_EPS) * gn_g + gn_b
    bonus = _head_sums(r * k2 * rho, bd) * v
    o_ref[0] = yn + bonus

    @pl.when(j == pl.num_programs(1) - 1)
    def _():
        s_out_ref[0] = st_s[...]


def _rwkv_prompt(main3, small3, mu, mus, wup, aup, par, bd, tb):
    n, t, _ = main3.shape
    full = lambda shape: pl.BlockSpec(shape, lambda b, j: (0,) * len(shape))
    return pl.pallas_call(
        functools.partial(_rwkv_kernel, tb=tb),
        grid=(n, t // tb),
        in_specs=[pl.BlockSpec((1, tb, 3 * W_BR), lambda b, j: (b, j, C_RKV_B // (3 * W_BR))),
                  pl.BlockSpec((1, tb, N_SMALL), lambda b, j: (b, j, 0)),
                  full((1, 3 * W_BR)), full((1, N_SMALL)), full((N_SMALL, W_BR)), full((N_SMALL, W_BR)),
                  full((8, W_BR)), full((W_BR, W_BR))],
        out_specs=[pl.BlockSpec((1, tb, W_BR), lambda b, j: (b, j, 0)),
                   pl.BlockSpec((1, N_HEADS, HEAD_DIM, HEAD_DIM), lambda b, j: (b, 0, 0, 0))],
        out_shape=[jax.ShapeDtypeStruct((n, t, W_BR), F32),
                   jax.ShapeDtypeStruct((n, N_HEADS, HEAD_DIM, HEAD_DIM), F32)],
        scratch_shapes=[pltpu.VMEM((tb + 8, 3 * W_BR), F32), pltpu.VMEM((tb + 8, N_SMALL), F32)]
                       + [pltpu.VMEM((tb, W_BR), BF16)] * 7 + [pltpu.VMEM((tb, W_BR), F32)] * 2
                       + [pltpu.VMEM((N_HEADS, HEAD_DIM, HEAD_DIM), F32)],
        compiler_params=_cparams(("arbitrary", "arbitrary")),
        name="rwkv_prompt",
    )(main3, small3, mu, mus, wup, aup, par, bd)


def _pool_kernel(u_ref, w_ref, scale_ref, o_ref, ext_s, *, tb):
    j = pl.program_id(1)

    @pl.when(j == 0)
    def _():
        ext_s[0:16, :] = jnp.zeros((16, W_BR), F32)

    u = u_ref[0]
    ext_s[16:16 + tb, :] = u
    pos = j * tb + _iota2((tb, POOL_GW), 0)
    outs = []
    for gi, wdw in enumerate(POOL_WINDOWS):
        ls = slice(gi * POOL_GW, (gi + 1) * POOL_GW)
        s = u[:, ls]
        for sh in range(1, wdw):
            s = s + ext_s[16 - sh:16 - sh + tb, ls]
        cnt = jnp.minimum(wdw, pos + 1).astype(F32)
        pooled = s / cnt - u[:, ls]
        outs.append(_mm(pooled.astype(BF16), w_ref[gi].astype(BF16)))
    ext_s[0:16, :] = u[tb - 16:tb, :]
    o_ref[0] = jnp.concatenate(outs, axis=1) * scale_ref[...]


def _pool_prompt(main3, pool_w, scale, tb):
    n, t, _ = main3.shape
    return pl.pallas_call(
        functools.partial(_pool_kernel, tb=tb),
        grid=(n, t // tb),
        in_specs=[pl.BlockSpec((1, tb, W_BR), lambda b, j: (b, j, C_U_D // W_BR)),
                  pl.BlockSpec((len(POOL_WINDOWS), POOL_GW, POOL_GW), lambda b, j: (0, 0, 0)),
                  pl.BlockSpec((1, W_BR), lambda b, j: (0, 0))],
        out_specs=pl.BlockSpec((1, tb, W_BR), lambda b, j: (b, j, 0)),
        out_shape=jax.ShapeDtypeStruct((n, t, W_BR), F32),
        scratch_shapes=[pltpu.VMEM((tb + 16, W_BR), F32)],
        compiler_params=_cparams(("arbitrary", "arbitrary")),
        name="pool_prompt",
    )(main3, pool_w, scale)


def _out_kernel(oa_ref, ob_ref, oc_ref, od_ref, za_ref, zb_ref, zc_ref, zd_ref, x_ref, w_ref, g_ref, b_ref, y_ref):
    acc = ALPHA_DN * x_ref[...]
    for i, (o_r, z_r) in enumerate(((oa_ref, za_ref), (ob_ref, zb_ref), (oc_ref, zc_ref), (od_ref, zd_ref))):
        gated = (o_r[...] * _silu(z_r[...])).astype(BF16)
        acc = acc + _mm(gated, w_ref[i * W_BR:(i + 1) * W_BR, :])
    mu = jnp.mean(acc, axis=-1, keepdims=True)
    xc = acc - mu
    var = jnp.mean(xc * xc, axis=-1, keepdims=True)
    y_ref[...] = xc * lax.rsqrt(var + LN_EPS) * g_ref[...] + b_ref[...]


def _out_proj(o_a, o_b, o_c, o_d, main2, x2d, w_out, ln_g, ln_b, tm):
    m = x2d.shape[0]
    ospec = pl.BlockSpec((tm, W_BR), lambda i: (i, 0))
    zspec = lambda col: pl.BlockSpec((tm, W_BR), lambda i: (i, col // W_BR))
    return pl.pallas_call(
        _out_kernel,
        grid=(m // tm,),
        in_specs=[ospec, ospec, ospec, ospec, zspec(C_Z_A), zspec(C_Z_B), zspec(C_Z_C), zspec(C_Z_D),
                  pl.BlockSpec((tm, D_MODEL), lambda i: (i, 0)),
                  pl.BlockSpec((D_MODEL, D_MODEL), lambda i: (0, 0)),
                  pl.BlockSpec((1, D_MODEL), lambda i: (0, 0)), pl.BlockSpec((1, D_MODEL), lambda i: (0, 0))],
        out_specs=pl.BlockSpec((tm, D_MODEL), lambda i: (i, 0)),
        out_shape=jax.ShapeDtypeStruct((m, D_MODEL), F32),
        compiler_params=_cparams(("arbitrary",)),
        name="out_proj",
    )(o_a, o_b, o_c, o_d, main2, main2, main2, main2, x2d, w_out, ln_g, ln_b)


def _rows8(row, nrows=1):
    return jnp.where(_iota2((8, row.shape[1]), 0) < nrows, jnp.broadcast_to(row, (8, row.shape[1])), 0.0)


def _dec_kernel(main_ref, small_ref, sa_ref, conv_ref, sb_ref, shift_ref, shifts_ref, dbuf_ref,
                convw_ref, gpar_ref, normg_ref, mu_ref, mus_ref, wup_ref, aup_ref, par_ref, poolw_ref, pscale_ref,
                oa_ref, ob_ref, od_ref, sa_out, conv_out, sb_out, dbuf_out, ext_s, *, pos):
    row = main_ref[0]
    sm = small_ref[0]
    bd = _head_block_diag()
    heads = range(N_HEADS)
    lanes = [slice(h * HEAD_DIM, (h + 1) * HEAD_DIM) for h in heads]
    hsum = lambda x: _dot3_l(_rows8(x), bd)[0:1, :]

    u = row[:, C_QKV_A:C_QKV_A + 3 * W_BR]
    buf = conv_ref[0]
    cw = convw_ref[...]
    c = _silu(buf[0:1, :] * cw[0:1, :] + buf[1:2, :] * cw[1:2, :] + buf[2:3, :] * cw[2:3, :] + u * cw[3:4, :])
    conv_out[0, 0:2, :] = buf[1:3, :]
    conv_out[0, 2:3, :] = u
    q_a, k_a, v_a = c[:, 0:W_BR], c[:, W_BR:2 * W_BR], c[:, 2 * W_BR:3 * W_BR]
    gpar = gpar_ref[...]
    g = gpar[0:1, :] * _softplus(sm + gpar[1:2, :])
    beta = _sigmoid(sm)
    eg = [jnp.exp(g[:, S_A + h:S_A + h + 1]) for h in heads]

    p = row[:, C_RKV_B:C_RKV_B + 3 * W_BR]
    ps = p + (shift_ref[0] - p) * mu_ref[...]
    pss = sm + (shifts_ref[0] - sm) * mus_ref[...]
    r, k_b, v_b = ps[:, 0:W_BR], ps[:, W_BR:2 * W_BR], ps[:, 2 * W_BR:3 * W_BR]
    par = par_ref[...]
    w0, a0, xi, alpha, rho, gn_g, gn_b = (par[i:i + 1, :] for i in range(7))
    kx = k_b * xi
    d = w0 + _mm(_rows8(jnp.tanh(pss)), wup_ref[...], HI)[0:1, :]
    a = _sigmoid(a0 + _mm(_rows8(pss), aup_ref[...], HI)[0:1, :])
    q_a = q_a * lax.rsqrt(hsum(q_a * q_a) + L2_EPS) * (HEAD_DIM ** -0.5)
    k_a = k_a * lax.rsqrt(hsum(k_a * k_a) + L2_EPS)
    kk = kx * lax.rsqrt(hsum(kx * kx) + L2_EPS)
    decay = jnp.exp(-RWKV_DECAY_SCALE * _sigmoid(d))
    k2 = k_b * (1.0 + (a - 1.0) * alpha)
    bonus = hsum(r * k2 * rho) * v_b

    sts_a = [sa_ref[0, h] for h in heads]
    sts_b = [sb_ref[0, h] for h in heads]
    k8 = [_rows8(k_a[:, hs]) for hs in lanes]
    ks = [_mm(k8[h], sts_a[h], HI)[0:1, :] for h in heads]
    s_kk = [_mm_nt(_rows8(-kk[:, lanes[h]]), sts_b[h], HI)[0:1, :] for h in heads]

    sub = _iota2((8, HEAD_DIM), 0)
    two_rows = lambda r0, r1: jnp.where(sub == 0, jnp.broadcast_to(r0, (8, HEAD_DIM)),
                                        jnp.where(sub == 1, jnp.broadcast_to(r1, (8, HEAD_DIM)), 0.0))
    v_new = [beta[:, S_B + h:S_B + h + 1] * (v_a[:, lanes[h]] - eg[h] * ks[h]) for h in heads]
    new_a = [sts_a[h] * eg[h] + _mm_tn(k8[h], _rows8(v_new[h]), HI) for h in heads]
    new_b = [sts_b[h] * decay[:, lanes[h]]
             + _mm_tn(two_rows(s_kk[h], v_b[:, lanes[h]]), two_rows(kk[:, lanes[h]] * a[:, lanes[h]], k2[:, lanes[h]]),
                      HI) for h in heads]

    o = jnp.concatenate([_mm(_rows8(q_a[:, lanes[h]]), new_a[h], HI)[0:1, :] for h in heads], axis=1)
    y = jnp.concatenate([_mm_nt(_rows8(r[:, lanes[h]]), new_b[h], HI)[0:1, :] for h in heads], axis=1)
    for h in heads:
        sa_out[0, h] = new_a[h]
        sb_out[0, h] = new_b[h]
    ms = hsum(o * o) * (1.0 / HEAD_DIM)
    mean = hsum(y) * (1.0 / HEAD_DIM)
    oa_ref[0] = o * lax.rsqrt(ms + RMS_EPS) * normg_ref[...]
    yc = y - mean
    var = hsum(yc * yc) * (1.0 / HEAD_DIM)
    ob_ref[0] = yc * lax.rsqrt(var + GN_EPS) * gn_g + gn_b + bonus

    ud = row[:, C_U_D:C_U_D + W_BR]
    ext_s[0:POOL_BUF, :] = dbuf_ref[0]
    ext_s[POOL_BUF:POOL_BUF + 1, :] = ud
    dbuf_out[0] = ext_s[1:POOL_BUF + 1, :]
    outs = []
    for gi, wdw in enumerate(POOL_WINDOWS):
        ls = slice(gi * POOL_GW, (gi + 1) * POOL_GW)
        s = jnp.sum(ext_s[POOL_BUF + 1 - wdw:POOL_BUF + 1, ls], axis=0, keepdims=True)
        pooled = s / float(min(wdw, pos + 1)) - ud[:, ls]
        outs.append(_mm(_rows8(pooled).astype(BF16), poolw_ref[gi].astype(BF16))[0:1, :])
    od_ref[0] = jnp.concatenate(outs, axis=1) * pscale_ref[...]


def _decode_mixers(main_s, small_s, st_a, conv_a, st_b, shift_rkv, shift_small, dbuf,
                   conv_w, gpar, normg, mu, mus, wup, aup, par, pool_w, pscale, pos):
    nd = main_s.shape[0]
    per_seq = lambda shape: pl.BlockSpec((1,) + shape, lambda b: (b,) + (0,) * len(shape))
    full = lambda shape: pl.BlockSpec(shape, lambda b: (0,) * len(shape))
    hh = (N_HEADS, HEAD_DIM, HEAD_DIM)
    return pl.pallas_call(
        functools.partial(_dec_kernel, pos=pos),
        grid=(nd,),
        in_specs=[per_seq((1, N_MAIN)), per_seq((1, N_SMALL)), per_seq(hh), per_seq((CONV_W - 1, 3 * W_BR)),
                  per_seq(hh), per_seq((1, 3 * W_BR)), per_seq((1, N_SMALL)), per_seq((POOL_BUF, W_BR)),
                  full((CONV_W, 3 * W_BR)), full((8, N_SMALL)), full((1, W_BR)),
                  full((1, 3 * W_BR)), full((1, N_SMALL)), full((N_SMALL, W_BR)), full((N_SMALL, W_BR)),
                  full((8, W_BR)), full((len(POOL_WINDOWS), POOL_GW, POOL_GW)), full((1, W_BR))],
        out_specs=[per_seq((1, W_BR)), per_seq((1, W_BR)), per_seq((1, W_BR)), per_seq(hh),
                   per_seq((CONV_W - 1, 3 * W_BR)), per_seq(hh), per_seq((POOL_BUF, W_BR))],
        out_shape=[jax.ShapeDtypeStruct((nd, 1, W_BR), F32)] * 3
                  + [jax.ShapeDtypeStruct((nd,) + hh, F32), jax.ShapeDtypeStruct((nd, CONV_W - 1, 3 * W_BR), F32),
                     jax.ShapeDtypeStruct((nd,) + hh, F32), jax.ShapeDtypeStruct((nd, POOL_BUF, W_BR), F32)],
        scratch_shapes=[pltpu.VMEM((16, W_BR), F32)],
        compiler_params=_cparams(("arbitrary",)),
        name="decode_mixers",
    )(main_s.reshape(nd, 1, N_MAIN), small_s.reshape(nd, 1, N_SMALL), st_a, conv_a, st_b,
      shift_rkv.reshape(nd, 1, 3 * W_BR), shift_small.reshape(nd, 1, N_SMALL), dbuf,
      conv_w, gpar, normg, mu, mus, wup, aup, par, pool_w, pscale)


def _dfox_kernel(pt_ref, qkv_ref, qrep_ref, small_ref, bias_ref, *rest, n_grp):
    kt_refs, vt_refs, lf_refs = rest[0:n_grp], rest[n_grp:2 * n_grp], rest[2 * n_grp:3 * n_grp]
    o_ref, lf_out, m_s, l_s, w_s, r_s, acc_s = rest[3 * n_grp:]
    j = pl.program_id(1)
    scale = HEAD_DIM ** -0.5
    sub8 = _iota2((N_HEADS, LANES), 0)
    lane8 = _iota2((N_HEADS, LANES), 1)

    @pl.when(j == 0)
    def _():
        lf_new = _log_sigmoid(small_ref[0] + bias_ref[...])
        lf_out[0] = lf_new
        qb = (qkv_ref[0, 0] * scale).astype(BF16).astype(F32)
        kb = qkv_ref[0, 1].astype(BF16).astype(F32)
        m_s[...] = jnp.broadcast_to(jnp.sum(qb * kb, axis=1, keepdims=True), (N_HEADS, LANES))
        l_s[...] = jnp.ones_like(l_s)
        w_s[...] = jnp.ones_like(w_s)
        acc_s[...] = jnp.zeros_like(acc_s)
        mine = jnp.where(lane8 == sub8 + S_F, jnp.broadcast_to(lf_new, (N_HEADS, LANES)), 0.0)
        r_s[...] = jnp.broadcast_to(jnp.sum(mine, axis=1, keepdims=True), (N_HEADS, LANES))

    grp = range(n_grp)
    r_t = _iota2((PAGE_SIZE, 2 * LANES), 0)
    c_t = _iota2((PAGE_SIZE, 2 * LANES), 1)
    later_or_all = jnp.where((r_t > c_t) | (c_t >= LANES), 1.0, 0.0).astype(BF16)
    s_t = [jnp.zeros((N_HEADS, LANES), F32) for _ in grp]
    for h in range(N_HEADS):
        q_h = qrep_ref[0, h]
        for g in grp:
            row = jnp.sum(kt_refs[g][0, 0, h] * q_h, axis=0, keepdims=True) * scale
            s_t[g] = jnp.where(sub8 == h, jnp.broadcast_to(row, (N_HEADS, LANES)), s_t[g])
    gates = [_dot3_l(lf_refs[g][0, 0], later_or_all) for g in grp]
    logits, r_run = [], r_s[...]
    for g in grp:
        logits.append(s_t[g] + gates[g][:, 0:LANES] + r_run)
        r_run = r_run + gates[g][:, LANES:2 * LANES]
    m_old = m_s[...]
    m_grp = logits[0]
    for g in grp[1:]:
        m_grp = jnp.maximum(m_grp, logits[g])
    m_new = jnp.maximum(m_old, jnp.max(m_grp, axis=1, keepdims=True))
    alpha = jnp.exp(m_old - m_new)
    p = [jnp.exp(logits[g] - m_new) for g in grp]
    p_sum = p[0]
    for g in grp[1:]:
        p_sum = p_sum + p[g]
    for h in range(N_HEADS):
        acc = acc_s[h] * jnp.broadcast_to(alpha[h:h + 1, :], (HEAD_DIM, LANES))
        for g in grp:
            acc = acc + vt_refs[g][0, 0, h] * jnp.broadcast_to(p[g][h:h + 1, :], (HEAD_DIM, LANES))
        acc_s[h] = acc
    l_s[...] = alpha * l_s[...] + jnp.sum(p_sum, axis=1, keepdims=True)
    w_s[...] = alpha * w_s[...]
    m_s[...] = m_new
    r_s[...] = r_run

    @pl.when(j == pl.num_programs(1) - 1)
    def _():
        ones_b = jnp.ones((N_HEADS, LANES), BF16)
        sub = _iota2((N_HEADS, HEAD_DIM), 0)
        red = jnp.zeros((N_HEADS, HEAD_DIM), F32)
        for h in range(N_HEADS):
            hi, mid, lo = _split3(acc_s[h])
            tot = _mm_nt(ones_b, hi) + _mm_nt(ones_b, mid) + _mm_nt(ones_b, lo)
            red = red + jnp.where(sub == h, tot, 0.0)
        o_ref[0] = (red + w_s[...][:, 0:HEAD_DIM] * qkv_ref[0, 2]) / l_s[...][:, 0:HEAD_DIM]


def _fox_decode(page_table, qkv_s, small_s, bias_row, cache_k, cache_v, cache_lf, layer, n_grp):
    nd, n_pages = page_table.shape
    kt = jnp.transpose(cache_k, (0, 1, 3, 4, 2))
    vt = jnp.transpose(cache_v, (0, 1, 3, 4, 2))
    lft = jnp.transpose(cache_lf, (0, 1, 3, 2))
    qkv8 = qkv_s.reshape(nd, 3, N_HEADS, HEAD_DIM)
    q_rep = jnp.broadcast_to(qkv8[:, 0, :, :, None], (nd, N_HEADS, HEAD_DIM, LANES))
    page = lambda g: (lambda b, j, pt: (layer, pt[b, n_pages - 1 - (j * n_grp + g)], 0, 0, 0))
    page4 = lambda g: (lambda b, j, pt: (layer, pt[b, n_pages - 1 - (j * n_grp + g)], 0, 0))
    kv_spec = lambda g: pl.BlockSpec((1, 1, N_HEADS, HEAD_DIM, PAGE_SIZE), page(g))
    grid_spec = pltpu.PrefetchScalarGridSpec(
        num_scalar_prefetch=1,
        grid=(nd, n_pages // n_grp),
        in_specs=[pl.BlockSpec((1, 3, N_HEADS, HEAD_DIM), lambda b, j, pt: (b, 0, 0, 0)),
                  pl.BlockSpec((1, N_HEADS, HEAD_DIM, LANES), lambda b, j, pt: (b, 0, 0, 0)),
                  pl.BlockSpec((1, 1, N_SMALL), lambda b, j, pt: (b, 0, 0)),
                  pl.BlockSpec((1, N_SMALL), lambda b, j, pt: (0, 0))]
                 + [kv_spec(g) for g in range(n_grp)] + [kv_spec(g) for g in range(n_grp)]
                 + [pl.BlockSpec((1, 1, N_HEADS, PAGE_SIZE), page4(g)) for g in range(n_grp)],
        out_specs=[pl.BlockSpec((1, N_HEADS, HEAD_DIM), lambda b, j, pt: (b, 0, 0)),
                   pl.BlockSpec((1, 1, N_SMALL), lambda b, j, pt: (b, 0, 0))],
        scratch_shapes=[pltpu.VMEM((N_HEADS, LANES), F32)] * 4 + [pltpu.VMEM((N_HEADS, HEAD_DIM, LANES), F32)],
    )
    o, lf_new = pl.pallas_call(
        functools.partial(_dfox_kernel, n_grp=n_grp),
        grid_spec=grid_spec,
        out_shape=[jax.ShapeDtypeStruct((nd, N_HEADS, HEAD_DIM), F32), jax.ShapeDtypeStruct((nd, 1, N_SMALL), F32)],
        compiler_params=_cparams(("arbitrary", "arbitrary")),
        name="fox_decode",
    )(page_table, qkv8, q_rep, small_s.reshape(nd, 1, N_SMALL), bias_row,
      *([kt] * n_grp), *([vt] * n_grp), *([lft] * n_grp))
    return o.reshape(nd, W_BR), lf_new


_MAIN_SEGMENTS = ((O_QKV_A, 3 * W_BR), (O_P_B, 3 * W_BR), (O_QKV_C, 3 * W_BR), (O_Z_A, W_BR), (O_Z_B, W_BR),
                  (O_Z_C, W_BR), (O_U_D, W_BR), (O_Z_D, W_BR))
_SMALL_SEGMENTS = ((O_A_A, 2 * N_HEADS), (O_WL, 2 * LORA_B), (O_F_C, N_HEADS))


def _wprep_kernel(wt_ref, main_ref, small_ref):
    wt = wt_ref[0]
    main_ref[...] = jnp.concatenate([wt[a:a + n, :] for a, n in _MAIN_SEGMENTS], axis=0).T.astype(BF16)
    used = sum(n for _, n in _SMALL_SEGMENTS)
    small = [wt[a:a + n, :] for a, n in _SMALL_SEGMENTS] + [jnp.zeros((N_SMALL - used, wt.shape[1]), F32)]
    small_ref[...] = jnp.concatenate(small, axis=0).T.astype(BF16)


def _prep_w_in(w_in, layer, tr=256):
    _, d, d_in = w_in.shape
    return pl.pallas_call(
        _wprep_kernel,
        grid=(d // tr,),
        in_specs=[pl.BlockSpec((1, d_in, tr), lambda i: (layer, 0, i))],
        out_specs=[pl.BlockSpec((tr, N_MAIN), lambda i: (i, 0)), pl.BlockSpec((tr, N_SMALL), lambda i: (i, 0))],
        out_shape=[jax.ShapeDtypeStruct((d, N_MAIN), BF16), jax.ShapeDtypeStruct((d, N_SMALL), BF16)],
        compiler_params=_cparams(("arbitrary",)),
        name="w_in_prep",
    )(jnp.swapaxes(w_in, 1, 2))


def _small_row(pairs):
    row = jnp.zeros((N_SMALL,), F32)
    for off, vec in pairs:
        row = row.at[off:off + vec.shape[0]].set(vec.astype(F32))
    return row[None, :]


def _layer_params(l, w_in, conv_A, A_log, dt_bias, norm_A, mu_B, w0_B, w_up_B, a0_B, a_up_B, xi_B, alpha_B, rho_B,
                  gn_g_B, gn_b_B, b_f_C, pool_w_D, pool_scale_D, w_out, ln_g, ln_b):
    w_main, w_small = _prep_w_in(w_in, l)
    gpar = jnp.concatenate([_small_row([(S_A, -jnp.exp(A_log[l].astype(F32)))]), _small_row([(S_A, dt_bias[l])]),
                            jnp.zeros((6, N_SMALL), F32)], axis=0)
    head_of_lane = jnp.arange(W_BR) // HEAD_DIM
    lane = jnp.arange(N_SMALL)[:, None]
    eg = (lane == head_of_lane[None, :] + S_A).astype(BF16)
    eb = (lane == head_of_lane[None, :] + S_B).astype(BF16)
    mu = mu_B[l].astype(F32)
    wup = jnp.zeros((N_SMALL, W_BR), F32).at[S_WL:S_WL + LORA_B].set(w_up_B[l].astype(F32))
    aup = jnp.zeros((N_SMALL, W_BR), F32).at[S_AL:S_AL + LORA_B].set(a_up_B[l].astype(F32))
    bd = (head_of_lane[:, None] == head_of_lane[None, :]).astype(BF16)
    par = jnp.stack([w0_B[l], a0_B[l], xi_B[l], alpha_B[l], rho_B[l], gn_g_B[l], gn_b_B[l],
                     jnp.zeros((W_BR,), F32)]).astype(F32)
    return dict(
        w_main=w_main, w_small=w_small, bd=bd, conv_w=conv_A[l].astype(F32), gpar=gpar, eg=eg, eb=eb,
        normg=jnp.tile(norm_A[l].astype(F32), N_HEADS)[None, :],
        mu=mu[None, 0:3 * W_BR], mus=_small_row([(S_WL, mu[3 * W_BR:])]), wup=wup, aup=aup, par=par,
        fbias=_small_row([(S_F, b_f_C[l])]), pool_w=pool_w_D[l].astype(F32), pscale=pool_scale_D[l].astype(F32)[None, :],
        w_out=w_out[l].astype(BF16), ln_g=ln_g[l].astype(F32)[None, :], ln_b=ln_b[l].astype(F32)[None, :])


def _prompt_layer(x3, P, layer, k_all, v_all, tb_rec, tq, tb_gate, tb_pool, tm_proj, tm_out):
    n, t, _ = x3.shape
    x2 = x3.reshape(n * t, D_MODEL)
    main2, small2, k_all, v_all = _proj(x2, P['w_main'], P['w_small'], k_all, v_all, layer, tm_proj)
    main3 = main2.reshape(n, t, N_MAIN)
    small3 = small2.reshape(n, t, N_SMALL)
    o_a, s_a = _gdn_prompt(main3, small3, P['conv_w'], P['gpar'], P['eg'], P['eb'], P['normg'], P['bd'], tb_rec)
    o_b, s_b = _rwkv_prompt(main3, small3, P['mu'], P['mus'], _bf(P['wup']), _bf(P['aup']), P['par'], P['bd'],
                            tb_rec)
    logf3, c3, ct3 = _gates(small3, P['fbias'], tb_gate)
    o_c = _fox_prompt(main3, c3, ct3, tq)
    o_d = _pool_prompt(main3, P['pool_w'], P['pscale'], tb_pool)
    flat = lambda a: a.reshape(n * t, W_BR)
    y2 = _out_proj(flat(o_a), flat(o_b), flat(o_c), flat(o_d), main2, x2, P['w_out'], P['ln_g'], P['ln_b'], tm_out)
    new = (s_a,
           main3[:, t - (CONV_W - 1):, C_QKV_A:C_QKV_A + 3 * W_BR],
           s_b,
           jnp.concatenate([main3[:, t - 1, C_RKV_B:C_RKV_B + 3 * W_BR], small3[:, t - 1, S_WL:S_WL + 2 * LORA_B]], axis=-1),
           logf3[:, :, S_F:S_F + N_HEADS],
           main3[:, t - POOL_BUF:, C_U_D:C_U_D + W_BR])
    return y2.reshape(n, t, D_MODEL), new, k_all, v_all


def _sample_layer(x3, st, cache, layer, k_all, v_all, page_table, P):
    nd = x3.shape[0]
    st_a, conv_a, st_b, shift_b, dbuf = st
    cache_k, cache_v, cache_lf = cache
    x2 = x3.reshape(nd, D_MODEL)
    main_s, small_s, k_all, v_all = _proj(x2, P['w_main'], P['w_small'], k_all, v_all, layer, nd)
    shift_small = jnp.zeros((nd, N_SMALL), F32).at[:, S_WL:S_WL + 2 * LORA_B].set(shift_b[:, 3 * W_BR:].astype(F32))
    o_a, o_b, o_d, s_a, conv_n, s_b, dbuf_n = _decode_mixers(
        main_s, small_s, st_a, conv_a, st_b, shift_b[:, 0:3 * W_BR], shift_small, dbuf,
        P['conv_w'], P['gpar'], P['normg'], P['mu'], P['mus'], P['wup'], P['aup'], P['par'], P['pool_w'], P['pscale'],
        page_table.shape[1] * PAGE_SIZE)
    qkv_s = main_s[:, C_QKV_C:C_QKV_C + 3 * W_BR]
    n_grp = math.gcd(page_table.shape[1], 16)
    o_c, lf_new = _fox_decode(page_table, qkv_s, small_s, P['fbias'], cache_k, cache_v, cache_lf, layer, n_grp)
    flat = lambda a: a.reshape(nd, W_BR)
    y2 = _out_proj(flat(o_a), flat(o_b), flat(o_c), flat(o_d), main_s, x2, P['w_out'], P['ln_g'], P['ln_b'], nd)
    new = (s_a, conv_n, s_b,
           jnp.concatenate([main_s[:, C_RKV_B:C_RKV_B + 3 * W_BR], small_s[:, S_WL:S_WL + 2 * LORA_B]], axis=-1),
           lf_new[:, :, S_F:S_F + N_HEADS],
           dbuf_n)
    return y2.reshape(nd, 1, D_MODEL), new, k_all, v_all


def kernel(x_prompt, x_sample, state_A_S, state_A_conv, state_B_S, state_B_shift, cache_C_k, cache_C_v, cache_C_logf, state_D_buf, page_table, w_in, conv_A, A_log, dt_bias, norm_A, mu_B, w0_B, w_up_B, a0_B, a_up_B, xi_B, alpha_B, rho_B, gn_g_B, gn_b_B, b_f_C, pool_w_D, pool_scale_D, w_out, ln_g, ln_b):
    depth = w_in.shape[0]
    t = x_prompt.shape[1]
    tb_rec = min(256, t)
    tq = min(256, t)
    tb_gate = min(512, t)
    tb_pool = min(512, t)
    tm_proj = min(1024, t)
    tm_out = min(256, x_prompt.shape[0] * t)
    y_p, y_s = x_prompt, x_sample
    prompt_new, sample_new = [], []
    nb, nd = x_prompt.shape[0], x_sample.shape[0]
    pk, pv = jnp.zeros((depth, nb, W_BR, t), F32), jnp.zeros((depth, nb, W_BR, t), F32)
    sk, sv = jnp.zeros((depth, nd, W_BR), F32), jnp.zeros((depth, nd, W_BR), F32)
    for l in range(depth):
        P = _layer_params(l, w_in, conv_A, A_log, dt_bias, norm_A, mu_B, w0_B, w_up_B, a0_B, a_up_B, xi_B, alpha_B,
                          rho_B, gn_g_B, gn_b_B, b_f_C, pool_w_D, pool_scale_D, w_out, ln_g, ln_b)
        y_p, new_p, pk, pv = _prompt_layer(y_p, P, l, pk, pv, tb_rec, tq, tb_gate, tb_pool, tm_proj, tm_out)
        st = (state_A_S[l], state_A_conv[l], state_B_S[l], state_B_shift[l], state_D_buf[l])
        y_s, new_s, sk, sv = _sample_layer(y_s, st, (cache_C_k, cache_C_v, cache_C_logf), l, sk, sv, page_table, P)
        prompt_new.append(new_p)
        sample_new.append(new_s)
    p_a_s, p_a_conv, p_b_s, p_b_shift, p_c_logf, p_d_buf = [jnp.stack([n[i] for n in prompt_new]) for i in range(6)]
    s_a_s, s_a_conv, s_b_s, s_b_shift, s_c_logf, s_d_buf = [jnp.stack([n[i] for n in sample_new]) for i in range(6)]
    p_c_k, p_c_v = (jnp.transpose(a.reshape(depth, nb, N_HEADS, HEAD_DIM, t), (0, 1, 4, 2, 3)) for a in (pk, pv))
    s_c_k, s_c_v = (a.reshape(depth, nd, 1, N_HEADS, HEAD_DIM) for a in (sk, sv))
    return (y_p, y_s, p_a_s, p_a_conv, p_b_s, p_b_shift, p_c_k, p_c_v, p_c_logf, p_d_buf,
            s_a_s, s_a_conv, s_b_s, s_b_shift, s_c_k, s_c_v, s_c_logf, s_d_buf)
```

```python
import functools
import math

import jax
import jax.numpy as jnp
from jax import lax
from jax.experimental import pallas as pl
from jax.experimental.pallas import tpu as pltpu

F32 = jnp.float32
BF16 = jnp.bfloat16
HI = lax.Precision.HIGHEST

D_MODEL = 2048
W_BR = 512
HEAD_DIM = 64
N_HEADS = W_BR // HEAD_DIM
CONV_W = 4
CHUNK = 64
LORA_B = 32
POOL_WINDOWS = (2, 4, 8, 16)
POOL_GW = W_BR // len(POOL_WINDOWS)
POOL_BUF = max(POOL_WINDOWS) - 1
PAGE_SIZE = 128
LANES = 128
SUBLANES = 8
FOX_ROW_SPLIT = 2
DEPTH = 2
ALPHA_DN = (2.0 * DEPTH) ** 0.25
LN_EPS = 1e-5
GN_EPS = 64e-5
RMS_EPS = 1e-6
L2_EPS = 1e-6
RWKV_DECAY_SCALE = math.exp(-0.5)
NEG_INF = -1e30
LOG2_E = 1.4426950408889634

C_QKV_A, C_RKV_B, C_QKV_C = 0, 1536, 3072
C_Z_A, C_Z_B, C_Z_C, C_U_D, C_Z_D = 4608, 5120, 5632, 6144, 6656
N_MAIN = 7168
S_A, S_B, S_WL, S_AL, S_F = 0, 8, 16, 48, 80
N_SMALL = 128
O_QKV_A, O_A_A, O_Z_A, O_P_B, O_WL, O_Z_B, O_QKV_C, O_F_C, O_Z_C, O_U_D, O_Z_D, D_IN = (
    0, 1536, 1552, 2064, 3600, 3664, 4176, 5712, 5720, 6232, 6744, 7256)

VMEM_LIMIT = 48 * 1024 * 1024


VMEM_COMPILER_RESERVE = 8 * 1024 * 1024


def _cparams(sem, vmem_limit=VMEM_LIMIT):
    return pltpu.CompilerParams(dimension_semantics=sem, vmem_limit_bytes=vmem_limit)


def _mm(a, b, prec=None):
    return jnp.dot(a, b, preferred_element_type=F32, precision=prec)


def _mm_nt(a, b, prec=None):
    return lax.dot_general(a, b, (((1,), (1,)), ((), ())), preferred_element_type=F32, precision=prec)


def _mm_tn(a, b, prec=None):
    return lax.dot_general(a, b, (((0,), (0,)), ((), ())), preferred_element_type=F32, precision=prec)


def _split3(x):
    hi = x.astype(BF16)
    r1 = x - hi.astype(F32)
    mid = r1.astype(BF16)
    lo = (r1 - mid.astype(F32)).astype(BF16)
    return hi, mid, lo


def _dot3_l(x, b_bf16):
    hi, mid, lo = _split3(x)
    return _mm(hi, b_bf16) + _mm(mid, b_bf16) + _mm(lo, b_bf16)


def _head_sums(x, same_head_bf16):
    return _mm(x.astype(BF16), same_head_bf16)


def _dot3_r(a_bf16, x):
    hi, mid, lo = _split3(x)
    return _mm(a_bf16, hi) + _mm(a_bf16, mid) + _mm(a_bf16, lo)


def _sigmoid(x):
    return 1.0 / (1.0 + jnp.exp(-x))


def _silu(x):
    return x * _sigmoid(x)


def _softplus(x):
    return jnp.maximum(x, 0.0) + jnp.log1p(jnp.exp(-jnp.abs(x)))


def _log_sigmoid(x):
    return -_softplus(-x)


def _iota2(shape, dim):
    return lax.broadcasted_iota(jnp.int32, shape, dim)


def _head_block_diag():
    r = _iota2((W_BR, W_BR), 0) // HEAD_DIM
    c = _iota2((W_BR, W_BR), 1) // HEAD_DIM
    return jnp.where(r == c, 1.0, 0.0).astype(BF16)


def _chunk_tril(n):
    r = _iota2((n, n), 0)
    c = _iota2((n, n), 1)
    return jnp.where((r >= c) & (r // CHUNK == c // CHUNK), 1.0, 0.0).astype(BF16)


_DN_NN = (((1,), (0,)), ((), ()))
_DN_NT = (((1,), (1,)), ((), ()))
_DN_TN = (((0,), (0,)), ((), ()))


def _bf(x):
    return x.astype(BF16)


def _mmb(a, b, dn=_DN_NN):
    return lax.dot_general(a, b, dn, preferred_element_type=F32)


def _tri_solve(a_list, rhs_list, blk):
    each = lambda f, *ls: [f(*xs) for xs in zip(*ls)]
    width = rhs_list[0].shape[1]
    ad = each(lambda a: a * blk, a_list)
    adb = each(_bf, ad)
    a2b = each(_bf, each(_mmb, adb, adb))
    a4b = each(_bf, each(_mmb, a2b, a2b))
    a8b = each(_bf, each(_mmb, a4b, a4b))
    z = each(lambda r, a, d: jnp.concatenate([r, a - d], axis=1), rhs_list, a_list, ad)
    z = each(lambda x, y: x - y, z, each(_mmb, adb, each(_bf, z)))
    for pw in (a2b, a4b, a8b):
        z = each(lambda x, y: x + y, z, each(_mmb, pw, each(_bf, z)))
    y = each(lambda x: x[:, 0:width], z)
    nb = each(lambda x: _bf(x[:, width:width + CHUNK]), z)
    n2b = each(_bf, each(_mmb, nb, nb))
    t = each(lambda x, u: x + u, y, each(_mmb, n2b, each(_bf, y)))
    return each(lambda x, u: x - u, t, each(_mmb, nb, each(_bf, t)))


def _chunk_masks():
    r = _iota2((CHUNK, CHUNK), 0)
    c = _iota2((CHUNK, CHUNK), 1)
    eye = jnp.where(r == c, 1.0, 0.0).astype(F32)
    strict = jnp.where(r > c, 1.0, 0.0).astype(F32)
    incl = jnp.where(r >= c, 1.0, 0.0).astype(F32)
    blk = jnp.where(r // 16 == c // 16, 1.0, 0.0).astype(F32)
    return eye, strict, incl, blk


def _proj_kernel(x_ref, w_ref, ws_ref, kin_ref, vin_ref, main_ref, small_ref, k_ref, v_ref, xb_ref, *, tn,
                 token_minor):
    j = pl.program_id(1)

    @pl.when(j == 0)
    def _():
        xb = x_ref[...].astype(BF16)
        xb_ref[...] = xb
        small_ref[...] = _mm(xb, ws_ref[...])

    acc = _mm(xb_ref[...], w_ref[...])
    main_ref[...] = acc
    for col, out_ref in ((C_QKV_C + W_BR, k_ref), (C_QKV_C + 2 * W_BR, v_ref)):
        @pl.when(j == col // tn)
        def _(col=col, out_ref=out_ref):
            kv = acc[:, col % tn:col % tn + W_BR]
            if token_minor:
                out_ref[0, 0] = kv.T
            else:
                out_ref[0] = kv


def _proj(x2d, w_main, w_small, k_all, v_all, layer, tm, tn=1024):
    m = x2d.shape[0]
    assert (C_QKV_C + W_BR) % tn + W_BR <= tn and (C_QKV_C + 2 * W_BR) % tn + W_BR <= tn
    blocks = (2 * (tm * D_MODEL * 4 + D_MODEL * tn * 2 + D_MODEL * N_SMALL * 2 + tm * tn * 4 + tm * N_SMALL * 4
                   + 2 * tm * W_BR * 4) + tm * D_MODEL * 2)
    token_minor = k_all.ndim == 4
    if token_minor:
        per_seq = k_all.shape[3] // tm
        kv_spec = pl.BlockSpec((1, 1, W_BR, tm), lambda i, j: (layer, i // per_seq, 0, i % per_seq))
    else:
        kv_spec = pl.BlockSpec((1, tm, W_BR), lambda i, j: (layer, i, 0))
    return pl.pallas_call(
        functools.partial(_proj_kernel, tn=tn, token_minor=token_minor),
        grid=(m // tm, N_MAIN // tn),
        in_specs=[pl.BlockSpec((tm, D_MODEL), lambda i, j: (i, 0)),
                  pl.BlockSpec((D_MODEL, tn), lambda i, j: (0, j)),
                  pl.BlockSpec((D_MODEL, N_SMALL), lambda i, j: (0, 0)),
                  pl.BlockSpec(memory_space=pl.ANY), pl.BlockSpec(memory_space=pl.ANY)],
        out_specs=[pl.BlockSpec((tm, tn), lambda i, j: (i, j)),
                   pl.BlockSpec((tm, N_SMALL), lambda i, j: (i, 0)), kv_spec, kv_spec],
        out_shape=[jax.ShapeDtypeStruct((m, N_MAIN), F32), jax.ShapeDtypeStruct((m, N_SMALL), F32),
                   jax.ShapeDtypeStruct(k_all.shape, F32), jax.ShapeDtypeStruct(v_all.shape, F32)],
        input_output_aliases={3: 2, 4: 3},
        scratch_shapes=[pltpu.VMEM((tm, D_MODEL), BF16)],
        compiler_params=_cparams(("arbitrary", "arbitrary"), max(VMEM_LIMIT, blocks + VMEM_COMPILER_RESERVE)),
        name="proj",
    )(x2d, w_main, w_small, k_all, v_all)


def _gates_kernel(small_ref, bias_ref, logf_ref, c_ref, ct_ref, carry_ref, *, tb):
    @pl.when(pl.program_id(1) == 0)
    def _():
        carry_ref[...] = jnp.zeros_like(carry_ref)

    logf = _log_sigmoid(small_ref[0] + bias_ref[...])
    r = _iota2((tb, tb), 0)
    c = _iota2((tb, tb), 1)
    tril = jnp.where(r >= c, 1.0, 0.0).astype(BF16)
    cum = _dot3_r(tril, logf) + carry_ref[0:1, :]
    carry_ref[...] = jnp.broadcast_to(cum[tb - 1:tb, :], carry_ref.shape)
    logf_ref[0] = logf
    c_ref[0] = cum
    ct_ref[0] = cum.T[S_F:S_F + N_HEADS, :]


def _gates(small3, bias_row, tb):
    n, t, _ = small3.shape
    return pl.pallas_call(
        functools.partial(_gates_kernel, tb=tb),
        grid=(n, t // tb),
        in_specs=[pl.BlockSpec((1, tb, N_SMALL), lambda b, j: (b, j, 0)),
                  pl.BlockSpec((1, N_SMALL), lambda b, j: (0, 0))],
        out_specs=[pl.BlockSpec((1, tb, N_SMALL), lambda b, j: (b, j, 0)),
                   pl.BlockSpec((1, tb, N_SMALL), lambda b, j: (b, j, 0)),
                   pl.BlockSpec((1, N_HEADS, tb), lambda b, j: (b, 0, j))],
        out_shape=[jax.ShapeDtypeStruct((n, t, N_SMALL), F32), jax.ShapeDtypeStruct((n, t, N_SMALL), F32),
                   jax.ShapeDtypeStruct((n, N_HEADS, t), F32)],
        scratch_shapes=[pltpu.VMEM((8, N_SMALL), F32)],
        compiler_params=_cparams(("arbitrary", "arbitrary")),
        name="fox_gates",
    )(small3, bias_row)


def _fox_kernel(qi_ref, kj_ref, q_ref, k_ref, v_ref, c_ref, ct_ref, o_ref, m_s, l_s, acc_s, cq_s, *, tq, n_sub):
    i = qi_ref[pl.program_id(1)]
    j = kj_ref[pl.program_id(1)]
    n_pairs = W_BR // LANES
    lo_half = _iota2((tq, LANES), 1) < HEAD_DIM
    pair_lanes = [slice(pr * LANES, (pr + 1) * LANES) for pr in range(n_pairs)]

    @pl.when(j == 0)
    def _():
        m_s[...] = jnp.full(m_s.shape, NEG_INF, F32)
        l_s[...] = jnp.zeros_like(l_s)
        acc_s[...] = jnp.zeros_like(acc_s)
        cq = c_ref[0]
        for h in range(N_HEADS):
            cq_s[h] = jnp.broadcast_to(cq[:, S_F + h:S_F + h + 1] * LOG2_E, (tq, LANES))

    def step(masked):
        k = k_ref[0].astype(BF16)
        v = v_ref[0]
        ct = ct_ref[0] * LOG2_E
        ones_bd = jnp.concatenate([jnp.where(lo_half, 1.0, 0.0), jnp.where(lo_half, 0.0, 1.0)], axis=0)
        v_bd = [jnp.concatenate(
            [jnp.concatenate([jnp.where(lo_half, v[:, ps], 0.0), jnp.where(lo_half, 0.0, v[:, ps])], axis=0), ones_bd],
            axis=1).astype(BF16) for ps in pair_lanes]
        rq = tq // n_sub
        lo_q = _iota2((rq, LANES), 1) < HEAD_DIM
        wide = lambda x: jnp.concatenate([x] * (tq // LANES), axis=1)
        for r in range(n_sub):
            rs = slice(r * rq, (r + 1) * rq)
            q = q_ref[0, rs, :] * (HEAD_DIM ** -0.5 * LOG2_E)
            if masked:
                keep = _iota2((rq, tq), 0) + r * rq >= _iota2((rq, tq), 1)
            s_all, m_all = [], []
            for pr, ps in enumerate(pair_lanes):
                qp, kp = q[:, ps], k[:, ps]
                for e, q_half in enumerate((jnp.where(lo_q, qp, 0.0), jnp.where(lo_q, 0.0, qp))):
                    h = 2 * pr + e
                    s = _mm_nt(q_half.astype(BF16), kp) + wide(cq_s[h, rs, :]) - ct[h:h + 1, :]
                    if masked:
                        s = jnp.where(keep, s, NEG_INF)
                    s_all.append(s)
                    m_all.append(jnp.maximum(m_s[h, rs, :], jnp.max(s, axis=1, keepdims=True)))
            p_all, alpha_all = [], []
            for h in range(N_HEADS):
                m_new = m_all[h]
                alpha = jnp.exp2(m_s[h, rs, :] - m_new)
                p = jnp.exp2(s_all[h] - wide(m_new))
                m_s[h, rs, :] = m_new
                p_all.append(p.astype(BF16))
                alpha_all.append(alpha)
            for pr, ps in enumerate(pair_lanes):
                p_pair = jnp.concatenate([p_all[2 * pr], p_all[2 * pr + 1]], axis=1)
                alpha_p = jnp.where(lo_q, alpha_all[2 * pr], alpha_all[2 * pr + 1])
                pv_l = _mm(p_pair, v_bd[pr])
                acc_s[rs, ps] = alpha_p * acc_s[rs, ps] + pv_l[:, 0:LANES]
                l_s[pr, rs, :] = alpha_p * l_s[pr, rs, :] + pv_l[:, LANES:2 * LANES]

    @pl.when(j < i)
    def _():
        step(False)

    @pl.when(j == i)
    def _():
        step(True)
        for pr, ps in enumerate(pair_lanes):
            o_ref[0, :, ps] = acc_s[:, ps] / l_s[pr]


def _fox_prompt(main3, c3, ct3, tq):
    n, t, _ = main3.shape
    nb = t // tq
    qb, kb, vb = C_QKV_C // W_BR, C_QKV_C // W_BR + 1, C_QKV_C // W_BR + 2
    pairs = [(i, j) for i in range(nb) for j in range(i + 1)]
    qi = jnp.asarray([p[0] for p in pairs], jnp.int32)
    kj = jnp.asarray([p[1] for p in pairs], jnp.int32)
    grid_spec = pltpu.PrefetchScalarGridSpec(
        num_scalar_prefetch=2,
        grid=(n, len(pairs)),
        in_specs=[pl.BlockSpec((1, tq, W_BR), lambda b, s, qi, kj: (b, qi[s], qb)),
                  pl.BlockSpec((1, tq, W_BR), lambda b, s, qi, kj: (b, kj[s], kb)),
                  pl.BlockSpec((1, tq, W_BR), lambda b, s, qi, kj: (b, kj[s], vb)),
                  pl.BlockSpec((1, tq, N_SMALL), lambda b, s, qi, kj: (b, qi[s], 0)),
                  pl.BlockSpec((1, N_HEADS, tq), lambda b, s, qi, kj: (b, 0, kj[s]))],
        out_specs=pl.BlockSpec((1, tq, W_BR), lambda b, s, qi, kj: (b, qi[s], 0)),
        scratch_shapes=[pltpu.VMEM((N_HEADS, tq, LANES), F32), pltpu.VMEM((W_BR // LANES, tq, LANES), F32),
                        pltpu.VMEM((tq, W_BR), F32), pltpu.VMEM((N_HEADS, tq, LANES), F32)],
    )
    return pl.pallas_call(
        functools.partial(_fox_kernel, tq=tq, n_sub=FOX_ROW_SPLIT),
        grid_spec=grid_spec,
        out_shape=jax.ShapeDtypeStruct((n, t, W_BR), F32),
        compiler_params=_cparams(("arbitrary", "arbitrary")),
        name="fox_prompt",
    )(qi, kj, main3, main3, main3, c3, ct3)


def _gdn_kernel(qkv_ref, small_ref, convw_ref, gpar_ref, eg_ref, eb_ref, normg_ref, bd_ref,
                o_ref, s_out_ref, ext_s, q_s, k_s, v_s, b_s, g_s, o_s, st_s, *, tb):
    j = pl.program_id(1)

    @pl.when(j == 0)
    def _():
        ext_s[0:8, :] = jnp.zeros((8, 3 * W_BR), F32)
        st_s[...] = jnp.zeros_like(st_s)

    u = qkv_ref[0]
    ext_s[8:8 + tb, :] = u
    cw = convw_ref[...]
    c = (ext_s[5:5 + tb, :] * cw[0:1, :] + ext_s[6:6 + tb, :] * cw[1:2, :]
         + ext_s[7:7 + tb, :] * cw[2:3, :] + u * cw[3:4, :])
    ext_s[0:8, :] = u[tb - 8:tb, :]
    c = _silu(c)
    bd = bd_ref[...]
    q = c[:, 0:W_BR]
    k = c[:, W_BR:2 * W_BR]
    q_s[...] = q * lax.rsqrt(_head_sums(q * q, bd) + L2_EPS) * (HEAD_DIM ** -0.5)
    k_s[...] = k * lax.rsqrt(_head_sums(k * k, bd) + L2_EPS)
    v_s[...] = c[:, 2 * W_BR:3 * W_BR]

    sm = small_ref[0]
    gpar = gpar_ref[...]
    g = gpar[0:1, :] * _softplus(sm + gpar[1:2, :])
    beta = _sigmoid(sm)
    gcum = _dot3_r(_chunk_tril(tb), g)
    g_s[...] = _dot3_l(gcum, eg_ref[...])
    b_s[...] = _dot3_l(beta, eb_ref[...])

    gcum_t = gcum.T[S_A:S_A + N_HEADS, :]

    _, strict, incl, blk = _chunk_masks()

    heads = range(N_HEADS)
    lanes = [slice(h * HEAD_DIM, (h + 1) * HEAD_DIM) for h in heads]
    n_chunks = tb // CHUNK
    group = 4

    ops = {}
    for c0 in range(0, n_chunks, group):
        chunks = range(c0, min(c0 + group, n_chunks))
        items = [(c, h) for c in chunks for h in heads]
        rows = lambda c: slice(c * CHUNK, (c + 1) * CHUNK)
        tile = lambda ref: [ref[rows(c), lanes[h]] for c, h in items]
        q_h, k_h, v_h, b_h, g_h = tile(q_s), tile(k_s), tile(v_s), tile(b_s), tile(g_s)
        grow = [jnp.broadcast_to(gcum_t[h:h + 1, c * CHUNK:(c + 1) * CHUNK], (CHUNK, CHUNK))
                for c, h in items]
        idx = range(len(items))
        decay = [jnp.exp(jnp.where(incl > 0, g_h[i] - grow[i], NEG_INF)) for i in idx]
        kb = [k_h[i] * b_h[i] for i in idx]
        both = [_mmb(_bf(jnp.concatenate([kb[i], q_h[i]], axis=0)), _bf(k_h[i]), _DN_NT) for i in idx]
        a = [both[i][0:CHUNK, :] * decay[i] * strict for i in idx]
        attn = [_bf(both[i][CHUNK:2 * CHUNK, :] * decay[i]) for i in idx]
        eg = [jnp.exp(g_h[i]) for i in idx]
        sol = _tri_solve(a, [jnp.concatenate([v_h[i] * b_h[i], kb[i] * eg[i]], axis=1) for i in idx], blk)
        ub = [_bf(sol[i][:, 0:HEAD_DIM]) for i in idx]
        wb = [_bf(sol[i][:, HEAD_DIM:2 * HEAD_DIM]) for i in idx]
        o1 = [_bf(q_h[i] * eg[i] - _mmb(attn[i], wb[i])) for i in idx]
        o2 = [_mmb(attn[i], ub[i]) for i in idx]
        glast = [g_h[i][CHUNK - 1:CHUNK, :] for i in idx]
        kdec = [_bf(k_h[i] * jnp.exp(glast[i] - g_h[i])) for i in idx]
        m = [_bf(_mmb(kdec[i], wb[i], _DN_TN)) for i in idx]
        cc = [_mmb(kdec[i], ub[i], _DN_TN) for i in idx]
        for i, key in enumerate(items):
            ops[key] = (o1[i], o2[i], m[i], cc[i], jnp.exp(glast[i]))

    sts = [st_s[h] for h in heads]
    for c in range(n_chunks):
        stb = [_bf(sts[h]) for h in heads]
        outs = [_mmb(ops[c, h][0], stb[h]) + ops[c, h][1] for h in heads]
        sts = [sts[h] * ops[c, h][4] - _mmb(ops[c, h][2], stb[h]) + ops[c, h][3] for h in heads]
        o_s[c * CHUNK:(c + 1) * CHUNK, :] = jnp.concatenate(outs, axis=1)
    for h in heads:
        st_s[h] = sts[h]

    o = o_s[...]
    ms = _head_sums(o * o, bd) * (1.0 / HEAD_DIM)
    o_ref[0] = o * lax.rsqrt(ms + RMS_EPS) * normg_ref[...]

    @pl.when(j == pl.num_programs(1) - 1)
    def _():
        s_out_ref[0] = st_s[...]


def _gdn_prompt(main3, small3, conv_w, gpar, eg, eb, normg, bd, tb):
    n, t, _ = main3.shape
    full = lambda shape: pl.BlockSpec(shape, lambda b, j: (0,) * len(shape))
    return pl.pallas_call(
        functools.partial(_gdn_kernel, tb=tb),
        grid=(n, t // tb),
        in_specs=[pl.BlockSpec((1, tb, 3 * W_BR), lambda b, j: (b, j, C_QKV_A // (3 * W_BR))),
                  pl.BlockSpec((1, tb, N_SMALL), lambda b, j: (b, j, 0)),
                  full((CONV_W, 3 * W_BR)), full((8, N_SMALL)), full((N_SMALL, W_BR)), full((N_SMALL, W_BR)),
                  full((1, W_BR)), full((W_BR, W_BR))],
        out_specs=[pl.BlockSpec((1, tb, W_BR), lambda b, j: (b, j, 0)),
                   pl.BlockSpec((1, N_HEADS, HEAD_DIM, HEAD_DIM), lambda b, j: (b, 0, 0, 0))],
        out_shape=[jax.ShapeDtypeStruct((n, t, W_BR), F32),
                   jax.ShapeDtypeStruct((n, N_HEADS, HEAD_DIM, HEAD_DIM), F32)],
        scratch_shapes=[pltpu.VMEM((tb + 8, 3 * W_BR), F32)] + [pltpu.VMEM((tb, W_BR), F32)] * 6
                       + [pltpu.VMEM((N_HEADS, HEAD_DIM, HEAD_DIM), F32)],
        compiler_params=_cparams(("arbitrary", "arbitrary")),
        name="gdn_prompt",
    )(main3, small3, conv_w, gpar, eg, eb, normg, bd)


def _rwkv_kernel(p_ref, small_ref, mu_ref, mus_ref, wup_ref, aup_ref, par_ref, bd_ref,
                 o_ref, s_out_ref, ext_s, exts_s, ah_s, bh_s, kh_s, rh_s, be_s, ke_s, v_s, wl_s, y_s, st_s, *, tb):
    j = pl.program_id(1)

    @pl.when(j == 0)
    def _():
        ext_s[0:8, :] = jnp.zeros((8, 3 * W_BR), F32)
        exts_s[0:8, :] = jnp.zeros((8, N_SMALL), F32)
        st_s[...] = jnp.zeros_like(st_s)

    p = p_ref[0]
    sm = small_ref[0]
    ext_s[8:8 + tb, :] = p
    exts_s[8:8 + tb, :] = sm
    prev = ext_s[7:7 + tb, :]
    prevs = exts_s[7:7 + tb, :]
    ext_s[0:8, :] = p[tb - 8:tb, :]
    exts_s[0:8, :] = sm[tb - 8:tb, :]
    ps = p + (prev - p) * mu_ref[...]
    pss = sm + (prevs - sm) * mus_ref[...]
    r = ps[:, 0:W_BR]
    k = ps[:, W_BR:2 * W_BR]
    v = ps[:, 2 * W_BR:3 * W_BR]
    par = par_ref[...]
    w0, a0, xi, alpha, rho, gn_g, gn_b = (par[i:i + 1, :] for i in range(7))
    d = w0 + _mm(_bf(jnp.tanh(pss)), wup_ref[...])
    logw = -RWKV_DECAY_SCALE * _sigmoid(d)
    a = _sigmoid(a0 + _mm(_bf(pss), aup_ref[...]))
    bd = bd_ref[...]
    kx = k * xi
    kk = kx * lax.rsqrt(_head_sums(kx * kx, bd) + L2_EPS)
    k2 = k * (1.0 + (a - 1.0) * alpha)
    lc = _dot3_r(_chunk_tril(tb), logw)
    nb = -(a * kk)
    ah_s[...] = _bf(kk * jnp.exp(lc - logw))
    bh_s[...] = _bf(nb * jnp.exp(-lc))
    kh_s[...] = _bf(k2 * jnp.exp(-lc))
    rh_s[...] = _bf(r * jnp.exp(lc))
    v_s[...] = _bf(v)
    ll = jnp.concatenate([jnp.broadcast_to(lc[c * CHUNK + CHUNK - 1:(c + 1) * CHUNK, :], (CHUNK, W_BR))
                          for c in range(tb // CHUNK)], axis=0)
    to_end = jnp.exp(ll - lc)
    be_s[...] = _bf(nb * to_end)
    ke_s[...] = _bf(k2 * to_end)
    wl_s[...] = jnp.exp(ll)

    _, strict, incl, blk = _chunk_masks()
    incl2 = jnp.concatenate([incl, incl], axis=1)

    heads = range(N_HEADS)
    lanes = [slice(h * HEAD_DIM, (h + 1) * HEAD_DIM) for h in heads]
    n_chunks = tb // CHUNK
    group = 4

    ops = {}
    for c0 in range(0, n_chunks, group):
        chunks = range(c0, min(c0 + group, n_chunks))
        items = [(c, h) for c in chunks for h in heads]
        rows = lambda c: slice(c * CHUNK, (c + 1) * CHUNK)
        tile = lambda ref: [ref[rows(c), lanes[h]] for c, h in items]
        a_h, b_h, k_h, r_h, v_h, be_h, ke_h = (tile(x) for x in (ah_s, bh_s, kh_s, rh_s, v_s, be_s, ke_s))
        idx = range(len(items))
        gram = [_mmb(jnp.concatenate([a_h[i], r_h[i]], axis=0), jnp.concatenate([b_h[i], k_h[i]], axis=0), _DN_NT)
                for i in idx]
        akv = [_mmb(_bf(gram[i][0:CHUNK, CHUNK:2 * CHUNK] * strict), v_h[i]) for i in idx]
        sol = _tri_solve([-(gram[i][0:CHUNK, 0:CHUNK] * strict) for i in idx],
                         [jnp.concatenate([a_h[i].astype(F32), akv[i]], axis=1) for i in idx], blk)
        w1b = [_bf(sol[i][:, 0:HEAD_DIM]) for i in idx]
        w2v = [jnp.concatenate([_bf(sol[i][:, HEAD_DIM:2 * HEAD_DIM]), v_h[i]], axis=0) for i in idx]
        r_bk = [_bf(gram[i][CHUNK:2 * CHUNK, :] * incl2) for i in idx]
        g1 = [_bf(r_h[i].astype(F32) + _mmb(r_bk[i][:, 0:CHUNK], w1b[i])) for i in idx]
        g2 = [_mmb(r_bk[i], w2v[i]) for i in idx]
        m1 = [_bf(_mmb(w1b[i], be_h[i], _DN_TN)) for i in idx]
        c2 = [_mmb(w2v[i], jnp.concatenate([be_h[i], ke_h[i]], axis=0), _DN_TN) for i in idx]
        for i, (c, h) in enumerate(items):
            ops[c, h] = (g1[i], g2[i], m1[i], c2[i], wl_s[c * CHUNK:c * CHUNK + 1, lanes[h]])

    sts = [st_s[h] for h in heads]
    for c in range(n_chunks):
        stb = [_bf(sts[h]) for h in heads]
        outs = [_mmb(ops[c, h][0], stb[h], _DN_NT) + ops[c, h][1] for h in heads]
        sts = [sts[h] * ops[c, h][4] + _mmb(stb[h], ops[c, h][2]) + ops[c, h][3] for h in heads]
        y_s[c * CHUNK:(c + 1) * CHUNK, :] = jnp.concatenate(outs, axis=1)
    for h in heads:
        st_s[h] = sts[h]

    y = y_s[...]
    mean = _head_sums(y, bd) * (1.0 / HEAD_DIM)
    yc = y - mean
    var = _head_sums(yc * yc, bd) * (1.0 / HEAD_DIM)
    yn = yc * lax.rsqrt(var + GN_EPS) * gn_g + gn_b
    bonus = _head_sums(r * k2 * rho, bd) * v
    o_ref[0] = yn + bonus

    @pl.when(j == pl.num_programs(1) - 1)
    def _():
        s_out_ref[0] = st_s[...]


def _rwkv_prompt(main3, small3, mu, mus, wup, aup, par, bd, tb):
    n, t, _ = main3.shape
    full = lambda shape: pl.BlockSpec(shape, lambda b, j: (0,) * len(shape))
    return pl.pallas_call(
        functools.partial(_rwkv_kernel, tb=tb),
        grid=(n, t // tb),
        in_specs=[pl.BlockSpec((1, tb, 3 * W_BR), lambda b, j: (b, j, C_RKV_B // (3 * W_BR))),
                  pl.BlockSpec((1, tb, N_SMALL), lambda b, j: (b, j, 0)),
                  full((1, 3 * W_BR)), full((1, N_SMALL)), full((N_SMALL, W_BR)), full((N_SMALL, W_BR)),
                  full((8, W_BR)), full((W_BR, W_BR))],
        out_specs=[pl.BlockSpec((1, tb, W_BR), lambda b, j: (b, j, 0)),
                   pl.BlockSpec((1, N_HEADS, HEAD_DIM, HEAD_DIM), lambda b, j: (b, 0, 0, 0))],
        out_shape=[jax.ShapeDtypeStruct((n, t, W_BR), F32),
                   jax.ShapeDtypeStruct((n, N_HEADS, HEAD_DIM, HEAD_DIM), F32)],
        scratch_shapes=[pltpu.VMEM((tb + 8, 3 * W_BR), F32), pltpu.VMEM((tb + 8, N_SMALL), F32)]
                       + [pltpu.VMEM((tb, W_BR), BF16)] * 7 + [pltpu.VMEM((tb, W_BR), F32)] * 2
                       + [pltpu.VMEM((N_HEADS, HEAD_DIM, HEAD_DIM), F32)],
        compiler_params=_cparams(("arbitrary", "arbitrary")),
        name="rwkv_prompt",
    )(main3, small3, mu, mus, wup, aup, par, bd)


def _pool_kernel(u_ref, w_ref, scale_ref, o_ref, ext_s, *, tb):
    j = pl.program_id(1)

    @pl.when(j == 0)
    def _():
        ext_s[0:16, :] = jnp.zeros((16, W_BR), F32)

    u = u_ref[0]
    ext_s[16:16 + tb, :] = u
    pos = j * tb + _iota2((tb, POOL_GW), 0)
    outs = []
    for gi, wdw in enumerate(POOL_WINDOWS):
        ls = slice(gi * POOL_GW, (gi + 1) * POOL_GW)
        s = u[:, ls]
        for sh in range(1, wdw):
            s = s + ext_s[16 - sh:16 - sh + tb, ls]
        cnt = jnp.minimum(wdw, pos + 1).astype(F32)
        pooled = s / cnt - u[:, ls]
        outs.append(_mm(pooled.astype(BF16), w_ref[gi].astype(BF16)))
    ext_s[0:16, :] = u[tb - 16:tb, :]
    o_ref[0] = jnp.concatenate(outs, axis=1) * scale_ref[...]


def _pool_prompt(main3, pool_w, scale, tb):
    n, t, _ = main3.shape
    return pl.pallas_call(
        functools.partial(_pool_kernel, tb=tb),
        grid=(n, t // tb),
        in_specs=[pl.BlockSpec((1, tb, W_BR), lambda b, j: (b, j, C_U_D // W_BR)),
                  pl.BlockSpec((len(POOL_WINDOWS), POOL_GW, POOL_GW), lambda b, j: (0, 0, 0)),
                  pl.BlockSpec((1, W_BR), lambda b, j: (0, 0))],
        out_specs=pl.BlockSpec((1, tb, W_BR), lambda b, j: (b, j, 0)),
        out_shape=jax.ShapeDtypeStruct((n, t, W_BR), F32),
        scratch_shapes=[pltpu.VMEM((tb + 16, W_BR), F32)],
        compiler_params=_cparams(("arbitrary", "arbitrary")),
        name="pool_prompt",
    )(main3, pool_w, scale)


def _out_kernel(oa_ref, ob_ref, oc_ref, od_ref, za_ref, zb_ref, zc_ref, zd_ref, x_ref, w_ref, g_ref, b_ref, y_ref):
    acc = ALPHA_DN * x_ref[...]
    for i, (o_r, z_r) in enumerate(((oa_ref, za_ref), (ob_ref, zb_ref), (oc_ref, zc_ref), (od_ref, zd_ref))):
        gated = (o_r[...] * _silu(z_r[...])).astype(BF16)
        acc = acc + _mm(gated, w_ref[i * W_BR:(i + 1) * W_BR, :])
    mu = jnp.mean(acc, axis=-1, keepdims=True)
    xc = acc - mu
    var = jnp.mean(xc * xc, axis=-1, keepdims=True)
    y_ref[...] = xc * lax.rsqrt(var + LN_EPS) * g_ref[...] + b_ref[...]


def _out_proj(o_a, o_b, o_c, o_d, main2, x2d, w_out, ln_g, ln_b, tm):
    m = x2d.shape[0]
    ospec = pl.BlockSpec((tm, W_BR), lambda i: (i, 0))
    zspec = lambda col: pl.BlockSpec((tm, W_BR), lambda i: (i, col // W_BR))
    return pl.pallas_call(
        _out_kernel,
        grid=(m // tm,),
        in_specs=[ospec, ospec, ospec, ospec, zspec(C_Z_A), zspec(C_Z_B), zspec(C_Z_C), zspec(C_Z_D),
                  pl.BlockSpec((tm, D_MODEL), lambda i: (i, 0)),
                  pl.BlockSpec((D_MODEL, D_MODEL), lambda i: (0, 0)),
                  pl.BlockSpec((1, D_MODEL), lambda i: (0, 0)), pl.BlockSpec((1, D_MODEL), lambda i: (0, 0))],
        out_specs=pl.BlockSpec((tm, D_MODEL), lambda i: (i, 0)),
        out_shape=jax.ShapeDtypeStruct((m, D_MODEL), F32),
        compiler_params=_cparams(("arbitrary",)),
        name="out_proj",
    )(o_a, o_b, o_c, o_d, main2, main2, main2, main2, x2d, w_out, ln_g, ln_b)


def _rows8(row):
    shape = (SUBLANES, row.shape[1])
    return jnp.where(_iota2(shape, 0) == 0, jnp.broadcast_to(row, shape), 0.0)


def _dec_kernel(main_ref, small_ref, sa_ref, conv_ref, sb_ref, shift_ref, shifts_ref, dbuf_ref,
                convw_ref, gpar_ref, normg_ref, mu_ref, mus_ref, wup_ref, aup_ref, par_ref, poolw_ref, pscale_ref,
                oa_ref, ob_ref, od_ref, sa_out, conv_out, sb_out, dbuf_out, ext_s, *, pos):
    row = main_ref[0]
    sm = small_ref[0]
    bd = _head_block_diag()
    heads = range(N_HEADS)
    lanes = [slice(h * HEAD_DIM, (h + 1) * HEAD_DIM) for h in heads]
    hsum = lambda x: _dot3_l(_rows8(x), bd)[0:1, :]

    u = row[:, C_QKV_A:C_QKV_A + 3 * W_BR]
    buf = conv_ref[0]
    cw = convw_ref[...]
    c = _silu(buf[0:1, :] * cw[0:1, :] + buf[1:2, :] * cw[1:2, :] + buf[2:3, :] * cw[2:3, :] + u * cw[3:4, :])
    conv_out[0, 0:2, :] = buf[1:3, :]
    conv_out[0, 2:3, :] = u
    q_a, k_a, v_a = c[:, 0:W_BR], c[:, W_BR:2 * W_BR], c[:, 2 * W_BR:3 * W_BR]
    gpar = gpar_ref[...]
    g = gpar[0:1, :] * _softplus(sm + gpar[1:2, :])
    beta = _sigmoid(sm)
    eg = [jnp.exp(g[:, S_A + h:S_A + h + 1]) for h in heads]

    p = row[:, C_RKV_B:C_RKV_B + 3 * W_BR]
    ps = p + (shift_ref[0] - p) * mu_ref[...]
    pss = sm + (shifts_ref[0] - sm) * mus_ref[...]
    r, k_b, v_b = ps[:, 0:W_BR], ps[:, W_BR:2 * W_BR], ps[:, 2 * W_BR:3 * W_BR]
    par = par_ref[...]
    w0, a0, xi, alpha, rho, gn_g, gn_b = (par[i:i + 1, :] for i in range(7))
    kx = k_b * xi
    d = w0 + _mm(_rows8(jnp.tanh(pss)), wup_ref[...], HI)[0:1, :]
    a = _sigmoid(a0 + _mm(_rows8(pss), aup_ref[...], HI)[0:1, :])
    q_a = q_a * lax.rsqrt(hsum(q_a * q_a) + L2_EPS) * (HEAD_DIM ** -0.5)
    k_a = k_a * lax.rsqrt(hsum(k_a * k_a) + L2_EPS)
    kk = kx * lax.rsqrt(hsum(kx * kx) + L2_EPS)
    decay = jnp.exp(-RWKV_DECAY_SCALE * _sigmoid(d))
    k2 = k_b * (1.0 + (a - 1.0) * alpha)
    bonus = hsum(r * k2 * rho) * v_b

    sts_a = [sa_ref[0, h] for h in heads]
    sts_b = [sb_ref[0, h] for h in heads]
    k8 = [_rows8(k_a[:, hs]) for hs in lanes]
    ks = [_mm(k8[h], sts_a[h], HI)[0:1, :] for h in heads]
    s_kk = [_mm_nt(_rows8(-kk[:, lanes[h]]), sts_b[h], HI)[0:1, :] for h in heads]

    sub = _iota2((8, HEAD_DIM), 0)
    two_rows = lambda r0, r1: jnp.where(sub == 0, jnp.broadcast_to(r0, (8, HEAD_DIM)),
                                        jnp.where(sub == 1, jnp.broadcast_to(r1, (8, HEAD_DIM)), 0.0))
    v_new = [beta[:, S_B + h:S_B + h + 1] * (v_a[:, lanes[h]] - eg[h] * ks[h]) for h in heads]
    new_a = [sts_a[h] * eg[h] + _mm_tn(k8[h], _rows8(v_new[h]), HI) for h in heads]
    new_b = [sts_b[h] * decay[:, lanes[h]]
             + _mm_tn(two_rows(s_kk[h], v_b[:, lanes[h]]), two_rows(kk[:, lanes[h]] * a[:, lanes[h]], k2[:, lanes[h]]),
                      HI) for h in heads]

    o = jnp.concatenate([_mm(_rows8(q_a[:, lanes[h]]), new_a[h], HI)[0:1, :] for h in heads], axis=1)
    y = jnp.concatenate([_mm_nt(_rows8(r[:, lanes[h]]), new_b[h], HI)[0:1, :] for h in heads], axis=1)
    for h in heads:
        sa_out[0, h] = new_a[h]
        sb_out[0, h] = new_b[h]
    ms = hsum(o * o) * (1.0 / HEAD_DIM)
    mean = hsum(y) * (1.0 / HEAD_DIM)
    oa_ref[0] = o * lax.rsqrt(ms + RMS_EPS) * normg_ref[...]
    yc = y - mean
    var = hsum(yc * yc) * (1.0 / HEAD_DIM)
    ob_ref[0] = yc * lax.rsqrt(var + GN_EPS) * gn_g + gn_b + bonus

    ud = row[:, C_U_D:C_U_D + W_BR]
    ext_s[0:POOL_BUF, :] = dbuf_ref[0]
    ext_s[POOL_BUF:POOL_BUF + 1, :] = ud
    dbuf_out[0] = ext_s[1:POOL_BUF + 1, :]
    outs = []
    for gi, wdw in enumerate(POOL_WINDOWS):
        ls = slice(gi * POOL_GW, (gi + 1) * POOL_GW)
        s = jnp.sum(ext_s[POOL_BUF + 1 - wdw:POOL_BUF + 1, ls], axis=0, keepdims=True)
        pooled = s / float(min(wdw, pos + 1)) - ud[:, ls]
        outs.append(_mm(_rows8(pooled).astype(BF16), poolw_ref[gi].astype(BF16))[0:1, :])
    od_ref[0] = jnp.concatenate(outs, axis=1) * pscale_ref[...]


def _decode_mixers(main_s, small_s, st_a, conv_a, st_b, shift_rkv, shift_small, dbuf,
                   conv_w, gpar, normg, mu, mus, wup, aup, par, pool_w, pscale, pos):
    nd = main_s.shape[0]
    per_seq = lambda shape: pl.BlockSpec((1,) + shape, lambda b: (b,) + (0,) * len(shape))
    full = lambda shape: pl.BlockSpec(shape, lambda b: (0,) * len(shape))
    hh = (N_HEADS, HEAD_DIM, HEAD_DIM)
    return pl.pallas_call(
        functools.partial(_dec_kernel, pos=pos),
        grid=(nd,),
        in_specs=[per_seq((1, N_MAIN)), per_seq((1, N_SMALL)), per_seq(hh), per_seq((CONV_W - 1, 3 * W_BR)),
                  per_seq(hh), per_seq((1, 3 * W_BR)), per_seq((1, N_SMALL)), per_seq((POOL_BUF, W_BR)),
                  full((CONV_W, 3 * W_BR)), full((8, N_SMALL)), full((1, W_BR)),
                  full((1, 3 * W_BR)), full((1, N_SMALL)), full((N_SMALL, W_BR)), full((N_SMALL, W_BR)),
                  full((8, W_BR)), full((len(POOL_WINDOWS), POOL_GW, POOL_GW)), full((1, W_BR))],
        out_specs=[per_seq((1, W_BR)), per_seq((1, W_BR)), per_seq((1, W_BR)), per_seq(hh),
                   per_seq((CONV_W - 1, 3 * W_BR)), per_seq(hh), per_seq((POOL_BUF, W_BR))],
        out_shape=[jax.ShapeDtypeStruct((nd, 1, W_BR), F32)] * 3
                  + [jax.ShapeDtypeStruct((nd,) + hh, F32), jax.ShapeDtypeStruct((nd, CONV_W - 1, 3 * W_BR), F32),
                     jax.ShapeDtypeStruct((nd,) + hh, F32), jax.ShapeDtypeStruct((nd, POOL_BUF, W_BR), F32)],
        scratch_shapes=[pltpu.VMEM((16, W_BR), F32)],
        compiler_params=_cparams(("arbitrary",)),
        name="decode_mixers",
    )(main_s.reshape(nd, 1, N_MAIN), small_s.reshape(nd, 1, N_SMALL), st_a, conv_a, st_b,
      shift_rkv.reshape(nd, 1, 3 * W_BR), shift_small.reshape(nd, 1, N_SMALL), dbuf,
      conv_w, gpar, normg, mu, mus, wup, aup, par, pool_w, pscale)


def _dfox_kernel(pt_ref, qkv_ref, qrep_ref, small_ref, bias_ref, *rest, n_grp):
    kt_refs, vt_refs, lf_refs = rest[0:n_grp], rest[n_grp:2 * n_grp], rest[2 * n_grp:3 * n_grp]
    o_ref, lf_out, m_s, l_s, w_s, r_s, acc_s = rest[3 * n_grp:]
    j = pl.program_id(1)
    scale = HEAD_DIM ** -0.5
    sub8 = _iota2((N_HEADS, LANES), 0)
    lane8 = _iota2((N_HEADS, LANES), 1)

    @pl.when(j == 0)
    def _():
        lf_new = _log_sigmoid(small_ref[0] + bias_ref[...])
        lf_out[0] = lf_new
        qb = (qkv_ref[0, 0] * scale).astype(BF16).astype(F32)
        kb = qkv_ref[0, 1].astype(BF16).astype(F32)
        m_s[...] = jnp.broadcast_to(jnp.sum(qb * kb, axis=1, keepdims=True), (N_HEADS, LANES))
        l_s[...] = jnp.ones_like(l_s)
        w_s[...] = jnp.ones_like(w_s)
        acc_s[...] = jnp.zeros_like(acc_s)
        mine = jnp.where(lane8 == sub8 + S_F, jnp.broadcast_to(lf_new, (N_HEADS, LANES)), 0.0)
        r_s[...] = jnp.broadcast_to(jnp.sum(mine, axis=1, keepdims=True), (N_HEADS, LANES))

    grp = range(n_grp)
    r_t = _iota2((PAGE_SIZE, 2 * LANES), 0)
    c_t = _iota2((PAGE_SIZE, 2 * LANES), 1)
    later_or_all = jnp.where((r_t > c_t) | (c_t >= LANES), 1.0, 0.0).astype(BF16)
    s_t = [jnp.zeros((N_HEADS, LANES), F32) for _ in grp]
    for h in range(N_HEADS):
        q_h = qrep_ref[0, h]
        for g in grp:
            row = jnp.sum(kt_refs[g][0, 0, h] * q_h, axis=0, keepdims=True) * scale
            s_t[g] = jnp.where(sub8 == h, jnp.broadcast_to(row, (N_HEADS, LANES)), s_t[g])
    gates = [_dot3_l(lf_refs[g][0, 0], later_or_all) for g in grp]
    logits, r_run = [], r_s[...]
    for g in grp:
        logits.append(s_t[g] + gates[g][:, 0:LANES] + r_run)
        r_run = r_run + gates[g][:, LANES:2 * LANES]
    m_old = m_s[...]
    m_grp = logits[0]
    for g in grp[1:]:
        m_grp = jnp.maximum(m_grp, logits[g])
    m_new = jnp.maximum(m_old, jnp.max(m_grp, axis=1, keepdims=True))
    alpha = jnp.exp(m_old - m_new)
    p = [jnp.exp(logits[g] - m_new) for g in grp]
    p_sum = p[0]
    for g in grp[1:]:
        p_sum = p_sum + p[g]
    for h in range(N_HEADS):
        acc = acc_s[h] * jnp.broadcast_to(alpha[h:h + 1, :], (HEAD_DIM, LANES))
        for g in grp:
            acc = acc + vt_refs[g][0, 0, h] * jnp.broadcast_to(p[g][h:h + 1, :], (HEAD_DIM, LANES))
        acc_s[h] = acc
    l_s[...] = alpha * l_s[...] + jnp.sum(p_sum, axis=1, keepdims=True)
    w_s[...] = alpha * w_s[...]
    m_s[...] = m_new
    r_s[...] = r_run

    @pl.when(j == pl.num_programs(1) - 1)
    def _():
        ones_b = jnp.ones((N_HEADS, LANES), BF16)
        sub = _iota2((N_HEADS, HEAD_DIM), 0)
        red = jnp.zeros((N_HEADS, HEAD_DIM), F32)
        for h in range(N_HEADS):
            hi, mid, lo = _split3(acc_s[h])
            tot = _mm_nt(ones_b, hi) + _mm_nt(ones_b, mid) + _mm_nt(ones_b, lo)
            red = red + jnp.where(sub == h, tot, 0.0)
        o_ref[0] = (red + w_s[...][:, 0:HEAD_DIM] * qkv_ref[0, 2]) / l_s[...][:, 0:HEAD_DIM]


def _fox_decode(page_table, qkv_s, small_s, bias_row, cache_k, cache_v, cache_lf, layer, n_grp):
    nd, n_pages = page_table.shape
    kt = jnp.transpose(cache_k, (0, 1, 3, 4, 2))
    vt = jnp.transpose(cache_v, (0, 1, 3, 4, 2))
    lft = jnp.transpose(cache_lf, (0, 1, 3, 2))
    qkv8 = qkv_s.reshape(nd, 3, N_HEADS, HEAD_DIM)
    q_rep = jnp.broadcast_to(qkv8[:, 0, :, :, None], (nd, N_HEADS, HEAD_DIM, LANES))
    page = lambda g: (lambda b, j, pt: (layer, pt[b, n_pages - 1 - (j * n_grp + g)], 0, 0, 0))
    page4 = lambda g: (lambda b, j, pt: (layer, pt[b, n_pages - 1 - (j * n_grp + g)], 0, 0))
    kv_spec = lambda g: pl.BlockSpec((1, 1, N_HEADS, HEAD_DIM, PAGE_SIZE), page(g))
    grid_spec = pltpu.PrefetchScalarGridSpec(
        num_scalar_prefetch=1,
        grid=(nd, n_pages // n_grp),
        in_specs=[pl.BlockSpec((1, 3, N_HEADS, HEAD_DIM), lambda b, j, pt: (b, 0, 0, 0)),
                  pl.BlockSpec((1, N_HEADS, HEAD_DIM, LANES), lambda b, j, pt: (b, 0, 0, 0)),
                  pl.BlockSpec((1, 1, N_SMALL), lambda b, j, pt: (b, 0, 0)),
                  pl.BlockSpec((1, N_SMALL), lambda b, j, pt: (0, 0))]
                 + [kv_spec(g) for g in range(n_grp)] + [kv_spec(g) for g in range(n_grp)]
                 + [pl.BlockSpec((1, 1, N_HEADS, PAGE_SIZE), page4(g)) for g in range(n_grp)],
        out_specs=[pl.BlockSpec((1, N_HEADS, HEAD_DIM), lambda b, j, pt: (b, 0, 0)),
                   pl.BlockSpec((1, 1, N_SMALL), lambda b, j, pt: (b, 0, 0))],
        scratch_shapes=[pltpu.VMEM((N_HEADS, LANES), F32)] * 4 + [pltpu.VMEM((N_HEADS, HEAD_DIM, LANES), F32)],
    )
    o, lf_new = pl.pallas_call(
        functools.partial(_dfox_kernel, n_grp=n_grp),
        grid_spec=grid_spec,
        out_shape=[jax.ShapeDtypeStruct((nd, N_HEADS, HEAD_DIM), F32), jax.ShapeDtypeStruct((nd, 1, N_SMALL), F32)],
        compiler_params=_cparams(("arbitrary", "arbitrary")),
        name="fox_decode",
    )(page_table, qkv8, q_rep, small_s.reshape(nd, 1, N_SMALL), bias_row,
      *([kt] * n_grp), *([vt] * n_grp), *([lft] * n_grp))
    return o.reshape(nd, W_BR), lf_new


_MAIN_SEGMENTS = ((O_QKV_A, 3 * W_BR), (O_P_B, 3 * W_BR), (O_QKV_C, 3 * W_BR), (O_Z_A, W_BR), (O_Z_B, W_BR),
                  (O_Z_C, W_BR), (O_U_D, W_BR), (O_Z_D, W_BR))
_SMALL_SEGMENTS = ((O_A_A, 2 * N_HEADS), (O_WL, 2 * LORA_B), (O_F_C, N_HEADS))


def _wprep_kernel(wt_ref, main_ref, small_ref):
    wt = wt_ref[0]
    main_ref[...] = jnp.concatenate([wt[a:a + n, :] for a, n in _MAIN_SEGMENTS], axis=0).T.astype(BF16)
    used = sum(n for _, n in _SMALL_SEGMENTS)
    small = [wt[a:a + n, :] for a, n in _SMALL_SEGMENTS] + [jnp.zeros((N_SMALL - used, wt.shape[1]), F32)]
    small_ref[...] = jnp.concatenate(small, axis=0).T.astype(BF16)


def _prep_w_in(w_in, layer, tr=256):
    _, d, d_in = w_in.shape
    return pl.pallas_call(
        _wprep_kernel,
        grid=(d // tr,),
        in_specs=[pl.BlockSpec((1, d_in, tr), lambda i: (layer, 0, i))],
        out_specs=[pl.BlockSpec((tr, N_MAIN), lambda i: (i, 0)), pl.BlockSpec((tr, N_SMALL), lambda i: (i, 0))],
        out_shape=[jax.ShapeDtypeStruct((d, N_MAIN), BF16), jax.ShapeDtypeStruct((d, N_SMALL), BF16)],
        compiler_params=_cparams(("arbitrary",)),
        name="w_in_prep",
    )(jnp.swapaxes(w_in, 1, 2))


def _small_row(pairs):
    row = jnp.zeros((N_SMALL,), F32)
    for off, vec in pairs:
        row = row.at[off:off + vec.shape[0]].set(vec.astype(F32))
    return row[None, :]


def _layer_params(l, w_in, conv_A, A_log, dt_bias, norm_A, mu_B, w0_B, w_up_B, a0_B, a_up_B, xi_B, alpha_B, rho_B,
                  gn_g_B, gn_b_B, b_f_C, pool_w_D, pool_scale_D, w_out, ln_g, ln_b):
    w_main, w_small = _prep_w_in(w_in, l)
    gpar = jnp.concatenate([_small_row([(S_A, -jnp.exp(A_log[l].astype(F32)))]), _small_row([(S_A, dt_bias[l])]),
                            jnp.zeros((6, N_SMALL), F32)], axis=0)
    head_of_lane = jnp.arange(W_BR) // HEAD_DIM
    lane = jnp.arange(N_SMALL)[:, None]
    eg = (lane == head_of_lane[None, :] + S_A).astype(BF16)
    eb = (lane == head_of_lane[None, :] + S_B).astype(BF16)
    mu = mu_B[l].astype(F32)
    wup = jnp.zeros((N_SMALL, W_BR), F32).at[S_WL:S_WL + LORA_B].set(w_up_B[l].astype(F32))
    aup = jnp.zeros((N_SMALL, W_BR), F32).at[S_AL:S_AL + LORA_B].set(a_up_B[l].astype(F32))
    bd = (head_of_lane[:, None] == head_of_lane[None, :]).astype(BF16)
    par = jnp.stack([w0_B[l], a0_B[l], xi_B[l], alpha_B[l], rho_B[l], gn_g_B[l], gn_b_B[l],
                     jnp.zeros((W_BR,), F32)]).astype(F32)
    return dict(
        w_main=w_main, w_small=w_small, bd=bd, conv_w=conv_A[l].astype(F32), gpar=gpar, eg=eg, eb=eb,
        normg=jnp.tile(norm_A[l].astype(F32), N_HEADS)[None, :],
        mu=mu[None, 0:3 * W_BR], mus=_small_row([(S_WL, mu[3 * W_BR:])]), wup=wup, aup=aup, par=par,
        fbias=_small_row([(S_F, b_f_C[l])]), pool_w=pool_w_D[l].astype(F32), pscale=pool_scale_D[l].astype(F32)[None, :],
        w_out=w_out[l].astype(BF16), ln_g=ln_g[l].astype(F32)[None, :], ln_b=ln_b[l].astype(F32)[None, :])


def _prompt_layer(x3, P, layer, k_all, v_all, tb_rec, tq, tb_gate, tb_pool, tm_proj, tm_out):
    n, t, _ = x3.shape
    x2 = x3.reshape(n * t, D_MODEL)
    main2, small2, k_all, v_all = _proj(x2, P['w_main'], P['w_small'], k_all, v_all, layer, tm_proj)
    main3 = main2.reshape(n, t, N_MAIN)
    small3 = small2.reshape(n, t, N_SMALL)
    o_a, s_a = _gdn_prompt(main3, small3, P['conv_w'], P['gpar'], P['eg'], P['eb'], P['normg'], P['bd'], tb_rec)
    o_b, s_b = _rwkv_prompt(main3, small3, P['mu'], P['mus'], _bf(P['wup']), _bf(P['aup']), P['par'], P['bd'],
                            tb_rec)
    logf3, c3, ct3 = _gates(small3, P['fbias'], tb_gate)
    o_c = _fox_prompt(main3, c3, ct3, tq)
    o_d = _pool_prompt(main3, P['pool_w'], P['pscale'], tb_pool)
    flat = lambda a: a.reshape(n * t, W_BR)
    y2 = _out_proj(flat(o_a), flat(o_b), flat(o_c), flat(o_d), main2, x2, P['w_out'], P['ln_g'], P['ln_b'], tm_out)
    new = (s_a,
           main3[:, t - (CONV_W - 1):, C_QKV_A:C_QKV_A + 3 * W_BR],
           s_b,
           jnp.concatenate([main3[:, t - 1, C_RKV_B:C_RKV_B + 3 * W_BR], small3[:, t - 1, S_WL:S_WL + 2 * LORA_B]], axis=-1),
           logf3[:, :, S_F:S_F + N_HEADS],
           main3[:, t - POOL_BUF:, C_U_D:C_U_D + W_BR])
    return y2.reshape(n, t, D_MODEL), new, k_all, v_all


def _sample_layer(x3, st, cache, layer, k_all, v_all, page_table, P):
    nd = x3.shape[0]
    st_a, conv_a, st_b, shift_b, dbuf = st
    cache_k, cache_v, cache_lf = cache
    x2 = x3.reshape(nd, D_MODEL)
    main_s, small_s, k_all, v_all = _proj(x2, P['w_main'], P['w_small'], k_all, v_all, layer, nd)
    shift_small = jnp.zeros((nd, N_SMALL), F32).at[:, S_WL:S_WL + 2 * LORA_B].set(shift_b[:, 3 * W_BR:].astype(F32))
    o_a, o_b, o_d, s_a, conv_n, s_b, dbuf_n = _decode_mixers(
        main_s, small_s, st_a, conv_a, st_b, shift_b[:, 0:3 * W_BR], shift_small, dbuf,
        P['conv_w'], P['gpar'], P['normg'], P['mu'], P['mus'], P['wup'], P['aup'], P['par'], P['pool_w'], P['pscale'],
        page_table.shape[1] * PAGE_SIZE)
    qkv_s = main_s[:, C_QKV_C:C_QKV_C + 3 * W_BR]
    n_grp = math.gcd(page_table.shape[1], 16)
    o_c, lf_new = _fox_decode(page_table, qkv_s, small_s, P['fbias'], cache_k, cache_v, cache_lf, layer, n_grp)
    flat = lambda a: a.reshape(nd, W_BR)
    y2 = _out_proj(flat(o_a), flat(o_b), flat(o_c), flat(o_d), main_s, x2, P['w_out'], P['ln_g'], P['ln_b'], nd)
    new = (s_a, conv_n, s_b,
           jnp.concatenate([main_s[:, C_RKV_B:C_RKV_B + 3 * W_BR], small_s[:, S_WL:S_WL + 2 * LORA_B]], axis=-1),
           lf_new[:, :, S_F:S_F + N_HEADS],
           dbuf_n)
    return y2.reshape(nd, 1, D_MODEL), new, k_all, v_all


def kernel(x_prompt, x_sample, state_A_S, state_A_conv, state_B_S, state_B_shift, cache_C_k, cache_C_v, cache_C_logf, state_D_buf, page_table, w_in, conv_A, A_log, dt_bias, norm_A, mu_B, w0_B, w_up_B, a0_B, a_up_B, xi_B, alpha_B, rho_B, gn_g_B, gn_b_B, b_f_C, pool_w_D, pool_scale_D, w_out, ln_g, ln_b):
    depth = w_in.shape[0]
    t = x_prompt.shape[1]
    tb_rec = min(256, t)
    tq = min(256, t)
    tb_gate = min(512, t)
    tb_pool = min(512, t)
    tm_proj = min(1024, t)
    tm_out = min(256, x_prompt.shape[0] * t)
    y_p, y_s = x_prompt, x_sample
    prompt_new, sample_new = [], []
    nb, nd = x_prompt.shape[0], x_sample.shape[0]
    pk, pv = jnp.zeros((depth, nb, W_BR, t), F32), jnp.zeros((depth, nb, W_BR, t), F32)
    sk, sv = jnp.zeros((depth, nd, W_BR), F32), jnp.zeros((depth, nd, W_BR), F32)
    for l in range(depth):
        P = _layer_params(l, w_in, conv_A, A_log, dt_bias, norm_A, mu_B, w0_B, w_up_B, a0_B, a_up_B, xi_B, alpha_B,
                          rho_B, gn_g_B, gn_b_B, b_f_C, pool_w_D, pool_scale_D, w_out, ln_g, ln_b)
        y_p, new_p, pk, pv = _prompt_layer(y_p, P, l, pk, pv, tb_rec, tq, tb_gate, tb_pool, tm_proj, tm_out)
        st = (state_A_S[l], state_A_conv[l], state_B_S[l], state_B_shift[l], state_D_buf[l])
        y_s, new_s, sk, sv = _sample_layer(y_s, st, (cache_C_k, cache_C_v, cache_C_logf), l, sk, sv, page_table, P)
        prompt_new.append(new_p)
        sample_new.append(new_s)
    p_a_s, p_a_conv, p_b_s, p_b_shift, p_c_logf, p_d_buf = [jnp.stack([n[i] for n in prompt_new]) for i in range(6)]
    s_a_s, s_a_conv, s_b_s, s_b_shift, s_c_logf, s_d_buf = [jnp.stack([n[i] for n in sample_new]) for i in range(6)]
    p_c_k, p_c_v = (jnp.transpose(a.reshape(depth, nb, N_HEADS, HEAD_DIM, t), (0, 1, 4, 2, 3)) for a in (pk, pv))
    s_c_k, s_c_v = (a.reshape(depth, nd, 1, N_HEADS, HEAD_DIM) for a in (sk, sv))
    return (y_p, y_s, p_a_s, p_a_conv, p_b_s, p_b_shift, p_c_k, p_c_v, p_c_logf, p_d_buf,
            s_a_s, s_a_conv, s_b_s, s_b_shift, s_c_k, s_c_v, s_c_logf, s_d_buf)
```

```python
import functools
import math

import jax
import jax.numpy as jnp
from jax import lax
from jax.experimental import pallas as pl
from jax.experimental.pallas import tpu as pltpu

F32 = jnp.float32
BF16 = jnp.bfloat16
HI = lax.Precision.HIGHEST

D_MODEL = 2048
W_BR = 512
HEAD_DIM = 64
N_HEADS = W_BR // HEAD_DIM
CONV_W = 4
CHUNK = 64
LORA_B = 32
POOL_WINDOWS = (2, 4, 8, 16)
POOL_GW = W_BR // len(POOL_WINDOWS)
POOL_BUF = max(POOL_WINDOWS) - 1
PAGE_SIZE = 128
LANES = 128
SUBLANES = 8
FOX_ROW_SPLIT = 2
DEPTH = 2
ALPHA_DN = (2.0 * DEPTH) ** 0.25
LN_EPS = 1e-5
GN_EPS = 64e-5
RMS_EPS = 1e-6
L2_EPS = 1e-6
RWKV_DECAY_SCALE = math.exp(-0.5)
NEG_INF = -1e30
LOG2_E = 1.4426950408889634

C_QKV_A, C_RKV_B, C_QKV_C = 0, 1536, 3072
C_Z_A, C_Z_B, C_Z_C, C_U_D, C_Z_D = 4608, 5120, 5632, 6144, 6656
N_MAIN = 7168
S_A, S_B, S_WL, S_AL, S_F = 0, 8, 16, 48, 80
N_SMALL = 128
O_QKV_A, O_A_A, O_Z_A, O_P_B, O_WL, O_Z_B, O_QKV_C, O_F_C, O_Z_C, O_U_D, O_Z_D, D_IN = (
    0, 1536, 1552, 2064, 3600, 3664, 4176, 5712, 5720, 6232, 6744, 7256)

VMEM_LIMIT = 48 * 1024 * 1024


VMEM_COMPILER_RESERVE = 8 * 1024 * 1024


def _cparams(sem, vmem_limit=VMEM_LIMIT):
    return pltpu.CompilerParams(dimension_semantics=sem, vmem_limit_bytes=vmem_limit)


def _mm(a, b, prec=None):
    return jnp.dot(a, b, preferred_element_type=F32, precision=prec)


def _mm_nt(a, b, prec=None):
    return lax.dot_general(a, b, (((1,), (1,)), ((), ())), preferred_element_type=F32, precision=prec)


def _mm_tn(a, b, prec=None):
    return lax.dot_general(a, b, (((0,), (0,)), ((), ())), preferred_element_type=F32, precision=prec)


def _split3(x):
    hi = x.astype(BF16)
    r1 = x - hi.astype(F32)
    mid = r1.astype(BF16)
    lo = (r1 - mid.astype(F32)).astype(BF16)
    return hi, mid, lo


def _dot3_l(x, b_bf16):
    hi, mid, lo = _split3(x)
    return _mm(hi, b_bf16) + _mm(mid, b_bf16) + _mm(lo, b_bf16)


def _head_sums(x, same_head_bf16):
    return _mm(x.astype(BF16), same_head_bf16)


def _dot3_r(a_bf16, x):
    hi, mid, lo = _split3(x)
    return _mm(a_bf16, hi) + _mm(a_bf16, mid) + _mm(a_bf16, lo)


def _sigmoid(x):
    return 1.0 / (1.0 + jnp.exp(-x))


def _silu(x):
    return x * _sigmoid(x)


def _softplus(x):
    return jnp.maximum(x, 0.0) + jnp.log1p(jnp.exp(-jnp.abs(x)))


def _log_sigmoid(x):
    return -_softplus(-x)


def _iota2(shape, dim):
    return lax.broadcasted_iota(jnp.int32, shape, dim)


def _head_block_diag():
    r = _iota2((W_BR, W_BR), 0) // HEAD_DIM
    c = _iota2((W_BR, W_BR), 1) // HEAD_DIM
    return jnp.where(r == c, 1.0, 0.0).astype(BF16)


def _chunk_tril(n):
    r = _iota2((n, n), 0)
    c = _iota2((n, n), 1)
    return jnp.where((r >= c) & (r // CHUNK == c // CHUNK), 1.0, 0.0).astype(BF16)


_DN_NN = (((1,), (0,)), ((), ()))
_DN_NT = (((1,), (1,)), ((), ()))
_DN_TN = (((0,), (0,)), ((), ()))


def _bf(x):
    return x.astype(BF16)


def _mmb(a, b, dn=_DN_NN):
    return lax.dot_general(a, b, dn, preferred_element_type=F32)


def _tri_solve(a_list, rhs_list, blk):
    each = lambda f, *ls: [f(*xs) for xs in zip(*ls)]
    width = rhs_list[0].shape[1]
    ad = each(lambda a: a * blk, a_list)
    adb = each(_bf, ad)
    a2b = each(_bf, each(_mmb, adb, adb))
    a4b = each(_bf, each(_mmb, a2b, a2b))
    a8b = each(_bf, each(_mmb, a4b, a4b))
    z = each(lambda r, a, d: jnp.concatenate([r, a - d], axis=1), rhs_list, a_list, ad)
    z = each(lambda x, y: x - y, z, each(_mmb, adb, each(_bf, z)))
    for pw in (a2b, a4b, a8b):
        z = each(lambda x, y: x + y, z, each(_mmb, pw, each(_bf, z)))
    y = each(lambda x: x[:, 0:width], z)
    nb = each(lambda x: _bf(x[:, width:width + CHUNK]), z)
    n2b = each(_bf, each(_mmb, nb, nb))
    t = each(lambda x, u: x + u, y, each(_mmb, n2b, each(_bf, y)))
    return each(lambda x, u: x - u, t, each(_mmb, nb, each(_bf, t)))


def _chunk_masks():
    r = _iota2((CHUNK, CHUNK), 0)
    c = _iota2((CHUNK, CHUNK), 1)
    eye = jnp.where(r == c, 1.0, 0.0).astype(F32)
    strict = jnp.where(r > c, 1.0, 0.0).astype(F32)
    incl = jnp.where(r >= c, 1.0, 0.0).astype(F32)
    blk = jnp.where(r // 16 == c // 16, 1.0, 0.0).astype(F32)
    return eye, strict, incl, blk


def _proj_kernel(x_ref, w_ref, ws_ref, kin_ref, vin_ref, main_ref, small_ref, k_ref, v_ref, xb_ref, *, tn,
                 token_minor):
    j = pl.program_id(1)

    @pl.when(j == 0)
    def _():
        xb = x_ref[...].astype(BF16)
        xb_ref[...] = xb
        small_ref[...] = _mm(xb, ws_ref[...])

    acc = _mm(xb_ref[...], w_ref[...])
    main_ref[...] = acc
    for col, out_ref in ((C_QKV_C + W_BR, k_ref), (C_QKV_C + 2 * W_BR, v_ref)):
        @pl.when(j == col // tn)
        def _(col=col, out_ref=out_ref):
            kv = acc[:, col % tn:col % tn + W_BR]
            if token_minor:
                out_ref[0, 0] = kv.T
            else:
                out_ref[0] = kv


def _proj(x2d, w_main, w_small, k_all, v_all, layer, tm, tn=1024):
    m = x2d.shape[0]
    assert (C_QKV_C + W_BR) % tn + W_BR <= tn and (C_QKV_C + 2 * W_BR) % tn + W_BR <= tn
    blocks = (2 * (tm * D_MODEL * 4 + D_MODEL * tn * 2 + D_MODEL * N_SMALL * 2 + tm * tn * 4 + tm * N_SMALL * 4
                   + 2 * tm * W_BR * 4) + tm * D_MODEL * 2)
    token_minor = k_all.ndim == 4
    if token_minor:
        per_seq = k_all.shape[3] // tm
        kv_spec = pl.BlockSpec((1, 1, W_BR, tm), lambda i, j: (layer, i // per_seq, 0, i % per_seq))
    else:
        kv_spec = pl.BlockSpec((1, tm, W_BR), lambda i, j: (layer, i, 0))
    return pl.pallas_call(
        functools.partial(_proj_kernel, tn=tn, token_minor=token_minor),
        grid=(m // tm, N_MAIN // tn),
        in_specs=[pl.BlockSpec((tm, D_MODEL), lambda i, j: (i, 0)),
                  pl.BlockSpec((D_MODEL, tn), lambda i, j: (0, j)),
                  pl.BlockSpec((D_MODEL, N_SMALL), lambda i, j: (0, 0)),
                  pl.BlockSpec(memory_space=pl.ANY), pl.BlockSpec(memory_space=pl.ANY)],
        out_specs=[pl.BlockSpec((tm, tn), lambda i, j: (i, j)),
                   pl.BlockSpec((tm, N_SMALL), lambda i, j: (i, 0)), kv_spec, kv_spec],
        out_shape=[jax.ShapeDtypeStruct((m, N_MAIN), F32), jax.ShapeDtypeStruct((m, N_SMALL), F32),
                   jax.ShapeDtypeStruct(k_all.shape, F32), jax.ShapeDtypeStruct(v_all.shape, F32)],
        input_output_aliases={3: 2, 4: 3},
        scratch_shapes=[pltpu.VMEM((tm, D_MODEL), BF16)],
        compiler_params=_cparams(("arbitrary", "arbitrary"), max(VMEM_LIMIT, blocks + VMEM_COMPILER_RESERVE)),
        name="proj",
    )(x2d, w_main, w_small, k_all, v_all)


def _gates_kernel(small_ref, bias_ref, logf_ref, c_ref, ct_ref, carry_ref, *, tb):
    @pl.when(pl.program_id(1) == 0)
    def _():
        carry_ref[...] = jnp.zeros_like(carry_ref)

    logf = _log_sigmoid(small_ref[0] + bias_ref[...])
    r = _iota2((tb, tb), 0)
    c = _iota2((tb, tb), 1)
    tril = jnp.where(r >= c, 1.0, 0.0).astype(BF16)
    cum = _dot3_r(tril, logf) + carry_ref[0:1, :]
    carry_ref[...] = jnp.broadcast_to(cum[tb - 1:tb, :], carry_ref.shape)
    logf_ref[0] = logf
    c_ref[0] = cum
    ct_ref[0] = cum.T[S_F:S_F + N_HEADS, :]


def _gates(small3, bias_row, tb):
    n, t, _ = small3.shape
    return pl.pallas_call(
        functools.partial(_gates_kernel, tb=tb),
        grid=(n, t // tb),
        in_specs=[pl.BlockSpec((1, tb, N_SMALL), lambda b, j: (b, j, 0)),
                  pl.BlockSpec((1, N_SMALL), lambda b, j: (0, 0))],
        out_specs=[pl.BlockSpec((1, tb, N_SMALL), lambda b, j: (b, j, 0)),
                   pl.BlockSpec((1, tb, N_SMALL), lambda b, j: (b, j, 0)),
                   pl.BlockSpec((1, N_HEADS, tb), lambda b, j: (b, 0, j))],
        out_shape=[jax.ShapeDtypeStruct((n, t, N_SMALL), F32), jax.ShapeDtypeStruct((n, t, N_SMALL), F32),
                   jax.ShapeDtypeStruct((n, N_HEADS, t), F32)],
        scratch_shapes=[pltpu.VMEM((8, N_SMALL), F32)],
        compiler_params=_cparams(("arbitrary", "arbitrary")),
        name="fox_gates",
    )(small3, bias_row)


def _fox_kernel(qi_ref, kj_ref, q_ref, k_ref, v_ref, c_ref, ct_ref, o_ref, m_s, l_s, acc_s, cq_s, *, tq, n_sub):
    i = qi_ref[pl.program_id(1)]
    j = kj_ref[pl.program_id(1)]
    n_pairs = W_BR // LANES
    lo_half = _iota2((tq, LANES), 1) < HEAD_DIM
    pair_lanes = [slice(pr * LANES, (pr + 1) * LANES) for pr in range(n_pairs)]

    @pl.when(j == 0)
    def _():
        m_s[...] = jnp.full(m_s.shape, NEG_INF, F32)
        l_s[...] = jnp.zeros_like(l_s)
        acc_s[...] = jnp.zeros_like(acc_s)
        cq = c_ref[0]
        for h in range(N_HEADS):
            cq_s[h] = jnp.broadcast_to(cq[:, S_F + h:S_F + h + 1] * LOG2_E, (tq, LANES))

    def step(masked):
        k = k_ref[0].astype(BF16)
        v = v_ref[0]
        ct = ct_ref[0] * LOG2_E
        ones_bd = jnp.concatenate([jnp.where(lo_half, 1.0, 0.0), jnp.where(lo_half, 0.0, 1.0)], axis=0)
        v_bd = [jnp.concatenate(
            [jnp.concatenate([jnp.where(lo_half, v[:, ps], 0.0), jnp.where(lo_half, 0.0, v[:, ps])], axis=0), ones_bd],
            axis=1).astype(BF16) for ps in pair_lanes]
        rq = tq // n_sub
        lo_q = _iota2((rq, LANES), 1) < HEAD_DIM
        wide = lambda x: jnp.concatenate([x] * (tq // LANES), axis=1)
        for r in range(n_sub):
            rs = slice(r * rq, (r + 1) * rq)
            q = q_ref[0, rs, :] * (HEAD_DIM ** -0.5 * LOG2_E)
            if masked:
                keep = _iota2((rq, tq), 0) + r * rq >= _iota2((rq, tq), 1)
            s_all, m_all = [], []
            for pr, ps in enumerate(pair_lanes):
                qp, kp = q[:, ps], k[:, ps]
                for e, q_half in enumerate((jnp.where(lo_q, qp, 0.0), jnp.where(lo_q, 0.0, qp))):
                    h = 2 * pr + e
                    s = _mm_nt(q_half.astype(BF16), kp) + wide(cq_s[h, rs, :]) - ct[h:h + 1, :]
                    if masked:
                        s = jnp.where(keep, s, NEG_INF)
                    s_all.append(s)
                    m_all.append(jnp.maximum(m_s[h, rs, :], jnp.max(s, axis=1, keepdims=True)))
            p_all, alpha_all = [], []
            for h in range(N_HEADS):
                m_new = m_all[h]
                alpha = jnp.exp2(m_s[h, rs, :] - m_new)
                p = jnp.exp2(s_all[h] - wide(m_new))
                m_s[h, rs, :] = m_new
                p_all.append(p.astype(BF16))
                alpha_all.append(alpha)
            for pr, ps in enumerate(pair_lanes):
                p_pair = jnp.concatenate([p_all[2 * pr], p_all[2 * pr + 1]], axis=1)
                alpha_p = jnp.where(lo_q, alpha_all[2 * pr], alpha_all[2 * pr + 1])
                pv_l = _mm(p_pair, v_bd[pr])
                acc_s[rs, ps] = alpha_p * acc_s[rs, ps] + pv_l[:, 0:LANES]
                l_s[pr, rs, :] = alpha_p * l_s[pr, rs, :] + pv_l[:, LANES:2 * LANES]

    @pl.when(j < i)
    def _():
        step(False)

    @pl.when(j == i)
    def _():
        step(True)
        for pr, ps in enumerate(pair_lanes):
            o_ref[0, :, ps] = acc_s[:, ps] / l_s[pr]


def _fox_prompt(main3, c3, ct3, tq):
    n, t, _ = main3.shape
    nb = t // tq
    qb, kb, vb = C_QKV_C // W_BR, C_QKV_C // W_BR + 1, C_QKV_C // W_BR + 2
    pairs = [(i, j) for i in range(nb) for j in range(i + 1)]
    qi = jnp.asarray([p[0] for p in pairs], jnp.int32)
    kj = jnp.asarray([p[1] for p in pairs], jnp.int32)
    grid_spec = pltpu.PrefetchScalarGridSpec(
        num_scalar_prefetch=2,
        grid=(n, len(pairs)),
        in_specs=[pl.BlockSpec((1, tq, W_BR), lambda b, s, qi, kj: (b, qi[s], qb)),
                  pl.BlockSpec((1, tq, W_BR), lambda b, s, qi, kj: (b, kj[s], kb)),
                  pl.BlockSpec((1, tq, W_BR), lambda b, s, qi, kj: (b, kj[s], vb)),
                  pl.BlockSpec((1, tq, N_SMALL), lambda b, s, qi, kj: (b, qi[s], 0)),
                  pl.BlockSpec((1, N_HEADS, tq), lambda b, s, qi, kj: (b, 0, kj[s]))],
        out_specs=pl.BlockSpec((1, tq, W_BR), lambda b, s, qi, kj: (b, qi[s], 0)),
        scratch_shapes=[pltpu.VMEM((N_HEADS, tq, LANES), F32), pltpu.VMEM((W_BR // LANES, tq, LANES), F32),
                        pltpu.VMEM((tq, W_BR), F32), pltpu.VMEM((N_HEADS, tq, LANES), F32)],
    )
    return pl.pallas_call(
        functools.partial(_fox_kernel, tq=tq, n_sub=FOX_ROW_SPLIT),
        grid_spec=grid_spec,
        out_shape=jax.ShapeDtypeStruct((n, t, W_BR), F32),
        compiler_params=_cparams(("arbitrary", "arbitrary")),
        name="fox_prompt",
    )(qi, kj, main3, main3, main3, c3, ct3)


def _gdn_kernel(qkv_ref, small_ref, convw_ref, gpar_ref, eg_ref, eb_ref, normg_ref, bd_ref,
                o_ref, s_out_ref, ext_s, q_s, k_s, v_s, b_s, g_s, o_s, st_s, *, tb):
    j = pl.program_id(1)

    @pl.when(j == 0)
    def _():
        ext_s[0:8, :] = jnp.zeros((8, 3 * W_BR), F32)
        st_s[...] = jnp.zeros_like(st_s)

    u = qkv_ref[0]
    ext_s[8:8 + tb, :] = u
    cw = convw_ref[...]
    c = (ext_s[5:5 + tb, :] * cw[0:1, :] + ext_s[6:6 + tb, :] * cw[1:2, :]
         + ext_s[7:7 + tb, :] * cw[2:3, :] + u * cw[3:4, :])
    ext_s[0:8, :] = u[tb - 8:tb, :]
    c = _silu(c)
    bd = bd_ref[...]
    q = c[:, 0:W_BR]
    k = c[:, W_BR:2 * W_BR]
    q_s[...] = q * lax.rsqrt(_head_sums(q * q, bd) + L2_EPS) * (HEAD_DIM ** -0.5)
    k_s[...] = k * lax.rsqrt(_head_sums(k * k, bd) + L2_EPS)
    v_s[...] = c[:, 2 * W_BR:3 * W_BR]

    sm = small_ref[0]
    gpar = gpar_ref[...]
    g = gpar[0:1, :] * _softplus(sm + gpar[1:2, :])
    beta = _sigmoid(sm)
    gcum = _dot3_r(_chunk_tril(tb), g)
    g_s[...] = _dot3_l(gcum, eg_ref[...])
    b_s[...] = _dot3_l(beta, eb_ref[...])

    gcum_t = gcum.T[S_A:S_A + N_HEADS, :]

    _, strict, incl, blk = _chunk_masks()

    heads = range(N_HEADS)
    lanes = [slice(h * HEAD_DIM, (h + 1) * HEAD_DIM) for h in heads]
    n_chunks = tb // CHUNK
    group = 4

    ops = {}
    for c0 in range(0, n_chunks, group):
        chunks = range(c0, min(c0 + group, n_chunks))
        items = [(c, h) for c in chunks for h in heads]
        rows = lambda c: slice(c * CHUNK, (c + 1) * CHUNK)
        tile = lambda ref: [ref[rows(c), lanes[h]] for c, h in items]
        q_h, k_h, v_h, b_h, g_h = tile(q_s), tile(k_s), tile(v_s), tile(b_s), tile(g_s)
        grow = [jnp.broadcast_to(gcum_t[h:h + 1, c * CHUNK:(c + 1) * CHUNK], (CHUNK, CHUNK))
                for c, h in items]
        idx = range(len(items))
        decay = [jnp.exp(jnp.where(incl > 0, g_h[i] - grow[i], NEG_INF)) for i in idx]
        kb = [k_h[i] * b_h[i] for i in idx]
        both = [_mmb(_bf(jnp.concatenate([kb[i], q_h[i]], axis=0)), _bf(k_h[i]), _DN_NT) for i in idx]
        a = [both[i][0:CHUNK, :] * decay[i] * strict for i in idx]
        attn = [_bf(both[i][CHUNK:2 * CHUNK, :] * decay[i]) for i in idx]
        eg = [jnp.exp(g_h[i]) for i in idx]
        sol = _tri_solve(a, [jnp.concatenate([v_h[i] * b_h[i], kb[i] * eg[i]], axis=1) for i in idx], blk)
        ub = [_bf(sol[i][:, 0:HEAD_DIM]) for i in idx]
        wb = [_bf(sol[i][:, HEAD_DIM:2 * HEAD_DIM]) for i in idx]
        o1 = [_bf(q_h[i] * eg[i] - _mmb(attn[i], wb[i])) for i in idx]
        o2 = [_mmb(attn[i], ub[i]) for i in idx]
        glast = [g_h[i][CHUNK - 1:CHUNK, :] for i in idx]
        kdec = [_bf(k_h[i] * jnp.exp(glast[i] - g_h[i])) for i in idx]
        m = [_bf(_mmb(kdec[i], wb[i], _DN_TN)) for i in idx]
        cc = [_mmb(kdec[i], ub[i], _DN_TN) for i in idx]
        for i, key in enumerate(items):
            ops[key] = (o1[i], o2[i], m[i], cc[i], jnp.exp(glast[i]))

    sts = [st_s[h] for h in heads]
    for c in range(n_chunks):
        stb = [_bf(sts[h]) for h in heads]
        outs = [_mmb(ops[c, h][0], stb[h]) + ops[c, h][1] for h in heads]
        sts = [sts[h] * ops[c, h][4] - _mmb(ops[c, h][2], stb[h]) + ops[c, h][3] for h in heads]
        o_s[c * CHUNK:(c + 1) * CHUNK, :] = jnp.concatenate(outs, axis=1)
    for h in heads:
        st_s[h] = sts[h]

    o = o_s[...]
    ms = _head_sums(o * o, bd) * (1.0 / HEAD_DIM)
    o_ref[0] = o * lax.rsqrt(ms + RMS_EPS) * normg_ref[...]

    @pl.when(j == pl.num_programs(1) - 1)
    def _():
        s_out_ref[0] = st_s[...]


def _gdn_prompt(main3, small3, conv_w, gpar, eg, eb, normg, bd, tb):
    n, t, _ = main3.shape
    full = lambda shape: pl.BlockSpec(shape, lambda b, j: (0,) * len(shape))
    return pl.pallas_call(
        functools.partial(_gdn_kernel, tb=tb),
        grid=(n, t // tb),
        in_specs=[pl.BlockSpec((1, tb, 3 * W_BR), lambda b, j: (b, j, C_QKV_A // (3 * W_BR))),
                  pl.BlockSpec((1, tb, N_SMALL), lambda b, j: (b, j, 0)),
                  full((CONV_W, 3 * W_BR)), full((8, N_SMALL)), full((N_SMALL, W_BR)), full((N_SMALL, W_BR)),
                  full((1, W_BR)), full((W_BR, W_BR))],
        out_specs=[pl.BlockSpec((1, tb, W_BR), lambda b, j: (b, j, 0)),
                   pl.BlockSpec((1, N_HEADS, HEAD_DIM, HEAD_DIM), lambda b, j: (b, 0, 0, 0))],
        out_shape=[jax.ShapeDtypeStruct((n, t, W_BR), F32),
                   jax.ShapeDtypeStruct((n, N_HEADS, HEAD_DIM, HEAD_DIM), F32)],
        scratch_shapes=[pltpu.VMEM((tb + 8, 3 * W_BR), F32)] + [pltpu.VMEM((tb, W_BR), F32)] * 6
                       + [pltpu.VMEM((N_HEADS, HEAD_DIM, HEAD_DIM), F32)],
        compiler_params=_cparams(("arbitrary", "arbitrary")),
        name="gdn_prompt",
    )(main3, small3, conv_w, gpar, eg, eb, normg, bd)


def _rwkv_kernel(p_ref, small_ref, mu_ref, mus_ref, wup_ref, aup_ref, par_ref, bd_ref,
                 o_ref, s_out_ref, ext_s, exts_s, ah_s, bh_s, kh_s, rh_s, be_s, ke_s, v_s, wl_s, y_s, st_s, *, tb):
    j = pl.program_id(1)

    @pl.when(j == 0)
    def _():
        ext_s[0:8, :] = jnp.zeros((8, 3 * W_BR), F32)
        exts_s[0:8, :] = jnp.zeros((8, N_SMALL), F32)
        st_s[...] = jnp.zeros_like(st_s)

    p = p_ref[0]
    sm = small_ref[0]
    ext_s[8:8 + tb, :] = p
    exts_s[8:8 + tb, :] = sm
    prev = ext_s[7:7 + tb, :]
    prevs = exts_s[7:7 + tb, :]
    ext_s[0:8, :] = p[tb - 8:tb, :]
    exts_s[0:8, :] = sm[tb - 8:tb, :]
    ps = p + (prev - p) * mu_ref[...]
    pss = sm + (prevs - sm) * mus_ref[...]
    r = ps[:, 0:W_BR]
    k = ps[:, W_BR:2 * W_BR]
    v = ps[:, 2 * W_BR:3 * W_BR]
    par = par_ref[...]
    w0, a0, xi, alpha, rho, gn_g, gn_b = (par[i:i + 1, :] for i in range(7))
    d = w0 + _mm(_bf(jnp.tanh(pss)), wup_ref[...])
    logw = -RWKV_DECAY_SCALE * _sigmoid(d)
    a = _sigmoid(a0 + _mm(_bf(pss), aup_ref[...]))
    bd = bd_ref[...]
    kx = k * xi
    kk = kx * lax.rsqrt(_head_sums(kx * kx, bd) + L2_EPS)
    k2 = k * (1.0 + (a - 1.0) * alpha)
    lc = _dot3_r(_chunk_tril(tb), logw)
    nb = -(a * kk)
    ah_s[...] = _bf(kk * jnp.exp(lc - logw))
    bh_s[...] = _bf(nb * jnp.exp(-lc))
    kh_s[...] = _bf(k2 * jnp.exp(-lc))
    rh_s[...] = _bf(r * jnp.exp(lc))
    v_s[...] = _bf(v)
    ll = jnp.concatenate([jnp.broadcast_to(lc[c * CHUNK + CHUNK - 1:(c + 1) * CHUNK, :], (CHUNK, W_BR))
                          for c in range(tb // CHUNK)], axis=0)
    to_end = jnp.exp(ll - lc)
    be_s[...] = _bf(nb * to_end)
    ke_s[...] = _bf(k2 * to_end)
    wl_s[...] = jnp.exp(ll)

    _, strict, incl, blk = _chunk_masks()
    incl2 = jnp.concatenate([incl, incl], axis=1)

    heads = range(N_HEADS)
    lanes = [slice(h * HEAD_DIM, (h + 1) * HEAD_DIM) for h in heads]
    n_chunks = tb // CHUNK
    group = 4

    ops = {}
    for c0 in range(0, n_chunks, group):
        chunks = range(c0, min(c0 + group, n_chunks))
        items = [(c, h) for c in chunks for h in heads]
        rows = lambda c: slice(c * CHUNK, (c + 1) * CHUNK)
        tile = lambda ref: [ref[rows(c), lanes[h]] for c, h in items]
        a_h, b_h, k_h, r_h, v_h, be_h, ke_h = (tile(x) for x in (ah_s, bh_s, kh_s, rh_s, v_s, be_s, ke_s))
        idx = range(len(items))
        gram = [_mmb(jnp.concatenate([a_h[i], r_h[i]], axis=0), jnp.concatenate([b_h[i], k_h[i]], axis=0), _DN_NT)
                for i in idx]
        akv = [_mmb(_bf(gram[i][0:CHUNK, CHUNK:2 * CHUNK] * strict), v_h[i]) for i in idx]
        sol = _tri_solve([-(gram[i][0:CHUNK, 0:CHUNK] * strict) for i in idx],
                         [jnp.concatenate([a_h[i].astype(F32), akv[i]], axis=1) for i in idx], blk)
        w1b = [_bf(sol[i][:, 0:HEAD_DIM]) for i in idx]
        w2v = [jnp.concatenate([_bf(sol[i][:, HEAD_DIM:2 * HEAD_DIM]), v_h[i]], axis=0) for i in idx]
        r_bk = [_bf(gram[i][CHUNK:2 * CHUNK, :] * incl2) for i in idx]
        g1 = [_bf(r_h[i].astype(F32) + _mmb(r_bk[i][:, 0:CHUNK], w1b[i])) for i in idx]
        g2 = [_mmb(r_bk[i], w2v[i]) for i in idx]
        m1 = [_bf(_mmb(w1b[i], be_h[i], _DN_TN)) for i in idx]
        c2 = [_mmb(w2v[i], jnp.concatenate([be_h[i], ke_h[i]], axis=0), _DN_TN) for i in idx]
        for i, (c, h) in enumerate(items):
            ops[c, h] = (g1[i], g2[i], m1[i], c2[i], wl_s[c * CHUNK:c * CHUNK + 1, lanes[h]])

    sts = [st_s[h] for h in heads]
    for c in range(n_chunks):
        stb = [_bf(sts[h]) for h in heads]
        outs = [_mmb(ops[c, h][0], stb[h], _DN_NT) + ops[c, h][1] for h in heads]
        sts = [sts[h] * ops[c, h][4] + _mmb(stb[h], ops[c, h][2]) + ops[c, h][3] for h in heads]
        y_s[c * CHUNK:(c + 1) * CHUNK, :] = jnp.concatenate(outs, axis=1)
    for h in heads:
        st_s[h] = sts[h]

    y = y_s[...]
    mean = _head_sums(y, bd) * (1.0 / HEAD_DIM)
    yc = y - mean
    var = _head_sums(yc * yc, bd) * (1.0 / HEAD_DIM)
    yn = yc * lax.rsqrt(var + GN_EPS) * gn_g + gn_b
    bonus = _head_sums(r * k2 * rho, bd) * v
    o_ref[0] = yn + bonus

    @pl.when(j == pl.num_programs(1) - 1)
    def _():
        s_out_ref[0] = st_s[...]


def _rwkv_prompt(main3, small3, mu, mus, wup, aup, par, bd, tb):
    n, t, _ = main3.shape
    full = lambda shape: pl.BlockSpec(shape, lambda b, j: (0,) * len(shape))
    return pl.pallas_call(
        functools.partial(_rwkv_kernel, tb=tb),
        grid=(n, t // tb),
        in_specs=[pl.BlockSpec((1, tb, 3 * W_BR), lambda b, j: (b, j, C_RKV_B // (3 * W_BR))),
                  pl.BlockSpec((1, tb, N_SMALL), lambda b, j: (b, j, 0)),
                  full((1, 3 * W_BR)), full((1, N_SMALL)), full((N_SMALL, W_BR)), full((N_SMALL, W_BR)),
                  full((8, W_BR)), full((W_BR, W_BR))],
        out_specs=[pl.BlockSpec((1, tb, W_BR), lambda b, j: (b, j, 0)),
                   pl.BlockSpec((1, N_HEADS, HEAD_DIM, HEAD_DIM), lambda b, j: (b, 0, 0, 0))],
        out_shape=[jax.ShapeDtypeStruct((n, t, W_BR), F32),
                   jax.ShapeDtypeStruct((n, N_HEADS, HEAD_DIM, HEAD_DIM), F32)],
        scratch_shapes=[pltpu.VMEM((tb + 8, 3 * W_BR), F32), pltpu.VMEM((tb + 8, N_SMALL), F32)]
                       + [pltpu.VMEM((tb, W_BR), BF16)] * 7 + [pltpu.VMEM((tb, W_BR), F32)] * 2
                       + [pltpu.VMEM((N_HEADS, HEAD_DIM, HEAD_DIM), F32)],
        compiler_params=_cparams(("arbitrary", "arbitrary")),
        name="rwkv_prompt",
    )(main3, small3, mu, mus, wup, aup, par, bd)


def _pool_kernel(u_ref, w_ref, scale_ref, o_ref, ext_s, *, tb):
    j = pl.program_id(1)

    @pl.when(j == 0)
    def _():
        ext_s[0:16, :] = jnp.zeros((16, W_BR), F32)

    u = u_ref[0]
    ext_s[16:16 + tb, :] = u
    pos = j * tb + _iota2((tb, POOL_GW), 0)
    outs = []
    for gi, wdw in enumerate(POOL_WINDOWS):
        ls = slice(gi * POOL_GW, (gi + 1) * POOL_GW)
        s = u[:, ls]
        for sh in range(1, wdw):
            s = s + ext_s[16 - sh:16 - sh + tb, ls]
        cnt = jnp.minimum(wdw, pos + 1).astype(F32)
        pooled = s / cnt - u[:, ls]
        outs.append(_mm(pooled.astype(BF16), w_ref[gi].astype(BF16)))
    ext_s[0:16, :] = u[tb - 16:tb, :]
    o_ref[0] = jnp.concatenate(outs, axis=1) * scale_ref[...]


def _pool_prompt(main3, pool_w, scale, tb):
    n, t, _ = main3.shape
    return pl.pallas_call(
        functools.partial(_pool_kernel, tb=tb),
        grid=(n, t // tb),
        in_specs=[pl.BlockSpec((1, tb, W_BR), lambda b, j: (b, j, C_U_D // W_BR)),
                  pl.BlockSpec((len(POOL_WINDOWS), POOL_GW, POOL_GW), lambda b, j: (0, 0, 0)),
                  pl.BlockSpec((1, W_BR), lambda b, j: (0, 0))],
        out_specs=pl.BlockSpec((1, tb, W_BR), lambda b, j: (b, j, 0)),
        out_shape=jax.ShapeDtypeStruct((n, t, W_BR), F32),
        scratch_shapes=[pltpu.VMEM((tb + 16, W_BR), F32)],
        compiler_params=_cparams(("arbitrary", "arbitrary")),
        name="pool_prompt",
    )(main3, pool_w, scale)


def _out_kernel(oa_ref, ob_ref, oc_ref, od_ref, za_ref, zb_ref, zc_ref, zd_ref, x_ref, w_ref, g_ref, b_ref, y_ref):
    acc = ALPHA_DN * x_ref[...]
    for i, (o_r, z_r) in enumerate(((oa_ref, za_ref), (ob_ref, zb_ref), (oc_ref, zc_ref), (od_ref, zd_ref))):
        gated = (o_r[...] * _silu(z_r[...])).astype(BF16)
        acc = acc + _mm(gated, w_ref[i * W_BR:(i + 1) * W_BR, :])
    mu = jnp.mean(acc, axis=-1, keepdims=True)
    xc = acc - mu
    var = jnp.mean(xc * xc, axis=-1, keepdims=True)
    y_ref[...] = xc * lax.rsqrt(var + LN_EPS) * g_ref[...] + b_ref[...]


def _out_proj(o_a, o_b, o_c, o_d, main2, x2d, w_out, ln_g, ln_b, tm):
    m = x2d.shape[0]
    ospec = pl.BlockSpec((tm, W_BR), lambda i: (i, 0))
    zspec = lambda col: pl.BlockSpec((tm, W_BR), lambda i: (i, col // W_BR))
    return pl.pallas_call(
        _out_kernel,
        grid=(m // tm,),
        in_specs=[ospec, ospec, ospec, ospec, zspec(C_Z_A), zspec(C_Z_B), zspec(C_Z_C), zspec(C_Z_D),
                  pl.BlockSpec((tm, D_MODEL), lambda i: (i, 0)),
                  pl.BlockSpec((D_MODEL, D_MODEL), lambda i: (0, 0)),
                  pl.BlockSpec((1, D_MODEL), lambda i: (0, 0)), pl.BlockSpec((1, D_MODEL), lambda i: (0, 0))],
        out_specs=pl.BlockSpec((tm, D_MODEL), lambda i: (i, 0)),
        out_shape=jax.ShapeDtypeStruct((m, D_MODEL), F32),
        compiler_params=_cparams(("arbitrary",)),
        name="out_proj",
    )(o_a, o_b, o_c, o_d, main2, main2, main2, main2, x2d, w_out, ln_g, ln_b)


def _rows8(row):
    shape = (SUBLANES, row.shape[1])
    return jnp.where(_iota2(shape, 0) == 0, jnp.broadcast_to(row, shape), 0.0)


def _dec_kernel(main_ref, small_ref, sa_ref, conv_ref, sb_ref, shift_ref, shifts_ref, dbuf_ref,
                convw_ref, gpar_ref, normg_ref, mu_ref, mus_ref, wup_ref, aup_ref, par_ref, poolw_ref, pscale_ref,
                oa_ref, ob_ref, od_ref, sa_out, conv_out, sb_out, dbuf_out, ext_s, *, pos):
    row = main_ref[0]
    sm = small_ref[0]
    bd = _head_block_diag()
    heads = range(N_HEADS)
    lanes = [slice(h * HEAD_DIM, (h + 1) * HEAD_DIM) for h in heads]
    hsum = lambda x: _dot3_l(_rows8(x), bd)[0:1, :]

    u = row[:, C_QKV_A:C_QKV_A + 3 * W_BR]
    buf = conv_ref[0]
    cw = convw_ref[...]
    c = _silu(buf[0:1, :] * cw[0:1, :] + buf[1:2, :] * cw[1:2, :] + buf[2:3, :] * cw[2:3, :] + u * cw[3:4, :])
    conv_out[0, 0:2, :] = buf[1:3, :]
    conv_out[0, 2:3, :] = u
    q_a, k_a, v_a = c[:, 0:W_BR], c[:, W_BR:2 * W_BR], c[:, 2 * W_BR:3 * W_BR]
    gpar = gpar_ref[...]
    g = gpar[0:1, :] * _softplus(sm + gpar[1:2, :])
    beta = _sigmoid(sm)
    eg = [jnp.exp(g[:, S_A + h:S_A + h + 1]) for h in heads]

    p = row[:, C_RKV_B:C_RKV_B + 3 * W_BR]
    ps = p + (shift_ref[0] - p) * mu_ref[...]
    pss = sm + (shifts_ref[0] - sm) * mus_ref[...]
    r, k_b, v_b = ps[:, 0:W_BR], ps[:, W_BR:2 * W_BR], ps[:, 2 * W_BR:3 * W_BR]
    par = par_ref[...]
    w0, a0, xi, alpha, rho, gn_g, gn_b = (par[i:i + 1, :] for i in range(7))
    kx = k_b * xi
    d = w0 + _mm(_rows8(jnp.tanh(pss)), wup_ref[...], HI)[0:1, :]
    a = _sigmoid(a0 + _mm(_rows8(pss), aup_ref[...], HI)[0:1, :])
    q_a = q_a * lax.rsqrt(hsum(q_a * q_a) + L2_EPS) * (HEAD_DIM ** -0.5)
    k_a = k_a * lax.rsqrt(hsum(k_a * k_a) + L2_EPS)
    kk = kx * lax.rsqrt(hsum(kx * kx) + L2_EPS)
    decay = jnp.exp(-RWKV_DECAY_SCALE * _sigmoid(d))
    k2 = k_b * (1.0 + (a - 1.0) * alpha)
    bonus = hsum(r * k2 * rho) * v_b

    sts_a = [sa_ref[0, h] for h in heads]
    sts_b = [sb_ref[0, h] for h in heads]
    k8 = [_rows8(k_a[:, hs]) for hs in lanes]
    ks = [_mm(k8[h], sts_a[h], HI)[0:1, :] for h in heads]
    s_kk = [_mm_nt(_rows8(-kk[:, lanes[h]]), sts_b[h], HI)[0:1, :] for h in heads]

    sub = _iota2((8, HEAD_DIM), 0)
    two_rows = lambda r0, r1: jnp.where(sub == 0, jnp.broadcast_to(r0, (8, HEAD_DIM)),
                                        jnp.where(sub == 1, jnp.broadcast_to(r1, (8, HEAD_DIM)), 0.0))
    v_new = [beta[:, S_B + h:S_B + h + 1] * (v_a[:, lanes[h]] - eg[h] * ks[h]) for h in heads]
    new_a = [sts_a[h] * eg[h] + _mm_tn(k8[h], _rows8(v_new[h]), HI) for h in heads]
    new_b = [sts_b[h] * decay[:, lanes[h]]
             + _mm_tn(two_rows(s_kk[h], v_b[:, lanes[h]]), two_rows(kk[:, lanes[h]] * a[:, lanes[h]], k2[:, lanes[h]]),
                      HI) for h in heads]

    o = jnp.concatenate([_mm(_rows8(q_a[:, lanes[h]]), new_a[h], HI)[0:1, :] for h in heads], axis=1)
    y = jnp.concatenate([_mm_nt(_rows8(r[:, lanes[h]]), new_b[h], HI)[0:1, :] for h in heads], axis=1)
    for h in heads:
        sa_out[0, h] = new_a[h]
        sb_out[0, h] = new_b[h]
    ms = hsum(o * o) * (1.0 / HEAD_DIM)
    mean = hsum(y) * (1.0 / HEAD_DIM)
    oa_ref[0] = o * lax.rsqrt(ms + RMS_EPS) * normg_ref[...]
    yc = y - mean
    var = hsum(yc * yc) * (1.0 / HEAD_DIM)
    ob_ref[0] = yc * lax.rsqrt(var + GN_EPS) * gn_g + gn_b + bonus

    ud = row[:, C_U_D:C_U_D + W_BR]
    ext_s[0:POOL_BUF, :] = dbuf_ref[0]
    ext_s[POOL_BUF:POOL_BUF + 1, :] = ud
    dbuf_out[0] = ext_s[1:POOL_BUF + 1, :]
    outs = []
    for gi, wdw in enumerate(POOL_WINDOWS):
        ls = slice(gi * POOL_GW, (gi + 1) * POOL_GW)
        s = jnp.sum(ext_s[POOL_BUF + 1 - wdw:POOL_BUF + 1, ls], axis=0, keepdims=True)
        pooled = s / float(min(wdw, pos + 1)) - ud[:, ls]
        outs.append(_mm(_rows8(pooled).astype(BF16), poolw_ref[gi].astype(BF16))[0:1, :])
    od_ref[0] = jnp.concatenate(outs, axis=1) * pscale_ref[...]


def _decode_mixers(main_s, small_s, st_a, conv_a, st_b, shift_rkv, shift_small, dbuf,
                   conv_w, gpar, normg, mu, mus, wup, aup, par, pool_w, pscale, pos):
    nd = main_s.shape[0]
    per_seq = lambda shape: pl.BlockSpec((1,) + shape, lambda b: (b,) + (0,) * len(shape))
    full = lambda shape: pl.BlockSpec(shape, lambda b: (0,) * len(shape))
    hh = (N_HEADS, HEAD_DIM, HEAD_DIM)
    return pl.pallas_call(
        functools.partial(_dec_kernel, pos=pos),
        grid=(nd,),
        in_specs=[per_seq((1, N_MAIN)), per_seq((1, N_SMALL)), per_seq(hh), per_seq((CONV_W - 1, 3 * W_BR)),
                  per_seq(hh), per_seq((1, 3 * W_BR)), per_seq((1, N_SMALL)), per_seq((POOL_BUF, W_BR)),
                  full((CONV_W, 3 * W_BR)), full((8, N_SMALL)), full((1, W_BR)),
                  full((1, 3 * W_BR)), full((1, N_SMALL)), full((N_SMALL, W_BR)), full((N_SMALL, W_BR)),
                  full((8, W_BR)), full((len(POOL_WINDOWS), POOL_GW, POOL_GW)), full((1, W_BR))],
        out_specs=[per_seq((1, W_BR)), per_seq((1, W_BR)), per_seq((1, W_BR)), per_seq(hh),
                   per_seq((CONV_W - 1, 3 * W_BR)), per_seq(hh), per_seq((POOL_BUF, W_BR))],
        out_shape=[jax.ShapeDtypeStruct((nd, 1, W_BR), F32)] * 3
                  + [jax.ShapeDtypeStruct((nd,) + hh, F32), jax.ShapeDtypeStruct((nd, CONV_W - 1, 3 * W_BR), F32),
                     jax.ShapeDtypeStruct((nd,) + hh, F32), jax.ShapeDtypeStruct((nd, POOL_BUF, W_BR), F32)],
        scratch_shapes=[pltpu.VMEM((16, W_BR), F32)],
        compiler_params=_cparams(("arbitrary",)),
        name="decode_mixers",
    )(main_s.reshape(nd, 1, N_MAIN), small_s.reshape(nd, 1, N_SMALL), st_a, conv_a, st_b,
      shift_rkv.reshape(nd, 1, 3 * W_BR), shift_small.reshape(nd, 1, N_SMALL), dbuf,
      conv_w, gpar, normg, mu, mus, wup, aup, par, pool_w, pscale)


def _dfox_kernel(pt_ref, qkv_ref, qrep_ref, small_ref, bias_ref, *rest, n_grp):
    kt_refs, vt_refs, lf_refs = rest[0:n_grp], rest[n_grp:2 * n_grp], rest[2 * n_grp:3 * n_grp]
    o_ref, lf_out, m_s, l_s, w_s, r_s, acc_s = rest[3 * n_grp:]
    j = pl.program_id(1)
    scale = HEAD_DIM ** -0.5
    sub8 = _iota2((N_HEADS, LANES), 0)
    lane8 = _iota2((N_HEADS, LANES), 1)

    @pl.when(j == 0)
    def _():
        lf_new = _log_sigmoid(small_ref[0] + bias_ref[...])
        lf_out[0] = lf_new
        qb = (qkv_ref[0, 0] * scale).astype(BF16).astype(F32)
        kb = qkv_ref[0, 1].astype(BF16).astype(F32)
        m_s[...] = jnp.broadcast_to(jnp.sum(qb * kb, axis=1, keepdims=True), (N_HEADS, LANES))
        l_s[...] = jnp.ones_like(l_s)
        w_s[...] = jnp.ones_like(w_s)
        acc_s[...] = jnp.zeros_like(acc_s)
        mine = jnp.where(lane8 == sub8 + S_F, jnp.broadcast_to(lf_new, (N_HEADS, LANES)), 0.0)
        r_s[...] = jnp.broadcast_to(jnp.sum(mine, axis=1, keepdims=True), (N_HEADS, LANES))

    grp = range(n_grp)
    r_t = _iota2((PAGE_SIZE, 2 * LANES), 0)
    c_t = _iota2((PAGE_SIZE, 2 * LANES), 1)
    later_or_all = jnp.where((r_t > c_t) | (c_t >= LANES), 1.0, 0.0).astype(BF16)
    s_t = [jnp.zeros((N_HEADS, LANES), F32) for _ in grp]
    for h in range(N_HEADS):
        q_h = qrep_ref[0, h]
        for g in grp:
            row = jnp.sum(kt_refs[g][0, 0, h] * q_h, axis=0, keepdims=True) * scale
            s_t[g] = jnp.where(sub8 == h, jnp.broadcast_to(row, (N_HEADS, LANES)), s_t[g])
    gates = [_dot3_l(lf_refs[g][0, 0], later_or_all) for g in grp]
    logits, r_run = [], r_s[...]
    for g in grp:
        logits.append(s_t[g] + gates[g][:, 0:LANES] + r_run)
        r_run = r_run + gates[g][:, LANES:2 * LANES]
    m_old = m_s[...]
    m_grp = logits[0]
    for g in grp[1:]:
        m_grp = jnp.maximum(m_grp, logits[g])
    m_new = jnp.maximum(m_old, jnp.max(m_grp, axis=1, keepdims=True))
    alpha = jnp.exp(m_old - m_new)
    p = [jnp.exp(logits[g] - m_new) for g in grp]
    p_sum = p[0]
    for g in grp[1:]:
        p_sum = p_sum + p[g]
    for h in range(N_HEADS):
        acc = acc_s[h] * jnp.broadcast_to(alpha[h:h + 1, :], (HEAD_DIM, LANES))
        for g in grp:
            acc = acc + vt_refs[g][0, 0, h] * jnp.broadcast_to(p[g][h:h + 1, :], (HEAD_DIM, LANES))
        acc_s[h] = acc
    l_s[...] = alpha * l_s[...] + jnp.sum(p_sum, axis=1, keepdims=True)
    w_s[...] = alpha * w_s[...]
    m_s[...] = m_new
    r_s[...] = r_run

    @pl.when(j == pl.num_programs(1) - 1)
    def _():
        ones_b = jnp.ones((N_HEADS, LANES), BF16)
        sub = _iota2((N_HEADS, HEAD_DIM), 0)
        red = jnp.zeros((N_HEADS, HEAD_DIM), F32)
        for h in range(N_HEADS):
            hi, mid, lo = _split3(acc_s[h])
            tot = _mm_nt(ones_b, hi) + _mm_nt(ones_b, mid) + _mm_nt(ones_b, lo)
            red = red + jnp.where(sub == h, tot, 0.0)
        o_ref[0] = (red + w_s[...][:, 0:HEAD_DIM] * qkv_ref[0, 2]) / l_s[...][:, 0:HEAD_DIM]


def _fox_decode(page_table, qkv_s, small_s, bias_row, cache_k, cache_v, cache_lf, layer, n_grp):
    nd, n_pages = page_table.shape
    kt = jnp.transpose(cache_k, (0, 1, 3, 4, 2))
    vt = jnp.transpose(cache_v, (0, 1, 3, 4, 2))
    lft = jnp.transpose(cache_lf, (0, 1, 3, 2))
    qkv8 = qkv_s.reshape(nd, 3, N_HEADS, HEAD_DIM)
    q_rep = jnp.broadcast_to(qkv8[:, 0, :, :, None], (nd, N_HEADS, HEAD_DIM, LANES))
    page = lambda g: (lambda b, j, pt: (layer, pt[b, n_pages - 1 - (j * n_grp + g)], 0, 0, 0))
    page4 = lambda g: (lambda b, j, pt: (layer, pt[b, n_pages - 1 - (j * n_grp + g)], 0, 0))
    kv_spec = lambda g: pl.BlockSpec((1, 1, N_HEADS, HEAD_DIM, PAGE_SIZE), page(g))
    grid_spec = pltpu.PrefetchScalarGridSpec(
        num_scalar_prefetch=1,
        grid=(nd, n_pages // n_grp),
        in_specs=[pl.BlockSpec((1, 3, N_HEADS, HEAD_DIM), lambda b, j, pt: (b, 0, 0, 0)),
                  pl.BlockSpec((1, N_HEADS, HEAD_DIM, LANES), lambda b, j, pt: (b, 0, 0, 0)),
                  pl.BlockSpec((1, 1, N_SMALL), lambda b, j, pt: (b, 0, 0)),
                  pl.BlockSpec((1, N_SMALL), lambda b, j, pt: (0, 0))]
                 + [kv_spec(g) for g in range(n_grp)] + [kv_spec(g) for g in range(n_grp)]
                 + [pl.BlockSpec((1, 1, N_HEADS, PAGE_SIZE), page4(g)) for g in range(n_grp)],
        out_specs=[pl.BlockSpec((1, N_HEADS, HEAD_DIM), lambda b, j, pt: (b, 0, 0)),
                   pl.BlockSpec((1, 1, N_SMALL), lambda b, j, pt: (b, 0, 0))],
        scratch_shapes=[pltpu.VMEM((N_HEADS, LANES), F32)] * 4 + [pltpu.VMEM((N_HEADS, HEAD_DIM, LANES), F32)],
    )
    o, lf_new = pl.pallas_call(
        functools.partial(_dfox_kernel, n_grp=n_grp),
        grid_spec=grid_spec,
        out_shape=[jax.ShapeDtypeStruct((nd, N_HEADS, HEAD_DIM), F32), jax.ShapeDtypeStruct((nd, 1, N_SMALL), F32)],
        compiler_params=_cparams(("arbitrary", "arbitrary")),
        name="fox_decode",
    )(page_table, qkv8, q_rep, small_s.reshape(nd, 1, N_SMALL), bias_row,
      *([kt] * n_grp), *([vt] * n_grp), *([lft] * n_grp))
    return o.reshape(nd, W_BR), lf_new


_MAIN_SEGMENTS = ((O_QKV_A, 3 * W_BR), (O_P_B, 3 * W_BR), (O_QKV_C, 3 * W_BR), (O_Z_A, W_BR), (O_Z_B, W_BR),
                  (O_Z_C, W_BR), (O_U_D, W_BR), (O_Z_D, W_BR))
_SMALL_SEGMENTS = ((O_A_A, 2 * N_HEADS), (O_WL, 2 * LORA_B), (O_F_C, N_HEADS))


def _wprep_kernel(wt_ref, main_ref, small_ref):
    wt = wt_ref[0]
    main_ref[...] = jnp.concatenate([wt[a:a + n, :] for a, n in _MAIN_SEGMENTS], axis=0).T.astype(BF16)
    used = sum(n for _, n in _SMALL_SEGMENTS)
    small = [wt[a:a + n, :] for a, n in _SMALL_SEGMENTS] + [jnp.zeros((N_SMALL - used, wt.shape[1]), F32)]
    small_ref[...] = jnp.concatenate(small, axis=0).T.astype(BF16)


def _prep_w_in(w_in, layer, tr=256):
    _, d, d_in = w_in.shape
    return pl.pallas_call(
        _wprep_kernel,
        grid=(d // tr,),
        in_specs=[pl.BlockSpec((1, d_in, tr), lambda i: (layer, 0, i))],
        out_specs=[pl.BlockSpec((tr, N_MAIN), lambda i: (i, 0)), pl.BlockSpec((tr, N_SMALL), lambda i: (i, 0))],
        out_shape=[jax.ShapeDtypeStruct((d, N_MAIN), BF16), jax.ShapeDtypeStruct((d, N_SMALL), BF16)],
        compiler_params=_cparams(("arbitrary",)),
        name="w_in_prep",
    )(jnp.swapaxes(w_in, 1, 2))


def _small_row(pairs):
    row = jnp.zeros((N_SMALL,), F32)
    for off, vec in pairs:
        row = row.at[off:off + vec.shape[0]].set(vec.astype(F32))
    return row[None, :]


def _layer_params(l, w_in, conv_A, A_log, dt_bias, norm_A, mu_B, w0_B, w_up_B, a0_B, a_up_B, xi_B, alpha_B, rho_B,
                  gn_g_B, gn_b_B, b_f_C, pool_w_D, pool_scale_D, w_out, ln_g, ln_b):
    w_main, w_small = _prep_w_in(w_in, l)
    gpar = jnp.concatenate([_small_row([(S_A, -jnp.exp(A_log[l].astype(F32)))]), _small_row([(S_A, dt_bias[l])]),
                            jnp.zeros((6, N_SMALL), F32)], axis=0)
    head_of_lane = jnp.arange(W_BR) // HEAD_DIM
    lane = jnp.arange(N_SMALL)[:, None]
    eg = (lane == head_of_lane[None, :] + S_A).astype(BF16)
    eb = (lane == head_of_lane[None, :] + S_B).astype(BF16)
    mu = mu_B[l].astype(F32)
    wup = jnp.zeros((N_SMALL, W_BR), F32).at[S_WL:S_WL + LORA_B].set(w_up_B[l].astype(F32))
    aup = jnp.zeros((N_SMALL, W_BR), F32).at[S_AL:S_AL + LORA_B].set(a_up_B[l].astype(F32))
    bd = (head_of_lane[:, None] == head_of_lane[None, :]).astype(BF16)
    par = jnp.stack([w0_B[l], a0_B[l], xi_B[l], alpha_B[l], rho_B[l], gn_g_B[l], gn_b_B[l],
                     jnp.zeros((W_BR,), F32)]).astype(F32)
    return dict(
        w_main=w_main, w_small=w_small, bd=bd, conv_w=conv_A[l].astype(F32), gpar=gpar, eg=eg, eb=eb,
        normg=jnp.tile(norm_A[l].astype(F32), N_HEADS)[None, :],
        mu=mu[None, 0:3 * W_BR], mus=_small_row([(S_WL, mu[3 * W_BR:])]), wup=wup, aup=aup, par=par,
        fbias=_small_row([(S_F, b_f_C[l])]), pool_w=pool_w_D[l].astype(F32), pscale=pool_scale_D[l].astype(F32)[None, :],
        w_out=w_out[l].astype(BF16), ln_g=ln_g[l].astype(F32)[None, :], ln_b=ln_b[l].astype(F32)[None, :])


def _prompt_layer(x3, P, layer, k_all, v_all, tb_rec, tq, tb_gate, tb_pool, tm_proj, tm_out):
    n, t, _ = x3.shape
    x2 = x3.reshape(n * t, D_MODEL)
    main2, small2, k_all, v_all = _proj(x2, P['w_main'], P['w_small'], k_all, v_all, layer, tm_proj)
    main3 = main2.reshape(n, t, N_MAIN)
    small3 = small2.reshape(n, t, N_SMALL)
    o_a, s_a = _gdn_prompt(main3, small3, P['conv_w'], P['gpar'], P['eg'], P['eb'], P['normg'], P['bd'], tb_rec)
    o_b, s_b = _rwkv_prompt(main3, small3, P['mu'], P['mus'], _bf(P['wup']), _bf(P['aup']), P['par'], P['bd'],
                            tb_rec)
    logf3, c3, ct3 = _gates(small3, P['fbias'], tb_gate)
    o_c = _fox_prompt(main3, c3, ct3, tq)
    o_d = _pool_prompt(main3, P['pool_w'], P['pscale'], tb_pool)
    flat = lambda a: a.reshape(n * t, W_BR)
    y2 = _out_proj(flat(o_a), flat(o_b), flat(o_c), flat(o_d), main2, x2, P['w_out'], P['ln_g'], P['ln_b'], tm_out)
    new = (s_a,
           main3[:, t - (CONV_W - 1):, C_QKV_A:C_QKV_A + 3 * W_BR],
           s_b,
           jnp.concatenate([main3[:, t - 1, C_RKV_B:C_RKV_B + 3 * W_BR], small3[:, t - 1, S_WL:S_WL + 2 * LORA_B]], axis=-1),
           logf3[:, :, S_F:S_F + N_HEADS],
           main3[:, t - POOL_BUF:, C_U_D:C_U_D + W_BR])
    return y2.reshape(n, t, D_MODEL), new, k_all, v_all


def _sample_layer(x3, st, cache, layer, k_all, v_all, page_table, P):
    nd = x3.shape[0]
    st_a, conv_a, st_b, shift_b, dbuf = st
    cache_k, cache_v, cache_lf = cache
    x2 = x3.reshape(nd, D_MODEL)
    main_s, small_s, k_all, v_all = _proj(x2, P['w_main'], P['w_small'], k_all, v_all, layer, nd)
    shift_small = jnp.zeros((nd, N_SMALL), F32).at[:, S_WL:S_WL + 2 * LORA_B].set(shift_b[:, 3 * W_BR:].astype(F32))
    o_a, o_b, o_d, s_a, conv_n, s_b, dbuf_n = _decode_mixers(
        main_s, small_s, st_a, conv_a, st_b, shift_b[:, 0:3 * W_BR], shift_small, dbuf,
        P['conv_w'], P['gpar'], P['normg'], P['mu'], P['mus'], P['wup'], P['aup'], P['par'], P['pool_w'], P['pscale'],
        page_table.shape[1] * PAGE_SIZE)
    qkv_s = main_s[:, C_QKV_C:C_QKV_C + 3 * W_BR]
    n_grp = math.gcd(page_table.shape[1], 32)
    o_c, lf_new = _fox_decode(page_table, qkv_s, small_s, P['fbias'], cache_k, cache_v, cache_lf, layer, n_grp)
    flat = lambda a: a.reshape(nd, W_BR)
    y2 = _out_proj(flat(o_a), flat(o_b), flat(o_c), flat(o_d), main_s, x2, P['w_out'], P['ln_g'], P['ln_b'], nd)
    new = (s_a, conv_n, s_b,
           jnp.concatenate([main_s[:, C_RKV_B:C_RKV_B + 3 * W_BR], small_s[:, S_WL:S_WL + 2 * LORA_B]], axis=-1),
           lf_new[:, :, S_F:S_F + N_HEADS],
           dbuf_n)
    return y2.reshape(nd, 1, D_MODEL), new, k_all, v_all


def kernel(x_prompt, x_sample, state_A_S, state_A_conv, state_B_S, state_B_shift, cache_C_k, cache_C_v, cache_C_logf, state_D_buf, page_table, w_in, conv_A, A_log, dt_bias, norm_A, mu_B, w0_B, w_up_B, a0_B, a_up_B, xi_B, alpha_B, rho_B, gn_g_B, gn_b_B, b_f_C, pool_w_D, pool_scale_D, w_out, ln_g, ln_b):
    depth = w_in.shape[0]
    t = x_prompt.shape[1]
    tb_rec = min(256, t)
    tq = min(256, t)
    tb_gate = min(512, t)
    tb_pool = min(512, t)
    tm_proj = min(1024, t)
    tm_out = min(256, x_prompt.shape[0] * t)
    y_p, y_s = x_prompt, x_sample
    prompt_new, sample_new = [], []
    nb, nd = x_prompt.shape[0], x_sample.shape[0]
    pk, pv = jnp.zeros((depth, nb, W_BR, t), F32), jnp.zeros((depth, nb, W_BR, t), F32)
    sk, sv = jnp.zeros((depth, nd, W_BR), F32), jnp.zeros((depth, nd, W_BR), F32)
    for l in range(depth):
        P = _layer_params(l, w_in, conv_A, A_log, dt_bias, norm_A, mu_B, w0_B, w_up_B, a0_B, a_up_B, xi_B, alpha_B,
                          rho_B, gn_g_B, gn_b_B, b_f_C, pool_w_D, pool_scale_D, w_out, ln_g, ln_b)
        y_p, new_p, pk, pv = _prompt_layer(y_p, P, l, pk, pv, tb_rec, tq, tb_gate, tb_pool, tm_proj, tm_out)
        st = (state_A_S[l], state_A_conv[l], state_B_S[l], state_B_shift[l], state_D_buf[l])
        y_s, new_s, sk, sv = _sample_layer(y_s, st, (cache_C_k, cache_C_v, cache_C_logf), l, sk, sv, page_table, P)
        prompt_new.append(new_p)
        sample_new.append(new_s)
    p_a_s, p_a_conv, p_b_s, p_b_shift, p_c_logf, p_d_buf = [jnp.stack([n[i] for n in prompt_new]) for i in range(6)]
    s_a_s, s_a_conv, s_b_s, s_b_shift, s_c_logf, s_d_buf = [jnp.stack([n[i] for n in sample_new]) for i in range(6)]
    p_c_k, p_c_v = (jnp.transpose(a.reshape(depth, nb, N_HEADS, HEAD_DIM, t), (0, 1, 4, 2, 3)) for a in (pk, pv))
    s_c_k, s_c_v = (a.reshape(depth, nd, 1, N_HEADS, HEAD_DIM) for a in (sk, sv))
    return (y_p, y_s, p_a_s, p_a_conv, p_b_s, p_b_shift, p_c_k, p_c_v, p_c_logf, p_d_buf,
            s_a_s, s_a_conv, s_b_s, s_b_shift, s_c_k, s_c_v, s_c_logf, s_d_buf)
```
